```python
import math
import jax, jax.numpy as jnp
from jax import lax
import numpy as np

D_MODEL = 1024
BATCH = 4
SEQ = 8192
DEPTH = 1

HEAD_DIM = 64
HEADS_PER_GROUP = 4
DILATED_PATTERNS = ((128, 1), (512, 4), (2048, 16))
N_GROUPS_A = 3
N_HEADS_A = N_GROUPS_A * HEADS_PER_GROUP
ATTN_WIDTH = N_HEADS_A * HEAD_DIM
ATTN_OUT_WIDTH = HEADS_PER_GROUP * HEAD_DIM
ALIBI_SPAN = 8.0
MASK_VALUE = -1e30
CONV_WIDTH = 768
CONV_K = 3
N_BRANCHES = 2
IN_COLS = 3 * ATTN_WIDTH + 3 * CONV_WIDTH + N_BRANCHES * D_MODEL
N_EXPERT_GROUPS = 4
EXPERTS_PER_GROUP = 8
N_EXPERTS = N_EXPERT_GROUPS * EXPERTS_PER_GROUP
TOP_K_WITHIN = 2
EXPERT_FF = 512
RMS_EPS = 1e-6

kernel_name = "hybrid_dilated_attn_shortconv_hmoe_encoder"


def rmsnorm(x, g):
    xf = x.astype(jnp.float32)
    y = xf * lax.rsqrt(jnp.mean(xf * xf, axis=-1, keepdims=True) + RMS_EPS) * g.astype(jnp.float32)
    return y.astype(x.dtype)


def alibi_slopes(n):
    return np.array([2.0 ** (-ALIBI_SPAN * (i + 1) / n) for i in range(n)], dtype=np.float32)


def dilated_window_attention(q, k, v, window, dilation, slopes):
    B, S, H, E = q.shape
    half = window // (2 * dilation)
    blk = half
    L = S // dilation
    nblk = -(-L // blk)
    Lp = nblk * blk

    def strided(a):
        return a.reshape(B, L, dilation, H, E).transpose(0, 2, 1, 3, 4)

    qs = jnp.pad(strided(q), ((0, 0), (0, 0), (0, Lp - L), (0, 0), (0, 0)))
    qs = qs.reshape(B, dilation, nblk, blk, H, E)

    def neighbours(a):
        ap = jnp.pad(strided(a), ((0, 0), (0, 0), (blk, Lp - L + blk), (0, 0), (0, 0)))
        return jnp.concatenate(
            [ap[:, :, s * blk:s * blk + Lp].reshape(B, dilation, nblk, blk, H, E) for s in range(3)],
            axis=3)

    ks = neighbours(k)
    vs = neighbours(v)
    scores = jnp.einsum('bdnqhe,bdnkhe->bdnhqk', qs, ks).astype(jnp.float32) * (E ** -0.5)

    qi = jnp.arange(blk)[:, None]
    kc = jnp.arange(3 * blk)[None, :]
    delta = kc - blk - qi
    key_pos = jnp.arange(nblk)[:, None, None] * blk - blk + kc[None]
    valid = (jnp.abs(delta) <= half)[None] & (key_pos >= 0) & (key_pos < L)
    dist = (jnp.abs(delta) * dilation).astype(jnp.float32)
    scores = scores - slopes.astype(jnp.float32)[:, None, None] * dist
    scores = jnp.where(valid[:, None], scores, MASK_VALUE)

    m = jnp.max(scores, axis=-1, keepdims=True)
    p = jnp.exp(scores - m)
    denom = jnp.sum(p, axis=-1, keepdims=True)
    out = jnp.einsum('bdnhqk,bdnkhe->bdnqhe', p, vs.astype(jnp.float32))
    out = out / jnp.moveaxis(denom, 3, 4)
    lse = jnp.moveaxis((m + jnp.log(denom))[..., 0], 3, 4)

    out = out.reshape(B, dilation, Lp, H, E)[:, :, :L].transpose(0, 2, 1, 3, 4).reshape(B, S, H, E)
    lse = lse.reshape(B, dilation, Lp, H)[:, :, :L].transpose(0, 2, 1, 3).reshape(B, S, H)
    return out, lse


def dilated_attention_mixer(q, k, v):
    B, S, _ = q.shape
    shp = (B, S, N_GROUPS_A, HEADS_PER_GROUP, HEAD_DIM)
    q, k, v = q.reshape(shp), k.reshape(shp), v.reshape(shp)
    slopes = jnp.asarray(alibi_slopes(N_HEADS_A).reshape(N_GROUPS_A, HEADS_PER_GROUP))
    outs, lses = [], []
    for g, (window, dilation) in enumerate(DILATED_PATTERNS):
        o, l = dilated_window_attention(q[:, :, g], k[:, :, g], v[:, :, g], window, dilation, slopes[g])
        outs.append(o)
        lses.append(l)
    outs = jnp.stack(outs, axis=0)
    w = jax.nn.softmax(jnp.stack(lses, axis=0), axis=0)
    merged = jnp.sum(w[..., None] * outs, axis=0)
    return merged.reshape(B, S, ATTN_OUT_WIDTH).astype(q.dtype)


def short_conv_mixer(b_gate, c_gate, x_in, conv_w):
    u = c_gate * x_in
    kern = conv_w.astype(u.dtype)[:, None, :]
    conv = lax.conv_general_dilated(u, kern, window_strides=(1,), padding=((CONV_K // 2, CONV_K // 2),),
                                    dimension_numbers=('NWC', 'WIO', 'NWC'), feature_group_count=CONV_WIDTH)
    return b_gate * conv


def hierarchical_moe(h, w_route_group, b_route_group, w_route_expert, b_route_expert, w1, w3, w2):
    B, S, D = h.shape
    t = h.reshape(B * S, D)
    coarse = (t @ w_route_group).astype(jnp.float32) + b_route_group.astype(jnp.float32)
    fine = (t @ w_route_expert).astype(jnp.float32) + b_route_expert.astype(jnp.float32)
    fine = fine.reshape(-1, N_EXPERT_GROUPS, EXPERTS_PER_GROUP)
    p_group = jax.nn.softmax(coarse, axis=-1)
    g_idx = lax.top_k(coarse, 1)[1][:, 0]
    pg = jnp.take_along_axis(p_group, g_idx[:, None], axis=1)[:, 0]
    fine_sel = jnp.take_along_axis(fine, g_idx[:, None, None], axis=1)[:, 0]
    p_exp = jax.nn.softmax(fine_sel, axis=-1)
    top_vals, top_idx = lax.top_k(p_exp, TOP_K_WITHIN)
    gate_w = pg[:, None] * top_vals / jnp.sum(top_vals, axis=-1, keepdims=True)
    expert_id = g_idx[:, None] * EXPERTS_PER_GROUP + top_idx
    combine = jnp.sum(jax.nn.one_hot(expert_id, N_EXPERTS, dtype=jnp.float32) * gate_w[..., None], axis=1)
    combine = combine.astype(t.dtype)
    out = jnp.zeros_like(t)
    for e in range(N_EXPERTS):
        hid = jax.nn.silu(t @ w1[e]) * (t @ w3[e])
        out = out + combine[:, e:e + 1] * (hid @ w2[e])
    return out.reshape(B, S, D)


def setup_inputs(seed: int = 0) -> dict:
    key = jax.random.key(seed)
    ks = jax.random.split(key, 20)
    f32 = jnp.float32
    D = D_MODEL

    def nrm(k, shape, fan_in):
        return jax.random.normal(k, shape, f32) * (fan_in ** -0.5)

    return {
        "x": jax.random.normal(ks[0], (BATCH, SEQ, D), f32),
        "norm_mix_g": 1.0 + 0.02 * jax.random.normal(ks[1], (DEPTH, D), f32),
        "w_in": nrm(ks[2], (DEPTH, D, IN_COLS), D),
        "b_gate": 0.01 * jax.random.normal(ks[3], (DEPTH, N_BRANCHES * D), f32),
        "conv_w": nrm(ks[4], (DEPTH, CONV_K, CONV_WIDTH), CONV_K),
        "w_attn_out": nrm(ks[5], (DEPTH, ATTN_OUT_WIDTH, D), ATTN_OUT_WIDTH),
        "w_conv_out": nrm(ks[6], (DEPTH, CONV_WIDTH, D), CONV_WIDTH),
        "w_out": nrm(ks[7], (DEPTH, D, D), D),
        "norm_ffn_g": 1.0 + 0.02 * jax.random.normal(ks[8], (DEPTH, D), f32),
        "w_route_group": nrm(ks[9], (DEPTH, D, N_EXPERT_GROUPS), D),
        "b_route_group": 0.01 * jax.random.normal(ks[10], (DEPTH, N_EXPERT_GROUPS), f32),
        "w_route_expert": nrm(ks[11], (DEPTH, D, N_EXPERTS), D),
        "b_route_expert": 0.01 * jax.random.normal(ks[12], (DEPTH, N_EXPERTS), f32),
        "w1": nrm(ks[13], (DEPTH, N_EXPERTS, D, EXPERT_FF), D),
        "w3": nrm(ks[14], (DEPTH, N_EXPERTS, D, EXPERT_FF), D),
        "w2": nrm(ks[15], (DEPTH, N_EXPERTS, EXPERT_FF, D), EXPERT_FF),
        "norm_final_g": 1.0 + 0.02 * jax.random.normal(ks[16], (D,), f32),
    }


def reference(x, norm_mix_g, w_in, b_gate, conv_w, w_attn_out, w_conv_out, w_out, norm_ffn_g,
              w_route_group, b_route_group, w_route_expert, b_route_expert, w1, w3, w2, norm_final_g):
    B, S, D = x.shape
    split_at = [ATTN_WIDTH, 2 * ATTN_WIDTH, 3 * ATTN_WIDTH,
                3 * ATTN_WIDTH + CONV_WIDTH, 3 * ATTN_WIDTH + 2 * CONV_WIDTH, 3 * ATTN_WIDTH + 3 * CONV_WIDTH]
    for l in range(DEPTH):
        h = rmsnorm(x, norm_mix_g[l])
        proj = h @ w_in[l]
        q, k, v, bg, cg, xin, gate_logits = jnp.split(proj, split_at, axis=-1)
        y_a = dilated_attention_mixer(q, k, v) @ w_attn_out[l]
        y_b = short_conv_mixer(bg, cg, xin, conv_w[l]) @ w_conv_out[l]
        gates = jax.nn.sigmoid(gate_logits + b_gate[l]).reshape(B, S, N_BRANCHES, D)
        merged = gates[:, :, 0] * y_a + gates[:, :, 1] * y_b
        x = x + merged @ w_out[l]
        h2 = rmsnorm(x, norm_ffn_g[l])
        x = x + hierarchical_moe(h2, w_route_group[l], b_route_group[l], w_route_expert[l],
                                 b_route_expert[l], w1[l], w3[l], w2[l])
    return rmsnorm(x, norm_final_g)
```

```python
import functools

import numpy as np
import jax
import jax.numpy as jnp
from jax import lax
from jax.experimental import pallas as pl
from jax.experimental.pallas import tpu as pltpu

F32 = jnp.float32
BF16 = jnp.bfloat16

D_MODEL = 1024
HEAD_DIM = 64
HEADS_PER_GROUP = 4
DILATED_PATTERNS = ((128, 1), (512, 4), (2048, 16))
N_GROUPS_A = 3
N_HEADS_A = N_GROUPS_A * HEADS_PER_GROUP
ATTN_WIDTH = N_HEADS_A * HEAD_DIM
GROUP_WIDTH = HEADS_PER_GROUP * HEAD_DIM
ALIBI_SPAN = 8.0
MASK_VALUE = -1e30
CONV_WIDTH = 768
N_EXPERT_GROUPS = 4
EXPERTS_PER_GROUP = 8
N_EXPERTS = 32
EXPERT_FF = 512
RMS_EPS = 1e-6

HALF = 64
LANES = 128
SUBLANES = 8
ROW_CHUNKS = D_MODEL // LANES

COL_K = ATTN_WIDTH
COL_V = 2 * ATTN_WIDTH
COL_BG = 3 * ATTN_WIDTH
COL_CG = COL_BG + CONV_WIDTH
COL_XIN = COL_CG + CONV_WIDTH
COL_GATE = COL_XIN + CONV_WIDTH
IN_COLS = COL_GATE + 2 * D_MODEL

TM_PROJ = 512
T_ATT = 2048
QB = 128
KB = QB + 2 * HALF
TM_MIX = 512
TM_MOE = 256
TM_CMB = 256

VMEM_LIMIT = 56 * 1024 * 1024


def _alibi_slopes():
    return np.array([2.0 ** (-ALIBI_SPAN * (i + 1) / N_HEADS_A) for i in range(N_HEADS_A)],
                    dtype=np.float32).reshape(N_GROUPS_A, HEADS_PER_GROUP)


def _rms(x, g):
    return x * lax.rsqrt(jnp.mean(x * x, axis=-1, keepdims=True) + RMS_EPS) * g


def _proj_kernel(x_ref, g_ref, w_ref, b_ref,
                 q0_ref, kv0_ref, q1_ref, kv1_ref, q2_ref, kv2_ref, bg_ref, u_ref, gate_ref, scr):
    h = _rms(x_ref[...], g_ref[...]).astype(BF16)

    def proj(c0, width):
        return jnp.dot(h, w_ref[:, c0:c0 + width], preferred_element_type=F32)

    qscale = HEAD_DIM ** -0.5
    q0_ref[...] = (proj(0, GROUP_WIDTH) * qscale).astype(BF16)
    kv0_ref[:, 0:GROUP_WIDTH] = proj(COL_K, GROUP_WIDTH).astype(BF16)
    kv0_ref[:, GROUP_WIDTH:] = proj(COL_V, GROUP_WIDTH).astype(BF16)

    for g, q_ref, kv_ref in ((1, q1_ref, kv1_ref), (2, q2_ref, kv2_ref)):
        d = DILATED_PATTERNS[g][1]
        n = TM_PROJ // d
        parts = (proj(g * GROUP_WIDTH, GROUP_WIDTH) * qscale,
                 proj(COL_K + g * GROUP_WIDTH, GROUP_WIDTH),
                 proj(COL_V + g * GROUP_WIDTH, GROUP_WIDTH))
        for i, part in enumerate(parts):
            for c in range(2):
                scr[2 * i + c] = part[:, c * LANES:(c + 1) * LANES]
        for r in range(d):
            rows = pl.ds(r, n, stride=d)
            q_ref[r] = jnp.concatenate([scr[c, rows, :] for c in range(2)], axis=1).astype(BF16)
            kv_ref[r] = jnp.concatenate([scr[c, rows, :] for c in range(2, 6)], axis=1).astype(BF16)

    bg_ref[...] = proj(COL_BG, CONV_WIDTH).astype(BF16)
    u_ref[...] = (proj(COL_CG, CONV_WIDTH) * proj(COL_XIN, CONV_WIDTH)).astype(BF16)
    for c in range(4):
        w = 2 * D_MODEL // 4
        z = proj(COL_GATE + c * w, w) + b_ref[:, c * w:(c + 1) * w]
        gate_ref[:, c * w:(c + 1) * w] = jax.nn.sigmoid(z).astype(BF16)


def _proj_call(x2, g, w_in, b_gate, batch, seq):
    n = x2.shape[0]
    steps_per_batch = seq // TM_PROJ
    d1, d2 = DILATED_PATTERNS[1][1], DILATED_PATTERNS[2][1]
    row = lambda i: (i, 0)
    res = lambda i: (i // steps_per_batch, 0, i % steps_per_batch, 0)
    const = lambda i: (0, 0)
    out_shape = [
        jax.ShapeDtypeStruct((n, GROUP_WIDTH), BF16),
        jax.ShapeDtypeStruct((n, 2 * GROUP_WIDTH), BF16),
        jax.ShapeDtypeStruct((batch, d1, seq // d1, GROUP_WIDTH), BF16),
        jax.ShapeDtypeStruct((batch, d1, seq // d1, 2 * GROUP_WIDTH), BF16),
        jax.ShapeDtypeStruct((batch, d2, seq // d2, GROUP_WIDTH), BF16),
        jax.ShapeDtypeStruct((batch, d2, seq // d2, 2 * GROUP_WIDTH), BF16),
        jax.ShapeDtypeStruct((n, CONV_WIDTH), BF16),
        jax.ShapeDtypeStruct((n, CONV_WIDTH), BF16),
        jax.ShapeDtypeStruct((n, 2 * D_MODEL), BF16),
    ]
    out_specs = [
        pl.BlockSpec((TM_PROJ, GROUP_WIDTH), row),
        pl.BlockSpec((TM_PROJ, 2 * GROUP_WIDTH), row),
        pl.BlockSpec((None, d1, TM_PROJ // d1, GROUP_WIDTH), res),
        pl.BlockSpec((None, d1, TM_PROJ // d1, 2 * GROUP_WIDTH), res),
        pl.BlockSpec((None, d2, TM_PROJ // d2, GROUP_WIDTH), res),
        pl.BlockSpec((None, d2, TM_PROJ // d2, 2 * GROUP_WIDTH), res),
        pl.BlockSpec((TM_PROJ, CONV_WIDTH), row),
        pl.BlockSpec((TM_PROJ, CONV_WIDTH), row),
        pl.BlockSpec((TM_PROJ, 2 * D_MODEL), row),
    ]
    return pl.pallas_call(
        _proj_kernel,
        grid=(n // TM_PROJ,),
        in_specs=[
            pl.BlockSpec((TM_PROJ, D_MODEL), row),
            pl.BlockSpec((1, D_MODEL), const),
            pl.BlockSpec((D_MODEL, IN_COLS), const),
            pl.BlockSpec((1, 2 * D_MODEL), const),
        ],
        out_specs=out_specs,
        out_shape=out_shape,
        scratch_shapes=[pltpu.VMEM((6, TM_PROJ, LANES), F32)],
        compiler_params=pltpu.CompilerParams(dimension_semantics=("arbitrary",),
                                             vmem_limit_bytes=VMEM_LIMIT),
        name="proj",
    )(x2, g, w_in, b_gate)


def _attn_sub_block(q_sub, kw, vw, bias_ref, g, lo, hi):
    lane_q = lax.broadcasted_iota(jnp.int32, (QB, KB), 1)
    lane_v = lax.broadcasted_iota(jnp.int32, (KB, GROUP_WIDTH), 1)
    edge_ok = (lane_q >= lo) & (lane_q < hi)
    k_t = kw.T
    zero = jnp.zeros((), BF16)
    probs, v_parts = [], []
    m_b = l_b = None
    for h in range(HEADS_PER_GROUP):
        head_q = (lane_q >= h * HEAD_DIM) & (lane_q < (h + 1) * HEAD_DIM)
        head_v = (lane_v >= h * HEAD_DIM) & (lane_v < (h + 1) * HEAD_DIM)
        s = jnp.dot(jnp.where(head_q, q_sub, zero), k_t, preferred_element_type=F32)
        s = jnp.where(edge_ok, s + bias_ref[g * HEADS_PER_GROUP + h], MASK_VALUE)
        m = jnp.max(s, axis=1, keepdims=True)
        p = jnp.exp(s - m)
        l = jnp.sum(p, axis=1, keepdims=True)
        probs.append(p.astype(BF16))
        v_parts.append(jnp.where(head_v, vw, zero))
        m_b = jnp.broadcast_to(m, (QB, GROUP_WIDTH)) if m_b is None else jnp.where(head_q, m, m_b)
        l_b = jnp.broadcast_to(l, (QB, GROUP_WIDTH)) if l_b is None else jnp.where(head_q, l, l_b)
    acc = jnp.dot(jnp.concatenate(probs, axis=1), jnp.concatenate(v_parts, axis=0),
                  preferred_element_type=F32)
    return acc, m_b, l_b


def _attn_kernel(q0_ref, kv0_ref, kv0p_ref, kv0n_ref,
                 q1_ref, kv1_ref, kv1p_ref, kv1n_ref,
                 q2_ref, kv2_ref, kv2p_ref, kv2n_ref,
                 y_ref,
                 cat0, cat1, cat2, bias_ref, m_st, l_st, a_st, m_tmp, l_tmp, a_tmp, *, seq):
    j = pl.program_id(1)

    qi = lax.broadcasted_iota(jnp.int32, (QB, KB), 0)
    kc = lax.broadcasted_iota(jnp.int32, (QB, KB), 1)
    adelta = jnp.abs(kc - HALF - qi)
    band = adelta <= HALF
    slopes = _alibi_slopes()
    for g in range(N_GROUPS_A):
        dist = (adelta * DILATED_PATTERNS[g][1]).astype(F32)
        for h in range(HEADS_PER_GROUP):
            bias_ref[g * HEADS_PER_GROUP + h] = jnp.where(band, -(float(slopes[g, h]) * dist), MASK_VALUE)

    for cat, own, prv, nxt in ((cat0, kv0_ref, kv0p_ref, kv0n_ref),
                               (cat1, kv1_ref, kv1p_ref, kv1n_ref),
                               (cat2, kv2_ref, kv2p_ref, kv2n_ref)):
        n_own = own.shape[-2]
        cat[:, 0:HALF, :] = prv[...].reshape(cat.shape[0], HALF, 2 * GROUP_WIDTH)
        cat[:, HALF:HALF + n_own, :] = own[...].reshape(cat.shape[0], n_own, 2 * GROUP_WIDTH)
        cat[:, HALF + n_own:, :] = nxt[...].reshape(cat.shape[0], HALF, 2 * GROUP_WIDTH)

    def window(cat, r, sb):
        rows = pl.ds(pl.multiple_of(sb * QB, QB), KB)
        return cat[r, rows, 0:GROUP_WIDTH], cat[r, rows, GROUP_WIDTH:]

    def edges(g, n_res, sb):
        length = seq // DILATED_PATTERNS[g][1]
        i0 = j * n_res + sb * QB
        return jnp.maximum(0, HALF - i0), jnp.minimum(KB, length + HALF - i0)

    def body0(sb, carry):
        rows = pl.ds(pl.multiple_of(sb * QB, QB), QB)
        kw, vw = window(cat0, 0, sb)
        lo, hi = edges(0, T_ATT, sb)
        acc, m_b, l_b = _attn_sub_block(q0_ref[rows, :], kw, vw, bias_ref, 0, lo, hi)
        for c in range(2):
            cols = slice(c * LANES, (c + 1) * LANES)
            m_st[c, rows, :] = m_b[:, cols]
            l_st[c, rows, :] = l_b[:, cols]
            a_st[c, rows, :] = acc[:, cols]
        return carry

    lax.fori_loop(0, T_ATT // QB, body0, 0)

    for g, q_ref, cat in ((1, q1_ref, cat1), (2, q2_ref, cat2)):
        d = DILATED_PATTERNS[g][1]
        n_res = T_ATT // d
        sb_per_res = n_res // QB

        def body(idx, carry, g=g, q_ref=q_ref, cat=cat, n_res=n_res, sb_per_res=sb_per_res):
            r = idx // sb_per_res
            sb = idx % sb_per_res
            kw, vw = window(cat, r, sb)
            lo, hi = edges(g, n_res, sb)
            q_sub = q_ref[r, pl.ds(pl.multiple_of(sb * QB, QB), QB), :]
            acc, m_b, l_b = _attn_sub_block(q_sub, kw, vw, bias_ref, g, lo, hi)
            rows = pl.ds(pl.multiple_of(idx * QB, QB), QB)
            m_tmp[rows, :] = m_b
            l_tmp[rows, :] = l_b
            a_tmp[rows, :] = acc
            return carry

        lax.fori_loop(0, T_ATT // QB, body, 0)

        for r in range(d):
            for ch in range(sb_per_res):
                src = slice(r * n_res + ch * QB, r * n_res + (ch + 1) * QB)
                tok = pl.ds(ch * QB * d + r, QB, stride=d)
                for c in range(2):
                    cols = slice(c * LANES, (c + 1) * LANES)
                    m_new_part = m_tmp[src, cols]
                    m_old = m_st[c, tok, :]
                    m_new = jnp.maximum(m_old, m_new_part)
                    e_old = jnp.exp(m_old - m_new)
                    e_new = jnp.exp(m_new_part - m_new)
                    m_st[c, tok, :] = m_new
                    l_st[c, tok, :] = e_old * l_st[c, tok, :] + e_new * l_tmp[src, cols]
                    a_st[c, tok, :] = e_old * a_st[c, tok, :] + e_new * a_tmp[src, cols]

    for c in range(2):
        y_ref[:, c * LANES:(c + 1) * LANES] = (a_st[c] / l_st[c]).astype(BF16)


def _attn_call(q0, kv0, q1, kv1, q2, kv2, batch, seq):
    n = q0.shape[0]
    tiles = seq // T_ATT
    specs = []
    scratch = []
    blocks_per_tile = T_ATT // HALF
    n_half_blocks = n // HALF
    specs += [
        pl.BlockSpec((T_ATT, GROUP_WIDTH), lambda b, j: (b * tiles + j, 0)),
        pl.BlockSpec((T_ATT, 2 * GROUP_WIDTH), lambda b, j: (b * tiles + j, 0)),
        pl.BlockSpec((HALF, 2 * GROUP_WIDTH),
                     lambda b, j: (jnp.maximum((b * tiles + j) * blocks_per_tile - 1, 0), 0)),
        pl.BlockSpec((HALF, 2 * GROUP_WIDTH),
                     lambda b, j: (jnp.minimum((b * tiles + j + 1) * blocks_per_tile, n_half_blocks - 1), 0)),
    ]
    scratch.append(pltpu.VMEM((1, T_ATT + 2 * HALF, 2 * GROUP_WIDTH), BF16))
    for g in (1, 2):
        d = DILATED_PATTERNS[g][1]
        n_res = T_ATT // d
        per_tile = n_res // HALF
        last = seq // d // HALF - 1
        specs += [
            pl.BlockSpec((None, d, n_res, GROUP_WIDTH), lambda b, j: (b, 0, j, 0)),
            pl.BlockSpec((None, d, n_res, 2 * GROUP_WIDTH), lambda b, j: (b, 0, j, 0)),
            pl.BlockSpec((None, d, HALF, 2 * GROUP_WIDTH),
                         lambda b, j, per_tile=per_tile: (b, 0, jnp.maximum(j * per_tile - 1, 0), 0)),
            pl.BlockSpec((None, d, HALF, 2 * GROUP_WIDTH),
                         lambda b, j, per_tile=per_tile, last=last: (b, 0, jnp.minimum((j + 1) * per_tile, last), 0)),
        ]
        scratch.append(pltpu.VMEM((d, n_res + 2 * HALF, 2 * GROUP_WIDTH), BF16))
    scratch.append(pltpu.VMEM((N_HEADS_A, QB, KB), F32))
    scratch += [pltpu.VMEM((2, T_ATT, LANES), F32) for _ in range(3)]
    scratch += [pltpu.VMEM((T_ATT, GROUP_WIDTH), F32) for _ in range(3)]
    return pl.pallas_call(
        functools.partial(_attn_kernel, seq=seq),
        grid=(batch, tiles),
        in_specs=specs,
        out_specs=pl.BlockSpec((T_ATT, GROUP_WIDTH), lambda b, j: (b * tiles + j, 0)),
        out_shape=jax.ShapeDtypeStruct((n, GROUP_WIDTH), BF16),
        scratch_shapes=scratch,
        compiler_params=pltpu.CompilerParams(dimension_semantics=("arbitrary", "arbitrary"),
                                             vmem_limit_bytes=VMEM_LIMIT),
        name="attn",
    )(q0, kv0, kv0, kv0, q1, kv1, kv1, kv1, q2, kv2, kv2, kv2)


def _split_dot(a, w_hi, w_lo):
    a_hi = a.astype(BF16)
    a_lo = (a - a_hi.astype(F32)).astype(BF16)
    return (jnp.dot(a_hi, w_hi, preferred_element_type=F32)
            + jnp.dot(a_lo, w_hi, preferred_element_type=F32)
            + jnp.dot(a_hi, w_lo, preferred_element_type=F32))


def _mix_kernel(x_ref, ya_ref, bg_ref, u_ref, up_ref, un_ref, gate_ref,
                wa_ref, wc_ref, wo_ref, cw_ref, g2_ref, wr_hi_ref, wr_lo_ref, br_ref,
                x1_ref, hflat_ref, route_ref, cnt_ref, base_ref, *, seq):
    i = pl.program_id(0)
    t0 = i * TM_MIX

    @pl.when(i == 0)
    def _():
        base_ref[...] = jnp.zeros_like(base_ref)

    u = u_ref[...].astype(F32)
    row = lax.broadcasted_iota(jnp.int32, (TM_MIX, CONV_WIDTH), 0)
    prev_row = jnp.where(t0 % seq == 0, 0.0, up_ref[15:16, :].astype(F32))
    next_row = jnp.where((t0 + TM_MIX) % seq == 0, 0.0, un_ref[0:1, :].astype(F32))
    u_prev = jnp.where(row == 0, prev_row, pltpu.roll(u, 1, axis=0))
    u_next = jnp.where(row == TM_MIX - 1, next_row, pltpu.roll(u, TM_MIX - 1, axis=0))
    conv = cw_ref[0:1, :] * u_prev + cw_ref[1:2, :] * u + cw_ref[2:3, :] * u_next
    yb_in = (bg_ref[...].astype(F32) * conv).astype(BF16)

    y_a = jnp.dot(ya_ref[...], wa_ref[...], preferred_element_type=F32)
    y_b = jnp.dot(yb_in, wc_ref[...], preferred_element_type=F32)
    merged = (gate_ref[:, 0:D_MODEL].astype(F32) * y_a
              + gate_ref[:, D_MODEL:].astype(F32) * y_b).astype(BF16)
    x1 = x_ref[...] + jnp.dot(merged, wo_ref[...], preferred_element_type=F32)
    x1_ref[...] = x1

    h2 = _rms(x1, g2_ref[...])
    for c in range(ROW_CHUNKS):
        hflat_ref[pl.ds(c, TM_MIX, stride=ROW_CHUNKS), :] = h2[:, c * LANES:(c + 1) * LANES]

    logits = _split_dot(h2, wr_hi_ref[...], wr_lo_ref[...]) + br_ref[...]
    lane = lax.broadcasted_iota(jnp.int32, (TM_MIX, LANES), 1)
    lane_f = lane.astype(F32)
    neg = -jnp.inf
    big = float(LANES)
    is_group = lane < N_EXPERT_GROUPS
    cm = jnp.where(is_group, logits, neg)
    cmax = jnp.max(cm, axis=1, keepdims=True)
    g_idx = jnp.min(jnp.where(cm == cmax, lane_f, big), axis=1, keepdims=True)
    p_group = 1.0 / jnp.sum(jnp.where(is_group, jnp.exp(logits - cmax), 0.0), axis=1, keepdims=True)
    f_lo = N_EXPERT_GROUPS + EXPERTS_PER_GROUP * g_idx
    in_group = (lane_f >= f_lo) & (lane_f < f_lo + EXPERTS_PER_GROUP)
    fm = jnp.where(in_group, logits, neg)
    f1 = jnp.max(fm, axis=1, keepdims=True)
    i1 = jnp.min(jnp.where(fm == f1, lane_f, big), axis=1, keepdims=True)
    fm2 = jnp.where(lane_f == i1, neg, fm)
    f2 = jnp.max(fm2, axis=1, keepdims=True)
    i2 = jnp.min(jnp.where(fm2 == f2, lane_f, big), axis=1, keepdims=True)
    e21 = jnp.exp(f2 - f1)
    w_1 = p_group / (1.0 + e21)
    w_2 = p_group * e21 / (1.0 + e21)
    e_1 = i1 - N_EXPERT_GROUPS
    e_2 = i2 - N_EXPERT_GROUPS

    onehot = jnp.where((lane_f == e_1) | (lane_f == e_2), 1.0, 0.0)
    r_i = lax.broadcasted_iota(jnp.int32, (TM_MIX, TM_MIX), 0)
    c_i = lax.broadcasted_iota(jnp.int32, (TM_MIX, TM_MIX), 1)
    tri = jnp.where(c_i < r_i, 1.0, 0.0).astype(BF16)
    before = jnp.dot(tri, onehot.astype(BF16), preferred_element_type=F32) + base_ref[0:1, :]
    rank_1 = jnp.sum(jnp.where(lane_f == e_1, before, 0.0), axis=1, keepdims=True)
    rank_2 = jnp.sum(jnp.where(lane_f == e_2, before, 0.0), axis=1, keepdims=True)
    base_ref[0:1, :] = base_ref[0:1, :] + jnp.sum(onehot, axis=0, keepdims=True)
    cnt_ref[...] = jnp.broadcast_to(base_ref[0:1, :], cnt_ref.shape)

    route = jnp.zeros((TM_MIX, LANES), F32)
    for k, val in enumerate((e_1, e_2, w_1, w_2, rank_1, rank_2)):
        route = jnp.where(lane == k, val, route)
    route_ref[...] = route


def _mix_call(x2, y_attn, bgate, u, gates, wa, wc, wo, conv_w, g2, wr_hi, wr_lo, br, seq):
    n = x2.shape[0]
    row = lambda i: (i, 0)
    const = lambda i: (0, 0)
    halo = 16
    per_tile = TM_MIX // halo
    last = n // halo - 1
    return pl.pallas_call(
        functools.partial(_mix_kernel, seq=seq),
        grid=(n // TM_MIX,),
        in_specs=[
            pl.BlockSpec((TM_MIX, D_MODEL), row),
            pl.BlockSpec((TM_MIX, GROUP_WIDTH), row),
            pl.BlockSpec((TM_MIX, CONV_WIDTH), row),
            pl.BlockSpec((TM_MIX, CONV_WIDTH), row),
            pl.BlockSpec((halo, CONV_WIDTH), lambda i: (jnp.maximum(i * per_tile - 1, 0), 0)),
            pl.BlockSpec((halo, CONV_WIDTH), lambda i: (jnp.minimum((i + 1) * per_tile, last), 0)),
            pl.BlockSpec((TM_MIX, 2 * D_MODEL), row),
            pl.BlockSpec((GROUP_WIDTH, D_MODEL), const),
            pl.BlockSpec((CONV_WIDTH, D_MODEL), const),
            pl.BlockSpec((D_MODEL, D_MODEL), const),
            pl.BlockSpec((3, CONV_WIDTH), const),
            pl.BlockSpec((1, D_MODEL), const),
            pl.BlockSpec((D_MODEL, LANES), const),
            pl.BlockSpec((D_MODEL, LANES), const),
            pl.BlockSpec((1, LANES), const),
        ],
        out_specs=[
            pl.BlockSpec((TM_MIX, D_MODEL), row),
            pl.BlockSpec((TM_MIX * ROW_CHUNKS, LANES), row),
            pl.BlockSpec((TM_MIX, LANES), row),
            pl.BlockSpec((SUBLANES, LANES), const),
        ],
        out_shape=[
            jax.ShapeDtypeStruct((n, D_MODEL), F32),
            jax.ShapeDtypeStruct((n * ROW_CHUNKS, LANES), F32),
            jax.ShapeDtypeStruct((n, LANES), F32),
            jax.ShapeDtypeStruct((SUBLANES, LANES), F32),
        ],
        scratch_shapes=[pltpu.VMEM((SUBLANES, LANES), F32)],
        compiler_params=pltpu.CompilerParams(dimension_semantics=("arbitrary",),
                                             vmem_limit_bytes=VMEM_LIMIT),
        name="mix",
    )(x2, y_attn, bgate, u, u, u, gates, wa, wc, wo, conv_w, g2, wr_hi, wr_lo, br)


def _row_gather(idx_ref, n_rows, src_hbm, dst, sem):
    def issue(j, carry):
        t = idx_ref[0, 0, j]
        pltpu.make_async_copy(src_hbm.at[pl.ds(pl.multiple_of(t * SUBLANES, SUBLANES), SUBLANES)],
                              dst.at[pl.ds(pl.multiple_of(j * SUBLANES, SUBLANES), SUBLANES)],
                              sem).start()
        return carry
    lax.fori_loop(0, n_rows, issue, 0, unroll=8)


def _row_gather_wait(n_rows, src_hbm, dst, sem):
    pltpu.make_async_copy(src_hbm.at[pl.ds(0, n_rows * SUBLANES)], dst, sem).wait()


def _rows_from_tiles(buf, first_row, n_rows):
    return jnp.concatenate(
        [buf[pl.ds(first_row * ROW_CHUNKS + c, n_rows, stride=ROW_CHUNKS), :] for c in range(ROW_CHUNKS)],
        axis=1)


def _expert_kernel(te_ref, nused_ref, src_ref, srcn_ref, h_hbm, w1_ref, w3_ref, w2_ref,
                   y_ref, buf0, buf1, sem):
    i = pl.program_id(0)
    n_used = nused_ref[0]
    bufs = (buf0, buf1)

    @pl.when(i == 0)
    def _():
        _row_gather(src_ref, TM_MOE, h_hbm, buf0, sem.at[0])

    for slot in range(2):
        @pl.when((i % 2 == slot) & (i + 1 < n_used))
        def _(slot=slot):
            _row_gather(srcn_ref, TM_MOE, h_hbm, bufs[1 - slot], sem.at[1 - slot])

    for slot in range(2):
        @pl.when((i % 2 == slot) & (i < n_used))
        def _(slot=slot):
            _row_gather_wait(TM_MOE, h_hbm, bufs[slot], sem.at[slot])
            x = _rows_from_tiles(bufs[slot], 0, TM_MOE).astype(BF16)
            a = jnp.dot(x, w1_ref[...], preferred_element_type=F32)
            b = jnp.dot(x, w3_ref[...], preferred_element_type=F32)
            hid = (a * jax.nn.sigmoid(a) * b).astype(BF16)
            y = jnp.dot(hid, w2_ref[...], preferred_element_type=F32)
            for c in range(ROW_CHUNKS):
                y_ref[pl.ds(c, TM_MOE, stride=ROW_CHUNKS), :] = y[:, c * LANES:(c + 1) * LANES]

    @pl.when(i >= n_used)
    def _():
        y_ref[...] = jnp.zeros_like(y_ref)


def _expert_call(tile_expert, n_used, src, h_flat, w1, w3, w2):
    n_tiles = src.shape[0]
    wspec = lambda shape: pl.BlockSpec((None,) + shape, lambda i, te, nu: (te[i], 0, 0))
    grid_spec = pltpu.PrefetchScalarGridSpec(
        num_scalar_prefetch=2,
        grid=(n_tiles,),
        in_specs=[
            pl.BlockSpec((1, 1, TM_MOE), lambda i, te, nu: (i, 0, 0), memory_space=pltpu.SMEM),
            pl.BlockSpec((1, 1, TM_MOE), lambda i, te, nu: (jnp.minimum(i + 1, n_tiles - 1), 0, 0),
                         memory_space=pltpu.SMEM),
            pl.BlockSpec(memory_space=pl.ANY),
            wspec((D_MODEL, EXPERT_FF)),
            wspec((D_MODEL, EXPERT_FF)),
            wspec((EXPERT_FF, D_MODEL)),
        ],
        out_specs=pl.BlockSpec((TM_MOE * ROW_CHUNKS, LANES), lambda i, te, nu: (i, 0)),
        scratch_shapes=[pltpu.VMEM((TM_MOE * ROW_CHUNKS, LANES), F32),
                        pltpu.VMEM((TM_MOE * ROW_CHUNKS, LANES), F32),
                        pltpu.SemaphoreType.DMA((2,))],
    )
    return pl.pallas_call(
        _expert_kernel,
        grid_spec=grid_spec,
        out_shape=jax.ShapeDtypeStruct((n_tiles * TM_MOE * ROW_CHUNKS, LANES), F32),
        compiler_params=pltpu.CompilerParams(dimension_semantics=("arbitrary",),
                                             vmem_limit_bytes=VMEM_LIMIT),
        name="experts",
    )(tile_expert, n_used, src, src, h_flat, w1, w3, w2)


def _combine_kernel(pos_ref, posn_ref, y_hbm, x1_ref, route_ref, g_ref, o_ref, buf0, buf1, sem):
    i = pl.program_id(0)
    n_steps = pl.num_programs(0)
    bufs = (buf0, buf1)

    @pl.when(i == 0)
    def _():
        _row_gather(pos_ref, 2 * TM_CMB, y_hbm, buf0, sem.at[0])

    for slot in range(2):
        @pl.when((i % 2 == slot) & (i + 1 < n_steps))
        def _(slot=slot):
            _row_gather(posn_ref, 2 * TM_CMB, y_hbm, bufs[1 - slot], sem.at[1 - slot])

    for slot in range(2):
        @pl.when(i % 2 == slot)
        def _(slot=slot):
            _row_gather_wait(2 * TM_CMB, y_hbm, bufs[slot], sem.at[slot])
            y_1 = _rows_from_tiles(bufs[slot], 0, TM_CMB)
            y_2 = _rows_from_tiles(bufs[slot], TM_CMB, TM_CMB)
            x = x1_ref[...] + route_ref[:, 2:3] * y_1 + route_ref[:, 3:4] * y_2
            o_ref[...] = _rms(x, g_ref[...])


def _combine_call(pos, y_flat, x1, route, g):
    n = x1.shape[0]
    n_steps = n // TM_CMB
    row = lambda i: (i, 0)
    return pl.pallas_call(
        _combine_kernel,
        grid=(n_steps,),
        in_specs=[
            pl.BlockSpec((1, 1, 2 * TM_CMB), lambda i: (i, 0, 0), memory_space=pltpu.SMEM),
            pl.BlockSpec((1, 1, 2 * TM_CMB), lambda i: (jnp.minimum(i + 1, n_steps - 1), 0, 0),
                         memory_space=pltpu.SMEM),
            pl.BlockSpec(memory_space=pl.ANY),
            pl.BlockSpec((TM_CMB, D_MODEL), row),
            pl.BlockSpec((TM_CMB, LANES), row),
            pl.BlockSpec((1, D_MODEL), lambda i: (0, 0)),
        ],
        out_specs=pl.BlockSpec((TM_CMB, D_MODEL), row),
        out_shape=jax.ShapeDtypeStruct((n, D_MODEL), F32),
        scratch_shapes=[pltpu.VMEM((2 * TM_CMB * ROW_CHUNKS, LANES), F32),
                        pltpu.VMEM((2 * TM_CMB * ROW_CHUNKS, LANES), F32),
                        pltpu.SemaphoreType.DMA((2,))],
        compiler_params=pltpu.CompilerParams(dimension_semantics=("arbitrary",),
                                             vmem_limit_bytes=VMEM_LIMIT),
        name="combine",
    )(pos, pos, y_flat, x1, route, g)


def _layer(x2, batch, seq, norm_mix_g, w_in, b_gate, conv_w, w_attn_out, w_conv_out, w_out, norm_ffn_g,
           w_route_group, b_route_group, w_route_expert, b_route_expert, w1, w3, w2, final_g):
    n = x2.shape[0]
    q0, kv0, q1, kv1, q2, kv2, bgate, u, gates = _proj_call(
        x2, norm_mix_g[None, :], w_in.astype(BF16), b_gate[None, :], batch, seq)
    y_attn = _attn_call(q0, kv0, q1, kv1, q2, kv2, batch, seq)

    n_route = N_EXPERT_GROUPS + N_EXPERTS
    w_route = jnp.pad(jnp.concatenate([w_route_group, w_route_expert], axis=1), ((0, 0), (0, LANES - n_route)))
    b_route = jnp.pad(jnp.concatenate([b_route_group, b_route_expert]), (0, LANES - n_route))[None, :]
    wr_hi = w_route.astype(BF16)
    wr_lo = (w_route - wr_hi.astype(F32)).astype(BF16)
    x1, h_flat, route, counts = _mix_call(
        x2, y_attn, bgate, u, gates, w_attn_out.astype(BF16), w_conv_out.astype(BF16), w_out.astype(BF16),
        conv_w, norm_ffn_g[None, :], wr_hi, wr_lo, b_route, seq)

    expert = route[:, 0:2].astype(jnp.int32)
    rank = route[:, 4:6].astype(jnp.int32)
    cnt = counts[0, :N_EXPERTS].astype(jnp.int32)
    tiles_e = (cnt + TM_MOE - 1) // TM_MOE
    tile_end = jnp.cumsum(tiles_e)
    row_off = (tile_end - tiles_e) * TM_MOE
    pos = row_off[expert] + rank
    n_tiles = (2 * n) // TM_MOE + N_EXPERTS
    token = jnp.broadcast_to(jnp.arange(n, dtype=jnp.int32)[:, None], (n, 2))
    src = jnp.zeros((n_tiles * TM_MOE,), jnp.int32).at[pos.reshape(-1)].set(
        token.reshape(-1), unique_indices=True)
    tile_expert = jnp.minimum(
        jnp.searchsorted(tile_end, jnp.arange(n_tiles, dtype=jnp.int32), side="right"),
        N_EXPERTS - 1).astype(jnp.int32)
    n_used = tile_end[-1:].astype(jnp.int32)

    y_flat = _expert_call(tile_expert, n_used, src.reshape(n_tiles, 1, TM_MOE), h_flat,
                          w1.astype(BF16), w3.astype(BF16), w2.astype(BF16))
    pos_tiles = pos.reshape(n // TM_CMB, TM_CMB, 2).transpose(0, 2, 1).reshape(n // TM_CMB, 1, 2 * TM_CMB)
    return _combine_call(pos_tiles, y_flat, x1, route, final_g[None, :])


def kernel(x, norm_mix_g, w_in, b_gate, conv_w, w_attn_out, w_conv_out, w_out, norm_ffn_g,
           w_route_group, b_route_group, w_route_expert, b_route_expert, w1, w3, w2, norm_final_g):
    batch, seq, d = x.shape
    depth = w_in.shape[0]
    assert d == D_MODEL and depth == 1 and seq % T_ATT == 0
    out = _layer(x.reshape(batch * seq, d), batch, seq, norm_mix_g[0], w_in[0], b_gate[0], conv_w[0],
                 w_attn_out[0], w_conv_out[0], w_out[0], norm_ffn_g[0], w_route_group[0], b_route_group[0],
                 w_route_expert[0], b_route_expert[0], w1[0], w3[0], w2[0], norm_final_g)
    return out.reshape(batch, seq, d)
```

```python
import functools

import numpy as np
import jax
import jax.numpy as jnp
from jax import lax
from jax.experimental import pallas as pl
from jax.experimental.pallas import tpu as pltpu

F32 = jnp.float32
BF16 = jnp.bfloat16

D_MODEL = 1024
HEAD_DIM = 64
HEADS_PER_GROUP = 4
DILATED_PATTERNS = ((128, 1), (512, 4), (2048, 16))
N_GROUPS_A = 3
N_HEADS_A = N_GROUPS_A * HEADS_PER_GROUP
ATTN_WIDTH = N_HEADS_A * HEAD_DIM
GROUP_WIDTH = HEADS_PER_GROUP * HEAD_DIM
ALIBI_SPAN = 8.0
MASK_VALUE = -1e30
CONV_WIDTH = 768
N_EXPERT_GROUPS = 4
EXPERTS_PER_GROUP = 8
N_EXPERTS = 32
EXPERT_FF = 512
RMS_EPS = 1e-6

HALF = 64
LANES = 128
SUBLANES = 8
ROW_CHUNKS = D_MODEL // LANES

COL_K = ATTN_WIDTH
COL_V = 2 * ATTN_WIDTH
COL_BG = 3 * ATTN_WIDTH
COL_CG = COL_BG + CONV_WIDTH
COL_XIN = COL_CG + CONV_WIDTH
COL_GATE = COL_XIN + CONV_WIDTH
IN_COLS = COL_GATE + 2 * D_MODEL

TM_PROJ = 512
T_ATT = 2048
QB = 128
KB = QB + 2 * HALF
TM_MIX = 512
TM_DSP = 512
TM_MOE = 256
TM_CMB = 256

VMEM_LIMIT = 56 * 1024 * 1024


def _alibi_slopes():
    return np.array([2.0 ** (-ALIBI_SPAN * (i + 1) / N_HEADS_A) for i in range(N_HEADS_A)],
                    dtype=np.float32).reshape(N_GROUPS_A, HEADS_PER_GROUP)


def _rms(x, g):
    return x * lax.rsqrt(jnp.mean(x * x, axis=-1, keepdims=True) + RMS_EPS) * g


def _proj_kernel(x_ref, g_ref, w_ref, b_ref,
                 q0_ref, kv0_ref, q1_ref, kv1_ref, q2_ref, kv2_ref, bg_ref, u_ref, gate_ref, scr):
    h = _rms(x_ref[...], g_ref[...]).astype(BF16)

    def proj(c0, width):
        return jnp.dot(h, w_ref[:, c0:c0 + width], preferred_element_type=F32)

    qscale = HEAD_DIM ** -0.5
    q0_ref[...] = (proj(0, GROUP_WIDTH) * qscale).astype(BF16)
    kv0_ref[:, 0:GROUP_WIDTH] = proj(COL_K, GROUP_WIDTH).astype(BF16)
    kv0_ref[:, GROUP_WIDTH:] = proj(COL_V, GROUP_WIDTH).astype(BF16)

    for g, q_ref, kv_ref in ((1, q1_ref, kv1_ref), (2, q2_ref, kv2_ref)):
        d = DILATED_PATTERNS[g][1]
        n = TM_PROJ // d
        parts = (proj(g * GROUP_WIDTH, GROUP_WIDTH) * qscale,
                 proj(COL_K + g * GROUP_WIDTH, GROUP_WIDTH),
                 proj(COL_V + g * GROUP_WIDTH, GROUP_WIDTH))
        for i, part in enumerate(parts):
            for c in range(2):
                scr[2 * i + c] = part[:, c * LANES:(c + 1) * LANES]
        for r in range(d):
            rows = pl.ds(r, n, stride=d)
            q_ref[r] = jnp.concatenate([scr[c, rows, :] for c in range(2)], axis=1).astype(BF16)
            kv_ref[r] = jnp.concatenate([scr[c, rows, :] for c in range(2, 6)], axis=1).astype(BF16)

    bg_ref[...] = proj(COL_BG, CONV_WIDTH).astype(BF16)
    u_ref[...] = (proj(COL_CG, CONV_WIDTH) * proj(COL_XIN, CONV_WIDTH)).astype(BF16)
    for c in range(4):
        w = 2 * D_MODEL // 4
        z = proj(COL_GATE + c * w, w) + b_ref[:, c * w:(c + 1) * w]
        gate_ref[:, c * w:(c + 1) * w] = jax.nn.sigmoid(z).astype(BF16)


def _proj_call(x2, g, w_in, b_gate, batch, seq):
    n = x2.shape[0]
    steps_per_batch = seq // TM_PROJ
    d1, d2 = DILATED_PATTERNS[1][1], DILATED_PATTERNS[2][1]
    row = lambda i: (i, 0)
    res = lambda i: (i // steps_per_batch, 0, i % steps_per_batch, 0)
    const = lambda i: (0, 0)
    out_shape = [
        jax.ShapeDtypeStruct((n, GROUP_WIDTH), BF16),
        jax.ShapeDtypeStruct((n, 2 * GROUP_WIDTH), BF16),
        jax.ShapeDtypeStruct((batch, d1, seq // d1, GROUP_WIDTH), BF16),
        jax.ShapeDtypeStruct((batch, d1, seq // d1, 2 * GROUP_WIDTH), BF16),
        jax.ShapeDtypeStruct((batch, d2, seq // d2, GROUP_WIDTH), BF16),
        jax.ShapeDtypeStruct((batch, d2, seq // d2, 2 * GROUP_WIDTH), BF16),
        jax.ShapeDtypeStruct((n, CONV_WIDTH), BF16),
        jax.ShapeDtypeStruct((n, CONV_WIDTH), BF16),
        jax.ShapeDtypeStruct((n, 2 * D_MODEL), BF16),
    ]
    out_specs = [
        pl.BlockSpec((TM_PROJ, GROUP_WIDTH), row),
        pl.BlockSpec((TM_PROJ, 2 * GROUP_WIDTH), row),
        pl.BlockSpec((None, d1, TM_PROJ // d1, GROUP_WIDTH), res),
        pl.BlockSpec((None, d1, TM_PROJ // d1, 2 * GROUP_WIDTH), res),
        pl.BlockSpec((None, d2, TM_PROJ // d2, GROUP_WIDTH), res),
        pl.BlockSpec((None, d2, TM_PROJ // d2, 2 * GROUP_WIDTH), res),
        pl.BlockSpec((TM_PROJ, CONV_WIDTH), row),
        pl.BlockSpec((TM_PROJ, CONV_WIDTH), row),
        pl.BlockSpec((TM_PROJ, 2 * D_MODEL), row),
    ]
    return pl.pallas_call(
        _proj_kernel,
        grid=(n // TM_PROJ,),
        in_specs=[
            pl.BlockSpec((TM_PROJ, D_MODEL), row),
            pl.BlockSpec((1, D_MODEL), const),
            pl.BlockSpec((D_MODEL, IN_COLS), const),
            pl.BlockSpec((1, 2 * D_MODEL), const),
        ],
        out_specs=out_specs,
        out_shape=out_shape,
        scratch_shapes=[pltpu.VMEM((6, TM_PROJ, LANES), F32)],
        compiler_params=pltpu.CompilerParams(dimension_semantics=("arbitrary",),
                                             vmem_limit_bytes=VMEM_LIMIT),
        name="proj",
    )(x2, g, w_in, b_gate)


def _attn_sub_block(q_sub, kw, vw, bias_ref, g, lo, hi):
    lane_q = lax.broadcasted_iota(jnp.int32, (QB, KB), 1)
    lane_v = lax.broadcasted_iota(jnp.int32, (KB, GROUP_WIDTH), 1)
    edge_ok = (lane_q >= lo) & (lane_q < hi)
    k_t = kw.T
    zero = jnp.zeros((), BF16)
    probs, v_parts = [], []
    m_b = l_b = None
    for h in range(HEADS_PER_GROUP):
        head_q = (lane_q >= h * HEAD_DIM) & (lane_q < (h + 1) * HEAD_DIM)
        head_v = (lane_v >= h * HEAD_DIM) & (lane_v < (h + 1) * HEAD_DIM)
        s = jnp.dot(jnp.where(head_q, q_sub, zero), k_t, preferred_element_type=F32)
        s = jnp.where(edge_ok, s + bias_ref[g * HEADS_PER_GROUP + h], MASK_VALUE)
        m = jnp.max(s, axis=1, keepdims=True)
        p = jnp.exp(s - m)
        l = jnp.sum(p, axis=1, keepdims=True)
        probs.append(p.astype(BF16))
        v_parts.append(jnp.where(head_v, vw, zero))
        m_b = jnp.broadcast_to(m, (QB, GROUP_WIDTH)) if m_b is None else jnp.where(head_q, m, m_b)
        l_b = jnp.broadcast_to(l, (QB, GROUP_WIDTH)) if l_b is None else jnp.where(head_q, l, l_b)
    acc = jnp.dot(jnp.concatenate(probs, axis=1), jnp.concatenate(v_parts, axis=0),
                  preferred_element_type=F32)
    return acc, m_b, l_b


def _attn_kernel(q0_ref, kv0_ref, kv0p_ref, kv0n_ref,
                 q1_ref, kv1_ref, kv1p_ref, kv1n_ref,
                 q2_ref, kv2_ref, kv2p_ref, kv2n_ref,
                 y_ref,
                 cat0, cat1, cat2, bias_ref, m_st, l_st, a_st, m_tmp, l_tmp, a_tmp, *, seq):
    j = pl.program_id(1)

    qi = lax.broadcasted_iota(jnp.int32, (QB, KB), 0)
    kc = lax.broadcasted_iota(jnp.int32, (QB, KB), 1)
    adelta = jnp.abs(kc - HALF - qi)
    band = adelta <= HALF
    slopes = _alibi_slopes()
    for g in range(N_GROUPS_A):
        dist = (adelta * DILATED_PATTERNS[g][1]).astype(F32)
        for h in range(HEADS_PER_GROUP):
            bias_ref[g * HEADS_PER_GROUP + h] = jnp.where(band, -(float(slopes[g, h]) * dist), MASK_VALUE)

    for cat, own, prv, nxt in ((cat0, kv0_ref, kv0p_ref, kv0n_ref),
                               (cat1, kv1_ref, kv1p_ref, kv1n_ref),
                               (cat2, kv2_ref, kv2p_ref, kv2n_ref)):
        n_own = own.shape[-2]
        cat[:, 0:HALF, :] = prv[...].reshape(cat.shape[0], HALF, 2 * GROUP_WIDTH)
        cat[:, HALF:HALF + n_own, :] = own[...].reshape(cat.shape[0], n_own, 2 * GROUP_WIDTH)
        cat[:, HALF + n_own:, :] = nxt[...].reshape(cat.shape[0], HALF, 2 * GROUP_WIDTH)

    def window(cat, r, sb):
        rows = pl.ds(pl.multiple_of(sb * QB, QB), KB)
        return cat[r, rows, 0:GROUP_WIDTH], cat[r, rows, GROUP_WIDTH:]

    def edges(g, n_res, sb):
        length = seq // DILATED_PATTERNS[g][1]
        i0 = j * n_res + sb * QB
        return jnp.maximum(0, HALF - i0), jnp.minimum(KB, length + HALF - i0)

    def body0(sb, carry):
        rows = pl.ds(pl.multiple_of(sb * QB, QB), QB)
        kw, vw = window(cat0, 0, sb)
        lo, hi = edges(0, T_ATT, sb)
        acc, m_b, l_b = _attn_sub_block(q0_ref[rows, :], kw, vw, bias_ref, 0, lo, hi)
        for c in range(2):
            cols = slice(c * LANES, (c + 1) * LANES)
            m_st[c, rows, :] = m_b[:, cols]
            l_st[c, rows, :] = l_b[:, cols]
            a_st[c, rows, :] = acc[:, cols]
        return carry

    lax.fori_loop(0, T_ATT // QB, body0, 0)

    for g, q_ref, cat in ((1, q1_ref, cat1), (2, q2_ref, cat2)):
        d = DILATED_PATTERNS[g][1]
        n_res = T_ATT // d
        sb_per_res = n_res // QB

        def body(idx, carry, g=g, q_ref=q_ref, cat=cat, n_res=n_res, sb_per_res=sb_per_res):
            r = idx // sb_per_res
            sb = idx % sb_per_res
            kw, vw = window(cat, r, sb)
            lo, hi = edges(g, n_res, sb)
            q_sub = q_ref[r, pl.ds(pl.multiple_of(sb * QB, QB), QB), :]
            acc, m_b, l_b = _attn_sub_block(q_sub, kw, vw, bias_ref, g, lo, hi)
            rows = pl.ds(pl.multiple_of(idx * QB, QB), QB)
            m_tmp[rows, :] = m_b
            l_tmp[rows, :] = l_b
            a_tmp[rows, :] = acc
            return carry

        lax.fori_loop(0, T_ATT // QB, body, 0)

        for r in range(d):
            for ch in range(sb_per_res):
                src = slice(r * n_res + ch * QB, r * n_res + (ch + 1) * QB)
                tok = pl.ds(ch * QB * d + r, QB, stride=d)
                for c in range(2):
                    cols = slice(c * LANES, (c + 1) * LANES)
                    m_new_part = m_tmp[src, cols]
                    m_old = m_st[c, tok, :]
                    m_new = jnp.maximum(m_old, m_new_part)
                    e_old = jnp.exp(m_old - m_new)
                    e_new = jnp.exp(m_new_part - m_new)
                    m_st[c, tok, :] = m_new
                    l_st[c, tok, :] = e_old * l_st[c, tok, :] + e_new * l_tmp[src, cols]
                    a_st[c, tok, :] = e_old * a_st[c, tok, :] + e_new * a_tmp[src, cols]

    for c in range(2):
        y_ref[:, c * LANES:(c + 1) * LANES] = (a_st[c] / l_st[c]).astype(BF16)


def _attn_call(q0, kv0, q1, kv1, q2, kv2, batch, seq):
    n = q0.shape[0]
    tiles = seq // T_ATT
    specs = []
    scratch = []
    blocks_per_tile = T_ATT // HALF
    n_half_blocks = n // HALF
    specs += [
        pl.BlockSpec((T_ATT, GROUP_WIDTH), lambda b, j: (b * tiles + j, 0)),
        pl.BlockSpec((T_ATT, 2 * GROUP_WIDTH), lambda b, j: (b * tiles + j, 0)),
        pl.BlockSpec((HALF, 2 * GROUP_WIDTH),
                     lambda b, j: (jnp.maximum((b * tiles + j) * blocks_per_tile - 1, 0), 0)),
        pl.BlockSpec((HALF, 2 * GROUP_WIDTH),
                     lambda b, j: (jnp.minimum((b * tiles + j + 1) * blocks_per_tile, n_half_blocks - 1), 0)),
    ]
    scratch.append(pltpu.VMEM((1, T_ATT + 2 * HALF, 2 * GROUP_WIDTH), BF16))
    for g in (1, 2):
        d = DILATED_PATTERNS[g][1]
        n_res = T_ATT // d
        per_tile = n_res // HALF
        last = seq // d // HALF - 1
        specs += [
            pl.BlockSpec((None, d, n_res, GROUP_WIDTH), lambda b, j: (b, 0, j, 0)),
            pl.BlockSpec((None, d, n_res, 2 * GROUP_WIDTH), lambda b, j: (b, 0, j, 0)),
            pl.BlockSpec((None, d, HALF, 2 * GROUP_WIDTH),
                         lambda b, j, per_tile=per_tile: (b, 0, jnp.maximum(j * per_tile - 1, 0), 0)),
            pl.BlockSpec((None, d, HALF, 2 * GROUP_WIDTH),
                         lambda b, j, per_tile=per_tile, last=last: (b, 0, jnp.minimum((j + 1) * per_tile, last), 0)),
        ]
        scratch.append(pltpu.VMEM((d, n_res + 2 * HALF, 2 * GROUP_WIDTH), BF16))
    scratch.append(pltpu.VMEM((N_HEADS_A, QB, KB), F32))
    scratch += [pltpu.VMEM((2, T_ATT, LANES), F32) for _ in range(3)]
    scratch += [pltpu.VMEM((T_ATT, GROUP_WIDTH), F32) for _ in range(3)]
    return pl.pallas_call(
        functools.partial(_attn_kernel, seq=seq),
        grid=(batch, tiles),
        in_specs=specs,
        out_specs=pl.BlockSpec((T_ATT, GROUP_WIDTH), lambda b, j: (b * tiles + j, 0)),
        out_shape=jax.ShapeDtypeStruct((n, GROUP_WIDTH), BF16),
        scratch_shapes=scratch,
        compiler_params=pltpu.CompilerParams(dimension_semantics=("arbitrary", "arbitrary"),
                                             vmem_limit_bytes=VMEM_LIMIT),
        name="attn",
    )(q0, kv0, kv0, kv0, q1, kv1, kv1, kv1, q2, kv2, kv2, kv2)


def _split_dot(a, w_hi, w_lo):
    a_hi = a.astype(BF16)
    a_lo = (a - a_hi.astype(F32)).astype(BF16)
    return (jnp.dot(a_hi, w_hi, preferred_element_type=F32)
            + jnp.dot(a_lo, w_hi, preferred_element_type=F32)
            + jnp.dot(a_hi, w_lo, preferred_element_type=F32))


def _mix_kernel(x_ref, ya_ref, bg_ref, u_ref, up_ref, un_ref, gate_ref,
                wa_ref, wc_ref, wo_ref, cw_ref, g2_ref, wr_hi_ref, wr_lo_ref, br_ref,
                x1_ref, hflat_ref, route_ref, routet_ref, cnt_ref, base_ref, *, seq):
    i = pl.program_id(0)
    t0 = i * TM_MIX

    @pl.when(i == 0)
    def _():
        base_ref[...] = jnp.zeros_like(base_ref)

    u = u_ref[...].astype(F32)
    row = lax.broadcasted_iota(jnp.int32, (TM_MIX, CONV_WIDTH), 0)
    prev_row = jnp.where(t0 % seq == 0, 0.0, up_ref[15:16, :].astype(F32))
    next_row = jnp.where((t0 + TM_MIX) % seq == 0, 0.0, un_ref[0:1, :].astype(F32))
    u_prev = jnp.where(row == 0, prev_row, pltpu.roll(u, 1, axis=0))
    u_next = jnp.where(row == TM_MIX - 1, next_row, pltpu.roll(u, TM_MIX - 1, axis=0))
    conv = cw_ref[0:1, :] * u_prev + cw_ref[1:2, :] * u + cw_ref[2:3, :] * u_next
    yb_in = (bg_ref[...].astype(F32) * conv).astype(BF16)

    y_a = jnp.dot(ya_ref[...], wa_ref[...], preferred_element_type=F32)
    y_b = jnp.dot(yb_in, wc_ref[...], preferred_element_type=F32)
    merged = (gate_ref[:, 0:D_MODEL].astype(F32) * y_a
              + gate_ref[:, D_MODEL:].astype(F32) * y_b).astype(BF16)
    x1 = x_ref[...] + jnp.dot(merged, wo_ref[...], preferred_element_type=F32)
    x1_ref[...] = x1

    h2 = _rms(x1, g2_ref[...])
    for c in range(ROW_CHUNKS):
        hflat_ref[pl.ds(c, TM_MIX, stride=ROW_CHUNKS), :] = h2[:, c * LANES:(c + 1) * LANES]

    logits = _split_dot(h2, wr_hi_ref[...], wr_lo_ref[...]) + br_ref[...]
    lane = lax.broadcasted_iota(jnp.int32, (TM_MIX, LANES), 1)
    lane_f = lane.astype(F32)
    neg = -jnp.inf
    big = float(LANES)
    is_group = lane < N_EXPERT_GROUPS
    cm = jnp.where(is_group, logits, neg)
    cmax = jnp.max(cm, axis=1, keepdims=True)
    g_idx = jnp.min(jnp.where(cm == cmax, lane_f, big), axis=1, keepdims=True)
    p_group = 1.0 / jnp.sum(jnp.where(is_group, jnp.exp(logits - cmax), 0.0), axis=1, keepdims=True)
    f_lo = N_EXPERT_GROUPS + EXPERTS_PER_GROUP * g_idx
    in_group = (lane_f >= f_lo) & (lane_f < f_lo + EXPERTS_PER_GROUP)
    fm = jnp.where(in_group, logits, neg)
    f1 = jnp.max(fm, axis=1, keepdims=True)
    i1 = jnp.min(jnp.where(fm == f1, lane_f, big), axis=1, keepdims=True)
    fm2 = jnp.where(lane_f == i1, neg, fm)
    f2 = jnp.max(fm2, axis=1, keepdims=True)
    i2 = jnp.min(jnp.where(fm2 == f2, lane_f, big), axis=1, keepdims=True)
    e21 = jnp.exp(f2 - f1)
    w_1 = p_group / (1.0 + e21)
    w_2 = p_group * e21 / (1.0 + e21)
    e_1 = i1 - N_EXPERT_GROUPS
    e_2 = i2 - N_EXPERT_GROUPS

    onehot = jnp.where((lane_f == e_1) | (lane_f == e_2), 1.0, 0.0)
    r_i = lax.broadcasted_iota(jnp.int32, (TM_MIX, TM_MIX), 0)
    c_i = lax.broadcasted_iota(jnp.int32, (TM_MIX, TM_MIX), 1)
    tri = jnp.where(c_i < r_i, 1.0, 0.0).astype(BF16)
    before = jnp.dot(tri, onehot.astype(BF16), preferred_element_type=F32) + base_ref[0:1, :]
    rank_1 = jnp.sum(jnp.where(lane_f == e_1, before, 0.0), axis=1, keepdims=True)
    rank_2 = jnp.sum(jnp.where(lane_f == e_2, before, 0.0), axis=1, keepdims=True)
    base_ref[0:1, :] = base_ref[0:1, :] + jnp.sum(onehot, axis=0, keepdims=True)
    cnt_ref[...] = jnp.broadcast_to(base_ref[0:1, :], cnt_ref.shape)

    route = jnp.zeros((TM_MIX, LANES), F32)
    for k, val in enumerate((e_1, e_2, w_1, w_2, rank_1, rank_2)):
        route = jnp.where(lane == k, val, route)
    route_ref[...] = route
    routet_ref[...] = route.T[0:SUBLANES, :]


def _mix_call(x2, y_attn, bgate, u, gates, wa, wc, wo, conv_w, g2, wr_hi, wr_lo, br, seq):
    n = x2.shape[0]
    row = lambda i: (i, 0)
    const = lambda i: (0, 0)
    halo = 16
    per_tile = TM_MIX // halo
    last = n // halo - 1
    return pl.pallas_call(
        functools.partial(_mix_kernel, seq=seq),
        grid=(n // TM_MIX,),
        in_specs=[
            pl.BlockSpec((TM_MIX, D_MODEL), row),
            pl.BlockSpec((TM_MIX, GROUP_WIDTH), row),
            pl.BlockSpec((TM_MIX, CONV_WIDTH), row),
            pl.BlockSpec((TM_MIX, CONV_WIDTH), row),
            pl.BlockSpec((halo, CONV_WIDTH), lambda i: (jnp.maximum(i * per_tile - 1, 0), 0)),
            pl.BlockSpec((halo, CONV_WIDTH), lambda i: (jnp.minimum((i + 1) * per_tile, last), 0)),
            pl.BlockSpec((TM_MIX, 2 * D_MODEL), row),
            pl.BlockSpec((GROUP_WIDTH, D_MODEL), const),
            pl.BlockSpec((CONV_WIDTH, D_MODEL), const),
            pl.BlockSpec((D_MODEL, D_MODEL), const),
            pl.BlockSpec((3, CONV_WIDTH), const),
            pl.BlockSpec((1, D_MODEL), const),
            pl.BlockSpec((D_MODEL, LANES), const),
            pl.BlockSpec((D_MODEL, LANES), const),
            pl.BlockSpec((1, LANES), const),
        ],
        out_specs=[
            pl.BlockSpec((TM_MIX, D_MODEL), row),
            pl.BlockSpec((TM_MIX * ROW_CHUNKS, LANES), row),
            pl.BlockSpec((TM_MIX, LANES), row),
            pl.BlockSpec((SUBLANES, TM_MIX), lambda i: (0, i)),
            pl.BlockSpec((SUBLANES, LANES), const),
        ],
        out_shape=[
            jax.ShapeDtypeStruct((n, D_MODEL), F32),
            jax.ShapeDtypeStruct((n * ROW_CHUNKS, LANES), F32),
            jax.ShapeDtypeStruct((n, LANES), F32),
            jax.ShapeDtypeStruct((SUBLANES, n), F32),
            jax.ShapeDtypeStruct((SUBLANES, LANES), F32),
        ],
        scratch_shapes=[pltpu.VMEM((SUBLANES, LANES), F32)],
        compiler_params=pltpu.CompilerParams(dimension_semantics=("arbitrary",),
                                             vmem_limit_bytes=VMEM_LIMIT),
        name="mix",
    )(x2, y_attn, bgate, u, u, u, gates, wa, wc, wo, conv_w, g2, wr_hi, wr_lo, br)


def _row_gather(idx_ref, n_rows, src_hbm, dst, sem):
    def issue(j, carry):
        t = idx_ref[0, 0, j]
        pltpu.make_async_copy(src_hbm.at[pl.ds(pl.multiple_of(t * SUBLANES, SUBLANES), SUBLANES)],
                              dst.at[pl.ds(pl.multiple_of(j * SUBLANES, SUBLANES), SUBLANES)],
                              sem).start()
        return carry
    lax.fori_loop(0, n_rows, issue, 0, unroll=8)


def _row_gather_wait(n_rows, src_hbm, dst, sem):
    pltpu.make_async_copy(src_hbm.at[pl.ds(0, n_rows * SUBLANES)], dst, sem).wait()


def _rows_from_tiles(buf, first_row, n_rows):
    return jnp.concatenate(
        [buf[pl.ds(first_row * ROW_CHUNKS + c, n_rows, stride=ROW_CHUNKS), :] for c in range(ROW_CHUNKS)],
        axis=1)


def _dispatch_kernel(tend_ref, pos_ref, h_ref, xs_hbm, zbuf, sem, zsem, *, n_tiles):
    @pl.when(pl.program_id(0) == 0)
    def _():
        zbuf[...] = jnp.zeros_like(zbuf)
        n_used = tend_ref[N_EXPERTS - 1]

        def fill(t):
            dst = xs_hbm.at[pl.ds(pl.multiple_of(t * (TM_MOE * SUBLANES), SUBLANES), TM_MOE * SUBLANES)]
            return pltpu.make_async_copy(zbuf, dst, zsem)

        def jobs():
            for e in range(N_EXPERTS):
                first = tend_ref[e - 1] if e else 0
                yield tend_ref[e] > first, tend_ref[e] - 1
            for t in range(N_EXPERTS):
                yield n_used + t < n_tiles, n_used + t

        for cond, t in jobs():
            pl.when(cond)(lambda t=t: fill(t).start())
        for cond, t in jobs():
            pl.when(cond)(lambda t=t: fill(t).wait())

    def issue(j, carry):
        src = h_ref.at[pl.ds(pl.multiple_of(j * SUBLANES, SUBLANES), SUBLANES)]
        for k in range(2):
            p = pos_ref[0, 0, k * TM_DSP + j]
            pltpu.make_async_copy(src, xs_hbm.at[pl.ds(pl.multiple_of(p * SUBLANES, SUBLANES), SUBLANES)],
                                  sem.at[k]).start()
        return carry
    lax.fori_loop(0, TM_DSP, issue, 0, unroll=8)
    for k in range(2):
        pltpu.make_async_copy(h_ref, xs_hbm.at[pl.ds(0, TM_DSP * SUBLANES)], sem.at[k]).wait()


def _dispatch_call(tile_end, pos_tiles, h_flat, n_tiles):
    n_steps = pos_tiles.shape[0]
    grid_spec = pltpu.PrefetchScalarGridSpec(
        num_scalar_prefetch=1,
        grid=(n_steps,),
        in_specs=[
            pl.BlockSpec((1, 1, 2 * TM_DSP), lambda i, te: (i, 0, 0), memory_space=pltpu.SMEM),
            pl.BlockSpec((TM_DSP * ROW_CHUNKS, LANES), lambda i, te: (i, 0)),
        ],
        out_specs=pl.BlockSpec(memory_space=pl.ANY),
        scratch_shapes=[pltpu.VMEM((TM_MOE * ROW_CHUNKS, LANES), F32),
                        pltpu.SemaphoreType.DMA((2,)),
                        pltpu.SemaphoreType.DMA(())],
    )
    return pl.pallas_call(
        functools.partial(_dispatch_kernel, n_tiles=n_tiles),
        grid_spec=grid_spec,
        out_shape=jax.ShapeDtypeStruct((n_tiles * TM_MOE * ROW_CHUNKS, LANES), F32),
        compiler_params=pltpu.CompilerParams(dimension_semantics=("arbitrary",),
                                             vmem_limit_bytes=VMEM_LIMIT),
        name="dispatch",
    )(tile_end, pos_tiles, h_flat)


def _expert_kernel(te_ref, nused_ref, xs_ref, w1_ref, w3_ref, w2_ref, y_ref, w1b, w3b, w2b):
    i = pl.program_id(0)

    @pl.when(i >= nused_ref[0])
    def _():
        y_ref[...] = jnp.zeros_like(y_ref)

    @pl.when(i < nused_ref[0])
    def _():
        @pl.when((i == 0) | (te_ref[i] != te_ref[jnp.maximum(i - 1, 0)]))
        def _():
            w1b[...] = w1_ref[...].astype(BF16)
            w3b[...] = w3_ref[...].astype(BF16)
            w2b[...] = w2_ref[...].astype(BF16)

        x = _rows_from_tiles(xs_ref, 0, TM_MOE).astype(BF16)
        a = jnp.dot(x, w1b[...], preferred_element_type=F32)
        b = jnp.dot(x, w3b[...], preferred_element_type=F32)
        hid = (a * jax.nn.sigmoid(a) * b).astype(BF16)
        y = jnp.dot(hid, w2b[...], preferred_element_type=F32)
        for c in range(ROW_CHUNKS):
            y_ref[pl.ds(c, TM_MOE, stride=ROW_CHUNKS), :] = y[:, c * LANES:(c + 1) * LANES]


def _expert_call(tile_expert, n_used, xs_flat, w1, w3, w2):
    n_tiles = tile_expert.shape[0]
    tile = lambda i, nu: jnp.minimum(i, nu[0] - 1)
    wspec = lambda shape: pl.BlockSpec((None,) + shape, lambda i, te, nu: (te[tile(i, nu)], 0, 0))
    grid_spec = pltpu.PrefetchScalarGridSpec(
        num_scalar_prefetch=2,
        grid=(n_tiles,),
        in_specs=[
            pl.BlockSpec((TM_MOE * ROW_CHUNKS, LANES), lambda i, te, nu: (tile(i, nu), 0)),
            wspec((D_MODEL, EXPERT_FF)),
            wspec((D_MODEL, EXPERT_FF)),
            wspec((EXPERT_FF, D_MODEL)),
        ],
        out_specs=pl.BlockSpec((TM_MOE * ROW_CHUNKS, LANES), lambda i, te, nu: (i, 0)),
        scratch_shapes=[pltpu.VMEM((D_MODEL, EXPERT_FF), BF16),
                        pltpu.VMEM((D_MODEL, EXPERT_FF), BF16),
                        pltpu.VMEM((EXPERT_FF, D_MODEL), BF16)],
    )
    return pl.pallas_call(
        _expert_kernel,
        grid_spec=grid_spec,
        out_shape=jax.ShapeDtypeStruct((n_tiles * TM_MOE * ROW_CHUNKS, LANES), F32),
        compiler_params=pltpu.CompilerParams(dimension_semantics=("arbitrary",),
                                             vmem_limit_bytes=VMEM_LIMIT),
        name="experts",
    )(tile_expert, n_used, xs_flat, w1, w3, w2)


def _combine_kernel(pos_ref, posn_ref, y_hbm, x1_ref, route_ref, g_ref, o_ref, buf0, buf1, sem):
    i = pl.program_id(0)
    n_steps = pl.num_programs(0)
    bufs = (buf0, buf1)

    @pl.when(i == 0)
    def _():
        _row_gather(pos_ref, 2 * TM_CMB, y_hbm, buf0, sem.at[0])

    for slot in range(2):
        @pl.when((i % 2 == slot) & (i + 1 < n_steps))
        def _(slot=slot):
            _row_gather(posn_ref, 2 * TM_CMB, y_hbm, bufs[1 - slot], sem.at[1 - slot])

    for slot in range(2):
        @pl.when(i % 2 == slot)
        def _(slot=slot):
            _row_gather_wait(2 * TM_CMB, y_hbm, bufs[slot], sem.at[slot])
            y_1 = _rows_from_tiles(bufs[slot], 0, TM_CMB)
            y_2 = _rows_from_tiles(bufs[slot], TM_CMB, TM_CMB)
            x = x1_ref[...] + route_ref[:, 2:3] * y_1 + route_ref[:, 3:4] * y_2
            o_ref[...] = _rms(x, g_ref[...])


def _combine_call(pos, y_flat, x1, route, g):
    n = x1.shape[0]
    n_steps = n // TM_CMB
    row = lambda i: (i, 0)
    return pl.pallas_call(
        _combine_kernel,
        grid=(n_steps,),
        in_specs=[
            pl.BlockSpec((1, 1, 2 * TM_CMB), lambda i: (i, 0, 0), memory_space=pltpu.SMEM),
            pl.BlockSpec((1, 1, 2 * TM_CMB), lambda i: (jnp.minimum(i + 1, n_steps - 1), 0, 0),
                         memory_space=pltpu.SMEM),
            pl.BlockSpec(memory_space=pl.ANY),
            pl.BlockSpec((TM_CMB, D_MODEL), row),
            pl.BlockSpec((TM_CMB, LANES), row),
            pl.BlockSpec((1, D_MODEL), lambda i: (0, 0)),
        ],
        out_specs=pl.BlockSpec((TM_CMB, D_MODEL), row),
        out_shape=jax.ShapeDtypeStruct((n, D_MODEL), F32),
        scratch_shapes=[pltpu.VMEM((2 * TM_CMB * ROW_CHUNKS, LANES), F32),
                        pltpu.VMEM((2 * TM_CMB * ROW_CHUNKS, LANES), F32),
                        pltpu.SemaphoreType.DMA((2,))],
        compiler_params=pltpu.CompilerParams(dimension_semantics=("arbitrary",),
                                             vmem_limit_bytes=VMEM_LIMIT),
        name="combine",
    )(pos, pos, y_flat, x1, route, g)


def _layer(x2, batch, seq, norm_mix_g, w_in, b_gate, conv_w, w_attn_out, w_conv_out, w_out, norm_ffn_g,
           w_route_group, b_route_group, w_route_expert, b_route_expert, w1, w3, w2, final_g):
    n = x2.shape[0]
    q0, kv0, q1, kv1, q2, kv2, bgate, u, gates = _proj_call(
        x2, norm_mix_g[None, :], w_in.astype(BF16), b_gate[None, :], batch, seq)
    y_attn = _attn_call(q0, kv0, q1, kv1, q2, kv2, batch, seq)

    n_route = N_EXPERT_GROUPS + N_EXPERTS
    w_route = jnp.pad(jnp.concatenate([w_route_group, w_route_expert], axis=1), ((0, 0), (0, LANES - n_route)))
    b_route = jnp.pad(jnp.concatenate([b_route_group, b_route_expert]), (0, LANES - n_route))[None, :]
    wr_hi = w_route.astype(BF16)
    wr_lo = (w_route - wr_hi.astype(F32)).astype(BF16)
    x1, h_flat, route, route_t, counts = _mix_call(
        x2, y_attn, bgate, u, gates, w_attn_out.astype(BF16), w_conv_out.astype(BF16), w_out.astype(BF16),
        conv_w, norm_ffn_g[None, :], wr_hi, wr_lo, b_route, seq)

    ids = jnp.arange(N_EXPERTS, dtype=jnp.int32)
    expert = route_t[0:2].astype(jnp.int32)
    rank = route_t[4:6].astype(jnp.int32)
    cnt = counts[0, :N_EXPERTS].astype(jnp.int32)
    tiles_e = (cnt + TM_MOE - 1) // TM_MOE
    tile_end = jnp.sum(jnp.where(ids[:, None] >= ids[None, :], tiles_e[None, :], 0), axis=1)
    tile_start = tile_end - tiles_e
    pos = jnp.sum(jnp.where(expert[..., None] == ids, tile_start * TM_MOE, 0), axis=-1) + rank
    n_tiles = (2 * n) // TM_MOE + N_EXPERTS
    tidx = jnp.arange(n_tiles, dtype=jnp.int32)
    tile_expert = jnp.minimum(jnp.sum((tile_end[None, :] <= tidx[:, None]).astype(jnp.int32), axis=1),
                              N_EXPERTS - 1)
    n_used = tile_end[-1:]

    def step_tiles(t):
        return pos.reshape(2, n // t, t).transpose(1, 0, 2).reshape(n // t, 1, 2 * t)

    xs_flat = _dispatch_call(tile_end, step_tiles(TM_DSP), h_flat, n_tiles)
    y_flat = _expert_call(tile_expert, n_used, xs_flat, w1, w3, w2)
    return _combine_call(step_tiles(TM_CMB), y_flat, x1, route, final_g[None, :])


def kernel(x, norm_mix_g, w_in, b_gate, conv_w, w_attn_out, w_conv_out, w_out, norm_ffn_g,
           w_route_group, b_route_group, w_route_expert, b_route_expert, w1, w3, w2, norm_final_g):
    batch, seq, d = x.shape
    depth = w_in.shape[0]
    assert d == D_MODEL and depth == 1 and seq % T_ATT == 0
    out = _layer(x.reshape(batch * seq, d), batch, seq, norm_mix_g[0], w_in[0], b_gate[0], conv_w[0],
                 w_attn_out[0], w_conv_out[0], w_out[0], norm_ffn_g[0], w_route_group[0], b_route_group[0],
                 w_route_expert[0], b_route_expert[0], w1[0], w3[0], w2[0], norm_final_g)
    return out.reshape(batch, seq, d)
```

```python
import functools

import numpy as np
import jax
import jax.numpy as jnp
from jax import lax
from jax.experimental import pallas as pl
from jax.experimental.pallas import tpu as pltpu

F32 = jnp.float32
BF16 = jnp.bfloat16

D_MODEL = 1024
HEAD_DIM = 64
HEADS_PER_GROUP = 4
DILATED_PATTERNS = ((128, 1), (512, 4), (2048, 16))
N_GROUPS_A = 3
N_HEADS_A = N_GROUPS_A * HEADS_PER_GROUP
ATTN_WIDTH = N_HEADS_A * HEAD_DIM
GROUP_WIDTH = HEADS_PER_GROUP * HEAD_DIM
ALIBI_SPAN = 8.0
MASK_VALUE = -1e30
CONV_WIDTH = 768
N_EXPERT_GROUPS = 4
EXPERTS_PER_GROUP = 8
N_EXPERTS = 32
EXPERT_FF = 512
RMS_EPS = 1e-6

HALF = 64
LANES = 128
SUBLANES = 8
ROW_CHUNKS = D_MODEL // LANES

COL_K = ATTN_WIDTH
COL_V = 2 * ATTN_WIDTH
COL_BG = 3 * ATTN_WIDTH
COL_CG = COL_BG + CONV_WIDTH
COL_XIN = COL_CG + CONV_WIDTH
COL_GATE = COL_XIN + CONV_WIDTH
IN_COLS = COL_GATE + 2 * D_MODEL

TM_PROJ = 512
T_ATT = 2048
QB = 128
KB = QB + 2 * HALF
TM_MIX = 512
TM_INV = 2048
TM_MOE = 256
TM_CMB = 256

VMEM_LIMIT = 56 * 1024 * 1024


def _alibi_slopes():
    return np.array([2.0 ** (-ALIBI_SPAN * (i + 1) / N_HEADS_A) for i in range(N_HEADS_A)],
                    dtype=np.float32).reshape(N_GROUPS_A, HEADS_PER_GROUP)


def _rms(x, g):
    return x * lax.rsqrt(jnp.mean(x * x, axis=-1, keepdims=True) + RMS_EPS) * g


def _proj_kernel(x_ref, g_ref, w_ref, b_ref,
                 q0_ref, kv0_ref, q1_ref, kv1_ref, q2_ref, kv2_ref, bg_ref, u_ref, gate_ref, scr):
    h = _rms(x_ref[...], g_ref[...]).astype(BF16)

    def proj(c0, width):
        return jnp.dot(h, w_ref[:, c0:c0 + width], preferred_element_type=F32)

    qscale = HEAD_DIM ** -0.5
    q0_ref[...] = (proj(0, GROUP_WIDTH) * qscale).astype(BF16)
    kv0_ref[:, 0:GROUP_WIDTH] = proj(COL_K, GROUP_WIDTH).astype(BF16)
    kv0_ref[:, GROUP_WIDTH:] = proj(COL_V, GROUP_WIDTH).astype(BF16)

    for g, q_ref, kv_ref in ((1, q1_ref, kv1_ref), (2, q2_ref, kv2_ref)):
        d = DILATED_PATTERNS[g][1]
        n = TM_PROJ // d
        parts = (proj(g * GROUP_WIDTH, GROUP_WIDTH) * qscale,
                 proj(COL_K + g * GROUP_WIDTH, GROUP_WIDTH),
                 proj(COL_V + g * GROUP_WIDTH, GROUP_WIDTH))
        for i, part in enumerate(parts):
            for c in range(2):
                scr[2 * i + c] = part[:, c * LANES:(c + 1) * LANES]
        for r in range(d):
            rows = pl.ds(r, n, stride=d)
            q_ref[r] = jnp.concatenate([scr[c, rows, :] for c in range(2)], axis=1).astype(BF16)
            kv_ref[r] = jnp.concatenate([scr[c, rows, :] for c in range(2, 6)], axis=1).astype(BF16)

    bg_ref[...] = proj(COL_BG, CONV_WIDTH).astype(BF16)
    u_ref[...] = (proj(COL_CG, CONV_WIDTH) * proj(COL_XIN, CONV_WIDTH)).astype(BF16)
    for c in range(4):
        w = 2 * D_MODEL // 4
        z = proj(COL_GATE + c * w, w) + b_ref[:, c * w:(c + 1) * w]
        gate_ref[:, c * w:(c + 1) * w] = jax.nn.sigmoid(z).astype(BF16)


def _proj_call(x2, g, w_in, b_gate, batch, seq):
    n = x2.shape[0]
    steps_per_batch = seq // TM_PROJ
    d1, d2 = DILATED_PATTERNS[1][1], DILATED_PATTERNS[2][1]
    row = lambda i: (i, 0)
    res = lambda i: (i // steps_per_batch, 0, i % steps_per_batch, 0)
    const = lambda i: (0, 0)
    out_shape = [
        jax.ShapeDtypeStruct((n, GROUP_WIDTH), BF16),
        jax.ShapeDtypeStruct((n, 2 * GROUP_WIDTH), BF16),
        jax.ShapeDtypeStruct((batch, d1, seq // d1, GROUP_WIDTH), BF16),
        jax.ShapeDtypeStruct((batch, d1, seq // d1, 2 * GROUP_WIDTH), BF16),
        jax.ShapeDtypeStruct((batch, d2, seq // d2, GROUP_WIDTH), BF16),
        jax.ShapeDtypeStruct((batch, d2, seq // d2, 2 * GROUP_WIDTH), BF16),
        jax.ShapeDtypeStruct((n, CONV_WIDTH), BF16),
        jax.ShapeDtypeStruct((n, CONV_WIDTH), BF16),
        jax.ShapeDtypeStruct((n, 2 * D_MODEL), BF16),
    ]
    out_specs = [
        pl.BlockSpec((TM_PROJ, GROUP_WIDTH), row),
        pl.BlockSpec((TM_PROJ, 2 * GROUP_WIDTH), row),
        pl.BlockSpec((None, d1, TM_PROJ // d1, GROUP_WIDTH), res),
        pl.BlockSpec((None, d1, TM_PROJ // d1, 2 * GROUP_WIDTH), res),
        pl.BlockSpec((None, d2, TM_PROJ // d2, GROUP_WIDTH), res),
        pl.BlockSpec((None, d2, TM_PROJ // d2, 2 * GROUP_WIDTH), res),
        pl.BlockSpec((TM_PROJ, CONV_WIDTH), row),
        pl.BlockSpec((TM_PROJ, CONV_WIDTH), row),
        pl.BlockSpec((TM_PROJ, 2 * D_MODEL), row),
    ]
    return pl.pallas_call(
        _proj_kernel,
        grid=(n // TM_PROJ,),
        in_specs=[
            pl.BlockSpec((TM_PROJ, D_MODEL), row),
            pl.BlockSpec((1, D_MODEL), const),
            pl.BlockSpec((D_MODEL, IN_COLS), const),
            pl.BlockSpec((1, 2 * D_MODEL), const),
        ],
        out_specs=out_specs,
        out_shape=out_shape,
        scratch_shapes=[pltpu.VMEM((6, TM_PROJ, LANES), F32)],
        compiler_params=pltpu.CompilerParams(dimension_semantics=("arbitrary",),
                                             vmem_limit_bytes=VMEM_LIMIT),
        name="proj",
    )(x2, g, w_in, b_gate)


def _attn_sub_block(q_sub, kw, vw, bias_ref, g, lo, hi):
    lane_q = lax.broadcasted_iota(jnp.int32, (QB, KB), 1)
    lane_v = lax.broadcasted_iota(jnp.int32, (KB, GROUP_WIDTH), 1)
    edge_ok = (lane_q >= lo) & (lane_q < hi)
    k_t = kw.T
    zero = jnp.zeros((), BF16)
    probs, v_parts = [], []
    m_b = l_b = None
    for h in range(HEADS_PER_GROUP):
        head_q = (lane_q >= h * HEAD_DIM) & (lane_q < (h + 1) * HEAD_DIM)
        head_v = (lane_v >= h * HEAD_DIM) & (lane_v < (h + 1) * HEAD_DIM)
        s = jnp.dot(jnp.where(head_q, q_sub, zero), k_t, preferred_element_type=F32)
        s = jnp.where(edge_ok, s + bias_ref[g * HEADS_PER_GROUP + h], MASK_VALUE)
        m = jnp.max(s, axis=1, keepdims=True)
        p = jnp.exp(s - m)
        l = jnp.sum(p, axis=1, keepdims=True)
        probs.append(p.astype(BF16))
        v_parts.append(jnp.where(head_v, vw, zero))
        m_b = jnp.broadcast_to(m, (QB, GROUP_WIDTH)) if m_b is None else jnp.where(head_q, m, m_b)
        l_b = jnp.broadcast_to(l, (QB, GROUP_WIDTH)) if l_b is None else jnp.where(head_q, l, l_b)
    acc = jnp.dot(jnp.concatenate(probs, axis=1), jnp.concatenate(v_parts, axis=0),
                  preferred_element_type=F32)
    return acc, m_b, l_b


def _attn_kernel(q0_ref, kv0_ref, kv0p_ref, kv0n_ref,
                 q1_ref, kv1_ref, kv1p_ref, kv1n_ref,
                 q2_ref, kv2_ref, kv2p_ref, kv2n_ref,
                 y_ref,
                 cat0, cat1, cat2, bias_ref, m_st, l_st, a_st, m_tmp, l_tmp, a_tmp, *, seq):
    j = pl.program_id(1)

    qi = lax.broadcasted_iota(jnp.int32, (QB, KB), 0)
    kc = lax.broadcasted_iota(jnp.int32, (QB, KB), 1)
    adelta = jnp.abs(kc - HALF - qi)
    band = adelta <= HALF
    slopes = _alibi_slopes()
    for g in range(N_GROUPS_A):
        dist = (adelta * DILATED_PATTERNS[g][1]).astype(F32)
        for h in range(HEADS_PER_GROUP):
            bias_ref[g * HEADS_PER_GROUP + h] = jnp.where(band, -(float(slopes[g, h]) * dist), MASK_VALUE)

    for cat, own, prv, nxt in ((cat0, kv0_ref, kv0p_ref, kv0n_ref),
                               (cat1, kv1_ref, kv1p_ref, kv1n_ref),
                               (cat2, kv2_ref, kv2p_ref, kv2n_ref)):
        n_own = own.shape[-2]
        cat[:, 0:HALF, :] = prv[...].reshape(cat.shape[0], HALF, 2 * GROUP_WIDTH)
        cat[:, HALF:HALF + n_own, :] = own[...].reshape(cat.shape[0], n_own, 2 * GROUP_WIDTH)
        cat[:, HALF + n_own:, :] = nxt[...].reshape(cat.shape[0], HALF, 2 * GROUP_WIDTH)

    def window(cat, r, sb):
        rows = pl.ds(pl.multiple_of(sb * QB, QB), KB)
        return cat[r, rows, 0:GROUP_WIDTH], cat[r, rows, GROUP_WIDTH:]

    def edges(g, n_res, sb):
        length = seq // DILATED_PATTERNS[g][1]
        i0 = j * n_res + sb * QB
        return jnp.maximum(0, HALF - i0), jnp.minimum(KB, length + HALF - i0)

    def body0(sb, carry):
        rows = pl.ds(pl.multiple_of(sb * QB, QB), QB)
        kw, vw = window(cat0, 0, sb)
        lo, hi = edges(0, T_ATT, sb)
        acc, m_b, l_b = _attn_sub_block(q0_ref[rows, :], kw, vw, bias_ref, 0, lo, hi)
        for c in range(2):
            cols = slice(c * LANES, (c + 1) * LANES)
            m_st[c, rows, :] = m_b[:, cols]
            l_st[c, rows, :] = l_b[:, cols]
            a_st[c, rows, :] = acc[:, cols]
        return carry

    lax.fori_loop(0, T_ATT // QB, body0, 0)

    for g, q_ref, cat in ((1, q1_ref, cat1), (2, q2_ref, cat2)):
        d = DILATED_PATTERNS[g][1]
        n_res = T_ATT // d
        sb_per_res = n_res // QB

        def body(idx, carry, g=g, q_ref=q_ref, cat=cat, n_res=n_res, sb_per_res=sb_per_res):
            r = idx // sb_per_res
            sb = idx % sb_per_res
            kw, vw = window(cat, r, sb)
            lo, hi = edges(g, n_res, sb)
            q_sub = q_ref[r, pl.ds(pl.multiple_of(sb * QB, QB), QB), :]
            acc, m_b, l_b = _attn_sub_block(q_sub, kw, vw, bias_ref, g, lo, hi)
            rows = pl.ds(pl.multiple_of(idx * QB, QB), QB)
            m_tmp[rows, :] = m_b
            l_tmp[rows, :] = l_b
            a_tmp[rows, :] = acc
            return carry

        lax.fori_loop(0, T_ATT // QB, body, 0)

        for r in range(d):
            for ch in range(sb_per_res):
                src = slice(r * n_res + ch * QB, r * n_res + (ch + 1) * QB)
                tok = pl.ds(ch * QB * d + r, QB, stride=d)
                for c in range(2):
                    cols = slice(c * LANES, (c + 1) * LANES)
                    m_new_part = m_tmp[src, cols]
                    m_old = m_st[c, tok, :]
                    m_new = jnp.maximum(m_old, m_new_part)
                    e_old = jnp.exp(m_old - m_new)
                    e_new = jnp.exp(m_new_part - m_new)
                    m_st[c, tok, :] = m_new
                    l_st[c, tok, :] = e_old * l_st[c, tok, :] + e_new * l_tmp[src, cols]
                    a_st[c, tok, :] = e_old * a_st[c, tok, :] + e_new * a_tmp[src, cols]

    for c in range(2):
        y_ref[:, c * LANES:(c + 1) * LANES] = (a_st[c] / l_st[c]).astype(BF16)


def _attn_call(q0, kv0, q1, kv1, q2, kv2, batch, seq):
    n = q0.shape[0]
    tiles = seq // T_ATT
    specs = []
    scratch = []
    blocks_per_tile = T_ATT // HALF
    n_half_blocks = n // HALF
    specs += [
        pl.BlockSpec((T_ATT, GROUP_WIDTH), lambda b, j: (b * tiles + j, 0)),
        pl.BlockSpec((T_ATT, 2 * GROUP_WIDTH), lambda b, j: (b * tiles + j, 0)),
        pl.BlockSpec((HALF, 2 * GROUP_WIDTH),
                     lambda b, j: (jnp.maximum((b * tiles + j) * blocks_per_tile - 1, 0), 0)),
        pl.BlockSpec((HALF, 2 * GROUP_WIDTH),
                     lambda b, j: (jnp.minimum((b * tiles + j + 1) * blocks_per_tile, n_half_blocks - 1), 0)),
    ]
    scratch.append(pltpu.VMEM((1, T_ATT + 2 * HALF, 2 * GROUP_WIDTH), BF16))
    for g in (1, 2):
        d = DILATED_PATTERNS[g][1]
        n_res = T_ATT // d
        per_tile = n_res // HALF
        last = seq // d // HALF - 1
        specs += [
            pl.BlockSpec((None, d, n_res, GROUP_WIDTH), lambda b, j: (b, 0, j, 0)),
            pl.BlockSpec((None, d, n_res, 2 * GROUP_WIDTH), lambda b, j: (b, 0, j, 0)),
            pl.BlockSpec((None, d, HALF, 2 * GROUP_WIDTH),
                         lambda b, j, per_tile=per_tile: (b, 0, jnp.maximum(j * per_tile - 1, 0), 0)),
            pl.BlockSpec((None, d, HALF, 2 * GROUP_WIDTH),
                         lambda b, j, per_tile=per_tile, last=last: (b, 0, jnp.minimum((j + 1) * per_tile, last), 0)),
        ]
        scratch.append(pltpu.VMEM((d, n_res + 2 * HALF, 2 * GROUP_WIDTH), BF16))
    scratch.append(pltpu.VMEM((N_HEADS_A, QB, KB), F32))
    scratch += [pltpu.VMEM((2, T_ATT, LANES), F32) for _ in range(3)]
    scratch += [pltpu.VMEM((T_ATT, GROUP_WIDTH), F32) for _ in range(3)]
    return pl.pallas_call(
        functools.partial(_attn_kernel, seq=seq),
        grid=(batch, tiles),
        in_specs=specs,
        out_specs=pl.BlockSpec((T_ATT, GROUP_WIDTH), lambda b, j: (b * tiles + j, 0)),
        out_shape=jax.ShapeDtypeStruct((n, GROUP_WIDTH), BF16),
        scratch_shapes=scratch,
        compiler_params=pltpu.CompilerParams(dimension_semantics=("arbitrary", "arbitrary"),
                                             vmem_limit_bytes=VMEM_LIMIT),
        name="attn",
    )(q0, kv0, kv0, kv0, q1, kv1, kv1, kv1, q2, kv2, kv2, kv2)


def _split_dot(a, w_hi, w_lo):
    a_hi = a.astype(BF16)
    a_lo = (a - a_hi.astype(F32)).astype(BF16)
    return (jnp.dot(a_hi, w_hi, preferred_element_type=F32)
            + jnp.dot(a_lo, w_hi, preferred_element_type=F32)
            + jnp.dot(a_hi, w_lo, preferred_element_type=F32))


def _mix_kernel(x_ref, ya_ref, bg_ref, u_ref, up_ref, un_ref, gate_ref,
                wa_ref, wc_ref, wo_ref, cw_ref, g2_ref, wr_hi_ref, wr_lo_ref, br_ref,
                x1_ref, hflat_ref, route_ref, routet_ref, cnt_ref, base_ref, *, seq):
    i = pl.program_id(0)
    t0 = i * TM_MIX

    @pl.when(i == 0)
    def _():
        base_ref[...] = jnp.zeros_like(base_ref)

    u = u_ref[...].astype(F32)
    row = lax.broadcasted_iota(jnp.int32, (TM_MIX, CONV_WIDTH), 0)
    prev_row = jnp.where(t0 % seq == 0, 0.0, up_ref[15:16, :].astype(F32))
    next_row = jnp.where((t0 + TM_MIX) % seq == 0, 0.0, un_ref[0:1, :].astype(F32))
    u_prev = jnp.where(row == 0, prev_row, pltpu.roll(u, 1, axis=0))
    u_next = jnp.where(row == TM_MIX - 1, next_row, pltpu.roll(u, TM_MIX - 1, axis=0))
    conv = cw_ref[0:1, :] * u_prev + cw_ref[1:2, :] * u + cw_ref[2:3, :] * u_next
    yb_in = (bg_ref[...].astype(F32) * conv).astype(BF16)

    y_a = jnp.dot(ya_ref[...], wa_ref[...], preferred_element_type=F32)
    y_b = jnp.dot(yb_in, wc_ref[...], preferred_element_type=F32)
    merged = (gate_ref[:, 0:D_MODEL].astype(F32) * y_a
              + gate_ref[:, D_MODEL:].astype(F32) * y_b).astype(BF16)
    x1 = x_ref[...] + jnp.dot(merged, wo_ref[...], preferred_element_type=F32)
    x1_ref[...] = x1

    h2 = _rms(x1, g2_ref[...])
    for c in range(ROW_CHUNKS):
        hflat_ref[pl.ds(c, TM_MIX, stride=ROW_CHUNKS), :] = h2[:, c * LANES:(c + 1) * LANES]

    logits = _split_dot(h2, wr_hi_ref[...], wr_lo_ref[...]) + br_ref[...]
    lane = lax.broadcasted_iota(jnp.int32, (TM_MIX, LANES), 1)
    lane_f = lane.astype(F32)
    neg = -jnp.inf
    big = float(LANES)
    is_group = lane < N_EXPERT_GROUPS
    cm = jnp.where(is_group, logits, neg)
    cmax = jnp.max(cm, axis=1, keepdims=True)
    g_idx = jnp.min(jnp.where(cm == cmax, lane_f, big), axis=1, keepdims=True)
    p_group = 1.0 / jnp.sum(jnp.where(is_group, jnp.exp(logits - cmax), 0.0), axis=1, keepdims=True)
    f_lo = N_EXPERT_GROUPS + EXPERTS_PER_GROUP * g_idx
    in_group = (lane_f >= f_lo) & (lane_f < f_lo + EXPERTS_PER_GROUP)
    fm = jnp.where(in_group, logits, neg)
    f1 = jnp.max(fm, axis=1, keepdims=True)
    i1 = jnp.min(jnp.where(fm == f1, lane_f, big), axis=1, keepdims=True)
    fm2 = jnp.where(lane_f == i1, neg, fm)
    f2 = jnp.max(fm2, axis=1, keepdims=True)
    i2 = jnp.min(jnp.where(fm2 == f2, lane_f, big), axis=1, keepdims=True)
    e21 = jnp.exp(f2 - f1)
    w_1 = p_group / (1.0 + e21)
    w_2 = p_group * e21 / (1.0 + e21)
    e_1 = i1 - N_EXPERT_GROUPS
    e_2 = i2 - N_EXPERT_GROUPS

    onehot = jnp.where((lane_f == e_1) | (lane_f == e_2), 1.0, 0.0)
    r_i = lax.broadcasted_iota(jnp.int32, (TM_MIX, TM_MIX), 0)
    c_i = lax.broadcasted_iota(jnp.int32, (TM_MIX, TM_MIX), 1)
    tri = jnp.where(c_i < r_i, 1.0, 0.0).astype(BF16)
    before = jnp.dot(tri, onehot.astype(BF16), preferred_element_type=F32) + base_ref[0:1, :]
    rank_1 = jnp.sum(jnp.where(lane_f == e_1, before, 0.0), axis=1, keepdims=True)
    rank_2 = jnp.sum(jnp.where(lane_f == e_2, before, 0.0), axis=1, keepdims=True)
    base_ref[0:1, :] = base_ref[0:1, :] + jnp.sum(onehot, axis=0, keepdims=True)
    cnt_ref[...] = jnp.broadcast_to(base_ref[0:1, :], cnt_ref.shape)

    route = jnp.zeros((TM_MIX, LANES), F32)
    for k, val in enumerate((e_1, e_2, w_1, w_2, rank_1, rank_2)):
        route = jnp.where(lane == k, val, route)
    route_ref[...] = route
    routet_ref[...] = route.T[0:SUBLANES, :]


def _mix_call(x2, y_attn, bgate, u, gates, wa, wc, wo, conv_w, g2, wr_hi, wr_lo, br, seq):
    n = x2.shape[0]
    row = lambda i: (i, 0)
    const = lambda i: (0, 0)
    halo = 16
    per_tile = TM_MIX // halo
    last = n // halo - 1
    return pl.pallas_call(
        functools.partial(_mix_kernel, seq=seq),
        grid=(n // TM_MIX,),
        in_specs=[
            pl.BlockSpec((TM_MIX, D_MODEL), row),
            pl.BlockSpec((TM_MIX, GROUP_WIDTH), row),
            pl.BlockSpec((TM_MIX, CONV_WIDTH), row),
            pl.BlockSpec((TM_MIX, CONV_WIDTH), row),
            pl.BlockSpec((halo, CONV_WIDTH), lambda i: (jnp.maximum(i * per_tile - 1, 0), 0)),
            pl.BlockSpec((halo, CONV_WIDTH), lambda i: (jnp.minimum((i + 1) * per_tile, last), 0)),
            pl.BlockSpec((TM_MIX, 2 * D_MODEL), row),
            pl.BlockSpec((GROUP_WIDTH, D_MODEL), const),
            pl.BlockSpec((CONV_WIDTH, D_MODEL), const),
            pl.BlockSpec((D_MODEL, D_MODEL), const),
            pl.BlockSpec((3, CONV_WIDTH), const),
            pl.BlockSpec((1, D_MODEL), const),
            pl.BlockSpec((D_MODEL, LANES), const),
            pl.BlockSpec((D_MODEL, LANES), const),
            pl.BlockSpec((1, LANES), const),
        ],
        out_specs=[
            pl.BlockSpec((TM_MIX, D_MODEL), row),
            pl.BlockSpec((TM_MIX * ROW_CHUNKS, LANES), row),
            pl.BlockSpec((TM_MIX, LANES), row),
            pl.BlockSpec((SUBLANES, TM_MIX), lambda i: (0, i)),
            pl.BlockSpec((SUBLANES, LANES), const),
        ],
        out_shape=[
            jax.ShapeDtypeStruct((n, D_MODEL), F32),
            jax.ShapeDtypeStruct((n * ROW_CHUNKS, LANES), F32),
            jax.ShapeDtypeStruct((n, LANES), F32),
            jax.ShapeDtypeStruct((SUBLANES, n), F32),
            jax.ShapeDtypeStruct((SUBLANES, LANES), F32),
        ],
        scratch_shapes=[pltpu.VMEM((SUBLANES, LANES), F32)],
        compiler_params=pltpu.CompilerParams(dimension_semantics=("arbitrary",),
                                             vmem_limit_bytes=VMEM_LIMIT),
        name="mix",
    )(x2, y_attn, bgate, u, u, u, gates, wa, wc, wo, conv_w, g2, wr_hi, wr_lo, br)


def _row_gather(idx_ref, n_rows, src_hbm, dst, sem):
    def issue(j, carry):
        t = idx_ref[0, 0, j]
        pltpu.make_async_copy(src_hbm.at[pl.ds(pl.multiple_of(t * SUBLANES, SUBLANES), SUBLANES)],
                              dst.at[pl.ds(pl.multiple_of(j * SUBLANES, SUBLANES), SUBLANES)],
                              sem).start()
        return carry
    lax.fori_loop(0, n_rows, issue, 0, unroll=8)


def _row_gather_wait(n_rows, src_hbm, dst, sem):
    pltpu.make_async_copy(src_hbm.at[pl.ds(0, n_rows * SUBLANES)], dst, sem).wait()


def _rows_from_tiles(buf, first_row, n_rows):
    return jnp.concatenate(
        [buf[pl.ds(first_row * ROW_CHUNKS + c, n_rows, stride=ROW_CHUNKS), :] for c in range(ROW_CHUNKS)],
        axis=1)


def _invmap_kernel(pos_ref, zeros_hbm, src_ref, sem):
    i = pl.program_id(0)

    @pl.when(i == 0)
    def _():
        init = pltpu.make_async_copy(zeros_hbm, src_ref, sem)
        init.start()
        init.wait()

    def body(j, carry):
        t = i * TM_INV + j
        src_ref[pos_ref[0, 0, j]] = t
        src_ref[pos_ref[0, 0, TM_INV + j]] = t
        return carry
    lax.fori_loop(0, TM_INV, body, 0, unroll=16)


def _invmap_call(pos_tiles, n_rows):
    return pl.pallas_call(
        _invmap_kernel,
        grid=(pos_tiles.shape[0],),
        in_specs=[pl.BlockSpec((1, 1, 2 * TM_INV), lambda i: (i, 0, 0), memory_space=pltpu.SMEM),
                  pl.BlockSpec(memory_space=pl.ANY)],
        out_specs=pl.BlockSpec((n_rows,), lambda i: (0,), memory_space=pltpu.SMEM),
        out_shape=jax.ShapeDtypeStruct((n_rows,), jnp.int32),
        scratch_shapes=[pltpu.SemaphoreType.DMA(())],
        compiler_params=pltpu.CompilerParams(dimension_semantics=("arbitrary",)),
        name="invmap",
    )(pos_tiles, jnp.zeros((n_rows,), jnp.int32))


def _expert_kernel(te_ref, nused_ref, src_ref, srcn_ref, h_hbm, w1_ref, w3_ref, w2_ref,
                   y_ref, buf0, buf1, w1b, w3b, w2b, sem):
    i = pl.program_id(0)
    n_used = nused_ref[0]
    bufs = (buf0, buf1)

    @pl.when(i == 0)
    def _():
        _row_gather(src_ref, TM_MOE, h_hbm, buf0, sem.at[0])

    for slot in range(2):
        @pl.when((i % 2 == slot) & (i + 1 < n_used))
        def _(slot=slot):
            _row_gather(srcn_ref, TM_MOE, h_hbm, bufs[1 - slot], sem.at[1 - slot])

    @pl.when((i < n_used) & ((i == 0) | (te_ref[i] != te_ref[jnp.maximum(i - 1, 0)])))
    def _():
        w1b[...] = w1_ref[...].astype(BF16)
        w3b[...] = w3_ref[...].astype(BF16)
        w2b[...] = w2_ref[...].astype(BF16)

    for slot in range(2):
        @pl.when((i % 2 == slot) & (i < n_used))
        def _(slot=slot):
            _row_gather_wait(TM_MOE, h_hbm, bufs[slot], sem.at[slot])
            x = _rows_from_tiles(bufs[slot], 0, TM_MOE).astype(BF16)
            a = jnp.dot(x, w1b[...], preferred_element_type=F32)
            b = jnp.dot(x, w3b[...], preferred_element_type=F32)
            hid = (a * jax.nn.sigmoid(a) * b).astype(BF16)
            y = jnp.dot(hid, w2b[...], preferred_element_type=F32)
            for c in range(ROW_CHUNKS):
                y_ref[pl.ds(c, TM_MOE, stride=ROW_CHUNKS), :] = y[:, c * LANES:(c + 1) * LANES]

    @pl.when(i >= n_used)
    def _():
        y_ref[...] = jnp.zeros_like(y_ref)


def _expert_call(tile_expert, n_used, src, h_flat, w1, w3, w2):
    n_tiles = tile_expert.shape[0]
    wspec = lambda shape: pl.BlockSpec(
        (None,) + shape, lambda i, te, nu: (te[jnp.minimum(i, nu[0] - 1)], 0, 0))
    grid_spec = pltpu.PrefetchScalarGridSpec(
        num_scalar_prefetch=2,
        grid=(n_tiles,),
        in_specs=[
            pl.BlockSpec((1, 1, TM_MOE), lambda i, te, nu: (i, 0, 0), memory_space=pltpu.SMEM),
            pl.BlockSpec((1, 1, TM_MOE), lambda i, te, nu: (jnp.minimum(i + 1, n_tiles - 1), 0, 0),
                         memory_space=pltpu.SMEM),
            pl.BlockSpec(memory_space=pl.ANY),
            wspec((D_MODEL, EXPERT_FF)),
            wspec((D_MODEL, EXPERT_FF)),
            wspec((EXPERT_FF, D_MODEL)),
        ],
        out_specs=pl.BlockSpec((TM_MOE * ROW_CHUNKS, LANES), lambda i, te, nu: (i, 0)),
        scratch_shapes=[pltpu.VMEM((TM_MOE * ROW_CHUNKS, LANES), F32),
                        pltpu.VMEM((TM_MOE * ROW_CHUNKS, LANES), F32),
                        pltpu.VMEM((D_MODEL, EXPERT_FF), BF16),
                        pltpu.VMEM((D_MODEL, EXPERT_FF), BF16),
                        pltpu.VMEM((EXPERT_FF, D_MODEL), BF16),
                        pltpu.SemaphoreType.DMA((2,))],
    )
    return pl.pallas_call(
        _expert_kernel,
        grid_spec=grid_spec,
        out_shape=jax.ShapeDtypeStruct((n_tiles * TM_MOE * ROW_CHUNKS, LANES), F32),
        compiler_params=pltpu.CompilerParams(dimension_semantics=("arbitrary",),
                                             vmem_limit_bytes=VMEM_LIMIT),
        name="experts",
    )(tile_expert, n_used, src.reshape(n_tiles, 1, TM_MOE), src.reshape(n_tiles, 1, TM_MOE),
      h_flat, w1, w3, w2)


def _combine_kernel(pos_ref, posn_ref, y_hbm, x1_ref, route_ref, g_ref, o_ref, buf0, buf1, sem):
    i = pl.program_id(0)
    n_steps = pl.num_programs(0)
    bufs = (buf0, buf1)

    @pl.when(i == 0)
    def _():
        _row_gather(pos_ref, 2 * TM_CMB, y_hbm, buf0, sem.at[0])

    for slot in range(2):
        @pl.when((i % 2 == slot) & (i + 1 < n_steps))
        def _(slot=slot):
            _row_gather(posn_ref, 2 * TM_CMB, y_hbm, bufs[1 - slot], sem.at[1 - slot])

    for slot in range(2):
        @pl.when(i % 2 == slot)
        def _(slot=slot):
            _row_gather_wait(2 * TM_CMB, y_hbm, bufs[slot], sem.at[slot])
            y_1 = _rows_from_tiles(bufs[slot], 0, TM_CMB)
            y_2 = _rows_from_tiles(bufs[slot], TM_CMB, TM_CMB)
            x = x1_ref[...] + route_ref[:, 2:3] * y_1 + route_ref[:, 3:4] * y_2
            o_ref[...] = _rms(x, g_ref[...])


def _combine_call(pos, y_flat, x1, route, g):
    n = x1.shape[0]
    n_steps = n // TM_CMB
    row = lambda i: (i, 0)
    return pl.pallas_call(
        _combine_kernel,
        grid=(n_steps,),
        in_specs=[
            pl.BlockSpec((1, 1, 2 * TM_CMB), lambda i: (i, 0, 0), memory_space=pltpu.SMEM),
            pl.BlockSpec((1, 1, 2 * TM_CMB), lambda i: (jnp.minimum(i + 1, n_steps - 1), 0, 0),
                         memory_space=pltpu.SMEM),
            pl.BlockSpec(memory_space=pl.ANY),
            pl.BlockSpec((TM_CMB, D_MODEL), row),
            pl.BlockSpec((TM_CMB, LANES), row),
            pl.BlockSpec((1, D_MODEL), lambda i: (0, 0)),
        ],
        out_specs=pl.BlockSpec((TM_CMB, D_MODEL), row),
        out_shape=jax.ShapeDtypeStruct((n, D_MODEL), F32),
        scratch_shapes=[pltpu.VMEM((2 * TM_CMB * ROW_CHUNKS, LANES), F32),
                        pltpu.VMEM((2 * TM_CMB * ROW_CHUNKS, LANES), F32),
                        pltpu.SemaphoreType.DMA((2,))],
        compiler_params=pltpu.CompilerParams(dimension_semantics=("arbitrary",),
                                             vmem_limit_bytes=VMEM_LIMIT),
        name="combine",
    )(pos, pos, y_flat, x1, route, g)


def _layer(x2, batch, seq, norm_mix_g, w_in, b_gate, conv_w, w_attn_out, w_conv_out, w_out, norm_ffn_g,
           w_route_group, b_route_group, w_route_expert, b_route_expert, w1, w3, w2, final_g):
    n = x2.shape[0]
    q0, kv0, q1, kv1, q2, kv2, bgate, u, gates = _proj_call(
        x2, norm_mix_g[None, :], w_in.astype(BF16), b_gate[None, :], batch, seq)
    y_attn = _attn_call(q0, kv0, q1, kv1, q2, kv2, batch, seq)

    n_route = N_EXPERT_GROUPS + N_EXPERTS
    w_route = jnp.pad(jnp.concatenate([w_route_group, w_route_expert], axis=1), ((0, 0), (0, LANES - n_route)))
    b_route = jnp.pad(jnp.concatenate([b_route_group, b_route_expert]), (0, LANES - n_route))[None, :]
    wr_hi = w_route.astype(BF16)
    wr_lo = (w_route - wr_hi.astype(F32)).astype(BF16)
    x1, h_flat, route, route_t, counts = _mix_call(
        x2, y_attn, bgate, u, gates, w_attn_out.astype(BF16), w_conv_out.astype(BF16), w_out.astype(BF16),
        conv_w, norm_ffn_g[None, :], wr_hi, wr_lo, b_route, seq)

    ids = jnp.arange(N_EXPERTS, dtype=jnp.int32)
    expert = route_t[0:2].astype(jnp.int32)
    rank = route_t[4:6].astype(jnp.int32)
    cnt = counts[0, :N_EXPERTS].astype(jnp.int32)
    tiles_e = (cnt + TM_MOE - 1) // TM_MOE
    tile_end = jnp.sum(jnp.where(ids[:, None] >= ids[None, :], tiles_e[None, :], 0), axis=1)
    tile_start = tile_end - tiles_e
    pos = jnp.sum(jnp.where(expert[..., None] == ids, tile_start * TM_MOE, 0), axis=-1) + rank
    n_tiles = (2 * n) // TM_MOE + N_EXPERTS
    tidx = jnp.arange(n_tiles, dtype=jnp.int32)
    tile_expert = jnp.minimum(jnp.sum((tile_end[None, :] <= tidx[:, None]).astype(jnp.int32), axis=1),
                              N_EXPERTS - 1)
    n_used = tile_end[-1:]

    def step_tiles(t):
        return pos.reshape(2, n // t, t).transpose(1, 0, 2).reshape(n // t, 1, 2 * t)

    src = _invmap_call(step_tiles(TM_INV), n_tiles * TM_MOE)
    y_flat = _expert_call(tile_expert, n_used, src, h_flat, w1, w3, w2)
    return _combine_call(step_tiles(TM_CMB), y_flat, x1, route, final_g[None, :])


def kernel(x, norm_mix_g, w_in, b_gate, conv_w, w_attn_out, w_conv_out, w_out, norm_ffn_g,
           w_route_group, b_route_group, w_route_expert, b_route_expert, w1, w3, w2, norm_final_g):
    batch, seq, d = x.shape
    depth = w_in.shape[0]
    assert d == D_MODEL and depth == 1 and seq % T_ATT == 0
    out = _layer(x.reshape(batch * seq, d), batch, seq, norm_mix_g[0], w_in[0], b_gate[0], conv_w[0],
                 w_attn_out[0], w_conv_out[0], w_out[0], norm_ffn_g[0], w_route_group[0], b_route_group[0],
                 w_route_expert[0], b_route_expert[0], w1[0], w3[0], w2[0], norm_final_g)
    return out.reshape(batch, seq, d)
```

```python
import functools

import numpy as np
import jax
import jax.numpy as jnp
from jax import lax
from jax.experimental import pallas as pl
from jax.experimental.pallas import tpu as pltpu

F32 = jnp.float32
BF16 = jnp.bfloat16

D_MODEL = 1024
HEAD_DIM = 64
HEADS_PER_GROUP = 4
DILATED_PATTERNS = ((128, 1), (512, 4), (2048, 16))
N_GROUPS_A = 3
N_HEADS_A = N_GROUPS_A * HEADS_PER_GROUP
ATTN_WIDTH = N_HEADS_A * HEAD_DIM
GROUP_WIDTH = HEADS_PER_GROUP * HEAD_DIM
ALIBI_SPAN = 8.0
MASK_VALUE = -1e30
CONV_WIDTH = 768
N_EXPERT_GROUPS = 4
EXPERTS_PER_GROUP = 8
N_EXPERTS = 32
EXPERT_FF = 512
RMS_EPS = 1e-6

HALF = 64
LANES = 128
SUBLANES = 8
ROW_CHUNKS = D_MODEL // LANES

COL_K = ATTN_WIDTH
COL_V = 2 * ATTN_WIDTH
COL_BG = 3 * ATTN_WIDTH
COL_CG = COL_BG + CONV_WIDTH
COL_XIN = COL_CG + CONV_WIDTH
COL_GATE = COL_XIN + CONV_WIDTH
IN_COLS = COL_GATE + 2 * D_MODEL

TM_PROJ = 512
T_ATT = 2048
QB = 128
KB = QB + 2 * HALF
ATT_UNROLL = 8
TM_MIX = 512
TM_DSP = 512
TM_MOE = 512
TM_CMB = 256

VMEM_LIMIT = 56 * 1024 * 1024


def _alibi_slopes():
    return np.array([2.0 ** (-ALIBI_SPAN * (i + 1) / N_HEADS_A) for i in range(N_HEADS_A)],
                    dtype=np.float32).reshape(N_GROUPS_A, HEADS_PER_GROUP)


def _rms(x, g):
    return x * lax.rsqrt(jnp.mean(x * x, axis=-1, keepdims=True) + RMS_EPS) * g


def _proj_kernel(x_ref, g_ref, w_ref, b_ref,
                 q0_ref, kv0_ref, q1_ref, kv1_ref, q2_ref, kv2_ref, bg_ref, u_ref, gate_ref, scr):
    h = _rms(x_ref[...], g_ref[...]).astype(BF16)

    def proj(c0, width):
        return jnp.dot(h, w_ref[:, c0:c0 + width], preferred_element_type=F32)

    qscale = HEAD_DIM ** -0.5
    q0_ref[...] = (proj(0, GROUP_WIDTH) * qscale).astype(BF16)
    kv0_ref[:, 0:GROUP_WIDTH] = proj(COL_K, GROUP_WIDTH).astype(BF16)
    kv0_ref[:, GROUP_WIDTH:] = proj(COL_V, GROUP_WIDTH).astype(BF16)

    for g, q_ref, kv_ref in ((1, q1_ref, kv1_ref), (2, q2_ref, kv2_ref)):
        d = DILATED_PATTERNS[g][1]
        n = TM_PROJ // d
        parts = (proj(g * GROUP_WIDTH, GROUP_WIDTH) * qscale,
                 proj(COL_K + g * GROUP_WIDTH, GROUP_WIDTH),
                 proj(COL_V + g * GROUP_WIDTH, GROUP_WIDTH))
        for i, part in enumerate(parts):
            for c in range(2):
                scr[2 * i + c] = part[:, c * LANES:(c + 1) * LANES]
        for r in range(d):
            rows = pl.ds(r, n, stride=d)
            q_ref[r] = jnp.concatenate([scr[c, rows, :] for c in range(2)], axis=1).astype(BF16)
            kv_ref[r] = jnp.concatenate([scr[c, rows, :] for c in range(2, 6)], axis=1).astype(BF16)

    bg_ref[...] = proj(COL_BG, CONV_WIDTH).astype(BF16)
    u_ref[...] = (proj(COL_CG, CONV_WIDTH) * proj(COL_XIN, CONV_WIDTH)).astype(BF16)
    for c in range(4):
        w = 2 * D_MODEL // 4
        z = proj(COL_GATE + c * w, w) + b_ref[:, c * w:(c + 1) * w]
        gate_ref[:, c * w:(c + 1) * w] = jax.nn.sigmoid(z).astype(BF16)


def _proj_call(x2, g, w_in, b_gate, batch, seq):
    n = x2.shape[0]
    steps_per_batch = seq // TM_PROJ
    d1, d2 = DILATED_PATTERNS[1][1], DILATED_PATTERNS[2][1]
    row = lambda i: (i, 0)
    res = lambda i: (i // steps_per_batch, 0, i % steps_per_batch, 0)
    const = lambda i: (0, 0)
    out_shape = [
        jax.ShapeDtypeStruct((n, GROUP_WIDTH), BF16),
        jax.ShapeDtypeStruct((n, 2 * GROUP_WIDTH), BF16),
        jax.ShapeDtypeStruct((batch, d1, seq // d1, GROUP_WIDTH), BF16),
        jax.ShapeDtypeStruct((batch, d1, seq // d1, 2 * GROUP_WIDTH), BF16),
        jax.ShapeDtypeStruct((batch, d2, seq // d2, GROUP_WIDTH), BF16),
        jax.ShapeDtypeStruct((batch, d2, seq // d2, 2 * GROUP_WIDTH), BF16),
        jax.ShapeDtypeStruct((n, CONV_WIDTH), BF16),
        jax.ShapeDtypeStruct((n, CONV_WIDTH), BF16),
        jax.ShapeDtypeStruct((n, 2 * D_MODEL), BF16),
    ]
    out_specs = [
        pl.BlockSpec((TM_PROJ, GROUP_WIDTH), row),
        pl.BlockSpec((TM_PROJ, 2 * GROUP_WIDTH), row),
        pl.BlockSpec((None, d1, TM_PROJ // d1, GROUP_WIDTH), res),
        pl.BlockSpec((None, d1, TM_PROJ // d1, 2 * GROUP_WIDTH), res),
        pl.BlockSpec((None, d2, TM_PROJ // d2, GROUP_WIDTH), res),
        pl.BlockSpec((None, d2, TM_PROJ // d2, 2 * GROUP_WIDTH), res),
        pl.BlockSpec((TM_PROJ, CONV_WIDTH), row),
        pl.BlockSpec((TM_PROJ, CONV_WIDTH), row),
        pl.BlockSpec((TM_PROJ, 2 * D_MODEL), row),
    ]
    return pl.pallas_call(
        _proj_kernel,
        grid=(n // TM_PROJ,),
        in_specs=[
            pl.BlockSpec((TM_PROJ, D_MODEL), row),
            pl.BlockSpec((1, D_MODEL), const),
            pl.BlockSpec((D_MODEL, IN_COLS), const),
            pl.BlockSpec((1, 2 * D_MODEL), const),
        ],
        out_specs=out_specs,
        out_shape=out_shape,
        scratch_shapes=[pltpu.VMEM((6, TM_PROJ, LANES), F32)],
        compiler_params=pltpu.CompilerParams(dimension_semantics=("arbitrary",),
                                             vmem_limit_bytes=VMEM_LIMIT),
        name="proj",
    )(x2, g, w_in, b_gate)


def _attn_sub_block(q_sub, kw, vw, bias_ref, g, lo, hi):
    assert KB == GROUP_WIDTH
    lane = lax.broadcasted_iota(jnp.int32, (QB, KB), 1)
    edge_ok = (lane >= lo) & (lane < hi)
    heads = [(lane >= h * HEAD_DIM) & (lane < (h + 1) * HEAD_DIM) for h in range(HEADS_PER_GROUP)]
    zero = jnp.zeros((), BF16)
    q_stack = jnp.concatenate([jnp.where(hm, q_sub, zero) for hm in heads], axis=0)
    s_all = lax.dot_general(q_stack, kw, (((1,), (1,)), ((), ())), preferred_element_type=F32)
    probs = []
    m_b = l_b = None
    for h, hm in enumerate(heads):
        s = s_all[h * QB:(h + 1) * QB] + bias_ref[g * HEADS_PER_GROUP + h]
        s = jnp.where(edge_ok, s, MASK_VALUE)
        m = jnp.max(s, axis=1, keepdims=True)
        p = jnp.exp(s - m)
        l = jnp.sum(p, axis=1, keepdims=True)
        probs.append(p.astype(BF16))
        m_b = jnp.broadcast_to(m, (QB, GROUP_WIDTH)) if m_b is None else jnp.where(hm, m, m_b)
        l_b = jnp.broadcast_to(l, (QB, GROUP_WIDTH)) if l_b is None else jnp.where(hm, l, l_b)
    o_all = jnp.dot(jnp.concatenate(probs, axis=0), vw, preferred_element_type=F32)
    acc = o_all[0:QB]
    for h in range(1, HEADS_PER_GROUP):
        acc = jnp.where(heads[h], o_all[h * QB:(h + 1) * QB], acc)
    return acc, m_b, l_b


def _attn_kernel(q0_ref, kv0_ref, kv0p_ref, kv0n_ref,
                 q1_ref, kv1_ref, kv1p_ref, kv1n_ref,
                 q2_ref, kv2_ref, kv2p_ref, kv2n_ref,
                 y_ref,
                 cat0, cat1, cat2, bias_ref, m_st, l_st, a_st, m_tmp, l_tmp, a_tmp, *, seq):
    j = pl.program_id(1)

    qi = lax.broadcasted_iota(jnp.int32, (QB, KB), 0)
    kc = lax.broadcasted_iota(jnp.int32, (QB, KB), 1)
    adelta = jnp.abs(kc - HALF - qi)
    band = adelta <= HALF
    slopes = _alibi_slopes()
    for g in range(N_GROUPS_A):
        dist = (adelta * DILATED_PATTERNS[g][1]).astype(F32)
        for h in range(HEADS_PER_GROUP):
            bias_ref[g * HEADS_PER_GROUP + h] = jnp.where(band, -(float(slopes[g, h]) * dist), MASK_VALUE)

    for cat, own, prv, nxt in ((cat0, kv0_ref, kv0p_ref, kv0n_ref),
                               (cat1, kv1_ref, kv1p_ref, kv1n_ref),
                               (cat2, kv2_ref, kv2p_ref, kv2n_ref)):
        n_own = own.shape[-2]
        cat[:, 0:HALF, :] = prv[...].reshape(cat.shape[0], HALF, 2 * GROUP_WIDTH)
        cat[:, HALF:HALF + n_own, :] = own[...].reshape(cat.shape[0], n_own, 2 * GROUP_WIDTH)
        cat[:, HALF + n_own:, :] = nxt[...].reshape(cat.shape[0], HALF, 2 * GROUP_WIDTH)

    def window(cat, r, sb):
        rows = pl.ds(pl.multiple_of(sb * QB, QB), KB)
        return cat[r, rows, 0:GROUP_WIDTH], cat[r, rows, GROUP_WIDTH:]

    def edges(g, n_res, sb):
        length = seq // DILATED_PATTERNS[g][1]
        i0 = j * n_res + sb * QB
        return jnp.maximum(0, HALF - i0), jnp.minimum(KB, length + HALF - i0)

    def body0(sb, carry):
        rows = pl.ds(pl.multiple_of(sb * QB, QB), QB)
        kw, vw = window(cat0, 0, sb)
        lo, hi = edges(0, T_ATT, sb)
        acc, m_b, l_b = _attn_sub_block(q0_ref[rows, :], kw, vw, bias_ref, 0, lo, hi)
        for c in range(2):
            cols = slice(c * LANES, (c + 1) * LANES)
            m_st[c, rows, :] = m_b[:, cols]
            l_st[c, rows, :] = l_b[:, cols]
            a_st[c, rows, :] = acc[:, cols]
        return carry

    lax.fori_loop(0, T_ATT // QB, body0, 0, unroll=ATT_UNROLL)

    for g, q_ref, cat in ((1, q1_ref, cat1), (2, q2_ref, cat2)):
        d = DILATED_PATTERNS[g][1]
        n_res = T_ATT // d
        sb_per_res = n_res // QB

        def body(idx, carry, g=g, q_ref=q_ref, cat=cat, n_res=n_res, sb_per_res=sb_per_res):
            r = idx // sb_per_res
            sb = idx % sb_per_res
            kw, vw = window(cat, r, sb)
            lo, hi = edges(g, n_res, sb)
            q_sub = q_ref[r, pl.ds(pl.multiple_of(sb * QB, QB), QB), :]
            acc, m_b, l_b = _attn_sub_block(q_sub, kw, vw, bias_ref, g, lo, hi)
            rows = pl.ds(pl.multiple_of(idx * QB, QB), QB)
            m_tmp[rows, :] = m_b
            l_tmp[rows, :] = l_b
            a_tmp[rows, :] = acc
            return carry

        lax.fori_loop(0, T_ATT // QB, body, 0, unroll=ATT_UNROLL)

        for r in range(d):
            for ch in range(sb_per_res):
                src = slice(r * n_res + ch * QB, r * n_res + (ch + 1) * QB)
                tok = pl.ds(ch * QB * d + r, QB, stride=d)
                for c in range(2):
                    cols = slice(c * LANES, (c + 1) * LANES)
                    m_new_part = m_tmp[src, cols]
                    m_old = m_st[c, tok, :]
                    m_new = jnp.maximum(m_old, m_new_part)
                    e_old = jnp.exp(m_old - m_new)
                    e_new = jnp.exp(m_new_part - m_new)
                    m_st[c, tok, :] = m_new
                    l_st[c, tok, :] = e_old * l_st[c, tok, :] + e_new * l_tmp[src, cols]
                    a_st[c, tok, :] = e_old * a_st[c, tok, :] + e_new * a_tmp[src, cols]

    for c in range(2):
        y_ref[:, c * LANES:(c + 1) * LANES] = (a_st[c] / l_st[c]).astype(BF16)


def _attn_call(q0, kv0, q1, kv1, q2, kv2, batch, seq):
    n = q0.shape[0]
    tiles = seq // T_ATT
    specs = []
    scratch = []
    blocks_per_tile = T_ATT // HALF
    n_half_blocks = n // HALF
    specs += [
        pl.BlockSpec((T_ATT, GROUP_WIDTH), lambda b, j: (b * tiles + j, 0)),
        pl.BlockSpec((T_ATT, 2 * GROUP_WIDTH), lambda b, j: (b * tiles + j, 0)),
        pl.BlockSpec((HALF, 2 * GROUP_WIDTH),
                     lambda b, j: (jnp.maximum((b * tiles + j) * blocks_per_tile - 1, 0), 0)),
        pl.BlockSpec((HALF, 2 * GROUP_WIDTH),
                     lambda b, j: (jnp.minimum((b * tiles + j + 1) * blocks_per_tile, n_half_blocks - 1), 0)),
    ]
    scratch.append(pltpu.VMEM((1, T_ATT + 2 * HALF, 2 * GROUP_WIDTH), BF16))
    for g in (1, 2):
        d = DILATED_PATTERNS[g][1]
        n_res = T_ATT // d
        per_tile = n_res // HALF
        last = seq // d // HALF - 1
        specs += [
            pl.BlockSpec((None, d, n_res, GROUP_WIDTH), lambda b, j: (b, 0, j, 0)),
            pl.BlockSpec((None, d, n_res, 2 * GROUP_WIDTH), lambda b, j: (b, 0, j, 0)),
            pl.BlockSpec((None, d, HALF, 2 * GROUP_WIDTH),
                         lambda b, j, per_tile=per_tile: (b, 0, jnp.maximum(j * per_tile - 1, 0), 0)),
            pl.BlockSpec((None, d, HALF, 2 * GROUP_WIDTH),
                         lambda b, j, per_tile=per_tile, last=last: (b, 0, jnp.minimum((j + 1) * per_tile, last), 0)),
        ]
        scratch.append(pltpu.VMEM((d, n_res + 2 * HALF, 2 * GROUP_WIDTH), BF16))
    scratch.append(pltpu.VMEM((N_HEADS_A, QB, KB), F32))
    scratch += [pltpu.VMEM((2, T_ATT, LANES), F32) for _ in range(3)]
    scratch += [pltpu.VMEM((T_ATT, GROUP_WIDTH), F32) for _ in range(3)]
    return pl.pallas_call(
        functools.partial(_attn_kernel, seq=seq),
        grid=(batch, tiles),
        in_specs=specs,
        out_specs=pl.BlockSpec((T_ATT, GROUP_WIDTH), lambda b, j: (b * tiles + j, 0)),
        out_shape=jax.ShapeDtypeStruct((n, GROUP_WIDTH), BF16),
        scratch_shapes=scratch,
        compiler_params=pltpu.CompilerParams(dimension_semantics=("arbitrary", "arbitrary"),
                                             vmem_limit_bytes=VMEM_LIMIT),
        name="attn",
    )(q0, kv0, kv0, kv0, q1, kv1, kv1, kv1, q2, kv2, kv2, kv2)


def _split_dot(a, w_hi, w_lo):
    a_hi = a.astype(BF16)
    a_lo = (a - a_hi.astype(F32)).astype(BF16)
    return (jnp.dot(a_hi, w_hi, preferred_element_type=F32)
            + jnp.dot(a_lo, w_hi, preferred_element_type=F32)
            + jnp.dot(a_hi, w_lo, preferred_element_type=F32))


def _mix_kernel(x_ref, ya_ref, bg_ref, u_ref, up_ref, un_ref, gate_ref,
                wa_ref, wc_ref, wo_ref, cw_ref, g2_ref, wr_hi_ref, wr_lo_ref, br_ref,
                x1_ref, hflat_ref, route_ref, routet_ref, cnt_ref, base_ref, *, seq):
    i = pl.program_id(0)
    t0 = i * TM_MIX

    @pl.when(i == 0)
    def _():
        base_ref[...] = jnp.zeros_like(base_ref)

    u = u_ref[...].astype(F32)
    row = lax.broadcasted_iota(jnp.int32, (TM_MIX, CONV_WIDTH), 0)
    prev_row = jnp.where(t0 % seq == 0, 0.0, up_ref[15:16, :].astype(F32))
    next_row = jnp.where((t0 + TM_MIX) % seq == 0, 0.0, un_ref[0:1, :].astype(F32))
    u_prev = jnp.where(row == 0, prev_row, pltpu.roll(u, 1, axis=0))
    u_next = jnp.where(row == TM_MIX - 1, next_row, pltpu.roll(u, TM_MIX - 1, axis=0))
    conv = cw_ref[0:1, :] * u_prev + cw_ref[1:2, :] * u + cw_ref[2:3, :] * u_next
    yb_in = (bg_ref[...].astype(F32) * conv).astype(BF16)

    y_a = jnp.dot(ya_ref[...], wa_ref[...], preferred_element_type=F32)
    y_b = jnp.dot(yb_in, wc_ref[...], preferred_element_type=F32)
    merged = (gate_ref[:, 0:D_MODEL].astype(F32) * y_a
              + gate_ref[:, D_MODEL:].astype(F32) * y_b).astype(BF16)
    x1 = x_ref[...] + jnp.dot(merged, wo_ref[...], preferred_element_type=F32)
    x1_ref[...] = x1

    h2 = _rms(x1, g2_ref[...])
    for c in range(ROW_CHUNKS):
        hflat_ref[pl.ds(c, TM_MIX, stride=ROW_CHUNKS), :] = h2[:, c * LANES:(c + 1) * LANES]

    logits = _split_dot(h2, wr_hi_ref[...], wr_lo_ref[...]) + br_ref[...]
    lane = lax.broadcasted_iota(jnp.int32, (TM_MIX, LANES), 1)
    lane_f = lane.astype(F32)
    neg = -jnp.inf
    big = float(LANES)
    is_group = lane < N_EXPERT_GROUPS
    cm = jnp.where(is_group, logits, neg)
    cmax = jnp.max(cm, axis=1, keepdims=True)
    g_idx = jnp.min(jnp.where(cm == cmax, lane_f, big), axis=1, keepdims=True)
    p_group = 1.0 / jnp.sum(jnp.where(is_group, jnp.exp(logits - cmax), 0.0), axis=1, keepdims=True)
    f_lo = N_EXPERT_GROUPS + EXPERTS_PER_GROUP * g_idx
    in_group = (lane_f >= f_lo) & (lane_f < f_lo + EXPERTS_PER_GROUP)
    fm = jnp.where(in_group, logits, neg)
    f1 = jnp.max(fm, axis=1, keepdims=True)
    i1 = jnp.min(jnp.where(fm == f1, lane_f, big), axis=1, keepdims=True)
    fm2 = jnp.where(lane_f == i1, neg, fm)
    f2 = jnp.max(fm2, axis=1, keepdims=True)
    i2 = jnp.min(jnp.where(fm2 == f2, lane_f, big), axis=1, keepdims=True)
    e21 = jnp.exp(f2 - f1)
    w_1 = p_group / (1.0 + e21)
    w_2 = p_group * e21 / (1.0 + e21)
    e_1 = i1 - N_EXPERT_GROUPS
    e_2 = i2 - N_EXPERT_GROUPS

    onehot = jnp.where((lane_f == e_1) | (lane_f == e_2), 1.0, 0.0)
    r_i = lax.broadcasted_iota(jnp.int32, (TM_MIX, TM_MIX), 0)
    c_i = lax.broadcasted_iota(jnp.int32, (TM_MIX, TM_MIX), 1)
    tri = jnp.where(c_i < r_i, 1.0, 0.0).astype(BF16)
    before = jnp.dot(tri, onehot.astype(BF16), preferred_element_type=F32) + base_ref[0:1, :]
    rank_1 = jnp.sum(jnp.where(lane_f == e_1, before, 0.0), axis=1, keepdims=True)
    rank_2 = jnp.sum(jnp.where(lane_f == e_2, before, 0.0), axis=1, keepdims=True)
    base_ref[0:1, :] = base_ref[0:1, :] + jnp.sum(onehot, axis=0, keepdims=True)
    cnt_ref[...] = jnp.broadcast_to(base_ref[0:1, :], cnt_ref.shape)

    route = jnp.zeros((TM_MIX, LANES), F32)
    for k, val in enumerate((e_1, e_2, w_1, w_2, rank_1, rank_2)):
        route = jnp.where(lane == k, val, route)
    route_ref[...] = route
    routet_ref[...] = route.T[0:SUBLANES, :]


def _mix_call(x2, y_attn, bgate, u, gates, wa, wc, wo, conv_w, g2, wr_hi, wr_lo, br, seq):
    n = x2.shape[0]
    row = lambda i: (i, 0)
    const = lambda i: (0, 0)
    halo = 16
    per_tile = TM_MIX // halo
    last = n // halo - 1
    return pl.pallas_call(
        functools.partial(_mix_kernel, seq=seq),
        grid=(n // TM_MIX,),
        in_specs=[
            pl.BlockSpec((TM_MIX, D_MODEL), row),
            pl.BlockSpec((TM_MIX, GROUP_WIDTH), row),
            pl.BlockSpec((TM_MIX, CONV_WIDTH), row),
            pl.BlockSpec((TM_MIX, CONV_WIDTH), row),
            pl.BlockSpec((halo, CONV_WIDTH), lambda i: (jnp.maximum(i * per_tile - 1, 0), 0)),
            pl.BlockSpec((halo, CONV_WIDTH), lambda i: (jnp.minimum((i + 1) * per_tile, last), 0)),
            pl.BlockSpec((TM_MIX, 2 * D_MODEL), row),
            pl.BlockSpec((GROUP_WIDTH, D_MODEL), const),
            pl.BlockSpec((CONV_WIDTH, D_MODEL), const),
            pl.BlockSpec((D_MODEL, D_MODEL), const),
            pl.BlockSpec((3, CONV_WIDTH), const),
            pl.BlockSpec((1, D_MODEL), const),
            pl.BlockSpec((D_MODEL, LANES), const),
            pl.BlockSpec((D_MODEL, LANES), const),
            pl.BlockSpec((1, LANES), const),
        ],
        out_specs=[
            pl.BlockSpec((TM_MIX, D_MODEL), row),
            pl.BlockSpec((TM_MIX * ROW_CHUNKS, LANES), row),
            pl.BlockSpec((TM_MIX, LANES), row),
            pl.BlockSpec((SUBLANES, TM_MIX), lambda i: (0, i)),
            pl.BlockSpec((SUBLANES, LANES), const),
        ],
        out_shape=[
            jax.ShapeDtypeStruct((n, D_MODEL), F32),
            jax.ShapeDtypeStruct((n * ROW_CHUNKS, LANES), F32),
            jax.ShapeDtypeStruct((n, LANES), F32),
            jax.ShapeDtypeStruct((SUBLANES, n), F32),
            jax.ShapeDtypeStruct((SUBLANES, LANES), F32),
        ],
        scratch_shapes=[pltpu.VMEM((SUBLANES, LANES), F32)],
        compiler_params=pltpu.CompilerParams(dimension_semantics=("arbitrary",),
                                             vmem_limit_bytes=VMEM_LIMIT),
        name="mix",
    )(x2, y_attn, bgate, u, u, u, gates, wa, wc, wo, conv_w, g2, wr_hi, wr_lo, br)


def _row_gather(idx_ref, n_rows, src_hbm, dst, sem):
    def issue(pair, carry):
        for k in range(2):
            j = 2 * pair + k
            t = idx_ref[0, 0, j]
            pltpu.make_async_copy(src_hbm.at[pl.ds(pl.multiple_of(t * SUBLANES, SUBLANES), SUBLANES)],
                                  dst.at[pl.ds(pl.multiple_of(j * SUBLANES, SUBLANES), SUBLANES)],
                                  sem).start(priority=k)
        return carry
    lax.fori_loop(0, n_rows // 2, issue, 0, unroll=4)


def _row_gather_wait(n_rows, src_hbm, dst, sem):
    pltpu.make_async_copy(src_hbm.at[pl.ds(0, n_rows * SUBLANES)], dst, sem).wait()


def _rows_from_tiles(buf, first_row, n_rows):
    return jnp.concatenate(
        [buf[pl.ds(first_row * ROW_CHUNKS + c, n_rows, stride=ROW_CHUNKS), :] for c in range(ROW_CHUNKS)],
        axis=1)


def _dispatch_kernel(tend_ref, pos_ref, h_ref, xs_hbm, zbuf, sem, zsem, *, n_tiles):
    @pl.when(pl.program_id(0) == 0)
    def _():
        zbuf[...] = jnp.zeros_like(zbuf)
        n_used = tend_ref[N_EXPERTS - 1]

        def fill(t):
            dst = xs_hbm.at[pl.ds(pl.multiple_of(t * (TM_MOE * SUBLANES), SUBLANES), TM_MOE * SUBLANES)]
            return pltpu.make_async_copy(zbuf, dst, zsem)

        def jobs():
            for e in range(N_EXPERTS):
                first = tend_ref[e - 1] if e else 0
                yield tend_ref[e] > first, tend_ref[e] - 1
            for t in range(N_EXPERTS):
                yield n_used + t < n_tiles, n_used + t

        for cond, t in jobs():
            pl.when(cond)(lambda t=t: fill(t).start())
        for cond, t in jobs():
            pl.when(cond)(lambda t=t: fill(t).wait())

    def issue(j, carry):
        src = h_ref.at[pl.ds(pl.multiple_of(j * SUBLANES, SUBLANES), SUBLANES)]
        for k in range(2):
            p = pos_ref[0, 0, k * TM_DSP + j]
            pltpu.make_async_copy(src, xs_hbm.at[pl.ds(pl.multiple_of(p * SUBLANES, SUBLANES), SUBLANES)],
                                  sem.at[k]).start(priority=k)
        return carry
    lax.fori_loop(0, TM_DSP, issue, 0, unroll=8)
    for k in range(2):
        pltpu.make_async_copy(h_ref, xs_hbm.at[pl.ds(0, TM_DSP * SUBLANES)], sem.at[k]).wait()


def _dispatch_call(tile_end, pos_tiles, h_flat, n_tiles):
    n_steps = pos_tiles.shape[0]
    grid_spec = pltpu.PrefetchScalarGridSpec(
        num_scalar_prefetch=1,
        grid=(n_steps,),
        in_specs=[
            pl.BlockSpec((1, 1, 2 * TM_DSP), lambda i, te: (i, 0, 0), memory_space=pltpu.SMEM),
            pl.BlockSpec((TM_DSP * ROW_CHUNKS, LANES), lambda i, te: (i, 0)),
        ],
        out_specs=pl.BlockSpec(memory_space=pl.ANY),
        scratch_shapes=[pltpu.VMEM((TM_MOE * ROW_CHUNKS, LANES), F32),
                        pltpu.SemaphoreType.DMA((2,)),
                        pltpu.SemaphoreType.DMA(())],
    )
    return pl.pallas_call(
        functools.partial(_dispatch_kernel, n_tiles=n_tiles),
        grid_spec=grid_spec,
        out_shape=jax.ShapeDtypeStruct((n_tiles * TM_MOE * ROW_CHUNKS, LANES), F32),
        compiler_params=pltpu.CompilerParams(dimension_semantics=("arbitrary",),
                                             vmem_limit_bytes=VMEM_LIMIT),
        name="dispatch",
    )(tile_end, pos_tiles, h_flat)


def _expert_kernel(te_ref, nused_ref, xs_ref, w1_ref, w3_ref, w2_ref, y_ref, w1b, w3b, w2b):
    i = pl.program_id(0)

    @pl.when(i >= nused_ref[0])
    def _():
        y_ref[...] = jnp.zeros_like(y_ref)

    @pl.when(i < nused_ref[0])
    def _():
        @pl.when((i == 0) | (te_ref[i] != te_ref[jnp.maximum(i - 1, 0)]))
        def _():
            w1b[...] = w1_ref[...].astype(BF16)
            w3b[...] = w3_ref[...].astype(BF16)
            w2b[...] = w2_ref[...].astype(BF16)

        x = _rows_from_tiles(xs_ref, 0, TM_MOE).astype(BF16)
        a = jnp.dot(x, w1b[...], preferred_element_type=F32)
        b = jnp.dot(x, w3b[...], preferred_element_type=F32)
        hid = (a * jax.nn.sigmoid(a) * b).astype(BF16)
        y = jnp.dot(hid, w2b[...], preferred_element_type=F32)
        for c in range(ROW_CHUNKS):
            y_ref[pl.ds(c, TM_MOE, stride=ROW_CHUNKS), :] = y[:, c * LANES:(c + 1) * LANES]


def _expert_call(tile_expert, n_used, xs_flat, w1, w3, w2):
    n_tiles = tile_expert.shape[0]
    tile = lambda i, nu: jnp.minimum(i, nu[0] - 1)
    wspec = lambda shape: pl.BlockSpec((None,) + shape, lambda i, te, nu: (te[tile(i, nu)], 0, 0))
    grid_spec = pltpu.PrefetchScalarGridSpec(
        num_scalar_prefetch=2,
        grid=(n_tiles,),
        in_specs=[
            pl.BlockSpec((TM_MOE * ROW_CHUNKS, LANES), lambda i, te, nu: (tile(i, nu), 0)),
            wspec((D_MODEL, EXPERT_FF)),
            wspec((D_MODEL, EXPERT_FF)),
            wspec((EXPERT_FF, D_MODEL)),
        ],
        out_specs=pl.BlockSpec((TM_MOE * ROW_CHUNKS, LANES), lambda i, te, nu: (i, 0)),
        scratch_shapes=[pltpu.VMEM((D_MODEL, EXPERT_FF), BF16),
                        pltpu.VMEM((D_MODEL, EXPERT_FF), BF16),
                        pltpu.VMEM((EXPERT_FF, D_MODEL), BF16)],
    )
    return pl.pallas_call(
        _expert_kernel,
        grid_spec=grid_spec,
        out_shape=jax.ShapeDtypeStruct((n_tiles * TM_MOE * ROW_CHUNKS, LANES), F32),
        compiler_params=pltpu.CompilerParams(dimension_semantics=("arbitrary",),
                                             vmem_limit_bytes=VMEM_LIMIT),
        name="experts",
    )(tile_expert, n_used, xs_flat, w1, w3, w2)


def _combine_kernel(pos_ref, posn_ref, y_hbm, x1_ref, route_ref, g_ref, o_ref, buf0, buf1, sem):
    i = pl.program_id(0)
    n_steps = pl.num_programs(0)
    bufs = (buf0, buf1)

    @pl.when(i == 0)
    def _():
        _row_gather(pos_ref, 2 * TM_CMB, y_hbm, buf0, sem.at[0])

    for slot in range(2):
        @pl.when((i % 2 == slot) & (i + 1 < n_steps))
        def _(slot=slot):
            _row_gather(posn_ref, 2 * TM_CMB, y_hbm, bufs[1 - slot], sem.at[1 - slot])

    for slot in range(2):
        @pl.when(i % 2 == slot)
        def _(slot=slot):
            _row_gather_wait(2 * TM_CMB, y_hbm, bufs[slot], sem.at[slot])
            y_1 = _rows_from_tiles(bufs[slot], 0, TM_CMB)
            y_2 = _rows_from_tiles(bufs[slot], TM_CMB, TM_CMB)
            x = x1_ref[...] + route_ref[:, 2:3] * y_1 + route_ref[:, 3:4] * y_2
            o_ref[...] = _rms(x, g_ref[...])


def _combine_call(pos, y_flat, x1, route, g):
    n = x1.shape[0]
    n_steps = n // TM_CMB
    row = lambda i: (i, 0)
    return pl.pallas_call(
        _combine_kernel,
        grid=(n_steps,),
        in_specs=[
            pl.BlockSpec((1, 1, 2 * TM_CMB), lambda i: (i, 0, 0), memory_space=pltpu.SMEM),
            pl.BlockSpec((1, 1, 2 * TM_CMB), lambda i: (jnp.minimum(i + 1, n_steps - 1), 0, 0),
                         memory_space=pltpu.SMEM),
            pl.BlockSpec(memory_space=pl.ANY),
            pl.BlockSpec((TM_CMB, D_MODEL), row),
            pl.BlockSpec((TM_CMB, LANES), row),
            pl.BlockSpec((1, D_MODEL), lambda i: (0, 0)),
        ],
        out_specs=pl.BlockSpec((TM_CMB, D_MODEL), row),
        out_shape=jax.ShapeDtypeStruct((n, D_MODEL), F32),
        scratch_shapes=[pltpu.VMEM((2 * TM_CMB * ROW_CHUNKS, LANES), F32),
                        pltpu.VMEM((2 * TM_CMB * ROW_CHUNKS, LANES), F32),
                        pltpu.SemaphoreType.DMA((2,))],
        compiler_params=pltpu.CompilerParams(dimension_semantics=("arbitrary",),
                                             vmem_limit_bytes=VMEM_LIMIT),
        name="combine",
    )(pos, pos, y_flat, x1, route, g)


def _layer(x2, batch, seq, norm_mix_g, w_in, b_gate, conv_w, w_attn_out, w_conv_out, w_out, norm_ffn_g,
           w_route_group, b_route_group, w_route_expert, b_route_expert, w1, w3, w2, final_g):
    n = x2.shape[0]
    q0, kv0, q1, kv1, q2, kv2, bgate, u, gates = _proj_call(
        x2, norm_mix_g[None, :], w_in.astype(BF16), b_gate[None, :], batch, seq)
    y_attn = _attn_call(q0, kv0, q1, kv1, q2, kv2, batch, seq)

    n_route = N_EXPERT_GROUPS + N_EXPERTS
    w_route = jnp.pad(jnp.concatenate([w_route_group, w_route_expert], axis=1), ((0, 0), (0, LANES - n_route)))
    b_route = jnp.pad(jnp.concatenate([b_route_group, b_route_expert]), (0, LANES - n_route))[None, :]
    wr_hi = w_route.astype(BF16)
    wr_lo = (w_route - wr_hi.astype(F32)).astype(BF16)
    x1, h_flat, route, route_t, counts = _mix_call(
        x2, y_attn, bgate, u, gates, w_attn_out.astype(BF16), w_conv_out.astype(BF16), w_out.astype(BF16),
        conv_w, norm_ffn_g[None, :], wr_hi, wr_lo, b_route, seq)

    ids = jnp.arange(N_EXPERTS, dtype=jnp.int32)
    expert = route_t[0:2].astype(jnp.int32)
    rank = route_t[4:6].astype(jnp.int32)
    cnt = counts[0, :N_EXPERTS].astype(jnp.int32)
    tiles_e = (cnt + TM_MOE - 1) // TM_MOE
    tile_end = jnp.sum(jnp.where(ids[:, None] >= ids[None, :], tiles_e[None, :], 0), axis=1)
    tile_start = tile_end - tiles_e
    pos = jnp.sum(jnp.where(expert[..., None] == ids, tile_start * TM_MOE, 0), axis=-1) + rank
    n_tiles = (2 * n) // TM_MOE + N_EXPERTS
    tidx = jnp.arange(n_tiles, dtype=jnp.int32)
    tile_expert = jnp.minimum(jnp.sum((tile_end[None, :] <= tidx[:, None]).astype(jnp.int32), axis=1),
                              N_EXPERTS - 1)
    n_used = tile_end[-1:]

    def step_tiles(t):
        return pos.reshape(2, n // t, t).transpose(1, 0, 2).reshape(n // t, 1, 2 * t)

    xs_flat = _dispatch_call(tile_end, step_tiles(TM_DSP), h_flat, n_tiles)
    y_flat = _expert_call(tile_expert, n_used, xs_flat, w1, w3, w2)
    return _combine_call(step_tiles(TM_CMB), y_flat, x1, route, final_g[None, :])


def kernel(x, norm_mix_g, w_in, b_gate, conv_w, w_attn_out, w_conv_out, w_out, norm_ffn_g,
           w_route_group, b_route_group, w_route_expert, b_route_expert, w1, w3, w2, norm_final_g):
    batch, seq, d = x.shape
    depth = w_in.shape[0]
    assert d == D_MODEL and depth == 1 and seq % T_ATT == 0
    out = _layer(x.reshape(batch * seq, d), batch, seq, norm_mix_g[0], w_in[0], b_gate[0], conv_w[0],
                 w_attn_out[0], w_conv_out[0], w_out[0], norm_ffn_g[0], w_route_group[0], b_route_group[0],
                 w_route_expert[0], b_route_expert[0], w1[0], w3[0], w2[0], norm_final_g)
    return out.reshape(batch, seq, d)
```

```python
import functools

import numpy as np
import jax
import jax.numpy as jnp
from jax import lax
from jax.experimental import pallas as pl
from jax.experimental.pallas import tpu as pltpu

F32 = jnp.float32
BF16 = jnp.bfloat16

D_MODEL = 1024
HEAD_DIM = 64
HEADS_PER_GROUP = 4
DILATED_PATTERNS = ((128, 1), (512, 4), (2048, 16))
N_GROUPS_A = 3
N_HEADS_A = N_GROUPS_A * HEADS_PER_GROUP
ATTN_WIDTH = N_HEADS_A * HEAD_DIM
GROUP_WIDTH = HEADS_PER_GROUP * HEAD_DIM
ALIBI_SPAN = 8.0
MASK_VALUE = -1e30
CONV_WIDTH = 768
N_EXPERT_GROUPS = 4
EXPERTS_PER_GROUP = 8
N_EXPERTS = 32
EXPERT_FF = 512
RMS_EPS = 1e-6

HALF = 64
LANES = 128
SUBLANES = 8
ROW_CHUNKS = D_MODEL // LANES

COL_K = ATTN_WIDTH
COL_V = 2 * ATTN_WIDTH
COL_BG = 3 * ATTN_WIDTH
COL_CG = COL_BG + CONV_WIDTH
COL_XIN = COL_CG + CONV_WIDTH
COL_GATE = COL_XIN + CONV_WIDTH
IN_COLS = COL_GATE + 2 * D_MODEL

TM_PROJ = 512
T_ATT = 2048
QB = 128
KB = QB + 2 * HALF
ATT_UNROLL = 8
TM_MIX = 512
TM_MOE = 512
TM_CMB = 256

VMEM_LIMIT = 56 * 1024 * 1024


def _alibi_slopes():
    return np.array([2.0 ** (-ALIBI_SPAN * (i + 1) / N_HEADS_A) for i in range(N_HEADS_A)],
                    dtype=np.float32).reshape(N_GROUPS_A, HEADS_PER_GROUP)


def _rms(x, g):
    return x * lax.rsqrt(jnp.mean(x * x, axis=-1, keepdims=True) + RMS_EPS) * g


def _proj_kernel(x_ref, g_ref, w_ref, b_ref,
                 q0_ref, kv0_ref, q1_ref, kv1_ref, q2_ref, kv2_ref, bg_ref, u_ref, gate_ref, scr):
    h = _rms(x_ref[...], g_ref[...]).astype(BF16)

    def proj(c0, width):
        return jnp.dot(h, w_ref[:, c0:c0 + width], preferred_element_type=F32)

    qscale = HEAD_DIM ** -0.5
    q0_ref[...] = (proj(0, GROUP_WIDTH) * qscale).astype(BF16)
    kv0_ref[:, 0:GROUP_WIDTH] = proj(COL_K, GROUP_WIDTH).astype(BF16)
    kv0_ref[:, GROUP_WIDTH:] = proj(COL_V, GROUP_WIDTH).astype(BF16)

    for g, q_ref, kv_ref in ((1, q1_ref, kv1_ref), (2, q2_ref, kv2_ref)):
        d = DILATED_PATTERNS[g][1]
        n = TM_PROJ // d
        parts = (proj(g * GROUP_WIDTH, GROUP_WIDTH) * qscale,
                 proj(COL_K + g * GROUP_WIDTH, GROUP_WIDTH),
                 proj(COL_V + g * GROUP_WIDTH, GROUP_WIDTH))
        for i, part in enumerate(parts):
            for c in range(2):
                scr[2 * i + c] = part[:, c * LANES:(c + 1) * LANES]
        for r in range(d):
            rows = pl.ds(r, n, stride=d)
            q_ref[r] = jnp.concatenate([scr[c, rows, :] for c in range(2)], axis=1).astype(BF16)
            kv_ref[r] = jnp.concatenate([scr[c, rows, :] for c in range(2, 6)], axis=1).astype(BF16)

    bg_ref[...] = proj(COL_BG, CONV_WIDTH).astype(BF16)
    u_ref[...] = (proj(COL_CG, CONV_WIDTH) * proj(COL_XIN, CONV_WIDTH)).astype(BF16)
    for c in range(4):
        w = 2 * D_MODEL // 4
        z = proj(COL_GATE + c * w, w) + b_ref[:, c * w:(c + 1) * w]
        gate_ref[:, c * w:(c + 1) * w] = jax.nn.sigmoid(z).astype(BF16)


def _proj_call(x2, g, w_in, b_gate, batch, seq):
    n = x2.shape[0]
    steps_per_batch = seq // TM_PROJ
    d1, d2 = DILATED_PATTERNS[1][1], DILATED_PATTERNS[2][1]
    row = lambda i: (i, 0)
    res = lambda i: (i // steps_per_batch, 0, i % steps_per_batch, 0)
    const = lambda i: (0, 0)
    out_shape = [
        jax.ShapeDtypeStruct((n, GROUP_WIDTH), BF16),
        jax.ShapeDtypeStruct((n, 2 * GROUP_WIDTH), BF16),
        jax.ShapeDtypeStruct((batch, d1, seq // d1, GROUP_WIDTH), BF16),
        jax.ShapeDtypeStruct((batch, d1, seq // d1, 2 * GROUP_WIDTH), BF16),
        jax.ShapeDtypeStruct((batch, d2, seq // d2, GROUP_WIDTH), BF16),
        jax.ShapeDtypeStruct((batch, d2, seq // d2, 2 * GROUP_WIDTH), BF16),
        jax.ShapeDtypeStruct((n, CONV_WIDTH), BF16),
        jax.ShapeDtypeStruct((n, CONV_WIDTH), BF16),
        jax.ShapeDtypeStruct((n, 2 * D_MODEL), BF16),
    ]
    out_specs = [
        pl.BlockSpec((TM_PROJ, GROUP_WIDTH), row),
        pl.BlockSpec((TM_PROJ, 2 * GROUP_WIDTH), row),
        pl.BlockSpec((None, d1, TM_PROJ // d1, GROUP_WIDTH), res),
        pl.BlockSpec((None, d1, TM_PROJ // d1, 2 * GROUP_WIDTH), res),
        pl.BlockSpec((None, d2, TM_PROJ // d2, GROUP_WIDTH), res),
        pl.BlockSpec((None, d2, TM_PROJ // d2, 2 * GROUP_WIDTH), res),
        pl.BlockSpec((TM_PROJ, CONV_WIDTH), row),
        pl.BlockSpec((TM_PROJ, CONV_WIDTH), row),
        pl.BlockSpec((TM_PROJ, 2 * D_MODEL), row),
    ]
    return pl.pallas_call(
        _proj_kernel,
        grid=(n // TM_PROJ,),
        in_specs=[
            pl.BlockSpec((TM_PROJ, D_MODEL), row),
            pl.BlockSpec((1, D_MODEL), const),
            pl.BlockSpec((D_MODEL, IN_COLS), const),
            pl.BlockSpec((1, 2 * D_MODEL), const),
        ],
        out_specs=out_specs,
        out_shape=out_shape,
        scratch_shapes=[pltpu.VMEM((6, TM_PROJ, LANES), F32)],
        compiler_params=pltpu.CompilerParams(dimension_semantics=("arbitrary",),
                                             vmem_limit_bytes=VMEM_LIMIT),
        name="proj",
    )(x2, g, w_in, b_gate)


def _attn_sub_block(q_sub, kw, vw, bias_ref, g, lo, hi):
    assert KB == GROUP_WIDTH
    lane = lax.broadcasted_iota(jnp.int32, (QB, KB), 1)
    edge_ok = (lane >= lo) & (lane < hi)
    heads = [(lane >= h * HEAD_DIM) & (lane < (h + 1) * HEAD_DIM) for h in range(HEADS_PER_GROUP)]
    zero = jnp.zeros((), BF16)
    q_stack = jnp.concatenate([jnp.where(hm, q_sub, zero) for hm in heads], axis=0)
    s_all = lax.dot_general(q_stack, kw, (((1,), (1,)), ((), ())), preferred_element_type=F32)
    probs = []
    m_b = l_b = None
    for h, hm in enumerate(heads):
        s = s_all[h * QB:(h + 1) * QB] + bias_ref[g * HEADS_PER_GROUP + h]
        s = jnp.where(edge_ok, s, MASK_VALUE)
        m = jnp.max(s, axis=1, keepdims=True)
        p = jnp.exp(s - m)
        l = jnp.sum(p, axis=1, keepdims=True)
        probs.append(p.astype(BF16))
        m_b = jnp.broadcast_to(m, (QB, GROUP_WIDTH)) if m_b is None else jnp.where(hm, m, m_b)
        l_b = jnp.broadcast_to(l, (QB, GROUP_WIDTH)) if l_b is None else jnp.where(hm, l, l_b)
    o_all = jnp.dot(jnp.concatenate(probs, axis=0), vw, preferred_element_type=F32)
    acc = o_all[0:QB]
    for h in range(1, HEADS_PER_GROUP):
        acc = jnp.where(heads[h], o_all[h * QB:(h + 1) * QB], acc)
    return acc, m_b, l_b


def _attn_kernel(q0_ref, kv0_ref, kv0p_ref, kv0n_ref,
                 q1_ref, kv1_ref, kv1p_ref, kv1n_ref,
                 q2_ref, kv2_ref, kv2p_ref, kv2n_ref,
                 y_ref,
                 cat0, cat1, cat2, bias_ref, m_st, l_st, a_st, m_tmp, l_tmp, a_tmp, *, seq):
    j = pl.program_id(1)

    qi = lax.broadcasted_iota(jnp.int32, (QB, KB), 0)
    kc = lax.broadcasted_iota(jnp.int32, (QB, KB), 1)
    adelta = jnp.abs(kc - HALF - qi)
    band = adelta <= HALF
    slopes = _alibi_slopes()
    for g in range(N_GROUPS_A):
        dist = (adelta * DILATED_PATTERNS[g][1]).astype(F32)
        for h in range(HEADS_PER_GROUP):
            bias_ref[g * HEADS_PER_GROUP + h] = jnp.where(band, -(float(slopes[g, h]) * dist), MASK_VALUE)

    for cat, own, prv, nxt in ((cat0, kv0_ref, kv0p_ref, kv0n_ref),
                               (cat1, kv1_ref, kv1p_ref, kv1n_ref),
                               (cat2, kv2_ref, kv2p_ref, kv2n_ref)):
        n_own = own.shape[-2]
        cat[:, 0:HALF, :] = prv[...].reshape(cat.shape[0], HALF, 2 * GROUP_WIDTH)
        cat[:, HALF:HALF + n_own, :] = own[...].reshape(cat.shape[0], n_own, 2 * GROUP_WIDTH)
        cat[:, HALF + n_own:, :] = nxt[...].reshape(cat.shape[0], HALF, 2 * GROUP_WIDTH)

    def window(cat, r, sb):
        rows = pl.ds(pl.multiple_of(sb * QB, QB), KB)
        return cat[r, rows, 0:GROUP_WIDTH], cat[r, rows, GROUP_WIDTH:]

    def edges(g, n_res, sb):
        length = seq // DILATED_PATTERNS[g][1]
        i0 = j * n_res + sb * QB
        return jnp.maximum(0, HALF - i0), jnp.minimum(KB, length + HALF - i0)

    def body0(sb, carry):
        rows = pl.ds(pl.multiple_of(sb * QB, QB), QB)
        kw, vw = window(cat0, 0, sb)
        lo, hi = edges(0, T_ATT, sb)
        acc, m_b, l_b = _attn_sub_block(q0_ref[rows, :], kw, vw, bias_ref, 0, lo, hi)
        for c in range(2):
            cols = slice(c * LANES, (c + 1) * LANES)
            m_st[c, rows, :] = m_b[:, cols]
            l_st[c, rows, :] = l_b[:, cols]
            a_st[c, rows, :] = acc[:, cols]
        return carry

    lax.fori_loop(0, T_ATT // QB, body0, 0, unroll=ATT_UNROLL)

    for g, q_ref, cat in ((1, q1_ref, cat1), (2, q2_ref, cat2)):
        d = DILATED_PATTERNS[g][1]
        n_res = T_ATT // d
        sb_per_res = n_res // QB

        def body(idx, carry, g=g, q_ref=q_ref, cat=cat, n_res=n_res, sb_per_res=sb_per_res):
            r = idx // sb_per_res
            sb = idx % sb_per_res
            kw, vw = window(cat, r, sb)
            lo, hi = edges(g, n_res, sb)
            q_sub = q_ref[r, pl.ds(pl.multiple_of(sb * QB, QB), QB), :]
            acc, m_b, l_b = _attn_sub_block(q_sub, kw, vw, bias_ref, g, lo, hi)
            rows = pl.ds(pl.multiple_of(idx * QB, QB), QB)
            m_tmp[rows, :] = m_b
            l_tmp[rows, :] = l_b
            a_tmp[rows, :] = acc
            return carry

        lax.fori_loop(0, T_ATT // QB, body, 0, unroll=ATT_UNROLL)

        for r in range(d):
            for ch in range(sb_per_res):
                src = slice(r * n_res + ch * QB, r * n_res + (ch + 1) * QB)
                tok = pl.ds(ch * QB * d + r, QB, stride=d)
                for c in range(2):
                    cols = slice(c * LANES, (c + 1) * LANES)
                    m_new_part = m_tmp[src, cols]
                    m_old = m_st[c, tok, :]
                    m_new = jnp.maximum(m_old, m_new_part)
                    e_old = jnp.exp(m_old - m_new)
                    e_new = jnp.exp(m_new_part - m_new)
                    m_st[c, tok, :] = m_new
                    l_st[c, tok, :] = e_old * l_st[c, tok, :] + e_new * l_tmp[src, cols]
                    a_st[c, tok, :] = e_old * a_st[c, tok, :] + e_new * a_tmp[src, cols]

    for c in range(2):
        y_ref[:, c * LANES:(c + 1) * LANES] = (a_st[c] / l_st[c]).astype(BF16)


def _attn_call(q0, kv0, q1, kv1, q2, kv2, batch, seq):
    n = q0.shape[0]
    tiles = seq // T_ATT
    specs = []
    scratch = []
    blocks_per_tile = T_ATT // HALF
    n_half_blocks = n // HALF
    specs += [
        pl.BlockSpec((T_ATT, GROUP_WIDTH), lambda b, j: (b * tiles + j, 0)),
        pl.BlockSpec((T_ATT, 2 * GROUP_WIDTH), lambda b, j: (b * tiles + j, 0)),
        pl.BlockSpec((HALF, 2 * GROUP_WIDTH),
                     lambda b, j: (jnp.maximum((b * tiles + j) * blocks_per_tile - 1, 0), 0)),
        pl.BlockSpec((HALF, 2 * GROUP_WIDTH),
                     lambda b, j: (jnp.minimum((b * tiles + j + 1) * blocks_per_tile, n_half_blocks - 1), 0)),
    ]
    scratch.append(pltpu.VMEM((1, T_ATT + 2 * HALF, 2 * GROUP_WIDTH), BF16))
    for g in (1, 2):
        d = DILATED_PATTERNS[g][1]
        n_res = T_ATT // d
        per_tile = n_res // HALF
        last = seq // d // HALF - 1
        specs += [
            pl.BlockSpec((None, d, n_res, GROUP_WIDTH), lambda b, j: (b, 0, j, 0)),
            pl.BlockSpec((None, d, n_res, 2 * GROUP_WIDTH), lambda b, j: (b, 0, j, 0)),
            pl.BlockSpec((None, d, HALF, 2 * GROUP_WIDTH),
                         lambda b, j, per_tile=per_tile: (b, 0, jnp.maximum(j * per_tile - 1, 0), 0)),
            pl.BlockSpec((None, d, HALF, 2 * GROUP_WIDTH),
                         lambda b, j, per_tile=per_tile, last=last: (b, 0, jnp.minimum((j + 1) * per_tile, last), 0)),
        ]
        scratch.append(pltpu.VMEM((d, n_res + 2 * HALF, 2 * GROUP_WIDTH), BF16))
    scratch.append(pltpu.VMEM((N_HEADS_A, QB, KB), F32))
    scratch += [pltpu.VMEM((2, T_ATT, LANES), F32) for _ in range(3)]
    scratch += [pltpu.VMEM((T_ATT, GROUP_WIDTH), F32) for _ in range(3)]
    return pl.pallas_call(
        functools.partial(_attn_kernel, seq=seq),
        grid=(batch, tiles),
        in_specs=specs,
        out_specs=pl.BlockSpec((T_ATT, GROUP_WIDTH), lambda b, j: (b * tiles + j, 0)),
        out_shape=jax.ShapeDtypeStruct((n, GROUP_WIDTH), BF16),
        scratch_shapes=scratch,
        compiler_params=pltpu.CompilerParams(dimension_semantics=("arbitrary", "arbitrary"),
                                             vmem_limit_bytes=VMEM_LIMIT),
        name="attn",
    )(q0, kv0, kv0, kv0, q1, kv1, kv1, kv1, q2, kv2, kv2, kv2)


def _split_dot(a, w_cat, w_hi):
    a_hi = a.astype(BF16)
    a_lo = (a - a_hi.astype(F32)).astype(BF16)
    both = jnp.dot(a_hi, w_cat, preferred_element_type=F32)
    return both[:, 0:LANES] + both[:, LANES:] + jnp.dot(a_lo, w_hi, preferred_element_type=F32)


ST_BASE, ST_TILE, ST_FREE = 0, 1, 2
AL_NEW, AL_COUNT = 0, 1
RT_E, RT_W, RT_POS = 0, 2, 4


def _mix_kernel(x_ref, ya_ref, bg_ref, u_ref, up_ref, un_ref, gate_ref,
                wa_ref, wc_ref, wo_ref, cw_ref, g2_ref, wr_cat_ref, wr_hi_ref, br_ref,
                x1_ref, route_ref, routet_ref, alloc_ref, xs_hbm,
                st_ref, hbuf, posv, poss, psem, dsem, *, seq, trash_row):
    i = pl.program_id(0)
    last_step = pl.num_programs(0) - 1
    slot = i % 2
    t0 = i * TM_MIX

    def rows_done(sset):
        for k in range(2):
            pltpu.make_async_copy(hbuf.at[0], xs_hbm.at[pl.ds(0, TM_MIX * SUBLANES)], dsem.at[sset, k]).wait()

    def send_rows(src_slot, sset, live):
        def one(j, carry):
            src = hbuf.at[src_slot, pl.ds(pl.multiple_of(j * SUBLANES, SUBLANES), SUBLANES)]
            for k in range(2):
                p = jnp.where(live, poss[src_slot, RT_POS + k, j], trash_row + k * TM_MIX + j)
                dst = xs_hbm.at[pl.ds(pl.multiple_of(p * SUBLANES, SUBLANES), SUBLANES)]
                pltpu.make_async_copy(src, dst, dsem.at[sset, k]).start(priority=k)
            return carry
        lax.fori_loop(0, TM_MIX, one, 0, unroll=8)

    @pl.when(i == 0)
    def _():
        st_ref[...] = jnp.zeros_like(st_ref)
        hbuf[1] = jnp.zeros((TM_MIX * ROW_CHUNKS, LANES), F32)
        posv[...] = jnp.zeros_like(posv)
        init = pltpu.make_async_copy(posv, poss.at[1], psem)
        init.start()
        init.wait()

    @pl.when(i > 0)
    def _():
        rows_done(1 - slot)

    u = u_ref[...].astype(F32)
    row = lax.broadcasted_iota(jnp.int32, (TM_MIX, CONV_WIDTH), 0)
    prev_row = jnp.where(t0 % seq == 0, 0.0, up_ref[15:16, :].astype(F32))
    next_row = jnp.where((t0 + TM_MIX) % seq == 0, 0.0, un_ref[0:1, :].astype(F32))
    u_prev = jnp.where(row == 0, prev_row, pltpu.roll(u, 1, axis=0))
    u_next = jnp.where(row == TM_MIX - 1, next_row, pltpu.roll(u, TM_MIX - 1, axis=0))
    conv = cw_ref[0:1, :] * u_prev + cw_ref[1:2, :] * u + cw_ref[2:3, :] * u_next
    yb_in = (bg_ref[...].astype(F32) * conv).astype(BF16)

    y_a = jnp.dot(ya_ref[...], wa_ref[...], preferred_element_type=F32)
    y_b = jnp.dot(yb_in, wc_ref[...], preferred_element_type=F32)
    merged = gate_ref[:, 0:D_MODEL] * y_a.astype(BF16) + gate_ref[:, D_MODEL:] * y_b.astype(BF16)
    x1 = x_ref[...] + jnp.dot(merged, wo_ref[...], preferred_element_type=F32)
    x1_ref[...] = x1

    h2 = _rms(x1, g2_ref[...])
    for c in range(ROW_CHUNKS):
        hbuf[slot, pl.ds(c, TM_MIX, stride=ROW_CHUNKS), :] = h2[:, c * LANES:(c + 1) * LANES]

    send_rows(1 - slot, slot, i > 0)

    logits = _split_dot(h2, wr_cat_ref[...], wr_hi_ref[...]) + br_ref[...]
    lane = lax.broadcasted_iota(jnp.int32, (TM_MIX, LANES), 1)
    lane_f = lane.astype(F32)
    neg = -jnp.inf
    big = float(LANES)
    is_group = lane < N_EXPERT_GROUPS
    cm = jnp.where(is_group, logits, neg)
    cmax = jnp.max(cm, axis=1, keepdims=True)
    g_idx = jnp.min(jnp.where(cm == cmax, lane_f, big), axis=1, keepdims=True)
    p_group = 1.0 / jnp.sum(jnp.where(is_group, jnp.exp(logits - cmax), 0.0), axis=1, keepdims=True)
    f_lo = N_EXPERT_GROUPS + EXPERTS_PER_GROUP * g_idx
    in_group = (lane_f >= f_lo) & (lane_f < f_lo + EXPERTS_PER_GROUP)
    fm = jnp.where(in_group, logits, neg)
    f1 = jnp.max(fm, axis=1, keepdims=True)
    i1 = jnp.min(jnp.where(fm == f1, lane_f, big), axis=1, keepdims=True)
    fm2 = jnp.where(lane_f == i1, neg, fm)
    f2 = jnp.max(fm2, axis=1, keepdims=True)
    i2 = jnp.min(jnp.where(fm2 == f2, lane_f, big), axis=1, keepdims=True)
    e21 = jnp.exp(f2 - f1)
    w_1 = p_group / (1.0 + e21)
    w_2 = p_group * e21 / (1.0 + e21)
    e_1 = i1 - N_EXPERT_GROUPS
    e_2 = i2 - N_EXPERT_GROUPS

    onehot = jnp.where((lane_f == e_1) | (lane_f == e_2), 1.0, 0.0)
    r_i = lax.broadcasted_iota(jnp.int32, (TM_MIX, TM_MIX), 0)
    c_i = lax.broadcasted_iota(jnp.int32, (TM_MIX, TM_MIX), 1)
    tri = jnp.where(c_i < r_i, 1.0, 0.0).astype(BF16)
    base = st_ref[ST_BASE:ST_BASE + 1, :]
    before = jnp.dot(tri, onehot.astype(BF16), preferred_element_type=F32) + base

    tile_rows = float(TM_MOE)
    cur_tile = st_ref[ST_TILE:ST_TILE + 1, :]
    next_free = st_ref[ST_FREE:ST_FREE + 1, :]
    count = jnp.sum(onehot, axis=0, keepdims=True)
    slot0 = jnp.floor(base * (1.0 / tile_rows))
    partial = (base - slot0 * tile_rows) > 0.0
    slot_last = jnp.floor((base + count - 1.0) * (1.0 / tile_rows))
    n_new = jnp.where(count > 0.0, slot_last - slot0 + 1.0 - jnp.where(partial, 1.0, 0.0), 0.0)
    e_r = lax.broadcasted_iota(jnp.int32, (LANES, LANES), 0)
    e_c = lax.broadcasted_iota(jnp.int32, (LANES, LANES), 1)
    earlier = jnp.where(e_r < e_c, 1.0, 0.0).astype(BF16)
    new_before = jnp.dot(jnp.broadcast_to(n_new, (SUBLANES, LANES)).astype(BF16), earlier,
                         preferred_element_type=F32)[0:1, :]
    fresh = next_free + new_before - jnp.where(partial, 1.0, 0.0) - slot0
    partial_slot = jnp.where(partial, slot0, -1.0)

    def tile_of(slot_idx, fresh_v, cur_v, partial_v):
        return jnp.where(slot_idx == partial_v, cur_v, fresh_v + slot_idx)

    def pick(row_vec, e):
        return jnp.sum(jnp.where(lane_f == e, row_vec, 0.0), axis=1, keepdims=True)

    positions = []
    for e in (e_1, e_2):
        rank = pick(before, e)
        s = jnp.floor(rank * (1.0 / tile_rows))
        tile = tile_of(s, pick(fresh, e), pick(cur_tile, e), pick(partial_slot, e))
        positions.append(tile * tile_rows + (rank - s * tile_rows))

    st_ref[ST_BASE:ST_BASE + 1, :] = base + count
    st_ref[ST_TILE:ST_TILE + 1, :] = jnp.where(count > 0.0, tile_of(slot_last, fresh, cur_tile, partial_slot),
                                                cur_tile)
    st_ref[ST_FREE:ST_FREE + 1, :] = next_free + jnp.sum(n_new, axis=1, keepdims=True)
    rec_row = lax.broadcasted_iota(jnp.int32, (SUBLANES, LANES), 0)
    alloc_ref[...] = jnp.where(rec_row == AL_NEW, n_new, jnp.where(rec_row == AL_COUNT, base + count, 0.0))

    route = jnp.zeros((TM_MIX, LANES), F32)
    for k, val in enumerate((e_1, e_2, w_1, w_2, positions[0], positions[1])):
        route = jnp.where(lane == k, val, route)
    route_ref[...] = route
    route_t = route.T[0:SUBLANES, :]
    routet_ref[...] = route_t
    posv[...] = route_t.astype(jnp.int32)
    to_smem = pltpu.make_async_copy(posv, poss.at[slot], psem)
    to_smem.start()
    to_smem.wait()

    @pl.when(i == last_step)
    def _():
        rows_done(slot)
        send_rows(slot, 1 - slot, True)
        rows_done(1 - slot)


def _mix_call(x2, y_attn, bgate, u, gates, wa, wc, wo, conv_w, g2, wr_cat, wr_hi, br, seq, n_tiles):
    n = x2.shape[0]
    n_steps = n // TM_MIX
    row = lambda i: (i, 0)
    const = lambda i: (0, 0)
    halo = 16
    per_tile = TM_MIX // halo
    last = n // halo - 1
    xs_rows = n_tiles * TM_MOE + 2 * TM_MIX
    return pl.pallas_call(
        functools.partial(_mix_kernel, seq=seq, trash_row=n_tiles * TM_MOE),
        grid=(n_steps,),
        in_specs=[
            pl.BlockSpec((TM_MIX, D_MODEL), row),
            pl.BlockSpec((TM_MIX, GROUP_WIDTH), row),
            pl.BlockSpec((TM_MIX, CONV_WIDTH), row),
            pl.BlockSpec((TM_MIX, CONV_WIDTH), row),
            pl.BlockSpec((halo, CONV_WIDTH), lambda i: (jnp.maximum(i * per_tile - 1, 0), 0)),
            pl.BlockSpec((halo, CONV_WIDTH), lambda i: (jnp.minimum((i + 1) * per_tile, last), 0)),
            pl.BlockSpec((TM_MIX, 2 * D_MODEL), row),
            pl.BlockSpec((GROUP_WIDTH, D_MODEL), const),
            pl.BlockSpec((CONV_WIDTH, D_MODEL), const),
            pl.BlockSpec((D_MODEL, D_MODEL), const),
            pl.BlockSpec((3, CONV_WIDTH), const),
            pl.BlockSpec((1, D_MODEL), const),
            pl.BlockSpec((D_MODEL, 2 * LANES), const),
            pl.BlockSpec((D_MODEL, LANES), const),
            pl.BlockSpec((1, LANES), const),
        ],
        out_specs=[
            pl.BlockSpec((TM_MIX, D_MODEL), row),
            pl.BlockSpec((TM_MIX, LANES), row),
            pl.BlockSpec((SUBLANES, TM_MIX), lambda i: (0, i)),
            pl.BlockSpec((None, SUBLANES, LANES), lambda i: (i, 0, 0)),
            pl.BlockSpec(memory_space=pl.ANY),
        ],
        out_shape=[
            jax.ShapeDtypeStruct((n, D_MODEL), F32),
            jax.ShapeDtypeStruct((n, LANES), F32),
            jax.ShapeDtypeStruct((SUBLANES, n), F32),
            jax.ShapeDtypeStruct((n_steps, SUBLANES, LANES), F32),
            jax.ShapeDtypeStruct((xs_rows * ROW_CHUNKS, LANES), F32),
        ],
        scratch_shapes=[pltpu.VMEM((SUBLANES, LANES), F32),
                        pltpu.VMEM((2, TM_MIX * ROW_CHUNKS, LANES), F32),
                        pltpu.VMEM((SUBLANES, TM_MIX), jnp.int32),
                        pltpu.SMEM((2, SUBLANES, TM_MIX), jnp.int32),
                        pltpu.SemaphoreType.DMA(()),
                        pltpu.SemaphoreType.DMA((2, 2))],
        compiler_params=pltpu.CompilerParams(dimension_semantics=("arbitrary",),
                                             vmem_limit_bytes=VMEM_LIMIT),
        name="mix",
    )(x2, y_attn, bgate, u, u, u, gates, wa, wc, wo, conv_w, g2, wr_cat, wr_hi, br)


def _row_gather(idx_ref, n_rows, src_hbm, dst, sem):
    def issue(pair, carry):
        for k in range(2):
            j = 2 * pair + k
            t = idx_ref[0, 0, j]
            pltpu.make_async_copy(src_hbm.at[pl.ds(pl.multiple_of(t * SUBLANES, SUBLANES), SUBLANES)],
                                  dst.at[pl.ds(pl.multiple_of(j * SUBLANES, SUBLANES), SUBLANES)],
                                  sem).start(priority=k)
        return carry
    lax.fori_loop(0, n_rows // 2, issue, 0, unroll=4)


def _row_gather_wait(n_rows, src_hbm, dst, sem):
    pltpu.make_async_copy(src_hbm.at[pl.ds(0, n_rows * SUBLANES)], dst, sem).wait()


def _rows_from_tiles(buf, first_row, n_rows):
    return jnp.concatenate(
        [buf[pl.ds(first_row * ROW_CHUNKS + c, n_rows, stride=ROW_CHUNKS), :] for c in range(ROW_CHUNKS)],
        axis=1)


def _expert_kernel(order_ref, te_ref, valid_ref, nused_ref, xs_ref, w1_ref, w3_ref, w2_ref,
                   y_ref, w1b, w3b, w2b):
    i = pl.program_id(0)

    @pl.when(i >= nused_ref[0])
    def _():
        y_ref[...] = jnp.zeros_like(y_ref)

    @pl.when(i < nused_ref[0])
    def _():
        @pl.when((i == 0) | (te_ref[i] != te_ref[jnp.maximum(i - 1, 0)]))
        def _():
            w1b[...] = w1_ref[...].astype(BF16)
            w3b[...] = w3_ref[...].astype(BF16)
            w2b[...] = w2_ref[...].astype(BF16)

        row = lax.broadcasted_iota(jnp.int32, (TM_MOE, D_MODEL), 0)
        x = jnp.where(row < valid_ref[i], _rows_from_tiles(xs_ref, 0, TM_MOE), 0.0).astype(BF16)
        a = jnp.dot(x, w1b[...], preferred_element_type=F32)
        b = jnp.dot(x, w3b[...], preferred_element_type=F32)
        hid = (a * jax.nn.sigmoid(a) * b).astype(BF16)
        y = jnp.dot(hid, w2b[...], preferred_element_type=F32)
        for c in range(ROW_CHUNKS):
            y_ref[pl.ds(c, TM_MOE, stride=ROW_CHUNKS), :] = y[:, c * LANES:(c + 1) * LANES]


def _expert_call(order, tile_expert, tile_valid, n_used, xs_flat, w1, w3, w2):
    n_tiles = order.shape[0]
    last = lambda i, nu: jnp.minimum(i, nu[0] - 1)
    wspec = lambda shape: pl.BlockSpec(
        (None,) + shape, lambda i, od, te, tv, nu: (te[last(i, nu)], 0, 0))
    grid_spec = pltpu.PrefetchScalarGridSpec(
        num_scalar_prefetch=4,
        grid=(n_tiles,),
        in_specs=[
            pl.BlockSpec((TM_MOE * ROW_CHUNKS, LANES), lambda i, od, te, tv, nu: (od[last(i, nu)], 0)),
            wspec((D_MODEL, EXPERT_FF)),
            wspec((D_MODEL, EXPERT_FF)),
            wspec((EXPERT_FF, D_MODEL)),
        ],
        out_specs=pl.BlockSpec((TM_MOE * ROW_CHUNKS, LANES), lambda i, od, te, tv, nu: (od[i], 0)),
        scratch_shapes=[pltpu.VMEM((D_MODEL, EXPERT_FF), BF16),
                        pltpu.VMEM((D_MODEL, EXPERT_FF), BF16),
                        pltpu.VMEM((EXPERT_FF, D_MODEL), BF16)],
    )
    return pl.pallas_call(
        _expert_kernel,
        grid_spec=grid_spec,
        out_shape=jax.ShapeDtypeStruct((n_tiles * TM_MOE * ROW_CHUNKS, LANES), F32),
        compiler_params=pltpu.CompilerParams(dimension_semantics=("arbitrary",),
                                             vmem_limit_bytes=VMEM_LIMIT),
        name="experts",
    )(order, tile_expert, tile_valid, n_used, xs_flat, w1, w3, w2)


def _combine_kernel(pos_ref, posn_ref, y_hbm, x1_ref, route_ref, g_ref, o_ref, buf0, buf1, sem):
    i = pl.program_id(0)
    n_steps = pl.num_programs(0)
    bufs = (buf0, buf1)

    @pl.when(i == 0)
    def _():
        _row_gather(pos_ref, 2 * TM_CMB, y_hbm, buf0, sem.at[0])

    for slot in range(2):
        @pl.when((i % 2 == slot) & (i + 1 < n_steps))
        def _(slot=slot):
            _row_gather(posn_ref, 2 * TM_CMB, y_hbm, bufs[1 - slot], sem.at[1 - slot])

    for slot in range(2):
        @pl.when(i % 2 == slot)
        def _(slot=slot):
            _row_gather_wait(2 * TM_CMB, y_hbm, bufs[slot], sem.at[slot])
            y_1 = _rows_from_tiles(bufs[slot], 0, TM_CMB)
            y_2 = _rows_from_tiles(bufs[slot], TM_CMB, TM_CMB)
            x = x1_ref[...] + route_ref[:, 2:3] * y_1 + route_ref[:, 3:4] * y_2
            o_ref[...] = _rms(x, g_ref[...])


def _combine_call(pos, y_flat, x1, route, g):
    n = x1.shape[0]
    n_steps = n // TM_CMB
    row = lambda i: (i, 0)
    return pl.pallas_call(
        _combine_kernel,
        grid=(n_steps,),
        in_specs=[
            pl.BlockSpec((1, 1, 2 * TM_CMB), lambda i: (i, 0, 0), memory_space=pltpu.SMEM),
            pl.BlockSpec((1, 1, 2 * TM_CMB), lambda i: (jnp.minimum(i + 1, n_steps - 1), 0, 0),
                         memory_space=pltpu.SMEM),
            pl.BlockSpec(memory_space=pl.ANY),
            pl.BlockSpec((TM_CMB, D_MODEL), row),
            pl.BlockSpec((TM_CMB, LANES), row),
            pl.BlockSpec((1, D_MODEL), lambda i: (0, 0)),
        ],
        out_specs=pl.BlockSpec((TM_CMB, D_MODEL), row),
        out_shape=jax.ShapeDtypeStruct((n, D_MODEL), F32),
        scratch_shapes=[pltpu.VMEM((2 * TM_CMB * ROW_CHUNKS, LANES), F32),
                        pltpu.VMEM((2 * TM_CMB * ROW_CHUNKS, LANES), F32),
                        pltpu.SemaphoreType.DMA((2,))],
        compiler_params=pltpu.CompilerParams(dimension_semantics=("arbitrary",),
                                             vmem_limit_bytes=VMEM_LIMIT),
        name="combine",
    )(pos, pos, y_flat, x1, route, g)


def _layer(x2, batch, seq, norm_mix_g, w_in, b_gate, conv_w, w_attn_out, w_conv_out, w_out, norm_ffn_g,
           w_route_group, b_route_group, w_route_expert, b_route_expert, w1, w3, w2, final_g):
    n = x2.shape[0]
    q0, kv0, q1, kv1, q2, kv2, bgate, u, gates = _proj_call(
        x2, norm_mix_g[None, :], w_in.astype(BF16), b_gate[None, :], batch, seq)
    y_attn = _attn_call(q0, kv0, q1, kv1, q2, kv2, batch, seq)

    n_route = N_EXPERT_GROUPS + N_EXPERTS
    w_route = jnp.pad(jnp.concatenate([w_route_group, w_route_expert], axis=1), ((0, 0), (0, LANES - n_route)))
    b_route = jnp.pad(jnp.concatenate([b_route_group, b_route_expert]), (0, LANES - n_route))[None, :]
    wr_hi = w_route.astype(BF16)
    wr_lo = (w_route - wr_hi.astype(F32)).astype(BF16)
    n_tiles = (2 * n) // TM_MOE + N_EXPERTS
    x1, route, route_t, alloc, xs_flat = _mix_call(
        x2, y_attn, bgate, u, gates, w_attn_out.astype(BF16), w_conv_out.astype(BF16), w_out.astype(BF16),
        conv_w, norm_ffn_g[None, :], jnp.concatenate([wr_hi, wr_lo], axis=1), wr_hi, b_route, seq, n_tiles)

    i32 = jnp.int32
    taken = alloc[:, AL_NEW, :N_EXPERTS].astype(i32).reshape(-1)
    cnt = alloc[-1, AL_COUNT, :N_EXPERTS].astype(i32)
    k = jnp.arange(taken.shape[0], dtype=i32)
    running = jnp.sum(jnp.where(k[:, None] >= k[None, :], taken[None, :], 0), axis=1)
    n_used = running[-1:]
    tile = jnp.arange(n_tiles, dtype=i32)
    owner = jnp.sum((running[None, :] <= tile[:, None]).astype(i32), axis=1) % N_EXPERTS
    owner = jnp.where(tile < n_used[0], owner, N_EXPERTS)
    key = owner * n_tiles + tile
    place = jnp.sum((key[None, :] < key[:, None]).astype(i32), axis=1)
    nth = jnp.sum(((owner[None, :] == owner[:, None]) & (tile[None, :] < tile[:, None])).astype(i32), axis=1)
    cnt_of = jnp.sum(jnp.where(owner[:, None] == jnp.arange(N_EXPERTS, dtype=i32)[None, :], cnt[None, :], 0), axis=1)
    valid = jnp.clip(cnt_of - nth * TM_MOE, 0, TM_MOE)
    at = place[None, :] == tile[:, None]
    order = jnp.sum(jnp.where(at, tile[None, :], 0), axis=1)
    step_expert = jnp.minimum(jnp.sum(jnp.where(at, owner[None, :], 0), axis=1), N_EXPERTS - 1)
    step_valid = jnp.sum(jnp.where(at, valid[None, :], 0), axis=1)

    y_flat = _expert_call(order, step_expert, step_valid, n_used, xs_flat, w1, w3, w2)
    pos = route_t[RT_POS:RT_POS + 2].astype(i32)
    pos_tiles = pos.reshape(2, n // TM_CMB, TM_CMB).transpose(1, 0, 2).reshape(n // TM_CMB, 1, 2 * TM_CMB)
    return _combine_call(pos_tiles, y_flat, x1, route, final_g[None, :])


def kernel(x, norm_mix_g, w_in, b_gate, conv_w, w_attn_out, w_conv_out, w_out, norm_ffn_g,
           w_route_group, b_route_group, w_route_expert, b_route_expert, w1, w3, w2, norm_final_g):
    batch, seq, d = x.shape
    depth = w_in.shape[0]
    assert d == D_MODEL and depth == 1 and seq % T_ATT == 0
    out = _layer(x.reshape(batch * seq, d), batch, seq, norm_mix_g[0], w_in[0], b_gate[0], conv_w[0],
                 w_attn_out[0], w_conv_out[0], w_out[0], norm_ffn_g[0], w_route_group[0], b_route_group[0],
                 w_route_expert[0], b_route_expert[0], w1[0], w3[0], w2[0], norm_final_g)
    return out.reshape(batch, seq, d)
```

```python
import functools

import numpy as np
import jax
import jax.numpy as jnp
from jax import lax
from jax.experimental import pallas as pl
from jax.experimental.pallas import tpu as pltpu

F32 = jnp.float32
BF16 = jnp.bfloat16

D_MODEL = 1024
HEAD_DIM = 64
HEADS_PER_GROUP = 4
DILATED_PATTERNS = ((128, 1), (512, 4), (2048, 16))
N_GROUPS_A = 3
N_HEADS_A = N_GROUPS_A * HEADS_PER_GROUP
ATTN_WIDTH = N_HEADS_A * HEAD_DIM
GROUP_WIDTH = HEADS_PER_GROUP * HEAD_DIM
ALIBI_SPAN = 8.0
MASK_VALUE = -1e30
CONV_WIDTH = 768
N_EXPERT_GROUPS = 4
EXPERTS_PER_GROUP = 8
N_EXPERTS = 32
EXPERT_FF = 512
RMS_EPS = 1e-6

HALF = 64
LANES = 128
SUBLANES = 8
ROW_CHUNKS = D_MODEL // LANES

COL_K = ATTN_WIDTH
COL_V = 2 * ATTN_WIDTH
COL_BG = 3 * ATTN_WIDTH
COL_CG = COL_BG + CONV_WIDTH
COL_XIN = COL_CG + CONV_WIDTH
COL_GATE = COL_XIN + CONV_WIDTH
IN_COLS = COL_GATE + 2 * D_MODEL

TM_PROJ = 512
T_ATT = 2048
QB = 128
KB = QB + 2 * HALF
ATT_UNROLL = 8
TM_MIX = 512
TM_MOE = 512
TM_CMB = 256

VMEM_LIMIT = 56 * 1024 * 1024


def _alibi_slopes():
    return np.array([2.0 ** (-ALIBI_SPAN * (i + 1) / N_HEADS_A) for i in range(N_HEADS_A)],
                    dtype=np.float32).reshape(N_GROUPS_A, HEADS_PER_GROUP)


def _rms(x, g):
    return x * lax.rsqrt(jnp.mean(x * x, axis=-1, keepdims=True) + RMS_EPS) * g


def _proj_kernel(x_ref, g_ref, w_ref, b_ref,
                 q0_ref, kv0_ref, q1_ref, kv1_ref, q2_ref, kv2_ref, bg_ref, u_ref, gate_ref, scr):
    h = _rms(x_ref[...], g_ref[...]).astype(BF16)

    def proj(c0, width):
        return jnp.dot(h, w_ref[:, c0:c0 + width], preferred_element_type=F32)

    qscale = HEAD_DIM ** -0.5
    q0_ref[...] = (proj(0, GROUP_WIDTH) * qscale).astype(BF16)
    kv0_ref[:, 0:GROUP_WIDTH] = proj(COL_K, GROUP_WIDTH).astype(BF16)
    kv0_ref[:, GROUP_WIDTH:] = proj(COL_V, GROUP_WIDTH).astype(BF16)

    for g, q_ref, kv_ref in ((1, q1_ref, kv1_ref), (2, q2_ref, kv2_ref)):
        d = DILATED_PATTERNS[g][1]
        n = TM_PROJ // d
        parts = (proj(g * GROUP_WIDTH, GROUP_WIDTH) * qscale,
                 proj(COL_K + g * GROUP_WIDTH, GROUP_WIDTH),
                 proj(COL_V + g * GROUP_WIDTH, GROUP_WIDTH))
        for i, part in enumerate(parts):
            for c in range(2):
                scr[2 * i + c] = part[:, c * LANES:(c + 1) * LANES]
        for r in range(d):
            rows = pl.ds(r, n, stride=d)
            q_ref[r] = jnp.concatenate([scr[c, rows, :] for c in range(2)], axis=1).astype(BF16)
            kv_ref[r] = jnp.concatenate([scr[c, rows, :] for c in range(2, 6)], axis=1).astype(BF16)

    bg_ref[...] = proj(COL_BG, CONV_WIDTH).astype(BF16)
    u_ref[...] = (proj(COL_CG, CONV_WIDTH) * proj(COL_XIN, CONV_WIDTH)).astype(BF16)
    for c in range(4):
        w = 2 * D_MODEL // 4
        z = proj(COL_GATE + c * w, w) + b_ref[:, c * w:(c + 1) * w]
        gate_ref[:, c * w:(c + 1) * w] = jax.nn.sigmoid(z).astype(BF16)


def _proj_call(x2, g, w_in, b_gate, batch, seq):
    n = x2.shape[0]
    steps_per_batch = seq // TM_PROJ
    d1, d2 = DILATED_PATTERNS[1][1], DILATED_PATTERNS[2][1]
    row = lambda i: (i, 0)
    res = lambda i: (i // steps_per_batch, 0, i % steps_per_batch, 0)
    const = lambda i: (0, 0)
    out_shape = [
        jax.ShapeDtypeStruct((n, GROUP_WIDTH), BF16),
        jax.ShapeDtypeStruct((n, 2 * GROUP_WIDTH), BF16),
        jax.ShapeDtypeStruct((batch, d1, seq // d1, GROUP_WIDTH), BF16),
        jax.ShapeDtypeStruct((batch, d1, seq // d1, 2 * GROUP_WIDTH), BF16),
        jax.ShapeDtypeStruct((batch, d2, seq // d2, GROUP_WIDTH), BF16),
        jax.ShapeDtypeStruct((batch, d2, seq // d2, 2 * GROUP_WIDTH), BF16),
        jax.ShapeDtypeStruct((n, CONV_WIDTH), BF16),
        jax.ShapeDtypeStruct((n, CONV_WIDTH), BF16),
        jax.ShapeDtypeStruct((n, 2 * D_MODEL), BF16),
    ]
    out_specs = [
        pl.BlockSpec((TM_PROJ, GROUP_WIDTH), row),
        pl.BlockSpec((TM_PROJ, 2 * GROUP_WIDTH), row),
        pl.BlockSpec((None, d1, TM_PROJ // d1, GROUP_WIDTH), res),
        pl.BlockSpec((None, d1, TM_PROJ // d1, 2 * GROUP_WIDTH), res),
        pl.BlockSpec((None, d2, TM_PROJ // d2, GROUP_WIDTH), res),
        pl.BlockSpec((None, d2, TM_PROJ // d2, 2 * GROUP_WIDTH), res),
        pl.BlockSpec((TM_PROJ, CONV_WIDTH), row),
        pl.BlockSpec((TM_PROJ, CONV_WIDTH), row),
        pl.BlockSpec((TM_PROJ, 2 * D_MODEL), row),
    ]
    return pl.pallas_call(
        _proj_kernel,
        grid=(n // TM_PROJ,),
        in_specs=[
            pl.BlockSpec((TM_PROJ, D_MODEL), row),
            pl.BlockSpec((1, D_MODEL), const),
            pl.BlockSpec((D_MODEL, IN_COLS), const),
            pl.BlockSpec((1, 2 * D_MODEL), const),
        ],
        out_specs=out_specs,
        out_shape=out_shape,
        scratch_shapes=[pltpu.VMEM((6, TM_PROJ, LANES), F32)],
        compiler_params=pltpu.CompilerParams(dimension_semantics=("arbitrary",),
                                             vmem_limit_bytes=VMEM_LIMIT),
        name="proj",
    )(x2, g, w_in, b_gate)


def _attn_sub_block(q_sub, kw, vw, bias_ref, g, lo, hi):
    assert KB == GROUP_WIDTH
    lane = lax.broadcasted_iota(jnp.int32, (QB, KB), 1)
    edge_ok = (lane >= lo) & (lane < hi)
    heads = [(lane >= h * HEAD_DIM) & (lane < (h + 1) * HEAD_DIM) for h in range(HEADS_PER_GROUP)]
    zero = jnp.zeros((), BF16)
    q_stack = jnp.concatenate([jnp.where(hm, q_sub, zero) for hm in heads], axis=0)
    s_all = lax.dot_general(q_stack, kw, (((1,), (1,)), ((), ())), preferred_element_type=F32)
    probs = []
    m_b = l_b = None
    for h, hm in enumerate(heads):
        s = s_all[h * QB:(h + 1) * QB] + bias_ref[g * HEADS_PER_GROUP + h]
        s = jnp.where(edge_ok, s, MASK_VALUE)
        m = jnp.max(s, axis=1, keepdims=True)
        p = jnp.exp(s - m)
        l = jnp.sum(p, axis=1, keepdims=True)
        probs.append(p.astype(BF16))
        m_b = jnp.broadcast_to(m, (QB, GROUP_WIDTH)) if m_b is None else jnp.where(hm, m, m_b)
        l_b = jnp.broadcast_to(l, (QB, GROUP_WIDTH)) if l_b is None else jnp.where(hm, l, l_b)
    o_all = jnp.dot(jnp.concatenate(probs, axis=0), vw, preferred_element_type=F32)
    acc = o_all[0:QB]
    for h in range(1, HEADS_PER_GROUP):
        acc = jnp.where(heads[h], o_all[h * QB:(h + 1) * QB], acc)
    return acc, m_b, l_b


def _attn_kernel(q0_ref, kv0_ref, kv0p_ref, kv0n_ref,
                 q1_ref, kv1_ref, kv1p_ref, kv1n_ref,
                 q2_ref, kv2_ref, kv2p_ref, kv2n_ref,
                 y_ref,
                 cat0, cat1, cat2, bias_ref, m_st, l_st, a_st, m_tmp, l_tmp, a_tmp, *, seq):
    j = pl.program_id(1)

    qi = lax.broadcasted_iota(jnp.int32, (QB, KB), 0)
    kc = lax.broadcasted_iota(jnp.int32, (QB, KB), 1)
    adelta = jnp.abs(kc - HALF - qi)
    band = adelta <= HALF
    slopes = _alibi_slopes()
    for g in range(N_GROUPS_A):
        dist = (adelta * DILATED_PATTERNS[g][1]).astype(F32)
        for h in range(HEADS_PER_GROUP):
            bias_ref[g * HEADS_PER_GROUP + h] = jnp.where(band, -(float(slopes[g, h]) * dist), MASK_VALUE)

    for cat, own, prv, nxt in ((cat0, kv0_ref, kv0p_ref, kv0n_ref),
                               (cat1, kv1_ref, kv1p_ref, kv1n_ref),
                               (cat2, kv2_ref, kv2p_ref, kv2n_ref)):
        n_own = own.shape[-2]
        cat[:, 0:HALF, :] = prv[...].reshape(cat.shape[0], HALF, 2 * GROUP_WIDTH)
        cat[:, HALF:HALF + n_own, :] = own[...].reshape(cat.shape[0], n_own, 2 * GROUP_WIDTH)
        cat[:, HALF + n_own:, :] = nxt[...].reshape(cat.shape[0], HALF, 2 * GROUP_WIDTH)

    def window(cat, r, sb):
        rows = pl.ds(pl.multiple_of(sb * QB, QB), KB)
        return cat[r, rows, 0:GROUP_WIDTH], cat[r, rows, GROUP_WIDTH:]

    def edges(g, n_res, sb):
        length = seq // DILATED_PATTERNS[g][1]
        i0 = j * n_res + sb * QB
        return jnp.maximum(0, HALF - i0), jnp.minimum(KB, length + HALF - i0)

    def body0(sb, carry):
        rows = pl.ds(pl.multiple_of(sb * QB, QB), QB)
        kw, vw = window(cat0, 0, sb)
        lo, hi = edges(0, T_ATT, sb)
        acc, m_b, l_b = _attn_sub_block(q0_ref[rows, :], kw, vw, bias_ref, 0, lo, hi)
        for c in range(2):
            cols = slice(c * LANES, (c + 1) * LANES)
            m_st[c, rows, :] = m_b[:, cols]
            l_st[c, rows, :] = l_b[:, cols]
            a_st[c, rows, :] = acc[:, cols]
        return carry

    lax.fori_loop(0, T_ATT // QB, body0, 0, unroll=ATT_UNROLL)

    for g, q_ref, cat in ((1, q1_ref, cat1), (2, q2_ref, cat2)):
        d = DILATED_PATTERNS[g][1]
        n_res = T_ATT // d
        sb_per_res = n_res // QB

        def body(idx, carry, g=g, q_ref=q_ref, cat=cat, n_res=n_res, sb_per_res=sb_per_res):
            r = idx // sb_per_res
            sb = idx % sb_per_res
            kw, vw = window(cat, r, sb)
            lo, hi = edges(g, n_res, sb)
            q_sub = q_ref[r, pl.ds(pl.multiple_of(sb * QB, QB), QB), :]
            acc, m_b, l_b = _attn_sub_block(q_sub, kw, vw, bias_ref, g, lo, hi)
            rows = pl.ds(pl.multiple_of(idx * QB, QB), QB)
            m_tmp[rows, :] = m_b
            l_tmp[rows, :] = l_b
            a_tmp[rows, :] = acc
            return carry

        lax.fori_loop(0, T_ATT // QB, body, 0, unroll=ATT_UNROLL)

        for r in range(d):
            for ch in range(sb_per_res):
                src = slice(r * n_res + ch * QB, r * n_res + (ch + 1) * QB)
                tok = pl.ds(ch * QB * d + r, QB, stride=d)
                for c in range(2):
                    cols = slice(c * LANES, (c + 1) * LANES)
                    m_new_part = m_tmp[src, cols]
                    m_old = m_st[c, tok, :]
                    m_new = jnp.maximum(m_old, m_new_part)
                    e_old = jnp.exp(m_old - m_new)
                    e_new = jnp.exp(m_new_part - m_new)
                    m_st[c, tok, :] = m_new
                    l_st[c, tok, :] = e_old * l_st[c, tok, :] + e_new * l_tmp[src, cols]
                    a_st[c, tok, :] = e_old * a_st[c, tok, :] + e_new * a_tmp[src, cols]

    for c in range(2):
        y_ref[:, c * LANES:(c + 1) * LANES] = (a_st[c] / l_st[c]).astype(BF16)


def _attn_call(q0, kv0, q1, kv1, q2, kv2, batch, seq):
    n = q0.shape[0]
    tiles = seq // T_ATT
    specs = []
    scratch = []
    blocks_per_tile = T_ATT // HALF
    n_half_blocks = n // HALF
    specs += [
        pl.BlockSpec((T_ATT, GROUP_WIDTH), lambda b, j: (b * tiles + j, 0)),
        pl.BlockSpec((T_ATT, 2 * GROUP_WIDTH), lambda b, j: (b * tiles + j, 0)),
        pl.BlockSpec((HALF, 2 * GROUP_WIDTH),
                     lambda b, j: (jnp.maximum((b * tiles + j) * blocks_per_tile - 1, 0), 0)),
        pl.BlockSpec((HALF, 2 * GROUP_WIDTH),
                     lambda b, j: (jnp.minimum((b * tiles + j + 1) * blocks_per_tile, n_half_blocks - 1), 0)),
    ]
    scratch.append(pltpu.VMEM((1, T_ATT + 2 * HALF, 2 * GROUP_WIDTH), BF16))
    for g in (1, 2):
        d = DILATED_PATTERNS[g][1]
        n_res = T_ATT // d
        per_tile = n_res // HALF
        last = seq // d // HALF - 1
        specs += [
            pl.BlockSpec((None, d, n_res, GROUP_WIDTH), lambda b, j: (b, 0, j, 0)),
            pl.BlockSpec((None, d, n_res, 2 * GROUP_WIDTH), lambda b, j: (b, 0, j, 0)),
            pl.BlockSpec((None, d, HALF, 2 * GROUP_WIDTH),
                         lambda b, j, per_tile=per_tile: (b, 0, jnp.maximum(j * per_tile - 1, 0), 0)),
            pl.BlockSpec((None, d, HALF, 2 * GROUP_WIDTH),
                         lambda b, j, per_tile=per_tile, last=last: (b, 0, jnp.minimum((j + 1) * per_tile, last), 0)),
        ]
        scratch.append(pltpu.VMEM((d, n_res + 2 * HALF, 2 * GROUP_WIDTH), BF16))
    scratch.append(pltpu.VMEM((N_HEADS_A, QB, KB), F32))
    scratch += [pltpu.VMEM((2, T_ATT, LANES), F32) for _ in range(3)]
    scratch += [pltpu.VMEM((T_ATT, GROUP_WIDTH), F32) for _ in range(3)]
    return pl.pallas_call(
        functools.partial(_attn_kernel, seq=seq),
        grid=(batch, tiles),
        in_specs=specs,
        out_specs=pl.BlockSpec((T_ATT, GROUP_WIDTH), lambda b, j: (b * tiles + j, 0)),
        out_shape=jax.ShapeDtypeStruct((n, GROUP_WIDTH), BF16),
        scratch_shapes=scratch,
        compiler_params=pltpu.CompilerParams(dimension_semantics=("arbitrary", "arbitrary"),
                                             vmem_limit_bytes=VMEM_LIMIT),
        name="attn",
    )(q0, kv0, kv0, kv0, q1, kv1, kv1, kv1, q2, kv2, kv2, kv2)


def _split_dot(a, w_cat, w_hi):
    a_hi = a.astype(BF16)
    a_lo = (a - a_hi.astype(F32)).astype(BF16)
    both = jnp.dot(a_hi, w_cat, preferred_element_type=F32)
    return both[:, 0:LANES] + both[:, LANES:] + jnp.dot(a_lo, w_hi, preferred_element_type=F32)


ST_BASE, ST_TILE, ST_FREE = 0, 1, 2
AL_NEW, AL_COUNT = 0, 1
RT_E, RT_W, RT_POS = 0, 2, 4


def _mix_kernel(x_ref, ya_ref, bg_ref, u_ref, up_ref, un_ref, gate_ref,
                wa_ref, wc_ref, wo_ref, cw_ref, g2_ref, wr_cat_ref, wr_hi_ref, br_ref,
                x1_ref, route_ref, routet_ref, alloc_ref, xs_hbm,
                st_ref, hbuf, posv, poss, psem, dsem, *, seq, trash_row):
    i = pl.program_id(0)
    last_step = pl.num_programs(0) - 1
    slot = i % 2
    t0 = i * TM_MIX

    def rows_done(sset):
        for k in range(2):
            pltpu.make_async_copy(hbuf.at[0], xs_hbm.at[pl.ds(0, TM_MIX * SUBLANES)], dsem.at[sset, k]).wait()

    def send_row(src_slot, sset, live, j):
        src = hbuf.at[src_slot, pl.ds(pl.multiple_of(j * SUBLANES, SUBLANES), SUBLANES)]
        for k in range(2):
            p = jnp.where(live, poss[src_slot, RT_POS + k, j], trash_row + k * TM_MIX + j)
            dst = xs_hbm.at[pl.ds(pl.multiple_of(p * SUBLANES, SUBLANES), SUBLANES)]
            pltpu.make_async_copy(src, dst, dsem.at[sset, k]).start(priority=k)

    def send_rows(src_slot, sset, live):
        def one(j, carry):
            send_row(src_slot, sset, live, j)
            return carry
        lax.fori_loop(0, TM_MIX, one, 0, unroll=8)

    def send_previous_rows(part):
        for j in range(part * TM_MIX // 4, (part + 1) * TM_MIX // 4):
            send_row(1 - slot, slot, i > 0, j)

    @pl.when(i == 0)
    def _():
        st_ref[...] = jnp.zeros_like(st_ref)
        hbuf[1] = jnp.zeros((TM_MIX * ROW_CHUNKS, LANES), F32)
        posv[...] = jnp.zeros_like(posv)
        init = pltpu.make_async_copy(posv, poss.at[1], psem)
        init.start()
        init.wait()

    @pl.when(i > 0)
    def _():
        rows_done(1 - slot)

    u = u_ref[...].astype(F32)
    row = lax.broadcasted_iota(jnp.int32, (TM_MIX, CONV_WIDTH), 0)
    prev_row = jnp.where(t0 % seq == 0, 0.0, up_ref[15:16, :].astype(F32))
    next_row = jnp.where((t0 + TM_MIX) % seq == 0, 0.0, un_ref[0:1, :].astype(F32))
    u_prev = jnp.where(row == 0, prev_row, pltpu.roll(u, 1, axis=0))
    u_next = jnp.where(row == TM_MIX - 1, next_row, pltpu.roll(u, TM_MIX - 1, axis=0))
    conv = cw_ref[0:1, :] * u_prev + cw_ref[1:2, :] * u + cw_ref[2:3, :] * u_next
    yb_in = (bg_ref[...].astype(F32) * conv).astype(BF16)
    send_previous_rows(0)

    y_a = jnp.dot(ya_ref[...], wa_ref[...], preferred_element_type=F32)
    send_previous_rows(1)
    y_b = jnp.dot(yb_in, wc_ref[...], preferred_element_type=F32)
    send_previous_rows(2)
    merged = gate_ref[:, 0:D_MODEL] * y_a.astype(BF16) + gate_ref[:, D_MODEL:] * y_b.astype(BF16)
    x1 = x_ref[...] + jnp.dot(merged, wo_ref[...], preferred_element_type=F32)
    x1_ref[...] = x1
    send_previous_rows(3)

    h2 = _rms(x1, g2_ref[...])
    for c in range(ROW_CHUNKS):
        hbuf[slot, pl.ds(c, TM_MIX, stride=ROW_CHUNKS), :] = h2[:, c * LANES:(c + 1) * LANES]

    logits = _split_dot(h2, wr_cat_ref[...], wr_hi_ref[...]) + br_ref[...]
    lane = lax.broadcasted_iota(jnp.int32, (TM_MIX, LANES), 1)
    lane_f = lane.astype(F32)
    neg = -jnp.inf
    big = float(LANES)
    is_group = lane < N_EXPERT_GROUPS
    cm = jnp.where(is_group, logits, neg)
    cmax = jnp.max(cm, axis=1, keepdims=True)
    g_idx = jnp.min(jnp.where(cm == cmax, lane_f, big), axis=1, keepdims=True)
    p_group = 1.0 / jnp.sum(jnp.where(is_group, jnp.exp(logits - cmax), 0.0), axis=1, keepdims=True)
    f_lo = N_EXPERT_GROUPS + EXPERTS_PER_GROUP * g_idx
    in_group = (lane_f >= f_lo) & (lane_f < f_lo + EXPERTS_PER_GROUP)
    fm = jnp.where(in_group, logits, neg)
    f1 = jnp.max(fm, axis=1, keepdims=True)
    i1 = jnp.min(jnp.where(fm == f1, lane_f, big), axis=1, keepdims=True)
    fm2 = jnp.where(lane_f == i1, neg, fm)
    f2 = jnp.max(fm2, axis=1, keepdims=True)
    i2 = jnp.min(jnp.where(fm2 == f2, lane_f, big), axis=1, keepdims=True)
    e21 = jnp.exp(f2 - f1)
    w_1 = p_group / (1.0 + e21)
    w_2 = p_group * e21 / (1.0 + e21)
    e_1 = i1 - N_EXPERT_GROUPS
    e_2 = i2 - N_EXPERT_GROUPS

    onehot = jnp.where((lane_f == e_1) | (lane_f == e_2), 1.0, 0.0)
    r_i = lax.broadcasted_iota(jnp.int32, (TM_MIX, TM_MIX), 0)
    c_i = lax.broadcasted_iota(jnp.int32, (TM_MIX, TM_MIX), 1)
    tri = jnp.where(c_i < r_i, 1.0, 0.0).astype(BF16)
    base = st_ref[ST_BASE:ST_BASE + 1, :]
    before = jnp.dot(tri, onehot.astype(BF16), preferred_element_type=F32) + base

    tile_rows = float(TM_MOE)
    cur_tile = st_ref[ST_TILE:ST_TILE + 1, :]
    next_free = st_ref[ST_FREE:ST_FREE + 1, :]
    count = jnp.sum(onehot, axis=0, keepdims=True)
    slot0 = jnp.floor(base * (1.0 / tile_rows))
    partial = (base - slot0 * tile_rows) > 0.0
    slot_last = jnp.floor((base + count - 1.0) * (1.0 / tile_rows))
    n_new = jnp.where(count > 0.0, slot_last - slot0 + 1.0 - jnp.where(partial, 1.0, 0.0), 0.0)
    e_r = lax.broadcasted_iota(jnp.int32, (LANES, LANES), 0)
    e_c = lax.broadcasted_iota(jnp.int32, (LANES, LANES), 1)
    earlier = jnp.where(e_r < e_c, 1.0, 0.0).astype(BF16)
    new_before = jnp.dot(jnp.broadcast_to(n_new, (SUBLANES, LANES)).astype(BF16), earlier,
                         preferred_element_type=F32)[0:1, :]
    fresh = next_free + new_before - jnp.where(partial, 1.0, 0.0) - slot0
    partial_slot = jnp.where(partial, slot0, -1.0)

    def tile_of(slot_idx, fresh_v, cur_v, partial_v):
        return jnp.where(slot_idx == partial_v, cur_v, fresh_v + slot_idx)

    def pick(row_vec, e):
        return jnp.sum(jnp.where(lane_f == e, row_vec, 0.0), axis=1, keepdims=True)

    positions = []
    for e in (e_1, e_2):
        rank = pick(before, e)
        s = jnp.floor(rank * (1.0 / tile_rows))
        tile = tile_of(s, pick(fresh, e), pick(cur_tile, e), pick(partial_slot, e))
        positions.append(tile * tile_rows + (rank - s * tile_rows))

    st_ref[ST_BASE:ST_BASE + 1, :] = base + count
    st_ref[ST_TILE:ST_TILE + 1, :] = jnp.where(count > 0.0, tile_of(slot_last, fresh, cur_tile, partial_slot),
                                                cur_tile)
    st_ref[ST_FREE:ST_FREE + 1, :] = next_free + jnp.sum(n_new, axis=1, keepdims=True)
    rec_row = lax.broadcasted_iota(jnp.int32, (SUBLANES, LANES), 0)
    alloc_ref[...] = jnp.where(rec_row == AL_NEW, n_new, jnp.where(rec_row == AL_COUNT, base + count, 0.0))

    route = jnp.zeros((TM_MIX, LANES), F32)
    for k, val in enumerate((e_1, e_2, w_1, w_2, positions[0], positions[1])):
        route = jnp.where(lane == k, val, route)
    route_ref[...] = route
    route_t = route.T[0:SUBLANES, :]
    routet_ref[...] = route_t
    posv[...] = route_t.astype(jnp.int32)
    to_smem = pltpu.make_async_copy(posv, poss.at[slot], psem)
    to_smem.start()
    to_smem.wait()

    @pl.when(i == last_step)
    def _():
        rows_done(slot)
        send_rows(slot, 1 - slot, True)
        rows_done(1 - slot)


def _mix_call(x2, y_attn, bgate, u, gates, wa, wc, wo, conv_w, g2, wr_cat, wr_hi, br, seq, n_tiles):
    n = x2.shape[0]
    n_steps = n // TM_MIX
    row = lambda i: (i, 0)
    const = lambda i: (0, 0)
    halo = 16
    per_tile = TM_MIX // halo
    last = n // halo - 1
    xs_rows = n_tiles * TM_MOE + 2 * TM_MIX
    return pl.pallas_call(
        functools.partial(_mix_kernel, seq=seq, trash_row=n_tiles * TM_MOE),
        grid=(n_steps,),
        in_specs=[
            pl.BlockSpec((TM_MIX, D_MODEL), row),
            pl.BlockSpec((TM_MIX, GROUP_WIDTH), row),
            pl.BlockSpec((TM_MIX, CONV_WIDTH), row),
            pl.BlockSpec((TM_MIX, CONV_WIDTH), row),
            pl.BlockSpec((halo, CONV_WIDTH), lambda i: (jnp.maximum(i * per_tile - 1, 0), 0)),
            pl.BlockSpec((halo, CONV_WIDTH), lambda i: (jnp.minimum((i + 1) * per_tile, last), 0)),
            pl.BlockSpec((TM_MIX, 2 * D_MODEL), row),
            pl.BlockSpec((GROUP_WIDTH, D_MODEL), const),
            pl.BlockSpec((CONV_WIDTH, D_MODEL), const),
            pl.BlockSpec((D_MODEL, D_MODEL), const),
            pl.BlockSpec((3, CONV_WIDTH), const),
            pl.BlockSpec((1, D_MODEL), const),
            pl.BlockSpec((D_MODEL, 2 * LANES), const),
            pl.BlockSpec((D_MODEL, LANES), const),
            pl.BlockSpec((1, LANES), const),
        ],
        out_specs=[
            pl.BlockSpec((TM_MIX, D_MODEL), row),
            pl.BlockSpec((TM_MIX, LANES), row),
            pl.BlockSpec((SUBLANES, TM_MIX), lambda i: (0, i)),
            pl.BlockSpec((None, SUBLANES, LANES), lambda i: (i, 0, 0)),
            pl.BlockSpec(memory_space=pl.ANY),
        ],
        out_shape=[
            jax.ShapeDtypeStruct((n, D_MODEL), F32),
            jax.ShapeDtypeStruct((n, LANES), F32),
            jax.ShapeDtypeStruct((SUBLANES, n), F32),
            jax.ShapeDtypeStruct((n_steps, SUBLANES, LANES), F32),
            jax.ShapeDtypeStruct((xs_rows * ROW_CHUNKS, LANES), F32),
        ],
        scratch_shapes=[pltpu.VMEM((SUBLANES, LANES), F32),
                        pltpu.VMEM((2, TM_MIX * ROW_CHUNKS, LANES), F32),
                        pltpu.VMEM((SUBLANES, TM_MIX), jnp.int32),
                        pltpu.SMEM((2, SUBLANES, TM_MIX), jnp.int32),
                        pltpu.SemaphoreType.DMA(()),
                        pltpu.SemaphoreType.DMA((2, 2))],
        compiler_params=pltpu.CompilerParams(dimension_semantics=("arbitrary",),
                                             vmem_limit_bytes=VMEM_LIMIT),
        name="mix",
    )(x2, y_attn, bgate, u, u, u, gates, wa, wc, wo, conv_w, g2, wr_cat, wr_hi, br)


def _row_gather(idx_ref, n_rows, src_hbm, dst, sem):
    def issue(pair, carry):
        for k in range(2):
            j = 2 * pair + k
            t = idx_ref[0, 0, j]
            pltpu.make_async_copy(src_hbm.at[pl.ds(pl.multiple_of(t * SUBLANES, SUBLANES), SUBLANES)],
                                  dst.at[pl.ds(pl.multiple_of(j * SUBLANES, SUBLANES), SUBLANES)],
                                  sem).start(priority=k)
        return carry
    lax.fori_loop(0, n_rows // 2, issue, 0, unroll=4)


def _row_gather_wait(n_rows, src_hbm, dst, sem):
    pltpu.make_async_copy(src_hbm.at[pl.ds(0, n_rows * SUBLANES)], dst, sem).wait()


def _rows_from_tiles(buf, first_row, n_rows):
    return jnp.concatenate(
        [buf[pl.ds(first_row * ROW_CHUNKS + c, n_rows, stride=ROW_CHUNKS), :] for c in range(ROW_CHUNKS)],
        axis=1)


def _expert_kernel(order_ref, te_ref, valid_ref, nused_ref, xs_ref, w1_ref, w3_ref, w2_ref,
                   y_ref, w1b, w3b, w2b):
    i = pl.program_id(0)

    @pl.when(i >= nused_ref[0])
    def _():
        y_ref[...] = jnp.zeros_like(y_ref)

    @pl.when(i < nused_ref[0])
    def _():
        @pl.when((i == 0) | (te_ref[i] != te_ref[jnp.maximum(i - 1, 0)]))
        def _():
            w1b[...] = w1_ref[...].astype(BF16)
            w3b[...] = w3_ref[...].astype(BF16)
            w2b[...] = w2_ref[...].astype(BF16)

        row = lax.broadcasted_iota(jnp.int32, (TM_MOE, D_MODEL), 0)
        x = jnp.where(row < valid_ref[i], _rows_from_tiles(xs_ref, 0, TM_MOE), 0.0).astype(BF16)
        a = jnp.dot(x, w1b[...], preferred_element_type=F32)
        b = jnp.dot(x, w3b[...], preferred_element_type=F32)
        hid = (a * jax.nn.sigmoid(a) * b).astype(BF16)
        y = jnp.dot(hid, w2b[...], preferred_element_type=F32)
        for c in range(ROW_CHUNKS):
            y_ref[pl.ds(c, TM_MOE, stride=ROW_CHUNKS), :] = y[:, c * LANES:(c + 1) * LANES]


def _expert_call(order, tile_expert, tile_valid, n_used, xs_flat, w1, w3, w2):
    n_tiles = order.shape[0]
    last = lambda i, nu: jnp.minimum(i, nu[0] - 1)
    wspec = lambda shape: pl.BlockSpec(
        (None,) + shape, lambda i, od, te, tv, nu: (te[last(i, nu)], 0, 0))
    grid_spec = pltpu.PrefetchScalarGridSpec(
        num_scalar_prefetch=4,
        grid=(n_tiles,),
        in_specs=[
            pl.BlockSpec((TM_MOE * ROW_CHUNKS, LANES), lambda i, od, te, tv, nu: (od[last(i, nu)], 0)),
            wspec((D_MODEL, EXPERT_FF)),
            wspec((D_MODEL, EXPERT_FF)),
            wspec((EXPERT_FF, D_MODEL)),
        ],
        out_specs=pl.BlockSpec((TM_MOE * ROW_CHUNKS, LANES), lambda i, od, te, tv, nu: (od[i], 0)),
        scratch_shapes=[pltpu.VMEM((D_MODEL, EXPERT_FF), BF16),
                        pltpu.VMEM((D_MODEL, EXPERT_FF), BF16),
                        pltpu.VMEM((EXPERT_FF, D_MODEL), BF16)],
    )
    return pl.pallas_call(
        _expert_kernel,
        grid_spec=grid_spec,
        out_shape=jax.ShapeDtypeStruct((n_tiles * TM_MOE * ROW_CHUNKS, LANES), F32),
        compiler_params=pltpu.CompilerParams(dimension_semantics=("arbitrary",),
                                             vmem_limit_bytes=VMEM_LIMIT),
        name="experts",
    )(order, tile_expert, tile_valid, n_used, xs_flat, w1, w3, w2)


def _combine_kernel(pos_ref, posn_ref, y_hbm, x1_ref, route_ref, g_ref, o_ref, buf0, buf1, sem):
    i = pl.program_id(0)
    n_steps = pl.num_programs(0)
    bufs = (buf0, buf1)

    @pl.when(i == 0)
    def _():
        _row_gather(pos_ref, 2 * TM_CMB, y_hbm, buf0, sem.at[0])

    for slot in range(2):
        @pl.when((i % 2 == slot) & (i + 1 < n_steps))
        def _(slot=slot):
            _row_gather(posn_ref, 2 * TM_CMB, y_hbm, bufs[1 - slot], sem.at[1 - slot])

    for slot in range(2):
        @pl.when(i % 2 == slot)
        def _(slot=slot):
            _row_gather_wait(2 * TM_CMB, y_hbm, bufs[slot], sem.at[slot])
            y_1 = _rows_from_tiles(bufs[slot], 0, TM_CMB)
            y_2 = _rows_from_tiles(bufs[slot], TM_CMB, TM_CMB)
            x = x1_ref[...] + route_ref[:, 2:3] * y_1 + route_ref[:, 3:4] * y_2
            o_ref[...] = _rms(x, g_ref[...])


def _combine_call(pos, y_flat, x1, route, g):
    n = x1.shape[0]
    n_steps = n // TM_CMB
    row = lambda i: (i, 0)
    return pl.pallas_call(
        _combine_kernel,
        grid=(n_steps,),
        in_specs=[
            pl.BlockSpec((1, 1, 2 * TM_CMB), lambda i: (i, 0, 0), memory_space=pltpu.SMEM),
            pl.BlockSpec((1, 1, 2 * TM_CMB), lambda i: (jnp.minimum(i + 1, n_steps - 1), 0, 0),
                         memory_space=pltpu.SMEM),
            pl.BlockSpec(memory_space=pl.ANY),
            pl.BlockSpec((TM_CMB, D_MODEL), row),
            pl.BlockSpec((TM_CMB, LANES), row),
            pl.BlockSpec((1, D_MODEL), lambda i: (0, 0)),
        ],
        out_specs=pl.BlockSpec((TM_CMB, D_MODEL), row),
        out_shape=jax.ShapeDtypeStruct((n, D_MODEL), F32),
        scratch_shapes=[pltpu.VMEM((2 * TM_CMB * ROW_CHUNKS, LANES), F32),
                        pltpu.VMEM((2 * TM_CMB * ROW_CHUNKS, LANES), F32),
                        pltpu.SemaphoreType.DMA((2,))],
        compiler_params=pltpu.CompilerParams(dimension_semantics=("arbitrary",),
                                             vmem_limit_bytes=VMEM_LIMIT),
        name="combine",
    )(pos, pos, y_flat, x1, route, g)


def _layer(x2, batch, seq, norm_mix_g, w_in, b_gate, conv_w, w_attn_out, w_conv_out, w_out, norm_ffn_g,
           w_route_group, b_route_group, w_route_expert, b_route_expert, w1, w3, w2, final_g):
    n = x2.shape[0]
    q0, kv0, q1, kv1, q2, kv2, bgate, u, gates = _proj_call(
        x2, norm_mix_g[None, :], w_in.astype(BF16), b_gate[None, :], batch, seq)
    y_attn = _attn_call(q0, kv0, q1, kv1, q2, kv2, batch, seq)

    n_route = N_EXPERT_GROUPS + N_EXPERTS
    w_route = jnp.pad(jnp.concatenate([w_route_group, w_route_expert], axis=1), ((0, 0), (0, LANES - n_route)))
    b_route = jnp.pad(jnp.concatenate([b_route_group, b_route_expert]), (0, LANES - n_route))[None, :]
    wr_hi = w_route.astype(BF16)
    wr_lo = (w_route - wr_hi.astype(F32)).astype(BF16)
    n_tiles = (2 * n) // TM_MOE + N_EXPERTS
    x1, route, route_t, alloc, xs_flat = _mix_call(
        x2, y_attn, bgate, u, gates, w_attn_out.astype(BF16), w_conv_out.astype(BF16), w_out.astype(BF16),
        conv_w, norm_ffn_g[None, :], jnp.concatenate([wr_hi, wr_lo], axis=1), wr_hi, b_route, seq, n_tiles)

    i32 = jnp.int32
    taken = alloc[:, AL_NEW, :N_EXPERTS].astype(i32).reshape(-1)
    cnt = alloc[-1, AL_COUNT, :N_EXPERTS].astype(i32)
    k = jnp.arange(taken.shape[0], dtype=i32)
    running = jnp.sum(jnp.where(k[:, None] >= k[None, :], taken[None, :], 0), axis=1)
    n_used = running[-1:]
    tile = jnp.arange(n_tiles, dtype=i32)
    owner = jnp.sum((running[None, :] <= tile[:, None]).astype(i32), axis=1) % N_EXPERTS
    owner = jnp.where(tile < n_used[0], owner, N_EXPERTS)
    key = owner * n_tiles + tile
    place = jnp.sum((key[None, :] < key[:, None]).astype(i32), axis=1)
    nth = jnp.sum(((owner[None, :] == owner[:, None]) & (tile[None, :] < tile[:, None])).astype(i32), axis=1)
    cnt_of = jnp.sum(jnp.where(owner[:, None] == jnp.arange(N_EXPERTS, dtype=i32)[None, :], cnt[None, :], 0), axis=1)
    valid = jnp.clip(cnt_of - nth * TM_MOE, 0, TM_MOE)
    at = place[None, :] == tile[:, None]
    order = jnp.sum(jnp.where(at, tile[None, :], 0), axis=1)
    step_expert = jnp.minimum(jnp.sum(jnp.where(at, owner[None, :], 0), axis=1), N_EXPERTS - 1)
    step_valid = jnp.sum(jnp.where(at, valid[None, :], 0), axis=1)

    y_flat = _expert_call(order, step_expert, step_valid, n_used, xs_flat, w1, w3, w2)
    pos = route_t[RT_POS:RT_POS + 2].astype(i32)
    pos_tiles = pos.reshape(2, n // TM_CMB, TM_CMB).transpose(1, 0, 2).reshape(n // TM_CMB, 1, 2 * TM_CMB)
    return _combine_call(pos_tiles, y_flat, x1, route, final_g[None, :])


def kernel(x, norm_mix_g, w_in, b_gate, conv_w, w_attn_out, w_conv_out, w_out, norm_ffn_g,
           w_route_group, b_route_group, w_route_expert, b_route_expert, w1, w3, w2, norm_final_g):
    batch, seq, d = x.shape
    depth = w_in.shape[0]
    assert d == D_MODEL and depth == 1 and seq % T_ATT == 0
    out = _layer(x.reshape(batch * seq, d), batch, seq, norm_mix_g[0], w_in[0], b_gate[0], conv_w[0],
                 w_attn_out[0], w_conv_out[0], w_out[0], norm_ffn_g[0], w_route_group[0], b_route_group[0],
                 w_route_expert[0], b_route_expert[0], w1[0], w3[0], w2[0], norm_final_g)
    return out.reshape(batch, seq, d)
```

```python
import functools

import numpy as np
import jax
import jax.numpy as jnp
from jax import lax
from jax.experimental import pallas as pl
from jax.experimental.pallas import tpu as pltpu

F32 = jnp.float32
BF16 = jnp.bfloat16

D_MODEL = 1024
HEAD_DIM = 64
HEADS_PER_GROUP = 4
DILATED_PATTERNS = ((128, 1), (512, 4), (2048, 16))
N_GROUPS_A = 3
N_HEADS_A = N_GROUPS_A * HEADS_PER_GROUP
ATTN_WIDTH = N_HEADS_A * HEAD_DIM
GROUP_WIDTH = HEADS_PER_GROUP * HEAD_DIM
ALIBI_SPAN = 8.0
MASK_VALUE = -1e30
CONV_WIDTH = 768
N_EXPERT_GROUPS = 4
EXPERTS_PER_GROUP = 8
N_EXPERTS = 32
EXPERT_FF = 512
RMS_EPS = 1e-6

HALF = 64
LANES = 128
SUBLANES = 8
ROW_CHUNKS = D_MODEL // LANES

COL_K = ATTN_WIDTH
COL_V = 2 * ATTN_WIDTH
COL_BG = 3 * ATTN_WIDTH
COL_CG = COL_BG + CONV_WIDTH
COL_XIN = COL_CG + CONV_WIDTH
COL_GATE = COL_XIN + CONV_WIDTH
IN_COLS = COL_GATE + 2 * D_MODEL

TM_PROJ = 512
T_ATT = 2048
QB = 128
KB = QB + 2 * HALF
ATT_UNROLL = 8
TM_MIX = 512
TM_MOE = 512
TM_CMB = 256

VMEM_LIMIT = 56 * 1024 * 1024


def _alibi_slopes():
    return np.array([2.0 ** (-ALIBI_SPAN * (i + 1) / N_HEADS_A) for i in range(N_HEADS_A)],
                    dtype=np.float32).reshape(N_GROUPS_A, HEADS_PER_GROUP)


def _rms(x, g):
    return x * lax.rsqrt(jnp.mean(x * x, axis=-1, keepdims=True) + RMS_EPS) * g


def _proj_kernel(x_ref, g_ref, w_ref, b_ref,
                 q0_ref, kv0_ref, q1_ref, kv1_ref, q2_ref, kv2_ref, bg_ref, u_ref, gate_ref, scr):
    h = _rms(x_ref[...], g_ref[...]).astype(BF16)

    def proj(c0, width):
        return jnp.dot(h, w_ref[:, c0:c0 + width], preferred_element_type=F32)

    qscale = HEAD_DIM ** -0.5
    q0_ref[...] = (proj(0, GROUP_WIDTH) * qscale).astype(BF16)
    kv0_ref[:, 0:GROUP_WIDTH] = proj(COL_K, GROUP_WIDTH).astype(BF16)
    kv0_ref[:, GROUP_WIDTH:] = proj(COL_V, GROUP_WIDTH).astype(BF16)

    for g, q_ref, kv_ref in ((1, q1_ref, kv1_ref), (2, q2_ref, kv2_ref)):
        d = DILATED_PATTERNS[g][1]
        n = TM_PROJ // d
        parts = (proj(g * GROUP_WIDTH, GROUP_WIDTH) * qscale,
                 proj(COL_K + g * GROUP_WIDTH, GROUP_WIDTH),
                 proj(COL_V + g * GROUP_WIDTH, GROUP_WIDTH))
        for i, part in enumerate(parts):
            for c in range(2):
                scr[2 * i + c] = part[:, c * LANES:(c + 1) * LANES]
        for r in range(d):
            rows = pl.ds(r, n, stride=d)
            q_ref[r] = jnp.concatenate([scr[c, rows, :] for c in range(2)], axis=1).astype(BF16)
            kv_ref[r] = jnp.concatenate([scr[c, rows, :] for c in range(2, 6)], axis=1).astype(BF16)

    bg_ref[...] = proj(COL_BG, CONV_WIDTH).astype(BF16)
    u_ref[...] = (proj(COL_CG, CONV_WIDTH) * proj(COL_XIN, CONV_WIDTH)).astype(BF16)
    for c in range(4):
        w = 2 * D_MODEL // 4
        z = proj(COL_GATE + c * w, w) + b_ref[:, c * w:(c + 1) * w]
        gate_ref[:, c * w:(c + 1) * w] = jax.nn.sigmoid(z).astype(BF16)


def _proj_call(x2, g, w_in, b_gate, batch, seq):
    n = x2.shape[0]
    steps_per_batch = seq // TM_PROJ
    d1, d2 = DILATED_PATTERNS[1][1], DILATED_PATTERNS[2][1]
    row = lambda i: (i, 0)
    res = lambda i: (i // steps_per_batch, 0, i % steps_per_batch, 0)
    const = lambda i: (0, 0)
    out_shape = [
        jax.ShapeDtypeStruct((n, GROUP_WIDTH), BF16),
        jax.ShapeDtypeStruct((n, 2 * GROUP_WIDTH), BF16),
        jax.ShapeDtypeStruct((batch, d1, seq // d1, GROUP_WIDTH), BF16),
        jax.ShapeDtypeStruct((batch, d1, seq // d1, 2 * GROUP_WIDTH), BF16),
        jax.ShapeDtypeStruct((batch, d2, seq // d2, GROUP_WIDTH), BF16),
        jax.ShapeDtypeStruct((batch, d2, seq // d2, 2 * GROUP_WIDTH), BF16),
        jax.ShapeDtypeStruct((n, CONV_WIDTH), BF16),
        jax.ShapeDtypeStruct((n, CONV_WIDTH), BF16),
        jax.ShapeDtypeStruct((n, 2 * D_MODEL), BF16),
    ]
    out_specs = [
        pl.BlockSpec((TM_PROJ, GROUP_WIDTH), row),
        pl.BlockSpec((TM_PROJ, 2 * GROUP_WIDTH), row),
        pl.BlockSpec((None, d1, TM_PROJ // d1, GROUP_WIDTH), res),
        pl.BlockSpec((None, d1, TM_PROJ // d1, 2 * GROUP_WIDTH), res),
        pl.BlockSpec((None, d2, TM_PROJ // d2, GROUP_WIDTH), res),
        pl.BlockSpec((None, d2, TM_PROJ // d2, 2 * GROUP_WIDTH), res),
        pl.BlockSpec((TM_PROJ, CONV_WIDTH), row),
        pl.BlockSpec((TM_PROJ, CONV_WIDTH), row),
        pl.BlockSpec((TM_PROJ, 2 * D_MODEL), row),
    ]
    return pl.pallas_call(
        _proj_kernel,
        grid=(n // TM_PROJ,),
        in_specs=[
            pl.BlockSpec((TM_PROJ, D_MODEL), row),
            pl.BlockSpec((1, D_MODEL), const),
            pl.BlockSpec((D_MODEL, IN_COLS), const),
            pl.BlockSpec((1, 2 * D_MODEL), const),
        ],
        out_specs=out_specs,
        out_shape=out_shape,
        scratch_shapes=[pltpu.VMEM((6, TM_PROJ, LANES), F32)],
        compiler_params=pltpu.CompilerParams(dimension_semantics=("arbitrary",),
                                             vmem_limit_bytes=VMEM_LIMIT),
        name="proj",
    )(x2, g, w_in, b_gate)


def _attn_sub_block(q_sub, kw, vw, bias_ref, g, lo, hi):
    assert KB == GROUP_WIDTH
    lane = lax.broadcasted_iota(jnp.int32, (QB, KB), 1)
    edge_ok = (lane >= lo) & (lane < hi)
    heads = [(lane >= h * HEAD_DIM) & (lane < (h + 1) * HEAD_DIM) for h in range(HEADS_PER_GROUP)]
    zero = jnp.zeros((), BF16)
    q_stack = jnp.concatenate([jnp.where(hm, q_sub, zero) for hm in heads], axis=0)
    s_all = lax.dot_general(q_stack, kw, (((1,), (1,)), ((), ())), preferred_element_type=F32)
    probs = []
    m_b = l_b = None
    for h, hm in enumerate(heads):
        s = s_all[h * QB:(h + 1) * QB] + bias_ref[g * HEADS_PER_GROUP + h]
        s = jnp.where(edge_ok, s, MASK_VALUE)
        m = jnp.max(s, axis=1, keepdims=True)
        p = jnp.exp(s - m)
        l = jnp.sum(p, axis=1, keepdims=True)
        probs.append(p.astype(BF16))
        m_b = jnp.broadcast_to(m, (QB, GROUP_WIDTH)) if m_b is None else jnp.where(hm, m, m_b)
        l_b = jnp.broadcast_to(l, (QB, GROUP_WIDTH)) if l_b is None else jnp.where(hm, l, l_b)
    o_all = jnp.dot(jnp.concatenate(probs, axis=0), vw, preferred_element_type=F32)
    acc = o_all[0:QB]
    for h in range(1, HEADS_PER_GROUP):
        acc = jnp.where(heads[h], o_all[h * QB:(h + 1) * QB], acc)
    return acc, m_b, l_b


def _attn_kernel(q0_ref, kv0_ref, kv0p_ref, kv0n_ref,
                 q1_ref, kv1_ref, kv1p_ref, kv1n_ref,
                 q2_ref, kv2_ref, kv2p_ref, kv2n_ref,
                 y_ref,
                 cat0, cat1, cat2, bias_ref, m_st, l_st, a_st, m_tmp, l_tmp, a_tmp, *, seq):
    j = pl.program_id(1)

    qi = lax.broadcasted_iota(jnp.int32, (QB, KB), 0)
    kc = lax.broadcasted_iota(jnp.int32, (QB, KB), 1)
    adelta = jnp.abs(kc - HALF - qi)
    band = adelta <= HALF
    slopes = _alibi_slopes()
    for g in range(N_GROUPS_A):
        dist = (adelta * DILATED_PATTERNS[g][1]).astype(F32)
        for h in range(HEADS_PER_GROUP):
            bias_ref[g * HEADS_PER_GROUP + h] = jnp.where(band, -(float(slopes[g, h]) * dist), MASK_VALUE)

    for cat, own, prv, nxt in ((cat0, kv0_ref, kv0p_ref, kv0n_ref),
                               (cat1, kv1_ref, kv1p_ref, kv1n_ref),
                               (cat2, kv2_ref, kv2p_ref, kv2n_ref)):
        n_own = own.shape[-2]
        cat[:, 0:HALF, :] = prv[...].reshape(cat.shape[0], HALF, 2 * GROUP_WIDTH)
        cat[:, HALF:HALF + n_own, :] = own[...].reshape(cat.shape[0], n_own, 2 * GROUP_WIDTH)
        cat[:, HALF + n_own:, :] = nxt[...].reshape(cat.shape[0], HALF, 2 * GROUP_WIDTH)

    def window(cat, r, sb):
        rows = pl.ds(pl.multiple_of(sb * QB, QB), KB)
        return cat[r, rows, 0:GROUP_WIDTH], cat[r, rows, GROUP_WIDTH:]

    def edges(g, n_res, sb):
        length = seq // DILATED_PATTERNS[g][1]
        i0 = j * n_res + sb * QB
        return jnp.maximum(0, HALF - i0), jnp.minimum(KB, length + HALF - i0)

    def body0(sb, carry):
        rows = pl.ds(pl.multiple_of(sb * QB, QB), QB)
        kw, vw = window(cat0, 0, sb)
        lo, hi = edges(0, T_ATT, sb)
        acc, m_b, l_b = _attn_sub_block(q0_ref[rows, :], kw, vw, bias_ref, 0, lo, hi)
        for c in range(2):
            cols = slice(c * LANES, (c + 1) * LANES)
            m_st[c, rows, :] = m_b[:, cols]
            l_st[c, rows, :] = l_b[:, cols]
            a_st[c, rows, :] = acc[:, cols]
        return carry

    lax.fori_loop(0, T_ATT // QB, body0, 0, unroll=ATT_UNROLL)

    for g, q_ref, cat in ((1, q1_ref, cat1), (2, q2_ref, cat2)):
        d = DILATED_PATTERNS[g][1]
        n_res = T_ATT // d
        sb_per_res = n_res // QB

        def body(idx, carry, g=g, q_ref=q_ref, cat=cat, n_res=n_res, sb_per_res=sb_per_res):
            r = idx // sb_per_res
            sb = idx % sb_per_res
            kw, vw = window(cat, r, sb)
            lo, hi = edges(g, n_res, sb)
            q_sub = q_ref[r, pl.ds(pl.multiple_of(sb * QB, QB), QB), :]
            acc, m_b, l_b = _attn_sub_block(q_sub, kw, vw, bias_ref, g, lo, hi)
            rows = pl.ds(pl.multiple_of(idx * QB, QB), QB)
            m_tmp[rows, :] = m_b
            l_tmp[rows, :] = l_b
            a_tmp[rows, :] = acc
            return carry

        lax.fori_loop(0, T_ATT // QB, body, 0, unroll=ATT_UNROLL)

        for r in range(d):
            for ch in range(sb_per_res):
                src = slice(r * n_res + ch * QB, r * n_res + (ch + 1) * QB)
                tok = pl.ds(ch * QB * d + r, QB, stride=d)
                for c in range(2):
                    cols = slice(c * LANES, (c + 1) * LANES)
                    m_new_part = m_tmp[src, cols]
                    m_old = m_st[c, tok, :]
                    m_new = jnp.maximum(m_old, m_new_part)
                    e_old = jnp.exp(m_old - m_new)
                    e_new = jnp.exp(m_new_part - m_new)
                    m_st[c, tok, :] = m_new
                    l_st[c, tok, :] = e_old * l_st[c, tok, :] + e_new * l_tmp[src, cols]
                    a_st[c, tok, :] = e_old * a_st[c, tok, :] + e_new * a_tmp[src, cols]

    for c in range(2):
        y_ref[:, c * LANES:(c + 1) * LANES] = (a_st[c] / l_st[c]).astype(BF16)


def _attn_call(q0, kv0, q1, kv1, q2, kv2, batch, seq):
    n = q0.shape[0]
    tiles = seq // T_ATT
    specs = []
    scratch = []
    blocks_per_tile = T_ATT // HALF
    n_half_blocks = n // HALF
    specs += [
        pl.BlockSpec((T_ATT, GROUP_WIDTH), lambda b, j: (b * tiles + j, 0)),
        pl.BlockSpec((T_ATT, 2 * GROUP_WIDTH), lambda b, j: (b * tiles + j, 0)),
        pl.BlockSpec((HALF, 2 * GROUP_WIDTH),
                     lambda b, j: (jnp.maximum((b * tiles + j) * blocks_per_tile - 1, 0), 0)),
        pl.BlockSpec((HALF, 2 * GROUP_WIDTH),
                     lambda b, j: (jnp.minimum((b * tiles + j + 1) * blocks_per_tile, n_half_blocks - 1), 0)),
    ]
    scratch.append(pltpu.VMEM((1, T_ATT + 2 * HALF, 2 * GROUP_WIDTH), BF16))
    for g in (1, 2):
        d = DILATED_PATTERNS[g][1]
        n_res = T_ATT // d
        per_tile = n_res // HALF
        last = seq // d // HALF - 1
        specs += [
            pl.BlockSpec((None, d, n_res, GROUP_WIDTH), lambda b, j: (b, 0, j, 0)),
            pl.BlockSpec((None, d, n_res, 2 * GROUP_WIDTH), lambda b, j: (b, 0, j, 0)),
            pl.BlockSpec((None, d, HALF, 2 * GROUP_WIDTH),
                         lambda b, j, per_tile=per_tile: (b, 0, jnp.maximum(j * per_tile - 1, 0), 0)),
            pl.BlockSpec((None, d, HALF, 2 * GROUP_WIDTH),
                         lambda b, j, per_tile=per_tile, last=last: (b, 0, jnp.minimum((j + 1) * per_tile, last), 0)),
        ]
        scratch.append(pltpu.VMEM((d, n_res + 2 * HALF, 2 * GROUP_WIDTH), BF16))
    scratch.append(pltpu.VMEM((N_HEADS_A, QB, KB), F32))
    scratch += [pltpu.VMEM((2, T_ATT, LANES), F32) for _ in range(3)]
    scratch += [pltpu.VMEM((T_ATT, GROUP_WIDTH), F32) for _ in range(3)]
    return pl.pallas_call(
        functools.partial(_attn_kernel, seq=seq),
        grid=(batch, tiles),
        in_specs=specs,
        out_specs=pl.BlockSpec((T_ATT, GROUP_WIDTH), lambda b, j: (b * tiles + j, 0)),
        out_shape=jax.ShapeDtypeStruct((n, GROUP_WIDTH), BF16),
        scratch_shapes=scratch,
        compiler_params=pltpu.CompilerParams(dimension_semantics=("arbitrary", "arbitrary"),
                                             vmem_limit_bytes=VMEM_LIMIT),
        name="attn",
    )(q0, kv0, kv0, kv0, q1, kv1, kv1, kv1, q2, kv2, kv2, kv2)


def _split_dot(a, w_cat, w_hi):
    a_hi = a.astype(BF16)
    a_lo = (a - a_hi.astype(F32)).astype(BF16)
    both = jnp.dot(a_hi, w_cat, preferred_element_type=F32)
    return both[:, 0:LANES] + both[:, LANES:] + jnp.dot(a_lo, w_hi, preferred_element_type=F32)


ST_BASE, ST_TILE, ST_FREE = 0, 1, 2
AL_NEW = 0
RT_E, RT_W, RT_POS = 0, 2, 4


def _mix_kernel(x_ref, ya_ref, bg_ref, u_ref, up_ref, un_ref, gate_ref,
                wa_ref, wc_ref, wo_ref, cw_ref, g2_ref, wr_cat_ref, wr_hi_ref, br_ref,
                x1_ref, route_ref, routet_ref, alloc_ref, xs_hbm,
                st_ref, hbuf, posv, poss, psem, dsem, *, seq, trash_row):
    i = pl.program_id(0)
    last_step = pl.num_programs(0) - 1
    slot = i % 2
    t0 = i * TM_MIX

    def rows_done(sset):
        for k in range(2):
            pltpu.make_async_copy(hbuf.at[0], xs_hbm.at[pl.ds(0, TM_MIX * SUBLANES)], dsem.at[sset, k]).wait()

    def send_row(src_slot, sset, live, j):
        src = hbuf.at[src_slot, pl.ds(pl.multiple_of(j * SUBLANES, SUBLANES), SUBLANES)]
        for k in range(2):
            p = jnp.where(live, poss[src_slot, RT_POS + k, j], trash_row + k * TM_MIX + j)
            dst = xs_hbm.at[pl.ds(pl.multiple_of(p * SUBLANES, SUBLANES), SUBLANES)]
            pltpu.make_async_copy(src, dst, dsem.at[sset, k]).start(priority=k)

    def send_rows(src_slot, sset, live):
        def one(j, carry):
            send_row(src_slot, sset, live, j)
            return carry
        lax.fori_loop(0, TM_MIX, one, 0, unroll=8)

    def send_previous_rows(part):
        for j in range(part * TM_MIX // 4, (part + 1) * TM_MIX // 4):
            send_row(1 - slot, slot, i > 0, j)

    @pl.when(i == 0)
    def _():
        st_ref[...] = jnp.zeros_like(st_ref)
        hbuf[1] = jnp.zeros((TM_MIX * ROW_CHUNKS, LANES), F32)
        posv[...] = jnp.zeros_like(posv)
        init = pltpu.make_async_copy(posv, poss.at[1], psem)
        init.start()
        init.wait()

    @pl.when(i > 0)
    def _():
        rows_done(1 - slot)

    u = u_ref[...].astype(F32)
    row = lax.broadcasted_iota(jnp.int32, (TM_MIX, CONV_WIDTH), 0)
    prev_row = jnp.where(t0 % seq == 0, 0.0, up_ref[15:16, :].astype(F32))
    next_row = jnp.where((t0 + TM_MIX) % seq == 0, 0.0, un_ref[0:1, :].astype(F32))
    u_prev = jnp.where(row == 0, prev_row, pltpu.roll(u, 1, axis=0))
    u_next = jnp.where(row == TM_MIX - 1, next_row, pltpu.roll(u, TM_MIX - 1, axis=0))
    conv = cw_ref[0:1, :] * u_prev + cw_ref[1:2, :] * u + cw_ref[2:3, :] * u_next
    yb_in = (bg_ref[...].astype(F32) * conv).astype(BF16)
    send_previous_rows(0)

    y_a = jnp.dot(ya_ref[...], wa_ref[...], preferred_element_type=F32)
    send_previous_rows(1)
    y_b = jnp.dot(yb_in, wc_ref[...], preferred_element_type=F32)
    send_previous_rows(2)
    merged = gate_ref[:, 0:D_MODEL] * y_a.astype(BF16) + gate_ref[:, D_MODEL:] * y_b.astype(BF16)
    x1 = x_ref[...] + jnp.dot(merged, wo_ref[...], preferred_element_type=F32)
    x1_ref[...] = x1
    send_previous_rows(3)

    h2 = _rms(x1, g2_ref[...])
    for c in range(ROW_CHUNKS):
        hbuf[slot, pl.ds(c, TM_MIX, stride=ROW_CHUNKS), :] = h2[:, c * LANES:(c + 1) * LANES]

    logits = _split_dot(h2, wr_cat_ref[...], wr_hi_ref[...]) + br_ref[...]
    lane = lax.broadcasted_iota(jnp.int32, (TM_MIX, LANES), 1)
    lane_f = lane.astype(F32)
    neg = -jnp.inf
    big = float(LANES)
    is_group = lane < N_EXPERT_GROUPS
    cm = jnp.where(is_group, logits, neg)
    cmax = jnp.max(cm, axis=1, keepdims=True)
    g_idx = jnp.min(jnp.where(cm == cmax, lane_f, big), axis=1, keepdims=True)
    p_group = 1.0 / jnp.sum(jnp.where(is_group, jnp.exp(logits - cmax), 0.0), axis=1, keepdims=True)
    f_lo = N_EXPERT_GROUPS + EXPERTS_PER_GROUP * g_idx
    in_group = (lane_f >= f_lo) & (lane_f < f_lo + EXPERTS_PER_GROUP)
    fm = jnp.where(in_group, logits, neg)
    f1 = jnp.max(fm, axis=1, keepdims=True)
    i1 = jnp.min(jnp.where(fm == f1, lane_f, big), axis=1, keepdims=True)
    fm2 = jnp.where(lane_f == i1, neg, fm)
    f2 = jnp.max(fm2, axis=1, keepdims=True)
    i2 = jnp.min(jnp.where(fm2 == f2, lane_f, big), axis=1, keepdims=True)
    e21 = jnp.exp(f2 - f1)
    w_1 = p_group / (1.0 + e21)
    w_2 = p_group * e21 / (1.0 + e21)
    e_1 = i1 - N_EXPERT_GROUPS
    e_2 = i2 - N_EXPERT_GROUPS

    onehot = jnp.where((lane_f == e_1) | (lane_f == e_2), 1.0, 0.0)
    r_i = lax.broadcasted_iota(jnp.int32, (TM_MIX, TM_MIX), 0)
    c_i = lax.broadcasted_iota(jnp.int32, (TM_MIX, TM_MIX), 1)
    tri = jnp.where(c_i < r_i, 1.0, 0.0).astype(BF16)
    base = st_ref[ST_BASE:ST_BASE + 1, :]
    before = jnp.dot(tri, onehot.astype(BF16), preferred_element_type=F32) + base

    tile_rows = float(TM_MOE)
    cur_tile = st_ref[ST_TILE:ST_TILE + 1, :]
    next_free = st_ref[ST_FREE:ST_FREE + 1, :]
    count = jnp.sum(onehot, axis=0, keepdims=True)
    slot0 = jnp.floor(base * (1.0 / tile_rows))
    partial = (base - slot0 * tile_rows) > 0.0
    slot_last = jnp.floor((base + count - 1.0) * (1.0 / tile_rows))
    n_new = jnp.where(count > 0.0, slot_last - slot0 + 1.0 - jnp.where(partial, 1.0, 0.0), 0.0)
    e_r = lax.broadcasted_iota(jnp.int32, (LANES, LANES), 0)
    e_c = lax.broadcasted_iota(jnp.int32, (LANES, LANES), 1)
    earlier = jnp.where(e_r < e_c, 1.0, 0.0).astype(BF16)
    new_before = jnp.dot(jnp.broadcast_to(n_new, (SUBLANES, LANES)).astype(BF16), earlier,
                         preferred_element_type=F32)[0:1, :]
    fresh = next_free + new_before - jnp.where(partial, 1.0, 0.0) - slot0
    partial_slot = jnp.where(partial, slot0, -1.0)

    def tile_of(slot_idx, fresh_v, cur_v, partial_v):
        return jnp.where(slot_idx == partial_v, cur_v, fresh_v + slot_idx)

    def pick(row_vec, e):
        return jnp.sum(jnp.where(lane_f == e, row_vec, 0.0), axis=1, keepdims=True)

    positions = []
    for e in (e_1, e_2):
        rank = pick(before, e)
        s = jnp.floor(rank * (1.0 / tile_rows))
        tile = tile_of(s, pick(fresh, e), pick(cur_tile, e), pick(partial_slot, e))
        positions.append(tile * tile_rows + (rank - s * tile_rows))

    st_ref[ST_BASE:ST_BASE + 1, :] = base + count
    st_ref[ST_TILE:ST_TILE + 1, :] = jnp.where(count > 0.0, tile_of(slot_last, fresh, cur_tile, partial_slot),
                                                cur_tile)
    st_ref[ST_FREE:ST_FREE + 1, :] = next_free + jnp.sum(n_new, axis=1, keepdims=True)
    alloc_ref[...] = jnp.broadcast_to(n_new, (SUBLANES, LANES))

    route = jnp.zeros((TM_MIX, LANES), F32)
    for k, val in enumerate((e_1, e_2, w_1, w_2, positions[0], positions[1])):
        route = jnp.where(lane == k, val, route)
    route_ref[...] = route
    route_t = route.T[0:SUBLANES, :]
    routet_ref[...] = route_t
    posv[...] = route_t.astype(jnp.int32)
    to_smem = pltpu.make_async_copy(posv, poss.at[slot], psem)
    to_smem.start()
    to_smem.wait()

    @pl.when(i == last_step)
    def _():
        rows_done(slot)
        send_rows(slot, 1 - slot, True)
        rows_done(1 - slot)

        posv[:, 0:LANES] = st_ref[...].astype(jnp.int32)
        state = pltpu.make_async_copy(posv, poss.at[0], psem)
        state.start()
        state.wait()
        hbuf[0] = jnp.zeros((TM_MIX * ROW_CHUNKS, LANES), F32)

        def zero_jobs():
            for e in range(N_EXPERTS):
                fill = poss[0, ST_BASE, e] & (TM_MOE - 1)
                first = poss[0, ST_TILE, e] * TM_MOE
                at = fill
                size = 1
                while size < TM_MOE:
                    take = (fill > 0) & ((at & size) != 0)
                    yield take, first + at, size
                    at = at + jnp.where(take, size, 0)
                    size *= 2
            for t in range(N_EXPERTS):
                tile = poss[0, ST_FREE, 0] + t
                yield tile < trash_row // TM_MOE, tile * TM_MOE, TM_MOE

        def zero_copy(first, rows):
            return pltpu.make_async_copy(
                hbuf.at[0, pl.ds(0, rows * SUBLANES)],
                xs_hbm.at[pl.ds(pl.multiple_of(first * SUBLANES, SUBLANES), rows * SUBLANES)], psem)

        for take, first, rows in zero_jobs():
            pl.when(take)(lambda first=first, rows=rows: zero_copy(first, rows).start())
        for take, first, rows in zero_jobs():
            pl.when(take)(lambda first=first, rows=rows: zero_copy(first, rows).wait())


def _mix_call(x2, y_attn, bgate, u, gates, wa, wc, wo, conv_w, g2, wr_cat, wr_hi, br, seq, n_tiles):
    n = x2.shape[0]
    n_steps = n // TM_MIX
    row = lambda i: (i, 0)
    const = lambda i: (0, 0)
    halo = 16
    per_tile = TM_MIX // halo
    last = n // halo - 1
    xs_rows = n_tiles * TM_MOE + 2 * TM_MIX
    return pl.pallas_call(
        functools.partial(_mix_kernel, seq=seq, trash_row=n_tiles * TM_MOE),
        grid=(n_steps,),
        in_specs=[
            pl.BlockSpec((TM_MIX, D_MODEL), row),
            pl.BlockSpec((TM_MIX, GROUP_WIDTH), row),
            pl.BlockSpec((TM_MIX, CONV_WIDTH), row),
            pl.BlockSpec((TM_MIX, CONV_WIDTH), row),
            pl.BlockSpec((halo, CONV_WIDTH), lambda i: (jnp.maximum(i * per_tile - 1, 0), 0)),
            pl.BlockSpec((halo, CONV_WIDTH), lambda i: (jnp.minimum((i + 1) * per_tile, last), 0)),
            pl.BlockSpec((TM_MIX, 2 * D_MODEL), row),
            pl.BlockSpec((GROUP_WIDTH, D_MODEL), const),
            pl.BlockSpec((CONV_WIDTH, D_MODEL), const),
            pl.BlockSpec((D_MODEL, D_MODEL), const),
            pl.BlockSpec((3, CONV_WIDTH), const),
            pl.BlockSpec((1, D_MODEL), const),
            pl.BlockSpec((D_MODEL, 2 * LANES), const),
            pl.BlockSpec((D_MODEL, LANES), const),
            pl.BlockSpec((1, LANES), const),
        ],
        out_specs=[
            pl.BlockSpec((TM_MIX, D_MODEL), row),
            pl.BlockSpec((TM_MIX, LANES), row),
            pl.BlockSpec((SUBLANES, TM_MIX), lambda i: (0, i)),
            pl.BlockSpec((None, SUBLANES, LANES), lambda i: (i, 0, 0)),
            pl.BlockSpec(memory_space=pl.ANY),
        ],
        out_shape=[
            jax.ShapeDtypeStruct((n, D_MODEL), F32),
            jax.ShapeDtypeStruct((n, LANES), F32),
            jax.ShapeDtypeStruct((SUBLANES, n), F32),
            jax.ShapeDtypeStruct((n_steps, SUBLANES, LANES), F32),
            jax.ShapeDtypeStruct((xs_rows * ROW_CHUNKS, LANES), F32),
        ],
        scratch_shapes=[pltpu.VMEM((SUBLANES, LANES), F32),
                        pltpu.VMEM((2, TM_MIX * ROW_CHUNKS, LANES), F32),
                        pltpu.VMEM((SUBLANES, TM_MIX), jnp.int32),
                        pltpu.SMEM((2, SUBLANES, TM_MIX), jnp.int32),
                        pltpu.SemaphoreType.DMA(()),
                        pltpu.SemaphoreType.DMA((2, 2))],
        compiler_params=pltpu.CompilerParams(dimension_semantics=("arbitrary",),
                                             vmem_limit_bytes=VMEM_LIMIT),
        name="mix",
    )(x2, y_attn, bgate, u, u, u, gates, wa, wc, wo, conv_w, g2, wr_cat, wr_hi, br)


def _row_gather(idx_ref, n_rows, src_hbm, dst, sem):
    def issue(pair, carry):
        for k in range(2):
            j = 2 * pair + k
            t = idx_ref[0, 0, j]
            pltpu.make_async_copy(src_hbm.at[pl.ds(pl.multiple_of(t * SUBLANES, SUBLANES), SUBLANES)],
                                  dst.at[pl.ds(pl.multiple_of(j * SUBLANES, SUBLANES), SUBLANES)],
                                  sem).start(priority=k)
        return carry
    lax.fori_loop(0, n_rows // 2, issue, 0, unroll=4)


def _row_gather_wait(n_rows, src_hbm, dst, sem):
    pltpu.make_async_copy(src_hbm.at[pl.ds(0, n_rows * SUBLANES)], dst, sem).wait()


def _rows_from_tiles(buf, first_row, n_rows):
    return jnp.concatenate(
        [buf[pl.ds(first_row * ROW_CHUNKS + c, n_rows, stride=ROW_CHUNKS), :] for c in range(ROW_CHUNKS)],
        axis=1)


def _expert_kernel(order_ref, te_ref, nused_ref, xs_hbm, w1_ref, w3_ref, w2_ref, y_hbm,
                   xin, yout, w13b, w2b, isem, osem):
    i = pl.program_id(0)
    slot = i % 2
    used = i < nused_ref[0]

    def copies(step, s, fetch):
        rows = pl.ds(order_ref[step] * TM_MOE, TM_MOE)
        out = []
        for c in range(ROW_CHUNKS):
            if fetch:
                out.append(pltpu.make_async_copy(xs_hbm.at[rows, c, :], xin.at[s, c], isem.at[s]))
            else:
                out.append(pltpu.make_async_copy(yout.at[s, c], y_hbm.at[rows, c, :], osem.at[s]))
        return out

    def start(cs):
        for cp in cs:
            cp.start()

    def wait(cs):
        for cp in cs:
            cp.wait()

    pl.when(i == 0)(lambda: start(copies(0, 0, True)))
    pl.when(i + 1 < nused_ref[0])(lambda: start(copies(i + 1, 1 - slot, True)))
    pl.when(i >= 2)(lambda: wait(copies(i - 2, slot, False)))

    @pl.when(jnp.logical_not(used))
    def _():
        yout[slot] = jnp.zeros(yout.shape[1:], F32)

    @pl.when(used & ((i == 0) | (te_ref[i] != te_ref[jnp.maximum(i - 1, 0)])))
    def _():
        w13b[:, 0:EXPERT_FF] = w1_ref[...].astype(BF16)
        w13b[:, EXPERT_FF:] = w3_ref[...].astype(BF16)
        w2b[...] = w2_ref[...].astype(BF16)

    @pl.when(used)
    def _():
        wait(copies(i, slot, True))
        x = jnp.concatenate([xin[slot, c] for c in range(ROW_CHUNKS)], axis=1)
        ab = jnp.dot(x.astype(BF16), w13b[...], preferred_element_type=F32)
        a = ab[:, 0:EXPERT_FF]
        hid = (a * jax.nn.sigmoid(a) * ab[:, EXPERT_FF:]).astype(BF16)
        y = jnp.dot(hid, w2b[...], preferred_element_type=F32)
        for c in range(ROW_CHUNKS):
            yout[slot, c] = y[:, c * LANES:(c + 1) * LANES]

    start(copies(i, slot, False))

    @pl.when(i == pl.num_programs(0) - 1)
    def _():
        wait(copies(i - 1, 1 - slot, False))
        wait(copies(i, slot, False))


def _expert_call(order, tile_expert, n_used, xs_rows, w1, w3, w2):
    n_tiles = order.shape[0]
    wspec = lambda shape: pl.BlockSpec(
        (None,) + shape, lambda i, od, te, nu: (te[jnp.minimum(i, nu[0] - 1)], 0, 0))
    tile_buf = pltpu.VMEM((2, ROW_CHUNKS, TM_MOE, LANES), F32)
    grid_spec = pltpu.PrefetchScalarGridSpec(
        num_scalar_prefetch=3,
        grid=(n_tiles,),
        in_specs=[
            pl.BlockSpec(memory_space=pl.ANY),
            wspec((D_MODEL, EXPERT_FF)),
            wspec((D_MODEL, EXPERT_FF)),
            wspec((EXPERT_FF, D_MODEL)),
        ],
        out_specs=pl.BlockSpec(memory_space=pl.ANY),
        scratch_shapes=[tile_buf, tile_buf,
                        pltpu.VMEM((D_MODEL, 2 * EXPERT_FF), BF16),
                        pltpu.VMEM((EXPERT_FF, D_MODEL), BF16),
                        pltpu.SemaphoreType.DMA((2,)),
                        pltpu.SemaphoreType.DMA((2,))],
    )
    return pl.pallas_call(
        _expert_kernel,
        grid_spec=grid_spec,
        out_shape=jax.ShapeDtypeStruct((n_tiles * TM_MOE, ROW_CHUNKS, LANES), F32),
        compiler_params=pltpu.CompilerParams(dimension_semantics=("arbitrary",),
                                             vmem_limit_bytes=VMEM_LIMIT),
        name="experts",
    )(order, tile_expert, n_used, xs_rows, w1, w3, w2)


def _combine_kernel(pos_ref, posn_ref, y_hbm, x1_ref, route_ref, g_ref, o_ref, buf0, buf1, sem):
    i = pl.program_id(0)
    n_steps = pl.num_programs(0)
    bufs = (buf0, buf1)

    @pl.when(i == 0)
    def _():
        _row_gather(pos_ref, 2 * TM_CMB, y_hbm, buf0, sem.at[0])

    for slot in range(2):
        @pl.when((i % 2 == slot) & (i + 1 < n_steps))
        def _(slot=slot):
            _row_gather(posn_ref, 2 * TM_CMB, y_hbm, bufs[1 - slot], sem.at[1 - slot])

    for slot in range(2):
        @pl.when(i % 2 == slot)
        def _(slot=slot):
            _row_gather_wait(2 * TM_CMB, y_hbm, bufs[slot], sem.at[slot])
            y_1 = _rows_from_tiles(bufs[slot], 0, TM_CMB)
            y_2 = _rows_from_tiles(bufs[slot], TM_CMB, TM_CMB)
            x = x1_ref[...] + route_ref[:, 2:3] * y_1 + route_ref[:, 3:4] * y_2
            o_ref[...] = _rms(x, g_ref[...])


def _combine_call(pos, y_flat, x1, route, g):
    n = x1.shape[0]
    n_steps = n // TM_CMB
    row = lambda i: (i, 0)
    return pl.pallas_call(
        _combine_kernel,
        grid=(n_steps,),
        in_specs=[
            pl.BlockSpec((1, 1, 2 * TM_CMB), lambda i: (i, 0, 0), memory_space=pltpu.SMEM),
            pl.BlockSpec((1, 1, 2 * TM_CMB), lambda i: (jnp.minimum(i + 1, n_steps - 1), 0, 0),
                         memory_space=pltpu.SMEM),
            pl.BlockSpec(memory_space=pl.ANY),
            pl.BlockSpec((TM_CMB, D_MODEL), row),
            pl.BlockSpec((TM_CMB, LANES), row),
            pl.BlockSpec((1, D_MODEL), lambda i: (0, 0)),
        ],
        out_specs=pl.BlockSpec((TM_CMB, D_MODEL), row),
        out_shape=jax.ShapeDtypeStruct((n, D_MODEL), F32),
        scratch_shapes=[pltpu.VMEM((2 * TM_CMB * ROW_CHUNKS, LANES), F32),
                        pltpu.VMEM((2 * TM_CMB * ROW_CHUNKS, LANES), F32),
                        pltpu.SemaphoreType.DMA((2,))],
        compiler_params=pltpu.CompilerParams(dimension_semantics=("arbitrary",),
                                             vmem_limit_bytes=VMEM_LIMIT),
        name="combine",
    )(pos, pos, y_flat, x1, route, g)


def _layer(x2, batch, seq, norm_mix_g, w_in, b_gate, conv_w, w_attn_out, w_conv_out, w_out, norm_ffn_g,
           w_route_group, b_route_group, w_route_expert, b_route_expert, w1, w3, w2, final_g):
    n = x2.shape[0]
    q0, kv0, q1, kv1, q2, kv2, bgate, u, gates = _proj_call(
        x2, norm_mix_g[None, :], w_in.astype(BF16), b_gate[None, :], batch, seq)
    y_attn = _attn_call(q0, kv0, q1, kv1, q2, kv2, batch, seq)

    n_route = N_EXPERT_GROUPS + N_EXPERTS
    w_route = jnp.pad(jnp.concatenate([w_route_group, w_route_expert], axis=1), ((0, 0), (0, LANES - n_route)))
    b_route = jnp.pad(jnp.concatenate([b_route_group, b_route_expert]), (0, LANES - n_route))[None, :]
    wr_hi = w_route.astype(BF16)
    wr_lo = (w_route - wr_hi.astype(F32)).astype(BF16)
    n_tiles = (2 * n) // TM_MOE + N_EXPERTS
    x1, route, route_t, alloc, xs_flat = _mix_call(
        x2, y_attn, bgate, u, gates, w_attn_out.astype(BF16), w_conv_out.astype(BF16), w_out.astype(BF16),
        conv_w, norm_ffn_g[None, :], jnp.concatenate([wr_hi, wr_lo], axis=1), wr_hi, b_route, seq, n_tiles)

    i32 = jnp.int32
    taken = alloc[:, AL_NEW, :N_EXPERTS].astype(i32).reshape(-1)
    k = jnp.arange(taken.shape[0], dtype=i32)
    running = jnp.sum(jnp.where(k[:, None] >= k[None, :], taken[None, :], 0), axis=1)
    n_used = running[-1:]
    tile = jnp.arange(n_tiles, dtype=i32)
    owner = jnp.sum((running[None, :] <= tile[:, None]).astype(i32), axis=1) % N_EXPERTS
    owner = jnp.where(tile < n_used[0], owner, N_EXPERTS)
    key = owner * n_tiles + tile
    place = jnp.sum((key[None, :] < key[:, None]).astype(i32), axis=1)
    at = place[None, :] == tile[:, None]
    order = jnp.sum(jnp.where(at, tile[None, :], 0), axis=1)
    step_expert = jnp.minimum(jnp.sum(jnp.where(at, owner[None, :], 0), axis=1), N_EXPERTS - 1)

    y_rows = _expert_call(order, step_expert, n_used, xs_flat.reshape(-1, ROW_CHUNKS, LANES), w1, w3, w2)
    y_flat = y_rows.reshape(-1, LANES)
    pos = route_t[RT_POS:RT_POS + 2].astype(i32)
    pos_tiles = pos.reshape(2, n // TM_CMB, TM_CMB).transpose(1, 0, 2).reshape(n // TM_CMB, 1, 2 * TM_CMB)
    return _combine_call(pos_tiles, y_flat, x1, route, final_g[None, :])


def kernel(x, norm_mix_g, w_in, b_gate, conv_w, w_attn_out, w_conv_out, w_out, norm_ffn_g,
           w_route_group, b_route_group, w_route_expert, b_route_expert, w1, w3, w2, norm_final_g):
    batch, seq, d = x.shape
    depth = w_in.shape[0]
    assert d == D_MODEL and depth == 1 and seq % T_ATT == 0
    out = _layer(x.reshape(batch * seq, d), batch, seq, norm_mix_g[0], w_in[0], b_gate[0], conv_w[0],
                 w_attn_out[0], w_conv_out[0], w_out[0], norm_ffn_g[0], w_route_group[0], b_route_group[0],
                 w_route_expert[0], b_route_expert[0], w1[0], w3[0], w2[0], norm_final_g)
    return out.reshape(batch, seq, d)
```

```python
import functools

import numpy as np
import jax
import jax.numpy as jnp
from jax import lax
from jax.experimental import pallas as pl
from jax.experimental.pallas import tpu as pltpu

F32 = jnp.float32
BF16 = jnp.bfloat16

D_MODEL = 1024
HEAD_DIM = 64
HEADS_PER_GROUP = 4
DILATED_PATTERNS = ((128, 1), (512, 4), (2048, 16))
N_GROUPS_A = 3
N_HEADS_A = N_GROUPS_A * HEADS_PER_GROUP
ATTN_WIDTH = N_HEADS_A * HEAD_DIM
GROUP_WIDTH = HEADS_PER_GROUP * HEAD_DIM
ALIBI_SPAN = 8.0
MASK_VALUE = -1e30
CONV_WIDTH = 768
N_EXPERT_GROUPS = 4
EXPERTS_PER_GROUP = 8
N_EXPERTS = 32
EXPERT_FF = 512
RMS_EPS = 1e-6

HALF = 64
LANES = 128
SUBLANES = 8
ROW_CHUNKS = D_MODEL // LANES

COL_K = ATTN_WIDTH
COL_V = 2 * ATTN_WIDTH
COL_BG = 3 * ATTN_WIDTH
COL_CG = COL_BG + CONV_WIDTH
COL_XIN = COL_CG + CONV_WIDTH
COL_GATE = COL_XIN + CONV_WIDTH
IN_COLS = COL_GATE + 2 * D_MODEL

TM_PROJ = 512
T_ATT = 2048
QB = 128
KB = QB + 2 * HALF
ATT_UNROLL = 8
TM_MIX = 512
TM_MOE = 512
TM_CMB = 256

VMEM_LIMIT = 56 * 1024 * 1024


def _alibi_slopes():
    return np.array([2.0 ** (-ALIBI_SPAN * (i + 1) / N_HEADS_A) for i in range(N_HEADS_A)],
                    dtype=np.float32).reshape(N_GROUPS_A, HEADS_PER_GROUP)


def _rms(x, g):
    return x * lax.rsqrt(jnp.mean(x * x, axis=-1, keepdims=True) + RMS_EPS) * g


def _proj_kernel(x_ref, g_ref, w_ref, b_ref,
                 q0_ref, kv0_ref, q1_ref, kv1_ref, q2_ref, kv2_ref, bg_ref, u_ref, gate_ref, scr):
    h = _rms(x_ref[...], g_ref[...]).astype(BF16)

    def proj(c0, width):
        return jnp.dot(h, w_ref[:, c0:c0 + width], preferred_element_type=F32)

    qscale = HEAD_DIM ** -0.5
    q0_ref[...] = (proj(0, GROUP_WIDTH) * qscale).astype(BF16)
    kv0_ref[:, 0:GROUP_WIDTH] = proj(COL_K, GROUP_WIDTH).astype(BF16)
    kv0_ref[:, GROUP_WIDTH:] = proj(COL_V, GROUP_WIDTH).astype(BF16)

    for g, q_ref, kv_ref in ((1, q1_ref, kv1_ref), (2, q2_ref, kv2_ref)):
        d = DILATED_PATTERNS[g][1]
        n = TM_PROJ // d
        parts = (proj(g * GROUP_WIDTH, GROUP_WIDTH) * qscale,
                 proj(COL_K + g * GROUP_WIDTH, GROUP_WIDTH),
                 proj(COL_V + g * GROUP_WIDTH, GROUP_WIDTH))
        for i, part in enumerate(parts):
            for c in range(2):
                scr[2 * i + c] = part[:, c * LANES:(c + 1) * LANES]
        for r in range(d):
            rows = pl.ds(r, n, stride=d)
            q_ref[r] = jnp.concatenate([scr[c, rows, :] for c in range(2)], axis=1).astype(BF16)
            kv_ref[r] = jnp.concatenate([scr[c, rows, :] for c in range(2, 6)], axis=1).astype(BF16)

    bg_ref[...] = proj(COL_BG, CONV_WIDTH).astype(BF16)
    u_ref[...] = (proj(COL_CG, CONV_WIDTH) * proj(COL_XIN, CONV_WIDTH)).astype(BF16)
    for c in range(4):
        w = 2 * D_MODEL // 4
        z = proj(COL_GATE + c * w, w) + b_ref[:, c * w:(c + 1) * w]
        gate_ref[:, c * w:(c + 1) * w] = jax.nn.sigmoid(z).astype(BF16)


def _proj_call(x2, g, w_in, b_gate, batch, seq):
    n = x2.shape[0]
    steps_per_batch = seq // TM_PROJ
    d1, d2 = DILATED_PATTERNS[1][1], DILATED_PATTERNS[2][1]
    row = lambda i: (i, 0)
    res = lambda i: (i // steps_per_batch, 0, i % steps_per_batch, 0)
    const = lambda i: (0, 0)
    out_shape = [
        jax.ShapeDtypeStruct((n, GROUP_WIDTH), BF16),
        jax.ShapeDtypeStruct((n, 2 * GROUP_WIDTH), BF16),
        jax.ShapeDtypeStruct((batch, d1, seq // d1, GROUP_WIDTH), BF16),
        jax.ShapeDtypeStruct((batch, d1, seq // d1, 2 * GROUP_WIDTH), BF16),
        jax.ShapeDtypeStruct((batch, d2, seq // d2, GROUP_WIDTH), BF16),
        jax.ShapeDtypeStruct((batch, d2, seq // d2, 2 * GROUP_WIDTH), BF16),
        jax.ShapeDtypeStruct((n, CONV_WIDTH), BF16),
        jax.ShapeDtypeStruct((n, CONV_WIDTH), BF16),
        jax.ShapeDtypeStruct((n, 2 * D_MODEL), BF16),
    ]
    out_specs = [
        pl.BlockSpec((TM_PROJ, GROUP_WIDTH), row),
        pl.BlockSpec((TM_PROJ, 2 * GROUP_WIDTH), row),
        pl.BlockSpec((None, d1, TM_PROJ // d1, GROUP_WIDTH), res),
        pl.BlockSpec((None, d1, TM_PROJ // d1, 2 * GROUP_WIDTH), res),
        pl.BlockSpec((None, d2, TM_PROJ // d2, GROUP_WIDTH), res),
        pl.BlockSpec((None, d2, TM_PROJ // d2, 2 * GROUP_WIDTH), res),
        pl.BlockSpec((TM_PROJ, CONV_WIDTH), row),
        pl.BlockSpec((TM_PROJ, CONV_WIDTH), row),
        pl.BlockSpec((TM_PROJ, 2 * D_MODEL), row),
    ]
    return pl.pallas_call(
        _proj_kernel,
        grid=(n // TM_PROJ,),
        in_specs=[
            pl.BlockSpec((TM_PROJ, D_MODEL), row),
            pl.BlockSpec((1, D_MODEL), const),
            pl.BlockSpec((D_MODEL, IN_COLS), const),
            pl.BlockSpec((1, 2 * D_MODEL), const),
        ],
        out_specs=out_specs,
        out_shape=out_shape,
        scratch_shapes=[pltpu.VMEM((6, TM_PROJ, LANES), F32)],
        compiler_params=pltpu.CompilerParams(dimension_semantics=("arbitrary",),
                                             vmem_limit_bytes=VMEM_LIMIT),
        name="proj",
    )(x2, g, w_in, b_gate)


def _attn_sub_block(q_sub, kw, vw, bias_ref, g, lo, hi):
    assert KB == GROUP_WIDTH
    lane = lax.broadcasted_iota(jnp.int32, (QB, KB), 1)
    edge_ok = (lane >= lo) & (lane < hi)
    heads = [(lane >= h * HEAD_DIM) & (lane < (h + 1) * HEAD_DIM) for h in range(HEADS_PER_GROUP)]
    zero = jnp.zeros((), BF16)
    q_stack = jnp.concatenate([jnp.where(hm, q_sub, zero) for hm in heads], axis=0)
    s_all = lax.dot_general(q_stack, kw, (((1,), (1,)), ((), ())), preferred_element_type=F32)
    probs = []
    m_b = l_b = None
    for h, hm in enumerate(heads):
        s = s_all[h * QB:(h + 1) * QB] + bias_ref[g * HEADS_PER_GROUP + h]
        s = jnp.where(edge_ok, s, MASK_VALUE)
        m = jnp.max(s, axis=1, keepdims=True)
        p = jnp.exp(s - m)
        l = jnp.sum(p, axis=1, keepdims=True)
        probs.append(p.astype(BF16))
        m_b = jnp.broadcast_to(m, (QB, GROUP_WIDTH)) if m_b is None else jnp.where(hm, m, m_b)
        l_b = jnp.broadcast_to(l, (QB, GROUP_WIDTH)) if l_b is None else jnp.where(hm, l, l_b)
    o_all = jnp.dot(jnp.concatenate(probs, axis=0), vw, preferred_element_type=F32)
    acc = o_all[0:QB]
    for h in range(1, HEADS_PER_GROUP):
        acc = jnp.where(heads[h], o_all[h * QB:(h + 1) * QB], acc)
    return acc, m_b, l_b


def _attn_kernel(q0_ref, kv0_ref, kv0p_ref, kv0n_ref,
                 q1_ref, kv1_ref, kv1p_ref, kv1n_ref,
                 q2_ref, kv2_ref, kv2p_ref, kv2n_ref,
                 y_ref,
                 cat0, cat1, cat2, bias_ref, m_st, l_st, a_st, m_tmp, l_tmp, a_tmp, *, seq):
    j = pl.program_id(1)

    qi = lax.broadcasted_iota(jnp.int32, (QB, KB), 0)
    kc = lax.broadcasted_iota(jnp.int32, (QB, KB), 1)
    adelta = jnp.abs(kc - HALF - qi)
    band = adelta <= HALF
    slopes = _alibi_slopes()
    for g in range(N_GROUPS_A):
        dist = (adelta * DILATED_PATTERNS[g][1]).astype(F32)
        for h in range(HEADS_PER_GROUP):
            bias_ref[g * HEADS_PER_GROUP + h] = jnp.where(band, -(float(slopes[g, h]) * dist), MASK_VALUE)

    for cat, own, prv, nxt in ((cat0, kv0_ref, kv0p_ref, kv0n_ref),
                               (cat1, kv1_ref, kv1p_ref, kv1n_ref),
                               (cat2, kv2_ref, kv2p_ref, kv2n_ref)):
        n_own = own.shape[-2]
        cat[:, 0:HALF, :] = prv[...].reshape(cat.shape[0], HALF, 2 * GROUP_WIDTH)
        cat[:, HALF:HALF + n_own, :] = own[...].reshape(cat.shape[0], n_own, 2 * GROUP_WIDTH)
        cat[:, HALF + n_own:, :] = nxt[...].reshape(cat.shape[0], HALF, 2 * GROUP_WIDTH)

    def window(cat, r, sb):
        rows = pl.ds(pl.multiple_of(sb * QB, QB), KB)
        return cat[r, rows, 0:GROUP_WIDTH], cat[r, rows, GROUP_WIDTH:]

    def edges(g, n_res, sb):
        length = seq // DILATED_PATTERNS[g][1]
        i0 = j * n_res + sb * QB
        return jnp.maximum(0, HALF - i0), jnp.minimum(KB, length + HALF - i0)

    def body0(sb, carry):
        rows = pl.ds(pl.multiple_of(sb * QB, QB), QB)
        kw, vw = window(cat0, 0, sb)
        lo, hi = edges(0, T_ATT, sb)
        acc, m_b, l_b = _attn_sub_block(q0_ref[rows, :], kw, vw, bias_ref, 0, lo, hi)
        for c in range(2):
            cols = slice(c * LANES, (c + 1) * LANES)
            m_st[c, rows, :] = m_b[:, cols]
            l_st[c, rows, :] = l_b[:, cols]
            a_st[c, rows, :] = acc[:, cols]
        return carry

    lax.fori_loop(0, T_ATT // QB, body0, 0, unroll=ATT_UNROLL)

    for g, q_ref, cat in ((1, q1_ref, cat1), (2, q2_ref, cat2)):
        d = DILATED_PATTERNS[g][1]
        n_res = T_ATT // d
        sb_per_res = n_res // QB

        def body(idx, carry, g=g, q_ref=q_ref, cat=cat, n_res=n_res, sb_per_res=sb_per_res):
            r = idx // sb_per_res
            sb = idx % sb_per_res
            kw, vw = window(cat, r, sb)
            lo, hi = edges(g, n_res, sb)
            q_sub = q_ref[r, pl.ds(pl.multiple_of(sb * QB, QB), QB), :]
            acc, m_b, l_b = _attn_sub_block(q_sub, kw, vw, bias_ref, g, lo, hi)
            rows = pl.ds(pl.multiple_of(idx * QB, QB), QB)
            m_tmp[rows, :] = m_b
            l_tmp[rows, :] = l_b
            a_tmp[rows, :] = acc
            return carry

        lax.fori_loop(0, T_ATT // QB, body, 0, unroll=ATT_UNROLL)

        for r in range(d):
            for ch in range(sb_per_res):
                src = slice(r * n_res + ch * QB, r * n_res + (ch + 1) * QB)
                tok = pl.ds(ch * QB * d + r, QB, stride=d)
                for c in range(2):
                    cols = slice(c * LANES, (c + 1) * LANES)
                    m_new_part = m_tmp[src, cols]
                    m_old = m_st[c, tok, :]
                    m_new = jnp.maximum(m_old, m_new_part)
                    e_old = jnp.exp(m_old - m_new)
                    e_new = jnp.exp(m_new_part - m_new)
                    m_st[c, tok, :] = m_new
                    l_st[c, tok, :] = e_old * l_st[c, tok, :] + e_new * l_tmp[src, cols]
                    a_st[c, tok, :] = e_old * a_st[c, tok, :] + e_new * a_tmp[src, cols]

    for c in range(2):
        y_ref[:, c * LANES:(c + 1) * LANES] = (a_st[c] / l_st[c]).astype(BF16)


def _attn_call(q0, kv0, q1, kv1, q2, kv2, batch, seq):
    n = q0.shape[0]
    tiles = seq // T_ATT
    specs = []
    scratch = []
    blocks_per_tile = T_ATT // HALF
    n_half_blocks = n // HALF
    specs += [
        pl.BlockSpec((T_ATT, GROUP_WIDTH), lambda b, j: (b * tiles + j, 0)),
        pl.BlockSpec((T_ATT, 2 * GROUP_WIDTH), lambda b, j: (b * tiles + j, 0)),
        pl.BlockSpec((HALF, 2 * GROUP_WIDTH),
                     lambda b, j: (jnp.maximum((b * tiles + j) * blocks_per_tile - 1, 0), 0)),
        pl.BlockSpec((HALF, 2 * GROUP_WIDTH),
                     lambda b, j: (jnp.minimum((b * tiles + j + 1) * blocks_per_tile, n_half_blocks - 1), 0)),
    ]
    scratch.append(pltpu.VMEM((1, T_ATT + 2 * HALF, 2 * GROUP_WIDTH), BF16))
    for g in (1, 2):
        d = DILATED_PATTERNS[g][1]
        n_res = T_ATT // d
        per_tile = n_res // HALF
        last = seq // d // HALF - 1
        specs += [
            pl.BlockSpec((None, d, n_res, GROUP_WIDTH), lambda b, j: (b, 0, j, 0)),
            pl.BlockSpec((None, d, n_res, 2 * GROUP_WIDTH), lambda b, j: (b, 0, j, 0)),
            pl.BlockSpec((None, d, HALF, 2 * GROUP_WIDTH),
                         lambda b, j, per_tile=per_tile: (b, 0, jnp.maximum(j * per_tile - 1, 0), 0)),
            pl.BlockSpec((None, d, HALF, 2 * GROUP_WIDTH),
                         lambda b, j, per_tile=per_tile, last=last: (b, 0, jnp.minimum((j + 1) * per_tile, last), 0)),
        ]
        scratch.append(pltpu.VMEM((d, n_res + 2 * HALF, 2 * GROUP_WIDTH), BF16))
    scratch.append(pltpu.VMEM((N_HEADS_A, QB, KB), F32))
    scratch += [pltpu.VMEM((2, T_ATT, LANES), F32) for _ in range(3)]
    scratch += [pltpu.VMEM((T_ATT, GROUP_WIDTH), F32) for _ in range(3)]
    return pl.pallas_call(
        functools.partial(_attn_kernel, seq=seq),
        grid=(batch, tiles),
        in_specs=specs,
        out_specs=pl.BlockSpec((T_ATT, GROUP_WIDTH), lambda b, j: (b * tiles + j, 0)),
        out_shape=jax.ShapeDtypeStruct((n, GROUP_WIDTH), BF16),
        scratch_shapes=scratch,
        compiler_params=pltpu.CompilerParams(dimension_semantics=("arbitrary", "arbitrary"),
                                             vmem_limit_bytes=VMEM_LIMIT),
        name="attn",
    )(q0, kv0, kv0, kv0, q1, kv1, kv1, kv1, q2, kv2, kv2, kv2)


def _split_dot(a, w_cat, w_hi):
    a_hi = a.astype(BF16)
    a_lo = (a - a_hi.astype(F32)).astype(BF16)
    both = jnp.dot(a_hi, w_cat, preferred_element_type=F32)
    return both[:, 0:LANES] + both[:, LANES:] + jnp.dot(a_lo, w_hi, preferred_element_type=F32)


ST_BASE, ST_TILE, ST_FREE = 0, 1, 2
AL_NEW = 0
RT_E, RT_W, RT_POS = 0, 2, 4


def _mix_kernel(x_ref, ya_ref, bg_ref, u_ref, up_ref, un_ref, gate_ref,
                wa_ref, wc_ref, wo_ref, cw_ref, g2_ref, wr_cat_ref, wr_hi_ref, br_ref,
                x1_ref, route_ref, routet_ref, alloc_ref, xs_hbm,
                st_ref, hbuf, posv, poss, psem, dsem, *, seq, trash_row):
    i = pl.program_id(0)
    last_step = pl.num_programs(0) - 1
    slot = i % 2
    t0 = i * TM_MIX

    def rows_done(sset):
        for k in range(2):
            pltpu.make_async_copy(hbuf.at[0], xs_hbm.at[pl.ds(0, TM_MIX * SUBLANES)], dsem.at[sset, k]).wait()

    def send_row(src_slot, sset, live, j):
        src = hbuf.at[src_slot, pl.ds(pl.multiple_of(j * SUBLANES, SUBLANES), SUBLANES)]
        for k in range(2):
            p = jnp.where(live, poss[src_slot, RT_POS + k, j], trash_row + k * TM_MIX + j)
            dst = xs_hbm.at[pl.ds(pl.multiple_of(p * SUBLANES, SUBLANES), SUBLANES)]
            pltpu.make_async_copy(src, dst, dsem.at[sset, k]).start(priority=k)

    def send_rows(src_slot, sset, live):
        def one(j, carry):
            send_row(src_slot, sset, live, j)
            return carry
        lax.fori_loop(0, TM_MIX, one, 0, unroll=8)

    def send_previous_rows(part):
        for j in range(part * TM_MIX // 4, (part + 1) * TM_MIX // 4):
            send_row(1 - slot, slot, i > 0, j)

    @pl.when(i == 0)
    def _():
        st_ref[...] = jnp.zeros_like(st_ref)
        hbuf[1] = jnp.zeros((TM_MIX * ROW_CHUNKS, LANES), F32)
        posv[...] = jnp.zeros_like(posv)
        init = pltpu.make_async_copy(posv, poss.at[1], psem)
        init.start()
        init.wait()

    @pl.when(i > 0)
    def _():
        rows_done(1 - slot)

    u = u_ref[...].astype(F32)
    row = lax.broadcasted_iota(jnp.int32, (TM_MIX, CONV_WIDTH), 0)
    prev_row = jnp.where(t0 % seq == 0, 0.0, up_ref[15:16, :].astype(F32))
    next_row = jnp.where((t0 + TM_MIX) % seq == 0, 0.0, un_ref[0:1, :].astype(F32))
    u_prev = jnp.where(row == 0, prev_row, pltpu.roll(u, 1, axis=0))
    u_next = jnp.where(row == TM_MIX - 1, next_row, pltpu.roll(u, TM_MIX - 1, axis=0))
    conv = cw_ref[0:1, :] * u_prev + cw_ref[1:2, :] * u + cw_ref[2:3, :] * u_next
    yb_in = (bg_ref[...].astype(F32) * conv).astype(BF16)
    send_previous_rows(0)

    y_a = jnp.dot(ya_ref[...], wa_ref[...], preferred_element_type=F32)
    send_previous_rows(1)
    y_b = jnp.dot(yb_in, wc_ref[...], preferred_element_type=F32)
    send_previous_rows(2)
    merged = gate_ref[:, 0:D_MODEL] * y_a.astype(BF16) + gate_ref[:, D_MODEL:] * y_b.astype(BF16)
    x1 = x_ref[...] + jnp.dot(merged, wo_ref[...], preferred_element_type=F32)
    x1_ref[...] = x1
    send_previous_rows(3)

    h2 = _rms(x1, g2_ref[...])
    for c in range(ROW_CHUNKS):
        hbuf[slot, pl.ds(c, TM_MIX, stride=ROW_CHUNKS), :] = h2[:, c * LANES:(c + 1) * LANES]

    logits = _split_dot(h2, wr_cat_ref[...], wr_hi_ref[...]) + br_ref[...]
    lane = lax.broadcasted_iota(jnp.int32, (TM_MIX, LANES), 1)
    lane_f = lane.astype(F32)
    neg = -jnp.inf
    big = float(LANES)
    is_group = lane < N_EXPERT_GROUPS
    cm = jnp.where(is_group, logits, neg)
    cmax = jnp.max(cm, axis=1, keepdims=True)
    g_idx = jnp.min(jnp.where(cm == cmax, lane_f, big), axis=1, keepdims=True)
    p_group = 1.0 / jnp.sum(jnp.where(is_group, jnp.exp(logits - cmax), 0.0), axis=1, keepdims=True)
    f_lo = N_EXPERT_GROUPS + EXPERTS_PER_GROUP * g_idx
    in_group = (lane_f >= f_lo) & (lane_f < f_lo + EXPERTS_PER_GROUP)
    fm = jnp.where(in_group, logits, neg)
    f1 = jnp.max(fm, axis=1, keepdims=True)
    i1 = jnp.min(jnp.where(fm == f1, lane_f, big), axis=1, keepdims=True)
    fm2 = jnp.where(lane_f == i1, neg, fm)
    f2 = jnp.max(fm2, axis=1, keepdims=True)
    i2 = jnp.min(jnp.where(fm2 == f2, lane_f, big), axis=1, keepdims=True)
    e21 = jnp.exp(f2 - f1)
    w_1 = p_group / (1.0 + e21)
    w_2 = p_group * e21 / (1.0 + e21)
    e_1 = i1 - N_EXPERT_GROUPS
    e_2 = i2 - N_EXPERT_GROUPS

    onehot = jnp.where((lane_f == e_1) | (lane_f == e_2), 1.0, 0.0)
    r_i = lax.broadcasted_iota(jnp.int32, (TM_MIX, TM_MIX), 0)
    c_i = lax.broadcasted_iota(jnp.int32, (TM_MIX, TM_MIX), 1)
    tri = jnp.where(c_i < r_i, 1.0, 0.0).astype(BF16)
    base = st_ref[ST_BASE:ST_BASE + 1, :]
    before = jnp.dot(tri, onehot.astype(BF16), preferred_element_type=F32) + base

    tile_rows = float(TM_MOE)
    cur_tile = st_ref[ST_TILE:ST_TILE + 1, :]
    next_free = st_ref[ST_FREE:ST_FREE + 1, :]
    count = jnp.sum(onehot, axis=0, keepdims=True)
    slot0 = jnp.floor(base * (1.0 / tile_rows))
    partial = (base - slot0 * tile_rows) > 0.0
    slot_last = jnp.floor((base + count - 1.0) * (1.0 / tile_rows))
    n_new = jnp.where(count > 0.0, slot_last - slot0 + 1.0 - jnp.where(partial, 1.0, 0.0), 0.0)
    e_r = lax.broadcasted_iota(jnp.int32, (LANES, LANES), 0)
    e_c = lax.broadcasted_iota(jnp.int32, (LANES, LANES), 1)
    earlier = jnp.where(e_r < e_c, 1.0, 0.0).astype(BF16)
    new_before = jnp.dot(jnp.broadcast_to(n_new, (SUBLANES, LANES)).astype(BF16), earlier,
                         preferred_element_type=F32)[0:1, :]
    fresh = next_free + new_before - jnp.where(partial, 1.0, 0.0) - slot0
    partial_slot = jnp.where(partial, slot0, -1.0)

    def tile_of(slot_idx, fresh_v, cur_v, partial_v):
        return jnp.where(slot_idx == partial_v, cur_v, fresh_v + slot_idx)

    def pick(row_vec, e):
        return jnp.sum(jnp.where(lane_f == e, row_vec, 0.0), axis=1, keepdims=True)

    positions = []
    for e in (e_1, e_2):
        rank = pick(before, e)
        s = jnp.floor(rank * (1.0 / tile_rows))
        tile = tile_of(s, pick(fresh, e), pick(cur_tile, e), pick(partial_slot, e))
        positions.append(tile * tile_rows + (rank - s * tile_rows))

    st_ref[ST_BASE:ST_BASE + 1, :] = base + count
    st_ref[ST_TILE:ST_TILE + 1, :] = jnp.where(count > 0.0, tile_of(slot_last, fresh, cur_tile, partial_slot),
                                                cur_tile)
    st_ref[ST_FREE:ST_FREE + 1, :] = next_free + jnp.sum(n_new, axis=1, keepdims=True)
    alloc_ref[...] = jnp.broadcast_to(n_new, (SUBLANES, LANES))

    route = jnp.zeros((TM_MIX, LANES), F32)
    for k, val in enumerate((e_1, e_2, w_1, w_2, positions[0], positions[1])):
        route = jnp.where(lane == k, val, route)
    route_ref[...] = route
    route_t = route.T[0:SUBLANES, :]
    routet_ref[...] = route_t
    posv[...] = route_t.astype(jnp.int32)
    to_smem = pltpu.make_async_copy(posv, poss.at[slot], psem)
    to_smem.start()
    to_smem.wait()

    @pl.when(i == last_step)
    def _():
        rows_done(slot)
        send_rows(slot, 1 - slot, True)
        rows_done(1 - slot)

        posv[:, 0:LANES] = st_ref[...].astype(jnp.int32)
        state = pltpu.make_async_copy(posv, poss.at[0], psem)
        state.start()
        state.wait()
        hbuf[0] = jnp.zeros((TM_MIX * ROW_CHUNKS, LANES), F32)

        def zero_jobs():
            for e in range(N_EXPERTS):
                fill = poss[0, ST_BASE, e] & (TM_MOE - 1)
                first = poss[0, ST_TILE, e] * TM_MOE
                at = fill
                size = 1
                while size < TM_MOE:
                    take = (fill > 0) & ((at & size) != 0)
                    yield take, first + at, size
                    at = at + jnp.where(take, size, 0)
                    size *= 2
            for t in range(N_EXPERTS):
                tile = poss[0, ST_FREE, 0] + t
                yield tile < trash_row // TM_MOE, tile * TM_MOE, TM_MOE

        def zero_copy(first, rows):
            return pltpu.make_async_copy(
                hbuf.at[0, pl.ds(0, rows * SUBLANES)],
                xs_hbm.at[pl.ds(pl.multiple_of(first * SUBLANES, SUBLANES), rows * SUBLANES)], psem)

        for take, first, rows in zero_jobs():
            pl.when(take)(lambda first=first, rows=rows: zero_copy(first, rows).start())
        for take, first, rows in zero_jobs():
            pl.when(take)(lambda first=first, rows=rows: zero_copy(first, rows).wait())


def _mix_call(x2, y_attn, bgate, u, gates, wa, wc, wo, conv_w, g2, wr_cat, wr_hi, br, seq, n_tiles):
    n = x2.shape[0]
    n_steps = n // TM_MIX
    row = lambda i: (i, 0)
    const = lambda i: (0, 0)
    halo = 16
    per_tile = TM_MIX // halo
    last = n // halo - 1
    xs_rows = n_tiles * TM_MOE + 2 * TM_MIX
    return pl.pallas_call(
        functools.partial(_mix_kernel, seq=seq, trash_row=n_tiles * TM_MOE),
        grid=(n_steps,),
        in_specs=[
            pl.BlockSpec((TM_MIX, D_MODEL), row),
            pl.BlockSpec((TM_MIX, GROUP_WIDTH), row),
            pl.BlockSpec((TM_MIX, CONV_WIDTH), row),
            pl.BlockSpec((TM_MIX, CONV_WIDTH), row),
            pl.BlockSpec((halo, CONV_WIDTH), lambda i: (jnp.maximum(i * per_tile - 1, 0), 0)),
            pl.BlockSpec((halo, CONV_WIDTH), lambda i: (jnp.minimum((i + 1) * per_tile, last), 0)),
            pl.BlockSpec((TM_MIX, 2 * D_MODEL), row),
            pl.BlockSpec((GROUP_WIDTH, D_MODEL), const),
            pl.BlockSpec((CONV_WIDTH, D_MODEL), const),
            pl.BlockSpec((D_MODEL, D_MODEL), const),
            pl.BlockSpec((3, CONV_WIDTH), const),
            pl.BlockSpec((1, D_MODEL), const),
            pl.BlockSpec((D_MODEL, 2 * LANES), const),
            pl.BlockSpec((D_MODEL, LANES), const),
            pl.BlockSpec((1, LANES), const),
        ],
        out_specs=[
            pl.BlockSpec((TM_MIX, D_MODEL), row),
            pl.BlockSpec((TM_MIX, LANES), row),
            pl.BlockSpec((SUBLANES, TM_MIX), lambda i: (0, i)),
            pl.BlockSpec((None, SUBLANES, LANES), lambda i: (i, 0, 0)),
            pl.BlockSpec(memory_space=pl.ANY),
        ],
        out_shape=[
            jax.ShapeDtypeStruct((n, D_MODEL), F32),
            jax.ShapeDtypeStruct((n, LANES), F32),
            jax.ShapeDtypeStruct((SUBLANES, n), F32),
            jax.ShapeDtypeStruct((n_steps, SUBLANES, LANES), F32),
            jax.ShapeDtypeStruct((xs_rows * ROW_CHUNKS, LANES), F32),
        ],
        scratch_shapes=[pltpu.VMEM((SUBLANES, LANES), F32),
                        pltpu.VMEM((2, TM_MIX * ROW_CHUNKS, LANES), F32),
                        pltpu.VMEM((SUBLANES, TM_MIX), jnp.int32),
                        pltpu.SMEM((2, SUBLANES, TM_MIX), jnp.int32),
                        pltpu.SemaphoreType.DMA(()),
                        pltpu.SemaphoreType.DMA((2, 2))],
        compiler_params=pltpu.CompilerParams(dimension_semantics=("arbitrary",),
                                             vmem_limit_bytes=VMEM_LIMIT),
        name="mix",
    )(x2, y_attn, bgate, u, u, u, gates, wa, wc, wo, conv_w, g2, wr_cat, wr_hi, br)


def _row_gather(idx_ref, n_rows, src_hbm, dst, sem):
    def issue(pair, carry):
        for k in range(2):
            j = 2 * pair + k
            t = idx_ref[0, 0, j]
            pltpu.make_async_copy(src_hbm.at[pl.ds(pl.multiple_of(t * SUBLANES, SUBLANES), SUBLANES)],
                                  dst.at[pl.ds(pl.multiple_of(j * SUBLANES, SUBLANES), SUBLANES)],
                                  sem).start(priority=k)
        return carry
    lax.fori_loop(0, n_rows // 2, issue, 0, unroll=4)


def _row_gather_wait(n_rows, src_hbm, dst, sem):
    pltpu.make_async_copy(src_hbm.at[pl.ds(0, n_rows * SUBLANES)], dst, sem).wait()


def _rows_from_tiles(buf, first_row, n_rows):
    return jnp.concatenate(
        [buf[pl.ds(first_row * ROW_CHUNKS + c, n_rows, stride=ROW_CHUNKS), :] for c in range(ROW_CHUNKS)],
        axis=1)


def _expert_kernel(order_ref, te_ref, nused_ref, xs_hbm, w1_ref, w3_ref, w2_ref, y_ref,
                   xin, w13b, w2b, isem):
    i = pl.program_id(0)
    slot = i % 2
    used = i < nused_ref[0]

    def fetch(step, s):
        rows = pl.ds(order_ref[step] * TM_MOE, TM_MOE)
        return [pltpu.make_async_copy(xs_hbm.at[rows, c, :], xin.at[s, c], isem.at[s]) for c in range(ROW_CHUNKS)]

    @pl.when(i == 0)
    def _():
        for cp in fetch(0, 0):
            cp.start()

    @pl.when(i + 1 < nused_ref[0])
    def _():
        for cp in fetch(i + 1, 1 - slot):
            cp.start()

    @pl.when(jnp.logical_not(used))
    def _():
        y_ref[...] = jnp.zeros_like(y_ref)

    @pl.when(used & ((i == 0) | (te_ref[i] != te_ref[jnp.maximum(i - 1, 0)])))
    def _():
        w13b[:, 0:EXPERT_FF] = w1_ref[...].astype(BF16)
        w13b[:, EXPERT_FF:] = w3_ref[...].astype(BF16)
        w2b[...] = w2_ref[...].astype(BF16)

    @pl.when(used)
    def _():
        for cp in fetch(i, slot):
            cp.wait()
        x = jnp.concatenate([xin[slot, c] for c in range(ROW_CHUNKS)], axis=1)
        ab = jnp.dot(x.astype(BF16), w13b[...], preferred_element_type=F32)
        a = ab[:, 0:EXPERT_FF]
        hid = (a * jax.nn.sigmoid(a) * ab[:, EXPERT_FF:]).astype(BF16)
        y = jnp.dot(hid, w2b[...], preferred_element_type=F32)
        for c in range(ROW_CHUNKS):
            y_ref[pl.ds(c, TM_MOE, stride=ROW_CHUNKS), :] = y[:, c * LANES:(c + 1) * LANES]


def _expert_call(order, tile_expert, n_used, xs_rows, w1, w3, w2):
    n_tiles = order.shape[0]
    last = lambda i, nu: jnp.minimum(i, nu[0] - 1)
    wspec = lambda shape: pl.BlockSpec(
        (None,) + shape, lambda i, od, te, nu: (te[last(i, nu)], 0, 0))
    grid_spec = pltpu.PrefetchScalarGridSpec(
        num_scalar_prefetch=3,
        grid=(n_tiles,),
        in_specs=[
            pl.BlockSpec(memory_space=pl.ANY),
            wspec((D_MODEL, EXPERT_FF)),
            wspec((D_MODEL, EXPERT_FF)),
            wspec((EXPERT_FF, D_MODEL)),
        ],
        out_specs=pl.BlockSpec((TM_MOE * ROW_CHUNKS, LANES), lambda i, od, te, nu: (od[i], 0)),
        scratch_shapes=[pltpu.VMEM((2, ROW_CHUNKS, TM_MOE, LANES), F32),
                        pltpu.VMEM((D_MODEL, 2 * EXPERT_FF), BF16),
                        pltpu.VMEM((EXPERT_FF, D_MODEL), BF16),
                        pltpu.SemaphoreType.DMA((2,))],
    )
    return pl.pallas_call(
        _expert_kernel,
        grid_spec=grid_spec,
        out_shape=jax.ShapeDtypeStruct((n_tiles * TM_MOE * ROW_CHUNKS, LANES), F32),
        compiler_params=pltpu.CompilerParams(dimension_semantics=("arbitrary",),
                                             vmem_limit_bytes=VMEM_LIMIT),
        name="experts",
    )(order, tile_expert, n_used, xs_rows, w1, w3, w2)


def _combine_kernel(pos_ref, posn_ref, y_hbm, x1_ref, route_ref, g_ref, o_ref, buf0, buf1, sem):
    i = pl.program_id(0)
    n_steps = pl.num_programs(0)
    bufs = (buf0, buf1)

    @pl.when(i == 0)
    def _():
        _row_gather(pos_ref, 2 * TM_CMB, y_hbm, buf0, sem.at[0])

    for slot in range(2):
        @pl.when((i % 2 == slot) & (i + 1 < n_steps))
        def _(slot=slot):
            _row_gather(posn_ref, 2 * TM_CMB, y_hbm, bufs[1 - slot], sem.at[1 - slot])

    for slot in range(2):
        @pl.when(i % 2 == slot)
        def _(slot=slot):
            _row_gather_wait(2 * TM_CMB, y_hbm, bufs[slot], sem.at[slot])
            y_1 = _rows_from_tiles(bufs[slot], 0, TM_CMB)
            y_2 = _rows_from_tiles(bufs[slot], TM_CMB, TM_CMB)
            x = x1_ref[...] + route_ref[:, 2:3] * y_1 + route_ref[:, 3:4] * y_2
            o_ref[...] = _rms(x, g_ref[...])


def _combine_call(pos, y_flat, x1, route, g):
    n = x1.shape[0]
    n_steps = n // TM_CMB
    row = lambda i: (i, 0)
    return pl.pallas_call(
        _combine_kernel,
        grid=(n_steps,),
        in_specs=[
            pl.BlockSpec((1, 1, 2 * TM_CMB), lambda i: (i, 0, 0), memory_space=pltpu.SMEM),
            pl.BlockSpec((1, 1, 2 * TM_CMB), lambda i: (jnp.minimum(i + 1, n_steps - 1), 0, 0),
                         memory_space=pltpu.SMEM),
            pl.BlockSpec(memory_space=pl.ANY),
            pl.BlockSpec((TM_CMB, D_MODEL), row),
            pl.BlockSpec((TM_CMB, LANES), row),
            pl.BlockSpec((1, D_MODEL), lambda i: (0, 0)),
        ],
        out_specs=pl.BlockSpec((TM_CMB, D_MODEL), row),
        out_shape=jax.ShapeDtypeStruct((n, D_MODEL), F32),
        scratch_shapes=[pltpu.VMEM((2 * TM_CMB * ROW_CHUNKS, LANES), F32),
                        pltpu.VMEM((2 * TM_CMB * ROW_CHUNKS, LANES), F32),
                        pltpu.SemaphoreType.DMA((2,))],
        compiler_params=pltpu.CompilerParams(dimension_semantics=("arbitrary",),
                                             vmem_limit_bytes=VMEM_LIMIT),
        name="combine",
    )(pos, pos, y_flat, x1, route, g)


def _layer(x2, batch, seq, norm_mix_g, w_in, b_gate, conv_w, w_attn_out, w_conv_out, w_out, norm_ffn_g,
           w_route_group, b_route_group, w_route_expert, b_route_expert, w1, w3, w2, final_g):
    n = x2.shape[0]
    q0, kv0, q1, kv1, q2, kv2, bgate, u, gates = _proj_call(
        x2, norm_mix_g[None, :], w_in.astype(BF16), b_gate[None, :], batch, seq)
    y_attn = _attn_call(q0, kv0, q1, kv1, q2, kv2, batch, seq)

    n_route = N_EXPERT_GROUPS + N_EXPERTS
    w_route = jnp.pad(jnp.concatenate([w_route_group, w_route_expert], axis=1), ((0, 0), (0, LANES - n_route)))
    b_route = jnp.pad(jnp.concatenate([b_route_group, b_route_expert]), (0, LANES - n_route))[None, :]
    wr_hi = w_route.astype(BF16)
    wr_lo = (w_route - wr_hi.astype(F32)).astype(BF16)
    n_tiles = (2 * n) // TM_MOE + N_EXPERTS
    x1, route, route_t, alloc, xs_flat = _mix_call(
        x2, y_attn, bgate, u, gates, w_attn_out.astype(BF16), w_conv_out.astype(BF16), w_out.astype(BF16),
        conv_w, norm_ffn_g[None, :], jnp.concatenate([wr_hi, wr_lo], axis=1), wr_hi, b_route, seq, n_tiles)

    i32 = jnp.int32
    taken = alloc[:, AL_NEW, :N_EXPERTS].astype(i32).reshape(-1)
    k = jnp.arange(taken.shape[0], dtype=i32)
    running = jnp.sum(jnp.where(k[:, None] >= k[None, :], taken[None, :], 0), axis=1)
    n_used = running[-1:]
    tile = jnp.arange(n_tiles, dtype=i32)
    owner = jnp.sum((running[None, :] <= tile[:, None]).astype(i32), axis=1) % N_EXPERTS
    owner = jnp.where(tile < n_used[0], owner, N_EXPERTS)
    key = owner * n_tiles + tile
    place = jnp.sum((key[None, :] < key[:, None]).astype(i32), axis=1)
    at = place[None, :] == tile[:, None]
    order = jnp.sum(jnp.where(at, tile[None, :], 0), axis=1)
    step_expert = jnp.minimum(jnp.sum(jnp.where(at, owner[None, :], 0), axis=1), N_EXPERTS - 1)

    y_flat = _expert_call(order, step_expert, n_used, xs_flat.reshape(-1, ROW_CHUNKS, LANES), w1, w3, w2)
    pos = route_t[RT_POS:RT_POS + 2].astype(i32)
    pos_tiles = pos.reshape(2, n // TM_CMB, TM_CMB).transpose(1, 0, 2).reshape(n // TM_CMB, 1, 2 * TM_CMB)
    return _combine_call(pos_tiles, y_flat, x1, route, final_g[None, :])


def kernel(x, norm_mix_g, w_in, b_gate, conv_w, w_attn_out, w_conv_out, w_out, norm_ffn_g,
           w_route_group, b_route_group, w_route_expert, b_route_expert, w1, w3, w2, norm_final_g):
    batch, seq, d = x.shape
    depth = w_in.shape[0]
    assert d == D_MODEL and depth == 1 and seq % T_ATT == 0
    out = _layer(x.reshape(batch * seq, d), batch, seq, norm_mix_g[0], w_in[0], b_gate[0], conv_w[0],
                 w_attn_out[0], w_conv_out[0], w_out[0], norm_ffn_g[0], w_route_group[0], b_route_group[0],
                 w_route_expert[0], b_route_expert[0], w1[0], w3[0], w2[0], norm_final_g)
    return out.reshape(batch, seq, d)
```

```python
import functools

import numpy as np
import jax
import jax.numpy as jnp
from jax import lax
from jax.experimental import pallas as pl
from jax.experimental.pallas import tpu as pltpu

F32 = jnp.float32
BF16 = jnp.bfloat16

D_MODEL = 1024
HEAD_DIM = 64
HEADS_PER_GROUP = 4
DILATED_PATTERNS = ((128, 1), (512, 4), (2048, 16))
N_GROUPS_A = 3
N_HEADS_A = N_GROUPS_A * HEADS_PER_GROUP
ATTN_WIDTH = N_HEADS_A * HEAD_DIM
GROUP_WIDTH = HEADS_PER_GROUP * HEAD_DIM
ALIBI_SPAN = 8.0
MASK_VALUE = -1e30
CONV_WIDTH = 768
N_EXPERT_GROUPS = 4
EXPERTS_PER_GROUP = 8
N_EXPERTS = 32
EXPERT_FF = 512
RMS_EPS = 1e-6

HALF = 64
LANES = 128
SUBLANES = 8
ROW_CHUNKS = D_MODEL // LANES

COL_K = ATTN_WIDTH
COL_V = 2 * ATTN_WIDTH
COL_BG = 3 * ATTN_WIDTH
COL_CG = COL_BG + CONV_WIDTH
COL_XIN = COL_CG + CONV_WIDTH
COL_GATE = COL_XIN + CONV_WIDTH
IN_COLS = COL_GATE + 2 * D_MODEL

TM_PROJ = 512
T_ATT = 2048
QB = 128
KB = QB + 2 * HALF
ATT_UNROLL = 8
TM_MIX = 512
TM_MOE = 512
TM_CMB = 256

VMEM_LIMIT = 56 * 1024 * 1024


def _alibi_slopes():
    return np.array([2.0 ** (-ALIBI_SPAN * (i + 1) / N_HEADS_A) for i in range(N_HEADS_A)],
                    dtype=np.float32).reshape(N_GROUPS_A, HEADS_PER_GROUP)


def _rms(x, g):
    return x * lax.rsqrt(jnp.mean(x * x, axis=-1, keepdims=True) + RMS_EPS) * g


def _proj_kernel(x_ref, g_ref, w_ref, b_ref,
                 q0_ref, kv0_ref, q1_ref, kv1_ref, q2_ref, kv2_ref, bg_ref, u_ref, gate_ref, scr):
    h = _rms(x_ref[...], g_ref[...]).astype(BF16)

    def proj(c0, width):
        return jnp.dot(h, w_ref[:, c0:c0 + width], preferred_element_type=F32)

    qscale = HEAD_DIM ** -0.5
    q0_ref[...] = (proj(0, GROUP_WIDTH) * qscale).astype(BF16)
    kv0_ref[:, 0:GROUP_WIDTH] = proj(COL_K, GROUP_WIDTH).astype(BF16)
    kv0_ref[:, GROUP_WIDTH:] = proj(COL_V, GROUP_WIDTH).astype(BF16)

    for g, q_ref, kv_ref in ((1, q1_ref, kv1_ref), (2, q2_ref, kv2_ref)):
        d = DILATED_PATTERNS[g][1]
        n = TM_PROJ // d
        parts = (proj(g * GROUP_WIDTH, GROUP_WIDTH) * qscale,
                 proj(COL_K + g * GROUP_WIDTH, GROUP_WIDTH),
                 proj(COL_V + g * GROUP_WIDTH, GROUP_WIDTH))
        for i, part in enumerate(parts):
            for c in range(2):
                scr[2 * i + c] = part[:, c * LANES:(c + 1) * LANES]
        for r in range(d):
            rows = pl.ds(r, n, stride=d)
            q_ref[r] = jnp.concatenate([scr[c, rows, :] for c in range(2)], axis=1).astype(BF16)
            kv_ref[r] = jnp.concatenate([scr[c, rows, :] for c in range(2, 6)], axis=1).astype(BF16)

    bg_ref[...] = proj(COL_BG, CONV_WIDTH).astype(BF16)
    u_ref[...] = (proj(COL_CG, CONV_WIDTH) * proj(COL_XIN, CONV_WIDTH)).astype(BF16)
    for c in range(4):
        w = 2 * D_MODEL // 4
        z = proj(COL_GATE + c * w, w) + b_ref[:, c * w:(c + 1) * w]
        gate_ref[:, c * w:(c + 1) * w] = jax.nn.sigmoid(z).astype(BF16)


def _proj_call(x2, g, w_in, b_gate, batch, seq):
    n = x2.shape[0]
    steps_per_batch = seq // TM_PROJ
    d1, d2 = DILATED_PATTERNS[1][1], DILATED_PATTERNS[2][1]
    row = lambda i: (i, 0)
    res = lambda i: (i // steps_per_batch, 0, i % steps_per_batch, 0)
    const = lambda i: (0, 0)
    out_shape = [
        jax.ShapeDtypeStruct((n, GROUP_WIDTH), BF16),
        jax.ShapeDtypeStruct((n, 2 * GROUP_WIDTH), BF16),
        jax.ShapeDtypeStruct((batch, d1, seq // d1, GROUP_WIDTH), BF16),
        jax.ShapeDtypeStruct((batch, d1, seq // d1, 2 * GROUP_WIDTH), BF16),
        jax.ShapeDtypeStruct((batch, d2, seq // d2, GROUP_WIDTH), BF16),
        jax.ShapeDtypeStruct((batch, d2, seq // d2, 2 * GROUP_WIDTH), BF16),
        jax.ShapeDtypeStruct((n, CONV_WIDTH), BF16),
        jax.ShapeDtypeStruct((n, CONV_WIDTH), BF16),
        jax.ShapeDtypeStruct((n, 2 * D_MODEL), BF16),
    ]
    out_specs = [
        pl.BlockSpec((TM_PROJ, GROUP_WIDTH), row),
        pl.BlockSpec((TM_PROJ, 2 * GROUP_WIDTH), row),
        pl.BlockSpec((None, d1, TM_PROJ // d1, GROUP_WIDTH), res),
        pl.BlockSpec((None, d1, TM_PROJ // d1, 2 * GROUP_WIDTH), res),
        pl.BlockSpec((None, d2, TM_PROJ // d2, GROUP_WIDTH), res),
        pl.BlockSpec((None, d2, TM_PROJ // d2, 2 * GROUP_WIDTH), res),
        pl.BlockSpec((TM_PROJ, CONV_WIDTH), row),
        pl.BlockSpec((TM_PROJ, CONV_WIDTH), row),
        pl.BlockSpec((TM_PROJ, 2 * D_MODEL), row),
    ]
    return pl.pallas_call(
        _proj_kernel,
        grid=(n // TM_PROJ,),
        in_specs=[
            pl.BlockSpec((TM_PROJ, D_MODEL), row),
            pl.BlockSpec((1, D_MODEL), const),
            pl.BlockSpec((D_MODEL, IN_COLS), const),
            pl.BlockSpec((1, 2 * D_MODEL), const),
        ],
        out_specs=out_specs,
        out_shape=out_shape,
        scratch_shapes=[pltpu.VMEM((6, TM_PROJ, LANES), F32)],
        compiler_params=pltpu.CompilerParams(dimension_semantics=("arbitrary",),
                                             vmem_limit_bytes=VMEM_LIMIT),
        name="proj",
    )(x2, g, w_in, b_gate)


def _attn_sub_block(q_sub, kw, vw, bias_ref, g, lo, hi):
    assert KB == GROUP_WIDTH
    lane = lax.broadcasted_iota(jnp.int32, (QB, KB), 1)
    edge_ok = (lane >= lo) & (lane < hi)
    heads = [(lane >= h * HEAD_DIM) & (lane < (h + 1) * HEAD_DIM) for h in range(HEADS_PER_GROUP)]
    zero = jnp.zeros((), BF16)
    q_stack = jnp.concatenate([jnp.where(hm, q_sub, zero) for hm in heads], axis=0)
    s_all = lax.dot_general(q_stack, kw, (((1,), (1,)), ((), ())), preferred_element_type=F32)
    probs = []
    m_b = l_b = None
    for h, hm in enumerate(heads):
        s = s_all[h * QB:(h + 1) * QB] + bias_ref[g * HEADS_PER_GROUP + h]
        s = jnp.where(edge_ok, s, MASK_VALUE)
        m = jnp.max(s, axis=1, keepdims=True)
        p = jnp.exp(s - m)
        l = jnp.sum(p, axis=1, keepdims=True)
        probs.append(p.astype(BF16))
        m_b = jnp.broadcast_to(m, (QB, GROUP_WIDTH)) if m_b is None else jnp.where(hm, m, m_b)
        l_b = jnp.broadcast_to(l, (QB, GROUP_WIDTH)) if l_b is None else jnp.where(hm, l, l_b)
    o_all = jnp.dot(jnp.concatenate(probs, axis=0), vw, preferred_element_type=F32)
    acc = o_all[0:QB]
    for h in range(1, HEADS_PER_GROUP):
        acc = jnp.where(heads[h], o_all[h * QB:(h + 1) * QB], acc)
    return acc, m_b, l_b


def _attn_kernel(q0_ref, kv0_ref, kv0p_ref, kv0n_ref,
                 q1_ref, kv1_ref, kv1p_ref, kv1n_ref,
                 q2_ref, kv2_ref, kv2p_ref, kv2n_ref,
                 y_ref,
                 cat0, cat1, cat2, bias_ref, m_st, l_st, a_st, m_tmp, l_tmp, a_tmp, *, seq):
    j = pl.program_id(1)

    qi = lax.broadcasted_iota(jnp.int32, (QB, KB), 0)
    kc = lax.broadcasted_iota(jnp.int32, (QB, KB), 1)
    adelta = jnp.abs(kc - HALF - qi)
    band = adelta <= HALF
    slopes = _alibi_slopes()
    for g in range(N_GROUPS_A):
        dist = (adelta * DILATED_PATTERNS[g][1]).astype(F32)
        for h in range(HEADS_PER_GROUP):
            bias_ref[g * HEADS_PER_GROUP + h] = jnp.where(band, -(float(slopes[g, h]) * dist), MASK_VALUE)

    for cat, own, prv, nxt in ((cat0, kv0_ref, kv0p_ref, kv0n_ref),
                               (cat1, kv1_ref, kv1p_ref, kv1n_ref),
                               (cat2, kv2_ref, kv2p_ref, kv2n_ref)):
        n_own = own.shape[-2]
        cat[:, 0:HALF, :] = prv[...].reshape(cat.shape[0], HALF, 2 * GROUP_WIDTH)
        cat[:, HALF:HALF + n_own, :] = own[...].reshape(cat.shape[0], n_own, 2 * GROUP_WIDTH)
        cat[:, HALF + n_own:, :] = nxt[...].reshape(cat.shape[0], HALF, 2 * GROUP_WIDTH)

    def window(cat, r, sb):
        rows = pl.ds(pl.multiple_of(sb * QB, QB), KB)
        return cat[r, rows, 0:GROUP_WIDTH], cat[r, rows, GROUP_WIDTH:]

    def edges(g, n_res, sb):
        length = seq // DILATED_PATTERNS[g][1]
        i0 = j * n_res + sb * QB
        return jnp.maximum(0, HALF - i0), jnp.minimum(KB, length + HALF - i0)

    def body0(sb, carry):
        rows = pl.ds(pl.multiple_of(sb * QB, QB), QB)
        kw, vw = window(cat0, 0, sb)
        lo, hi = edges(0, T_ATT, sb)
        acc, m_b, l_b = _attn_sub_block(q0_ref[rows, :], kw, vw, bias_ref, 0, lo, hi)
        for c in range(2):
            cols = slice(c * LANES, (c + 1) * LANES)
            m_st[c, rows, :] = m_b[:, cols]
            l_st[c, rows, :] = l_b[:, cols]
            a_st[c, rows, :] = acc[:, cols]
        return carry

    lax.fori_loop(0, T_ATT // QB, body0, 0, unroll=ATT_UNROLL)

    for g, q_ref, cat in ((1, q1_ref, cat1), (2, q2_ref, cat2)):
        d = DILATED_PATTERNS[g][1]
        n_res = T_ATT // d
        sb_per_res = n_res // QB

        def body(idx, carry, g=g, q_ref=q_ref, cat=cat, n_res=n_res, sb_per_res=sb_per_res):
            r = idx // sb_per_res
            sb = idx % sb_per_res
            kw, vw = window(cat, r, sb)
            lo, hi = edges(g, n_res, sb)
            q_sub = q_ref[r, pl.ds(pl.multiple_of(sb * QB, QB), QB), :]
            acc, m_b, l_b = _attn_sub_block(q_sub, kw, vw, bias_ref, g, lo, hi)
            rows = pl.ds(pl.multiple_of(idx * QB, QB), QB)
            m_tmp[rows, :] = m_b
            l_tmp[rows, :] = l_b
            a_tmp[rows, :] = acc
            return carry

        lax.fori_loop(0, T_ATT // QB, body, 0, unroll=ATT_UNROLL)

        for r in range(d):
            for ch in range(sb_per_res):
                src = slice(r * n_res + ch * QB, r * n_res + (ch + 1) * QB)
                tok = pl.ds(ch * QB * d + r, QB, stride=d)
                for c in range(2):
                    cols = slice(c * LANES, (c + 1) * LANES)
                    m_new_part = m_tmp[src, cols]
                    m_old = m_st[c, tok, :]
                    m_new = jnp.maximum(m_old, m_new_part)
                    e_old = jnp.exp(m_old - m_new)
                    e_new = jnp.exp(m_new_part - m_new)
                    m_st[c, tok, :] = m_new
                    l_st[c, tok, :] = e_old * l_st[c, tok, :] + e_new * l_tmp[src, cols]
                    a_st[c, tok, :] = e_old * a_st[c, tok, :] + e_new * a_tmp[src, cols]

    for c in range(2):
        y_ref[:, c * LANES:(c + 1) * LANES] = (a_st[c] / l_st[c]).astype(BF16)


def _attn_call(q0, kv0, q1, kv1, q2, kv2, batch, seq):
    n = q0.shape[0]
    tiles = seq // T_ATT
    specs = []
    scratch = []
    blocks_per_tile = T_ATT // HALF
    n_half_blocks = n // HALF
    specs += [
        pl.BlockSpec((T_ATT, GROUP_WIDTH), lambda b, j: (b * tiles + j, 0)),
        pl.BlockSpec((T_ATT, 2 * GROUP_WIDTH), lambda b, j: (b * tiles + j, 0)),
        pl.BlockSpec((HALF, 2 * GROUP_WIDTH),
                     lambda b, j: (jnp.maximum((b * tiles + j) * blocks_per_tile - 1, 0), 0)),
        pl.BlockSpec((HALF, 2 * GROUP_WIDTH),
                     lambda b, j: (jnp.minimum((b * tiles + j + 1) * blocks_per_tile, n_half_blocks - 1), 0)),
    ]
    scratch.append(pltpu.VMEM((1, T_ATT + 2 * HALF, 2 * GROUP_WIDTH), BF16))
    for g in (1, 2):
        d = DILATED_PATTERNS[g][1]
        n_res = T_ATT // d
        per_tile = n_res // HALF
        last = seq // d // HALF - 1
        specs += [
            pl.BlockSpec((None, d, n_res, GROUP_WIDTH), lambda b, j: (b, 0, j, 0)),
            pl.BlockSpec((None, d, n_res, 2 * GROUP_WIDTH), lambda b, j: (b, 0, j, 0)),
            pl.BlockSpec((None, d, HALF, 2 * GROUP_WIDTH),
                         lambda b, j, per_tile=per_tile: (b, 0, jnp.maximum(j * per_tile - 1, 0), 0)),
            pl.BlockSpec((None, d, HALF, 2 * GROUP_WIDTH),
                         lambda b, j, per_tile=per_tile, last=last: (b, 0, jnp.minimum((j + 1) * per_tile, last), 0)),
        ]
        scratch.append(pltpu.VMEM((d, n_res + 2 * HALF, 2 * GROUP_WIDTH), BF16))
    scratch.append(pltpu.VMEM((N_HEADS_A, QB, KB), F32))
    scratch += [pltpu.VMEM((2, T_ATT, LANES), F32) for _ in range(3)]
    scratch += [pltpu.VMEM((T_ATT, GROUP_WIDTH), F32) for _ in range(3)]
    return pl.pallas_call(
        functools.partial(_attn_kernel, seq=seq),
        grid=(batch, tiles),
        in_specs=specs,
        out_specs=pl.BlockSpec((T_ATT, GROUP_WIDTH), lambda b, j: (b * tiles + j, 0)),
        out_shape=jax.ShapeDtypeStruct((n, GROUP_WIDTH), BF16),
        scratch_shapes=scratch,
        compiler_params=pltpu.CompilerParams(dimension_semantics=("arbitrary", "arbitrary"),
                                             vmem_limit_bytes=VMEM_LIMIT),
        name="attn",
    )(q0, kv0, kv0, kv0, q1, kv1, kv1, kv1, q2, kv2, kv2, kv2)


def _split_dot(a, w_cat, w_hi):
    a_hi = a.astype(BF16)
    a_lo = (a - a_hi.astype(F32)).astype(BF16)
    both = jnp.dot(a_hi, w_cat, preferred_element_type=F32)
    return both[:, 0:LANES] + both[:, LANES:] + jnp.dot(a_lo, w_hi, preferred_element_type=F32)


ST_BASE, ST_TILE, ST_FREE = 0, 1, 2
AL_NEW = 0
RT_E, RT_W, RT_POS = 0, 2, 4


def _mix_kernel(x_ref, ya_ref, bg_ref, u_ref, up_ref, un_ref, gate_ref,
                wa_ref, wc_ref, wo_ref, cw_ref, g2_ref, wr_cat_ref, wr_hi_ref, br_ref,
                x1_ref, route_ref, routet_ref, alloc_ref, xs_hbm,
                st_ref, hbuf, posv, poss, psem, dsem, *, seq, trash_row):
    i = pl.program_id(0)
    last_step = pl.num_programs(0) - 1
    slot = i % 2
    t0 = i * TM_MIX

    def rows_done(sset):
        for k in range(2):
            pltpu.make_async_copy(hbuf.at[0], xs_hbm.at[pl.ds(0, TM_MIX * SUBLANES)], dsem.at[sset, k]).wait()

    def send_row(src_slot, sset, j):
        src = hbuf.at[src_slot, pl.ds(pl.multiple_of(j * SUBLANES, SUBLANES), SUBLANES)]
        for k in range(2):
            p = poss[src_slot, RT_POS + k, j]
            dst = xs_hbm.at[pl.ds(pl.multiple_of(p * SUBLANES, SUBLANES), SUBLANES)]
            pltpu.make_async_copy(src, dst, dsem.at[sset, k]).start(priority=k)

    def send_rows(src_slot, sset):
        def one(j, carry):
            send_row(src_slot, sset, j)
            return carry
        lax.fori_loop(0, TM_MIX, one, 0, unroll=8)

    def send_previous_rows(part):
        for j in range(part * TM_MIX // 4, (part + 1) * TM_MIX // 4):
            send_row(1 - slot, slot, j)

    @pl.when(i == 0)
    def _():
        st_ref[...] = jnp.zeros_like(st_ref)
        hbuf[1] = jnp.zeros((TM_MIX * ROW_CHUNKS, LANES), F32)
        spare = (trash_row + lax.broadcasted_iota(jnp.int32, posv.shape, 1)
                 + jnp.where(lax.broadcasted_iota(jnp.int32, posv.shape, 0) == RT_POS + 1, TM_MIX, 0))
        posv[...] = spare
        init = pltpu.make_async_copy(posv, poss.at[1], psem)
        init.start()
        init.wait()

    @pl.when(i > 0)
    def _():
        rows_done(1 - slot)

    u = u_ref[...].astype(F32)
    row = lax.broadcasted_iota(jnp.int32, (TM_MIX, CONV_WIDTH), 0)
    prev_row = jnp.where(t0 % seq == 0, 0.0, up_ref[15:16, :].astype(F32))
    next_row = jnp.where((t0 + TM_MIX) % seq == 0, 0.0, un_ref[0:1, :].astype(F32))
    u_prev = jnp.where(row == 0, prev_row, pltpu.roll(u, 1, axis=0))
    u_next = jnp.where(row == TM_MIX - 1, next_row, pltpu.roll(u, TM_MIX - 1, axis=0))
    conv = cw_ref[0:1, :] * u_prev + cw_ref[1:2, :] * u + cw_ref[2:3, :] * u_next
    yb_in = (bg_ref[...].astype(F32) * conv).astype(BF16)
    send_previous_rows(0)

    y_a = jnp.dot(ya_ref[...], wa_ref[...], preferred_element_type=F32)
    send_previous_rows(1)
    y_b = jnp.dot(yb_in, wc_ref[...], preferred_element_type=F32)
    send_previous_rows(2)
    merged = gate_ref[:, 0:D_MODEL] * y_a.astype(BF16) + gate_ref[:, D_MODEL:] * y_b.astype(BF16)
    x1 = x_ref[...] + jnp.dot(merged, wo_ref[...], preferred_element_type=F32)
    x1_ref[...] = x1
    send_previous_rows(3)

    h2 = _rms(x1, g2_ref[...])
    for c in range(ROW_CHUNKS):
        hbuf[slot, pl.ds(c, TM_MIX, stride=ROW_CHUNKS), :] = h2[:, c * LANES:(c + 1) * LANES]

    logits = _split_dot(h2, wr_cat_ref[...], wr_hi_ref[...]) + br_ref[...]
    lane = lax.broadcasted_iota(jnp.int32, (TM_MIX, LANES), 1)
    lane_f = lane.astype(F32)
    neg = -jnp.inf
    big = float(LANES)
    is_group = lane < N_EXPERT_GROUPS
    cm = jnp.where(is_group, logits, neg)
    cmax = jnp.max(cm, axis=1, keepdims=True)
    g_idx = jnp.min(jnp.where(cm == cmax, lane_f, big), axis=1, keepdims=True)
    p_group = 1.0 / jnp.sum(jnp.where(is_group, jnp.exp(logits - cmax), 0.0), axis=1, keepdims=True)
    f_lo = N_EXPERT_GROUPS + EXPERTS_PER_GROUP * g_idx
    in_group = (lane_f >= f_lo) & (lane_f < f_lo + EXPERTS_PER_GROUP)
    fm = jnp.where(in_group, logits, neg)
    f1 = jnp.max(fm, axis=1, keepdims=True)
    i1 = jnp.min(jnp.where(fm == f1, lane_f, big), axis=1, keepdims=True)
    fm2 = jnp.where(lane_f == i1, neg, fm)
    f2 = jnp.max(fm2, axis=1, keepdims=True)
    i2 = jnp.min(jnp.where(fm2 == f2, lane_f, big), axis=1, keepdims=True)
    e21 = jnp.exp(f2 - f1)
    w_1 = p_group / (1.0 + e21)
    w_2 = p_group * e21 / (1.0 + e21)
    e_1 = i1 - N_EXPERT_GROUPS
    e_2 = i2 - N_EXPERT_GROUPS

    onehot = jnp.where((lane_f == e_1) | (lane_f == e_2), 1.0, 0.0)
    r_i = lax.broadcasted_iota(jnp.int32, (TM_MIX, TM_MIX), 0)
    c_i = lax.broadcasted_iota(jnp.int32, (TM_MIX, TM_MIX), 1)
    tri = jnp.where(c_i < r_i, 1.0, 0.0).astype(BF16)
    base = st_ref[ST_BASE:ST_BASE + 1, :]
    before = jnp.dot(tri, onehot.astype(BF16), preferred_element_type=F32) + base

    tile_rows = float(TM_MOE)
    cur_tile = st_ref[ST_TILE:ST_TILE + 1, :]
    next_free = st_ref[ST_FREE:ST_FREE + 1, :]
    count = jnp.sum(onehot, axis=0, keepdims=True)
    slot0 = jnp.floor(base * (1.0 / tile_rows))
    partial = (base - slot0 * tile_rows) > 0.0
    slot_last = jnp.floor((base + count - 1.0) * (1.0 / tile_rows))
    n_new = jnp.where(count > 0.0, slot_last - slot0 + 1.0 - jnp.where(partial, 1.0, 0.0), 0.0)
    e_r = lax.broadcasted_iota(jnp.int32, (LANES, LANES), 0)
    e_c = lax.broadcasted_iota(jnp.int32, (LANES, LANES), 1)
    earlier = jnp.where(e_r < e_c, 1.0, 0.0).astype(BF16)
    new_before = jnp.dot(jnp.broadcast_to(n_new, (SUBLANES, LANES)).astype(BF16), earlier,
                         preferred_element_type=F32)[0:1, :]
    fresh = next_free + new_before - jnp.where(partial, 1.0, 0.0) - slot0
    partial_slot = jnp.where(partial, slot0, -1.0)

    def tile_of(slot_idx, fresh_v, cur_v, partial_v):
        return jnp.where(slot_idx == partial_v, cur_v, fresh_v + slot_idx)

    def pick(row_vec, e):
        return jnp.sum(jnp.where(lane_f == e, row_vec, 0.0), axis=1, keepdims=True)

    positions = []
    for e in (e_1, e_2):
        rank = pick(before, e)
        s = jnp.floor(rank * (1.0 / tile_rows))
        tile = tile_of(s, pick(fresh, e), pick(cur_tile, e), pick(partial_slot, e))
        positions.append(tile * tile_rows + (rank - s * tile_rows))

    st_ref[ST_BASE:ST_BASE + 1, :] = base + count
    st_ref[ST_TILE:ST_TILE + 1, :] = jnp.where(count > 0.0, tile_of(slot_last, fresh, cur_tile, partial_slot),
                                                cur_tile)
    st_ref[ST_FREE:ST_FREE + 1, :] = next_free + jnp.sum(n_new, axis=1, keepdims=True)
    alloc_ref[...] = jnp.broadcast_to(n_new, (SUBLANES, LANES))

    route = jnp.zeros((TM_MIX, LANES), F32)
    for k, val in enumerate((e_1, e_2, w_1, w_2, positions[0], positions[1])):
        route = jnp.where(lane == k, val, route)
    route_ref[...] = route
    route_t = route.T[0:SUBLANES, :]
    routet_ref[...] = route_t
    posv[...] = route_t.astype(jnp.int32)
    to_smem = pltpu.make_async_copy(posv, poss.at[slot], psem)
    to_smem.start()
    to_smem.wait()

    @pl.when(i == last_step)
    def _():
        rows_done(slot)
        send_rows(slot, 1 - slot)
        rows_done(1 - slot)

        posv[:, 0:LANES] = st_ref[...].astype(jnp.int32)
        state = pltpu.make_async_copy(posv, poss.at[0], psem)
        state.start()
        state.wait()
        hbuf[0] = jnp.zeros((TM_MIX * ROW_CHUNKS, LANES), F32)

        def zero_jobs():
            for e in range(N_EXPERTS):
                fill = poss[0, ST_BASE, e] & (TM_MOE - 1)
                first = poss[0, ST_TILE, e] * TM_MOE
                at = fill
                size = 1
                while size < TM_MOE:
                    take = (fill > 0) & ((at & size) != 0)
                    yield take, first + at, size
                    at = at + jnp.where(take, size, 0)
                    size *= 2
            for t in range(N_EXPERTS):
                tile = poss[0, ST_FREE, 0] + t
                yield tile < trash_row // TM_MOE, tile * TM_MOE, TM_MOE

        def zero_copy(first, rows):
            return pltpu.make_async_copy(
                hbuf.at[0, pl.ds(0, rows * SUBLANES)],
                xs_hbm.at[pl.ds(pl.multiple_of(first * SUBLANES, SUBLANES), rows * SUBLANES)], psem)

        for take, first, rows in zero_jobs():
            pl.when(take)(lambda first=first, rows=rows: zero_copy(first, rows).start())
        for take, first, rows in zero_jobs():
            pl.when(take)(lambda first=first, rows=rows: zero_copy(first, rows).wait())


def _mix_call(x2, y_attn, bgate, u, gates, wa, wc, wo, conv_w, g2, wr_cat, wr_hi, br, seq, n_tiles):
    n = x2.shape[0]
    n_steps = n // TM_MIX
    row = lambda i: (i, 0)
    const = lambda i: (0, 0)
    halo = 16
    per_tile = TM_MIX // halo
    last = n // halo - 1
    xs_rows = n_tiles * TM_MOE + 2 * TM_MIX
    return pl.pallas_call(
        functools.partial(_mix_kernel, seq=seq, trash_row=n_tiles * TM_MOE),
        grid=(n_steps,),
        in_specs=[
            pl.BlockSpec((TM_MIX, D_MODEL), row),
            pl.BlockSpec((TM_MIX, GROUP_WIDTH), row),
            pl.BlockSpec((TM_MIX, CONV_WIDTH), row),
            pl.BlockSpec((TM_MIX, CONV_WIDTH), row),
            pl.BlockSpec((halo, CONV_WIDTH), lambda i: (jnp.maximum(i * per_tile - 1, 0), 0)),
            pl.BlockSpec((halo, CONV_WIDTH), lambda i: (jnp.minimum((i + 1) * per_tile, last), 0)),
            pl.BlockSpec((TM_MIX, 2 * D_MODEL), row),
            pl.BlockSpec((GROUP_WIDTH, D_MODEL), const),
            pl.BlockSpec((CONV_WIDTH, D_MODEL), const),
            pl.BlockSpec((D_MODEL, D_MODEL), const),
            pl.BlockSpec((3, CONV_WIDTH), const),
            pl.BlockSpec((1, D_MODEL), const),
            pl.BlockSpec((D_MODEL, 2 * LANES), const),
            pl.BlockSpec((D_MODEL, LANES), const),
            pl.BlockSpec((1, LANES), const),
        ],
        out_specs=[
            pl.BlockSpec((TM_MIX, D_MODEL), row),
            pl.BlockSpec((TM_MIX, LANES), row),
            pl.BlockSpec((SUBLANES, TM_MIX), lambda i: (0, i)),
            pl.BlockSpec((None, SUBLANES, LANES), lambda i: (i, 0, 0)),
            pl.BlockSpec(memory_space=pl.ANY),
        ],
        out_shape=[
            jax.ShapeDtypeStruct((n, D_MODEL), F32),
            jax.ShapeDtypeStruct((n, LANES), F32),
            jax.ShapeDtypeStruct((SUBLANES, n), F32),
            jax.ShapeDtypeStruct((n_steps, SUBLANES, LANES), F32),
            jax.ShapeDtypeStruct((xs_rows * ROW_CHUNKS, LANES), F32),
        ],
        scratch_shapes=[pltpu.VMEM((SUBLANES, LANES), F32),
                        pltpu.VMEM((2, TM_MIX * ROW_CHUNKS, LANES), F32),
                        pltpu.VMEM((SUBLANES, TM_MIX), jnp.int32),
                        pltpu.SMEM((2, SUBLANES, TM_MIX), jnp.int32),
                        pltpu.SemaphoreType.DMA(()),
                        pltpu.SemaphoreType.DMA((2, 2))],
        compiler_params=pltpu.CompilerParams(dimension_semantics=("arbitrary",),
                                             vmem_limit_bytes=VMEM_LIMIT),
        name="mix",
    )(x2, y_attn, bgate, u, u, u, gates, wa, wc, wo, conv_w, g2, wr_cat, wr_hi, br)


def _row_gather(idx_ref, n_rows, src_hbm, dst, sem):
    def issue(pair, carry):
        for k in range(2):
            j = 2 * pair + k
            t = idx_ref[0, 0, j]
            pltpu.make_async_copy(src_hbm.at[pl.ds(pl.multiple_of(t * SUBLANES, SUBLANES), SUBLANES)],
                                  dst.at[pl.ds(pl.multiple_of(j * SUBLANES, SUBLANES), SUBLANES)],
                                  sem).start(priority=k)
        return carry
    lax.fori_loop(0, n_rows // 2, issue, 0, unroll=4)


def _row_gather_inline(idx_ref, n_rows, src_hbm, dst, sem):
    for j in range(n_rows):
        t = idx_ref[0, 0, j]
        pltpu.make_async_copy(src_hbm.at[pl.ds(pl.multiple_of(t * SUBLANES, SUBLANES), SUBLANES)],
                              dst.at[pl.ds(j * SUBLANES, SUBLANES)], sem).start(priority=j % 2)


def _row_gather_wait(n_rows, src_hbm, dst, sem):
    pltpu.make_async_copy(src_hbm.at[pl.ds(0, n_rows * SUBLANES)], dst, sem).wait()


def _rows_from_tiles(buf, first_row, n_rows):
    return jnp.concatenate(
        [buf[pl.ds(first_row * ROW_CHUNKS + c, n_rows, stride=ROW_CHUNKS), :] for c in range(ROW_CHUNKS)],
        axis=1)


def _expert_kernel(order_ref, te_ref, nused_ref, xs_hbm, w1_ref, w3_ref, w2_ref, y_ref,
                   xin, w13b, w2b, isem):
    i = pl.program_id(0)
    slot = i % 2
    used = i < nused_ref[0]

    def fetch(step, s):
        rows = pl.ds(order_ref[step] * TM_MOE, TM_MOE)
        return [pltpu.make_async_copy(xs_hbm.at[rows, c, :], xin.at[s, c], isem.at[s]) for c in range(ROW_CHUNKS)]

    @pl.when(i == 0)
    def _():
        for cp in fetch(0, 0):
            cp.start()

    @pl.when(i + 1 < nused_ref[0])
    def _():
        for cp in fetch(i + 1, 1 - slot):
            cp.start()

    @pl.when(jnp.logical_not(used))
    def _():
        y_ref[...] = jnp.zeros_like(y_ref)

    @pl.when(used & ((i == 0) | (te_ref[i] != te_ref[jnp.maximum(i - 1, 0)])))
    def _():
        w13b[:, 0:EXPERT_FF] = w1_ref[...].astype(BF16)
        w13b[:, EXPERT_FF:] = w3_ref[...].astype(BF16)
        w2b[...] = w2_ref[...].astype(BF16)

    @pl.when(used)
    def _():
        for cp in fetch(i, slot):
            cp.wait()
        x = jnp.concatenate([xin[slot, c] for c in range(ROW_CHUNKS)], axis=1)
        ab = jnp.dot(x.astype(BF16), w13b[...], preferred_element_type=F32)
        a = ab[:, 0:EXPERT_FF]
        hid = (a * jax.nn.sigmoid(a) * ab[:, EXPERT_FF:]).astype(BF16)
        y = jnp.dot(hid, w2b[...], preferred_element_type=F32)
        for c in range(ROW_CHUNKS):
            y_ref[pl.ds(c, TM_MOE, stride=ROW_CHUNKS), :] = y[:, c * LANES:(c + 1) * LANES]


def _expert_call(order, tile_expert, n_used, xs_rows, w1, w3, w2):
    n_tiles = order.shape[0]
    last = lambda i, nu: jnp.minimum(i, nu[0] - 1)
    wspec = lambda shape: pl.BlockSpec(
        (None,) + shape, lambda i, od, te, nu: (te[last(i, nu)], 0, 0))
    grid_spec = pltpu.PrefetchScalarGridSpec(
        num_scalar_prefetch=3,
        grid=(n_tiles,),
        in_specs=[
            pl.BlockSpec(memory_space=pl.ANY),
            wspec((D_MODEL, EXPERT_FF)),
            wspec((D_MODEL, EXPERT_FF)),
            wspec((EXPERT_FF, D_MODEL)),
        ],
        out_specs=pl.BlockSpec((TM_MOE * ROW_CHUNKS, LANES), lambda i, od, te, nu: (od[i], 0)),
        scratch_shapes=[pltpu.VMEM((2, ROW_CHUNKS, TM_MOE, LANES), F32),
                        pltpu.VMEM((D_MODEL, 2 * EXPERT_FF), BF16),
                        pltpu.VMEM((EXPERT_FF, D_MODEL), BF16),
                        pltpu.SemaphoreType.DMA((2,))],
    )
    return pl.pallas_call(
        _expert_kernel,
        grid_spec=grid_spec,
        out_shape=jax.ShapeDtypeStruct((n_tiles * TM_MOE * ROW_CHUNKS, LANES), F32),
        compiler_params=pltpu.CompilerParams(dimension_semantics=("arbitrary",),
                                             vmem_limit_bytes=VMEM_LIMIT),
        name="experts",
    )(order, tile_expert, n_used, xs_rows, w1, w3, w2)


def _combine_kernel(pos_ref, posn_ref, y_hbm, x1_ref, route_ref, g_ref, o_ref, buf0, buf1, sem):
    i = pl.program_id(0)
    bufs = (buf0, buf1)

    @pl.when(i == 0)
    def _():
        _row_gather(pos_ref, 2 * TM_CMB, y_hbm, buf0, sem.at[0])

    for slot in range(2):
        @pl.when(i % 2 == slot)
        def _(slot=slot):
            _row_gather_wait(2 * TM_CMB, y_hbm, bufs[slot], sem.at[slot])
            _row_gather_inline(posn_ref, 2 * TM_CMB, y_hbm, bufs[1 - slot], sem.at[1 - slot])
            y_1 = _rows_from_tiles(bufs[slot], 0, TM_CMB)
            y_2 = _rows_from_tiles(bufs[slot], TM_CMB, TM_CMB)
            x = x1_ref[...] + route_ref[:, 2:3] * y_1 + route_ref[:, 3:4] * y_2
            o_ref[...] = _rms(x, g_ref[...])

            @pl.when(i == pl.num_programs(0) - 1)
            def _():
                _row_gather_wait(2 * TM_CMB, y_hbm, bufs[1 - slot], sem.at[1 - slot])


def _combine_call(pos, y_flat, x1, route, g):
    n = x1.shape[0]
    n_steps = n // TM_CMB
    row = lambda i: (i, 0)
    return pl.pallas_call(
        _combine_kernel,
        grid=(n_steps,),
        in_specs=[
            pl.BlockSpec((1, 1, 2 * TM_CMB), lambda i: (i, 0, 0), memory_space=pltpu.SMEM),
            pl.BlockSpec((1, 1, 2 * TM_CMB), lambda i: (jnp.minimum(i + 1, n_steps - 1), 0, 0),
                         memory_space=pltpu.SMEM),
            pl.BlockSpec(memory_space=pl.ANY),
            pl.BlockSpec((TM_CMB, D_MODEL), row),
            pl.BlockSpec((TM_CMB, LANES), row),
            pl.BlockSpec((1, D_MODEL), lambda i: (0, 0)),
        ],
        out_specs=pl.BlockSpec((TM_CMB, D_MODEL), row),
        out_shape=jax.ShapeDtypeStruct((n, D_MODEL), F32),
        scratch_shapes=[pltpu.VMEM((2 * TM_CMB * ROW_CHUNKS, LANES), F32),
                        pltpu.VMEM((2 * TM_CMB * ROW_CHUNKS, LANES), F32),
                        pltpu.SemaphoreType.DMA((2,))],
        compiler_params=pltpu.CompilerParams(dimension_semantics=("arbitrary",),
                                             vmem_limit_bytes=VMEM_LIMIT),
        name="combine",
    )(pos, pos, y_flat, x1, route, g)


def _layer(x2, batch, seq, norm_mix_g, w_in, b_gate, conv_w, w_attn_out, w_conv_out, w_out, norm_ffn_g,
           w_route_group, b_route_group, w_route_expert, b_route_expert, w1, w3, w2, final_g):
    n = x2.shape[0]
    q0, kv0, q1, kv1, q2, kv2, bgate, u, gates = _proj_call(
        x2, norm_mix_g[None, :], w_in.astype(BF16), b_gate[None, :], batch, seq)
    y_attn = _attn_call(q0, kv0, q1, kv1, q2, kv2, batch, seq)

    n_route = N_EXPERT_GROUPS + N_EXPERTS
    w_route = jnp.pad(jnp.concatenate([w_route_group, w_route_expert], axis=1), ((0, 0), (0, LANES - n_route)))
    b_route = jnp.pad(jnp.concatenate([b_route_group, b_route_expert]), (0, LANES - n_route))[None, :]
    wr_hi = w_route.astype(BF16)
    wr_lo = (w_route - wr_hi.astype(F32)).astype(BF16)
    n_tiles = (2 * n) // TM_MOE + N_EXPERTS
    x1, route, route_t, alloc, xs_flat = _mix_call(
        x2, y_attn, bgate, u, gates, w_attn_out.astype(BF16), w_conv_out.astype(BF16), w_out.astype(BF16),
        conv_w, norm_ffn_g[None, :], jnp.concatenate([wr_hi, wr_lo], axis=1), wr_hi, b_route, seq, n_tiles)

    i32 = jnp.int32
    taken = alloc[:, AL_NEW, :N_EXPERTS].astype(i32).reshape(-1)
    k = jnp.arange(taken.shape[0], dtype=i32)
    running = jnp.sum(jnp.where(k[:, None] >= k[None, :], taken[None, :], 0), axis=1)
    n_used = running[-1:]
    tile = jnp.arange(n_tiles, dtype=i32)
    owner = jnp.sum((running[None, :] <= tile[:, None]).astype(i32), axis=1) % N_EXPERTS
    owner = jnp.where(tile < n_used[0], owner, N_EXPERTS)
    key = owner * n_tiles + tile
    place = jnp.sum((key[None, :] < key[:, None]).astype(i32), axis=1)
    at = place[None, :] == tile[:, None]
    order = jnp.sum(jnp.where(at, tile[None, :], 0), axis=1)
    step_expert = jnp.minimum(jnp.sum(jnp.where(at, owner[None, :], 0), axis=1), N_EXPERTS - 1)

    y_flat = _expert_call(order, step_expert, n_used, xs_flat.reshape(-1, ROW_CHUNKS, LANES), w1, w3, w2)
    pos = route_t[RT_POS:RT_POS + 2].astype(i32)
    pos_tiles = pos.reshape(2, n // TM_CMB, TM_CMB).transpose(1, 0, 2).reshape(n // TM_CMB, 1, 2 * TM_CMB)
    return _combine_call(pos_tiles, y_flat, x1, route, final_g[None, :])


def kernel(x, norm_mix_g, w_in, b_gate, conv_w, w_attn_out, w_conv_out, w_out, norm_ffn_g,
           w_route_group, b_route_group, w_route_expert, b_route_expert, w1, w3, w2, norm_final_g):
    batch, seq, d = x.shape
    depth = w_in.shape[0]
    assert d == D_MODEL and depth == 1 and seq % T_ATT == 0
    out = _layer(x.reshape(batch * seq, d), batch, seq, norm_mix_g[0], w_in[0], b_gate[0], conv_w[0],
                 w_attn_out[0], w_conv_out[0], w_out[0], norm_ffn_g[0], w_route_group[0], b_route_group[0],
                 w_route_expert[0], b_route_expert[0], w1[0], w3[0], w2[0], norm_final_g)
    return out.reshape(batch, seq, d)
```

```python
import functools

import numpy as np
import jax
import jax.numpy as jnp
from jax import lax
from jax.experimental import pallas as pl
from jax.experimental.pallas import tpu as pltpu

F32 = jnp.float32
BF16 = jnp.bfloat16

D_MODEL = 1024
HEAD_DIM = 64
HEADS_PER_GROUP = 4
DILATED_PATTERNS = ((128, 1), (512, 4), (2048, 16))
N_GROUPS_A = 3
N_HEADS_A = N_GROUPS_A * HEADS_PER_GROUP
ATTN_WIDTH = N_HEADS_A * HEAD_DIM
GROUP_WIDTH = HEADS_PER_GROUP * HEAD_DIM
ALIBI_SPAN = 8.0
MASK_VALUE = -1e30
CONV_WIDTH = 768
N_EXPERT_GROUPS = 4
EXPERTS_PER_GROUP = 8
N_EXPERTS = 32
EXPERT_FF = 512
RMS_EPS = 1e-6

HALF = 64
LANES = 128
SUBLANES = 8
ROW_CHUNKS = D_MODEL // LANES

COL_K = ATTN_WIDTH
COL_V = 2 * ATTN_WIDTH
COL_BG = 3 * ATTN_WIDTH
COL_CG = COL_BG + CONV_WIDTH
COL_XIN = COL_CG + CONV_WIDTH
COL_GATE = COL_XIN + CONV_WIDTH
IN_COLS = COL_GATE + 2 * D_MODEL

TM_PROJ = 1024
T_ATT = 2048
QB = 128
KB = QB + 2 * HALF
ATT_UNROLL = 8
TM_MIX = 512
MIX_TILES = 2
TM_MOE = 512
TM_CMB = 256

VMEM_LIMIT = 56 * 1024 * 1024


def _alibi_slopes():
    return np.array([2.0 ** (-ALIBI_SPAN * (i + 1) / N_HEADS_A) for i in range(N_HEADS_A)],
                    dtype=np.float32).reshape(N_GROUPS_A, HEADS_PER_GROUP)


def _rms(x, g):
    return x * lax.rsqrt(jnp.mean(x * x, axis=-1, keepdims=True) + RMS_EPS) * g


def _proj_kernel(x_ref, g_ref, w_ref, b_ref,
                 q0_ref, kv0_ref, q1_ref, kv1_ref, q2_ref, kv2_ref, bg_ref, u_ref, gate_ref, scr):
    h = _rms(x_ref[...], g_ref[...]).astype(BF16)

    def proj(c0, width):
        return jnp.dot(h, w_ref[:, c0:c0 + width], preferred_element_type=F32)

    qscale = HEAD_DIM ** -0.5
    q0_ref[...] = (proj(0, GROUP_WIDTH) * qscale).astype(BF16)
    kv0_ref[:, 0:GROUP_WIDTH] = proj(COL_K, GROUP_WIDTH).astype(BF16)
    kv0_ref[:, GROUP_WIDTH:] = proj(COL_V, GROUP_WIDTH).astype(BF16)

    for g, q_ref, kv_ref in ((1, q1_ref, kv1_ref), (2, q2_ref, kv2_ref)):
        d = DILATED_PATTERNS[g][1]
        n = TM_PROJ // d
        parts = (proj(g * GROUP_WIDTH, GROUP_WIDTH) * qscale,
                 proj(COL_K + g * GROUP_WIDTH, GROUP_WIDTH),
                 proj(COL_V + g * GROUP_WIDTH, GROUP_WIDTH))
        for i, part in enumerate(parts):
            for c in range(2):
                scr[2 * i + c] = part[:, c * LANES:(c + 1) * LANES]
        for r in range(d):
            rows = pl.ds(r, n, stride=d)
            q_ref[r] = jnp.concatenate([scr[c, rows, :] for c in range(2)], axis=1).astype(BF16)
            kv_ref[r] = jnp.concatenate([scr[c, rows, :] for c in range(2, 6)], axis=1).astype(BF16)

    bg_ref[...] = proj(COL_BG, CONV_WIDTH).astype(BF16)
    u_ref[...] = (proj(COL_CG, CONV_WIDTH) * proj(COL_XIN, CONV_WIDTH)).astype(BF16)
    for c in range(4):
        w = 2 * D_MODEL // 4
        z = proj(COL_GATE + c * w, w) + b_ref[:, c * w:(c + 1) * w]
        gate_ref[:, c * w:(c + 1) * w] = jax.nn.sigmoid(z).astype(BF16)


def _proj_call(x2, g, w_in, b_gate, batch, seq):
    n = x2.shape[0]
    steps_per_batch = seq // TM_PROJ
    d1, d2 = DILATED_PATTERNS[1][1], DILATED_PATTERNS[2][1]
    row = lambda i: (i, 0)
    res = lambda i: (i // steps_per_batch, 0, i % steps_per_batch, 0)
    const = lambda i: (0, 0)
    out_shape = [
        jax.ShapeDtypeStruct((n, GROUP_WIDTH), BF16),
        jax.ShapeDtypeStruct((n, 2 * GROUP_WIDTH), BF16),
        jax.ShapeDtypeStruct((batch, d1, seq // d1, GROUP_WIDTH), BF16),
        jax.ShapeDtypeStruct((batch, d1, seq // d1, 2 * GROUP_WIDTH), BF16),
        jax.ShapeDtypeStruct((batch, d2, seq // d2, GROUP_WIDTH), BF16),
        jax.ShapeDtypeStruct((batch, d2, seq // d2, 2 * GROUP_WIDTH), BF16),
        jax.ShapeDtypeStruct((n, CONV_WIDTH), BF16),
        jax.ShapeDtypeStruct((n, CONV_WIDTH), BF16),
        jax.ShapeDtypeStruct((n, 2 * D_MODEL), BF16),
    ]
    out_specs = [
        pl.BlockSpec((TM_PROJ, GROUP_WIDTH), row),
        pl.BlockSpec((TM_PROJ, 2 * GROUP_WIDTH), row),
        pl.BlockSpec((None, d1, TM_PROJ // d1, GROUP_WIDTH), res),
        pl.BlockSpec((None, d1, TM_PROJ // d1, 2 * GROUP_WIDTH), res),
        pl.BlockSpec((None, d2, TM_PROJ // d2, GROUP_WIDTH), res),
        pl.BlockSpec((None, d2, TM_PROJ // d2, 2 * GROUP_WIDTH), res),
        pl.BlockSpec((TM_PROJ, CONV_WIDTH), row),
        pl.BlockSpec((TM_PROJ, CONV_WIDTH), row),
        pl.BlockSpec((TM_PROJ, 2 * D_MODEL), row),
    ]
    return pl.pallas_call(
        _proj_kernel,
        grid=(n // TM_PROJ,),
        in_specs=[
            pl.BlockSpec((TM_PROJ, D_MODEL), row),
            pl.BlockSpec((1, D_MODEL), const),
            pl.BlockSpec((D_MODEL, IN_COLS), const, pipeline_mode=pl.Buffered(1)),
            pl.BlockSpec((1, 2 * D_MODEL), const),
        ],
        out_specs=out_specs,
        out_shape=out_shape,
        scratch_shapes=[pltpu.VMEM((6, TM_PROJ, LANES), F32)],
        compiler_params=pltpu.CompilerParams(dimension_semantics=("arbitrary",),
                                             vmem_limit_bytes=VMEM_LIMIT),
        name="proj",
    )(x2, g, w_in, b_gate)


def _attn_sub_block(q_sub, kw, vw, bias_ref, g, lo, hi):
    assert KB == GROUP_WIDTH
    lane = lax.broadcasted_iota(jnp.int32, (QB, KB), 1)
    edge_ok = (lane >= lo) & (lane < hi)
    heads = [(lane >= h * HEAD_DIM) & (lane < (h + 1) * HEAD_DIM) for h in range(HEADS_PER_GROUP)]
    zero = jnp.zeros((), BF16)
    q_stack = jnp.concatenate([jnp.where(hm, q_sub, zero) for hm in heads], axis=0)
    s_all = lax.dot_general(q_stack, kw, (((1,), (1,)), ((), ())), preferred_element_type=F32)
    probs = []
    m_b = l_b = None
    for h, hm in enumerate(heads):
        s = s_all[h * QB:(h + 1) * QB] + bias_ref[g * HEADS_PER_GROUP + h]
        s = jnp.where(edge_ok, s, MASK_VALUE)
        m = jnp.max(s, axis=1, keepdims=True)
        p = jnp.exp(s - m)
        l = jnp.sum(p, axis=1, keepdims=True)
        probs.append(p.astype(BF16))
        m_b = jnp.broadcast_to(m, (QB, GROUP_WIDTH)) if m_b is None else jnp.where(hm, m, m_b)
        l_b = jnp.broadcast_to(l, (QB, GROUP_WIDTH)) if l_b is None else jnp.where(hm, l, l_b)
    o_all = jnp.dot(jnp.concatenate(probs, axis=0), vw, preferred_element_type=F32)
    acc = o_all[0:QB]
    for h in range(1, HEADS_PER_GROUP):
        acc = jnp.where(heads[h], o_all[h * QB:(h + 1) * QB], acc)
    return acc, m_b, l_b


def _attn_kernel(q0_ref, kv0_ref, kv0p_ref, kv0n_ref,
                 q1_ref, kv1_ref, kv1p_ref, kv1n_ref,
                 q2_ref, kv2_ref, kv2p_ref, kv2n_ref,
                 y_ref,
                 cat0, cat1, cat2, bias_ref, m_st, l_st, a_st, m_tmp, l_tmp, a_tmp, *, seq):
    j = pl.program_id(1)

    qi = lax.broadcasted_iota(jnp.int32, (QB, KB), 0)
    kc = lax.broadcasted_iota(jnp.int32, (QB, KB), 1)
    adelta = jnp.abs(kc - HALF - qi)
    band = adelta <= HALF
    slopes = _alibi_slopes()
    for g in range(N_GROUPS_A):
        dist = (adelta * DILATED_PATTERNS[g][1]).astype(F32)
        for h in range(HEADS_PER_GROUP):
            bias_ref[g * HEADS_PER_GROUP + h] = jnp.where(band, -(float(slopes[g, h]) * dist), MASK_VALUE)

    for cat, own, prv, nxt in ((cat0, kv0_ref, kv0p_ref, kv0n_ref),
                               (cat1, kv1_ref, kv1p_ref, kv1n_ref),
                               (cat2, kv2_ref, kv2p_ref, kv2n_ref)):
        n_own = own.shape[-2]
        cat[:, 0:HALF, :] = prv[...].reshape(cat.shape[0], HALF, 2 * GROUP_WIDTH)
        cat[:, HALF:HALF + n_own, :] = own[...].reshape(cat.shape[0], n_own, 2 * GROUP_WIDTH)
        cat[:, HALF + n_own:, :] = nxt[...].reshape(cat.shape[0], HALF, 2 * GROUP_WIDTH)

    def window(cat, r, sb):
        rows = pl.ds(pl.multiple_of(sb * QB, QB), KB)
        return cat[r, rows, 0:GROUP_WIDTH], cat[r, rows, GROUP_WIDTH:]

    def edges(g, n_res, sb):
        length = seq // DILATED_PATTERNS[g][1]
        i0 = j * n_res + sb * QB
        return jnp.maximum(0, HALF - i0), jnp.minimum(KB, length + HALF - i0)

    def body0(sb, carry):
        rows = pl.ds(pl.multiple_of(sb * QB, QB), QB)
        kw, vw = window(cat0, 0, sb)
        lo, hi = edges(0, T_ATT, sb)
        acc, m_b, l_b = _attn_sub_block(q0_ref[rows, :], kw, vw, bias_ref, 0, lo, hi)
        for c in range(2):
            cols = slice(c * LANES, (c + 1) * LANES)
            m_st[c, rows, :] = m_b[:, cols]
            l_st[c, rows, :] = l_b[:, cols]
            a_st[c, rows, :] = acc[:, cols]
        return carry

    lax.fori_loop(0, T_ATT // QB, body0, 0, unroll=ATT_UNROLL)

    for g, q_ref, cat in ((1, q1_ref, cat1), (2, q2_ref, cat2)):
        d = DILATED_PATTERNS[g][1]
        n_res = T_ATT // d
        sb_per_res = n_res // QB

        def body(idx, carry, g=g, q_ref=q_ref, cat=cat, n_res=n_res, sb_per_res=sb_per_res):
            r = idx // sb_per_res
            sb = idx % sb_per_res
            kw, vw = window(cat, r, sb)
            lo, hi = edges(g, n_res, sb)
            q_sub = q_ref[r, pl.ds(pl.multiple_of(sb * QB, QB), QB), :]
            acc, m_b, l_b = _attn_sub_block(q_sub, kw, vw, bias_ref, g, lo, hi)
            rows = pl.ds(pl.multiple_of(idx * QB, QB), QB)
            m_tmp[rows, :] = m_b
            l_tmp[rows, :] = l_b
            a_tmp[rows, :] = acc
            return carry

        lax.fori_loop(0, T_ATT // QB, body, 0, unroll=ATT_UNROLL)

        for r in range(d):
            for ch in range(sb_per_res):
                src = slice(r * n_res + ch * QB, r * n_res + (ch + 1) * QB)
                tok = pl.ds(ch * QB * d + r, QB, stride=d)
                for c in range(2):
                    cols = slice(c * LANES, (c + 1) * LANES)
                    m_new_part = m_tmp[src, cols]
                    m_old = m_st[c, tok, :]
                    m_new = jnp.maximum(m_old, m_new_part)
                    e_old = jnp.exp(m_old - m_new)
                    e_new = jnp.exp(m_new_part - m_new)
                    m_st[c, tok, :] = m_new
                    l_st[c, tok, :] = e_old * l_st[c, tok, :] + e_new * l_tmp[src, cols]
                    a_st[c, tok, :] = e_old * a_st[c, tok, :] + e_new * a_tmp[src, cols]

    for c in range(2):
        y_ref[:, c * LANES:(c + 1) * LANES] = (a_st[c] / l_st[c]).astype(BF16)


def _attn_call(q0, kv0, q1, kv1, q2, kv2, batch, seq):
    n = q0.shape[0]
    tiles = seq // T_ATT
    specs = []
    scratch = []
    blocks_per_tile = T_ATT // HALF
    n_half_blocks = n // HALF
    specs += [
        pl.BlockSpec((T_ATT, GROUP_WIDTH), lambda b, j: (b * tiles + j, 0)),
        pl.BlockSpec((T_ATT, 2 * GROUP_WIDTH), lambda b, j: (b * tiles + j, 0)),
        pl.BlockSpec((HALF, 2 * GROUP_WIDTH),
                     lambda b, j: (jnp.maximum((b * tiles + j) * blocks_per_tile - 1, 0), 0)),
        pl.BlockSpec((HALF, 2 * GROUP_WIDTH),
                     lambda b, j: (jnp.minimum((b * tiles + j + 1) * blocks_per_tile, n_half_blocks - 1), 0)),
    ]
    scratch.append(pltpu.VMEM((1, T_ATT + 2 * HALF, 2 * GROUP_WIDTH), BF16))
    for g in (1, 2):
        d = DILATED_PATTERNS[g][1]
        n_res = T_ATT // d
        per_tile = n_res // HALF
        last = seq // d // HALF - 1
        specs += [
            pl.BlockSpec((None, d, n_res, GROUP_WIDTH), lambda b, j: (b, 0, j, 0)),
            pl.BlockSpec((None, d, n_res, 2 * GROUP_WIDTH), lambda b, j: (b, 0, j, 0)),
            pl.BlockSpec((None, d, HALF, 2 * GROUP_WIDTH),
                         lambda b, j, per_tile=per_tile: (b, 0, jnp.maximum(j * per_tile - 1, 0), 0)),
            pl.BlockSpec((None, d, HALF, 2 * GROUP_WIDTH),
                         lambda b, j, per_tile=per_tile, last=last: (b, 0, jnp.minimum((j + 1) * per_tile, last), 0)),
        ]
        scratch.append(pltpu.VMEM((d, n_res + 2 * HALF, 2 * GROUP_WIDTH), BF16))
    scratch.append(pltpu.VMEM((N_HEADS_A, QB, KB), F32))
    scratch += [pltpu.VMEM((2, T_ATT, LANES), F32) for _ in range(3)]
    scratch += [pltpu.VMEM((T_ATT, GROUP_WIDTH), F32) for _ in range(3)]
    return pl.pallas_call(
        functools.partial(_attn_kernel, seq=seq),
        grid=(batch, tiles),
        in_specs=specs,
        out_specs=pl.BlockSpec((T_ATT, GROUP_WIDTH), lambda b, j: (b * tiles + j, 0)),
        out_shape=jax.ShapeDtypeStruct((n, GROUP_WIDTH), BF16),
        scratch_shapes=scratch,
        compiler_params=pltpu.CompilerParams(dimension_semantics=("arbitrary", "arbitrary"),
                                             vmem_limit_bytes=VMEM_LIMIT),
        name="attn",
    )(q0, kv0, kv0, kv0, q1, kv1, kv1, kv1, q2, kv2, kv2, kv2)


def _split_dot(a, w_cat, w_hi):
    a_hi = a.astype(BF16)
    a_lo = (a - a_hi.astype(F32)).astype(BF16)
    both = jnp.dot(a_hi, w_cat, preferred_element_type=F32)
    return both[:, 0:LANES] + both[:, LANES:] + jnp.dot(a_lo, w_hi, preferred_element_type=F32)


ST_BASE, ST_TILE, ST_FREE = 0, 1, 2
AL_NEW = 0
RT_E, RT_W, RT_POS = 0, 2, 4


def _mix_kernel(x_ref, ya_ref, bg_ref, u_ref, up_ref, un_ref, gate_ref,
                wa_ref, wc_ref, wo_ref, cw_ref, g2_ref, wr_cat_ref, wr_hi_ref, br_ref,
                x1_ref, route_ref, routet_ref, alloc_ref, xs_hbm,
                st_ref, hb0, hb1, posv, poss, psem, dsem, *, seq, trash_row):
    i = pl.program_id(0)
    last_step = pl.num_programs(0) - 1
    hbs = (hb0, hb1)

    def rows_done(h):
        for k in range(2):
            pltpu.make_async_copy(hbs[h], xs_hbm.at[pl.ds(0, TM_MIX * SUBLANES)], dsem.at[h, k]).wait()

    def send_row(h, j, first_sublane):
        src = hbs[h].at[pl.ds(first_sublane, SUBLANES)]
        for k in range(2):
            dst = xs_hbm.at[pl.ds(pl.multiple_of(poss[h, RT_POS + k, j], SUBLANES), SUBLANES)]
            pltpu.make_async_copy(src, dst, dsem.at[h, k]).start(priority=k)

    def send_rows(h):
        def one(j, carry):
            send_row(h, j, pl.multiple_of(j * SUBLANES, SUBLANES))
            return carry
        lax.fori_loop(0, TM_MIX, one, 0, unroll=8)

    def positions_to_smem(h):
        return pltpu.make_async_copy(posv, poss.at[h], psem.at[h])

    @pl.when(i == 0)
    def _():
        st_ref[...] = jnp.zeros_like(st_ref)
        hb1[...] = jnp.zeros_like(hb1)
        spare = (trash_row + lax.broadcasted_iota(jnp.int32, posv.shape, 1)
                 + jnp.where(lax.broadcasted_iota(jnp.int32, posv.shape, 0) == RT_POS + 1, TM_MIX, 0))
        posv[...] = spare * SUBLANES
        positions_to_smem(1).start()

    def tile(h):
        rows = pl.ds(h * TM_MIX, TM_MIX)
        t0 = (i * MIX_TILES + h) * TM_MIX
        positions_to_smem(1 - h).wait()

        def send_other(part):
            for j in range(part * TM_MIX // 4, (part + 1) * TM_MIX // 4):
                send_row(1 - h, j, j * SUBLANES)

        u = u_ref[rows, :].astype(F32)
        row = lax.broadcasted_iota(jnp.int32, (TM_MIX, CONV_WIDTH), 0)
        if h == 0:
            prev_row = jnp.where(t0 % seq == 0, 0.0, up_ref[15:16, :].astype(F32))
        else:
            prev_row = u_ref[h * TM_MIX - 1:h * TM_MIX, :].astype(F32)
        if h == MIX_TILES - 1:
            next_row = jnp.where((t0 + TM_MIX) % seq == 0, 0.0, un_ref[0:1, :].astype(F32))
        else:
            next_row = u_ref[(h + 1) * TM_MIX:(h + 1) * TM_MIX + 1, :].astype(F32)
        u_prev = jnp.where(row == 0, prev_row, pltpu.roll(u, 1, axis=0))
        u_next = jnp.where(row == TM_MIX - 1, next_row, pltpu.roll(u, TM_MIX - 1, axis=0))
        conv = cw_ref[0:1, :] * u_prev + cw_ref[1:2, :] * u + cw_ref[2:3, :] * u_next
        yb_in = (bg_ref[rows, :].astype(F32) * conv).astype(BF16)
        send_other(0)

        y_a = jnp.dot(ya_ref[rows, :], wa_ref[...], preferred_element_type=F32)
        send_other(1)
        y_b = jnp.dot(yb_in, wc_ref[...], preferred_element_type=F32)
        send_other(2)
        merged = gate_ref[rows, 0:D_MODEL] * y_a.astype(BF16) + gate_ref[rows, D_MODEL:] * y_b.astype(BF16)
        x1 = x_ref[rows, :] + jnp.dot(merged, wo_ref[...], preferred_element_type=F32)
        x1_ref[rows, :] = x1
        send_other(3)

        h2 = _rms(x1, g2_ref[...])
        if h == 0:
            pl.when(i > 0)(lambda: rows_done(0))
        else:
            rows_done(h)
        for c in range(ROW_CHUNKS):
            hbs[h][pl.ds(c, TM_MIX, stride=ROW_CHUNKS), :] = h2[:, c * LANES:(c + 1) * LANES]

        logits = _split_dot(h2, wr_cat_ref[...], wr_hi_ref[...]) + br_ref[...]
        lane = lax.broadcasted_iota(jnp.int32, (TM_MIX, LANES), 1)
        lane_f = lane.astype(F32)
        neg = -jnp.inf
        big = float(LANES)
        is_group = lane < N_EXPERT_GROUPS
        cm = jnp.where(is_group, logits, neg)
        cmax = jnp.max(cm, axis=1, keepdims=True)
        g_idx = jnp.min(jnp.where(cm == cmax, lane_f, big), axis=1, keepdims=True)
        p_group = 1.0 / jnp.sum(jnp.where(is_group, jnp.exp(logits - cmax), 0.0), axis=1, keepdims=True)
        f_lo = N_EXPERT_GROUPS + EXPERTS_PER_GROUP * g_idx
        in_group = (lane_f >= f_lo) & (lane_f < f_lo + EXPERTS_PER_GROUP)
        fm = jnp.where(in_group, logits, neg)
        f1 = jnp.max(fm, axis=1, keepdims=True)
        i1 = jnp.min(jnp.where(fm == f1, lane_f, big), axis=1, keepdims=True)
        fm2 = jnp.where(lane_f == i1, neg, fm)
        f2 = jnp.max(fm2, axis=1, keepdims=True)
        i2 = jnp.min(jnp.where(fm2 == f2, lane_f, big), axis=1, keepdims=True)
        e21 = jnp.exp(f2 - f1)
        w_1 = p_group / (1.0 + e21)
        w_2 = p_group * e21 / (1.0 + e21)
        e_1 = i1 - N_EXPERT_GROUPS
        e_2 = i2 - N_EXPERT_GROUPS

        onehot = jnp.where((lane_f == e_1) | (lane_f == e_2), 1.0, 0.0)
        r_i = lax.broadcasted_iota(jnp.int32, (TM_MIX, TM_MIX), 0)
        c_i = lax.broadcasted_iota(jnp.int32, (TM_MIX, TM_MIX), 1)
        tri = jnp.where(c_i < r_i, 1.0, 0.0).astype(BF16)
        base = st_ref[ST_BASE:ST_BASE + 1, :]
        before = jnp.dot(tri, onehot.astype(BF16), preferred_element_type=F32) + base

        tile_rows = float(TM_MOE)
        cur_tile = st_ref[ST_TILE:ST_TILE + 1, :]
        next_free = st_ref[ST_FREE:ST_FREE + 1, :]
        count = jnp.sum(onehot, axis=0, keepdims=True)
        slot0 = jnp.floor(base * (1.0 / tile_rows))
        partial = (base - slot0 * tile_rows) > 0.0
        slot_last = jnp.floor((base + count - 1.0) * (1.0 / tile_rows))
        n_new = jnp.where(count > 0.0, slot_last - slot0 + 1.0 - jnp.where(partial, 1.0, 0.0), 0.0)
        e_r = lax.broadcasted_iota(jnp.int32, (LANES, LANES), 0)
        e_c = lax.broadcasted_iota(jnp.int32, (LANES, LANES), 1)
        earlier = jnp.where(e_r < e_c, 1.0, 0.0).astype(BF16)
        new_before = jnp.dot(jnp.broadcast_to(n_new, (SUBLANES, LANES)).astype(BF16), earlier,
                             preferred_element_type=F32)[0:1, :]
        fresh = next_free + new_before - jnp.where(partial, 1.0, 0.0) - slot0
        partial_slot = jnp.where(partial, slot0, -1.0)

        def tile_of(slot_idx, fresh_v, cur_v, partial_v):
            return jnp.where(slot_idx == partial_v, cur_v, fresh_v + slot_idx)

        def pick(row_vec, e):
            return jnp.sum(jnp.where(lane_f == e, row_vec, 0.0), axis=1, keepdims=True)

        positions = []
        for e in (e_1, e_2):
            rank = pick(before, e)
            s = jnp.floor(rank * (1.0 / tile_rows))
            tid = tile_of(s, pick(fresh, e), pick(cur_tile, e), pick(partial_slot, e))
            positions.append(tid * tile_rows + (rank - s * tile_rows))

        st_ref[ST_BASE:ST_BASE + 1, :] = base + count
        st_ref[ST_TILE:ST_TILE + 1, :] = jnp.where(count > 0.0, tile_of(slot_last, fresh, cur_tile, partial_slot),
                                                    cur_tile)
        st_ref[ST_FREE:ST_FREE + 1, :] = next_free + jnp.sum(n_new, axis=1, keepdims=True)
        alloc_ref[h] = jnp.broadcast_to(n_new, (SUBLANES, LANES))

        route = jnp.zeros((TM_MIX, LANES), F32)
        for k, val in enumerate((e_1, e_2, w_1, w_2, positions[0], positions[1])):
            route = jnp.where(lane == k, val, route)
        route_ref[rows, :] = route
        route_t = route.T[0:SUBLANES, :]
        routet_ref[:, h * TM_MIX:(h + 1) * TM_MIX] = route_t
        posv[...] = route_t.astype(jnp.int32) * SUBLANES
        positions_to_smem(h).start()

    tile(0)
    tile(1)

    @pl.when(i == last_step)
    def _():
        positions_to_smem(1).wait()
        rows_done(0)
        send_rows(1)
        rows_done(1)

        posv[:, 0:LANES] = st_ref[...].astype(jnp.int32)
        state = positions_to_smem(0)
        state.start()
        state.wait()
        hb0[...] = jnp.zeros_like(hb0)

        def zero_jobs():
            for e in range(N_EXPERTS):
                fill = poss[0, ST_BASE, e] & (TM_MOE - 1)
                first = poss[0, ST_TILE, e] * TM_MOE
                at = fill
                size = 1
                while size < TM_MOE:
                    take = (fill > 0) & ((at & size) != 0)
                    yield take, first + at, size
                    at = at + jnp.where(take, size, 0)
                    size *= 2
            for t in range(N_EXPERTS):
                tile = poss[0, ST_FREE, 0] + t
                yield tile < trash_row // TM_MOE, tile * TM_MOE, TM_MOE

        def zero_copy(first, rows):
            return pltpu.make_async_copy(
                hb0.at[pl.ds(0, rows * SUBLANES)],
                xs_hbm.at[pl.ds(pl.multiple_of(first * SUBLANES, SUBLANES), rows * SUBLANES)], psem.at[0])

        for take, first, rows in zero_jobs():
            pl.when(take)(lambda first=first, rows=rows: zero_copy(first, rows).start())
        for take, first, rows in zero_jobs():
            pl.when(take)(lambda first=first, rows=rows: zero_copy(first, rows).wait())


def _mix_call(x2, y_attn, bgate, u, gates, wa, wc, wo, conv_w, g2, wr_cat, wr_hi, br, seq, n_tiles):
    n = x2.shape[0]
    step_rows = MIX_TILES * TM_MIX
    assert seq % step_rows == 0
    n_steps = n // step_rows
    row = lambda i: (i, 0)
    const = lambda i: (0, 0)
    halo = 16
    per_step = step_rows // halo
    last = n // halo - 1
    xs_rows = n_tiles * TM_MOE + 2 * TM_MIX
    return pl.pallas_call(
        functools.partial(_mix_kernel, seq=seq, trash_row=n_tiles * TM_MOE),
        grid=(n_steps,),
        in_specs=[
            pl.BlockSpec((step_rows, D_MODEL), row),
            pl.BlockSpec((step_rows, GROUP_WIDTH), row),
            pl.BlockSpec((step_rows, CONV_WIDTH), row),
            pl.BlockSpec((step_rows, CONV_WIDTH), row),
            pl.BlockSpec((halo, CONV_WIDTH), lambda i: (jnp.maximum(i * per_step - 1, 0), 0)),
            pl.BlockSpec((halo, CONV_WIDTH), lambda i: (jnp.minimum((i + 1) * per_step, last), 0)),
            pl.BlockSpec((step_rows, 2 * D_MODEL), row),
            pl.BlockSpec((GROUP_WIDTH, D_MODEL), const),
            pl.BlockSpec((CONV_WIDTH, D_MODEL), const),
            pl.BlockSpec((D_MODEL, D_MODEL), const),
            pl.BlockSpec((3, CONV_WIDTH), const),
            pl.BlockSpec((1, D_MODEL), const),
            pl.BlockSpec((D_MODEL, 2 * LANES), const),
            pl.BlockSpec((D_MODEL, LANES), const),
            pl.BlockSpec((1, LANES), const),
        ],
        out_specs=[
            pl.BlockSpec((step_rows, D_MODEL), row),
            pl.BlockSpec((step_rows, LANES), row),
            pl.BlockSpec((SUBLANES, step_rows), lambda i: (0, i)),
            pl.BlockSpec((MIX_TILES, SUBLANES, LANES), lambda i: (i, 0, 0)),
            pl.BlockSpec(memory_space=pl.ANY),
        ],
        out_shape=[
            jax.ShapeDtypeStruct((n, D_MODEL), F32),
            jax.ShapeDtypeStruct((n, LANES), F32),
            jax.ShapeDtypeStruct((SUBLANES, n), F32),
            jax.ShapeDtypeStruct((n // TM_MIX, SUBLANES, LANES), F32),
            jax.ShapeDtypeStruct((xs_rows * ROW_CHUNKS, LANES), F32),
        ],
        scratch_shapes=[pltpu.VMEM((SUBLANES, LANES), F32),
                        pltpu.VMEM((TM_MIX * ROW_CHUNKS, LANES), F32),
                        pltpu.VMEM((TM_MIX * ROW_CHUNKS, LANES), F32),
                        pltpu.VMEM((SUBLANES, TM_MIX), jnp.int32),
                        pltpu.SMEM((MIX_TILES, SUBLANES, TM_MIX), jnp.int32),
                        pltpu.SemaphoreType.DMA((MIX_TILES,)),
                        pltpu.SemaphoreType.DMA((MIX_TILES, 2))],
        compiler_params=pltpu.CompilerParams(dimension_semantics=("arbitrary",),
                                             vmem_limit_bytes=VMEM_LIMIT),
        name="mix",
    )(x2, y_attn, bgate, u, u, u, gates, wa, wc, wo, conv_w, g2, wr_cat, wr_hi, br)


def _row_gather(idx_ref, n_rows, src_hbm, dst, sem):
    def issue(pair, carry):
        for k in range(2):
            j = 2 * pair + k
            t = idx_ref[0, 0, j]
            pltpu.make_async_copy(src_hbm.at[pl.ds(pl.multiple_of(t * SUBLANES, SUBLANES), SUBLANES)],
                                  dst.at[pl.ds(pl.multiple_of(j * SUBLANES, SUBLANES), SUBLANES)],
                                  sem).start(priority=k)
        return carry
    lax.fori_loop(0, n_rows // 2, issue, 0, unroll=4)


def _row_gather_wait(n_rows, src_hbm, dst, sem):
    pltpu.make_async_copy(src_hbm.at[pl.ds(0, n_rows * SUBLANES)], dst, sem).wait()


def _rows_from_tiles(buf, first_row, n_rows):
    return jnp.concatenate(
        [buf[pl.ds(first_row * ROW_CHUNKS + c, n_rows, stride=ROW_CHUNKS), :] for c in range(ROW_CHUNKS)],
        axis=1)


def _expert_kernel(order_ref, te_ref, nused_ref, xs_hbm, w1_ref, w3_ref, w2_ref, y_ref,
                   xin, w13b, w2b, isem):
    i = pl.program_id(0)
    slot = i % 2
    used = i < nused_ref[0]

    def fetch(step, s):
        rows = pl.ds(order_ref[step] * TM_MOE, TM_MOE)
        return [pltpu.make_async_copy(xs_hbm.at[rows, c, :], xin.at[s, c], isem.at[s]) for c in range(ROW_CHUNKS)]

    @pl.when(i == 0)
    def _():
        for cp in fetch(0, 0):
            cp.start()

    @pl.when(i + 1 < nused_ref[0])
    def _():
        for cp in fetch(i + 1, 1 - slot):
            cp.start()

    @pl.when(jnp.logical_not(used))
    def _():
        y_ref[...] = jnp.zeros_like(y_ref)

    @pl.when(used & ((i == 0) | (te_ref[i] != te_ref[jnp.maximum(i - 1, 0)])))
    def _():
        w13b[:, 0:EXPERT_FF] = w1_ref[...].astype(BF16)
        w13b[:, EXPERT_FF:] = w3_ref[...].astype(BF16)
        w2b[...] = w2_ref[...].astype(BF16)

    @pl.when(used)
    def _():
        for cp in fetch(i, slot):
            cp.wait()
        x = jnp.concatenate([xin[slot, c] for c in range(ROW_CHUNKS)], axis=1)
        ab = jnp.dot(x.astype(BF16), w13b[...], preferred_element_type=F32)
        a = ab[:, 0:EXPERT_FF]
        hid = (a * jax.nn.sigmoid(a) * ab[:, EXPERT_FF:]).astype(BF16)
        y = jnp.dot(hid, w2b[...], preferred_element_type=F32)
        for c in range(ROW_CHUNKS):
            y_ref[pl.ds(c, TM_MOE, stride=ROW_CHUNKS), :] = y[:, c * LANES:(c + 1) * LANES]


def _expert_call(order, tile_expert, n_used, xs_rows, w1, w3, w2):
    n_tiles = order.shape[0]
    last = lambda i, nu: jnp.minimum(i, nu[0] - 1)
    wspec = lambda shape: pl.BlockSpec(
        (None,) + shape, lambda i, od, te, nu: (te[last(i, nu)], 0, 0))
    grid_spec = pltpu.PrefetchScalarGridSpec(
        num_scalar_prefetch=3,
        grid=(n_tiles,),
        in_specs=[
            pl.BlockSpec(memory_space=pl.ANY),
            wspec((D_MODEL, EXPERT_FF)),
            wspec((D_MODEL, EXPERT_FF)),
            wspec((EXPERT_FF, D_MODEL)),
        ],
        out_specs=pl.BlockSpec((TM_MOE * ROW_CHUNKS, LANES), lambda i, od, te, nu: (od[i], 0)),
        scratch_shapes=[pltpu.VMEM((2, ROW_CHUNKS, TM_MOE, LANES), F32),
                        pltpu.VMEM((D_MODEL, 2 * EXPERT_FF), BF16),
                        pltpu.VMEM((EXPERT_FF, D_MODEL), BF16),
                        pltpu.SemaphoreType.DMA((2,))],
    )
    return pl.pallas_call(
        _expert_kernel,
        grid_spec=grid_spec,
        out_shape=jax.ShapeDtypeStruct((n_tiles * TM_MOE * ROW_CHUNKS, LANES), F32),
        compiler_params=pltpu.CompilerParams(dimension_semantics=("arbitrary",),
                                             vmem_limit_bytes=VMEM_LIMIT),
        name="experts",
    )(order, tile_expert, n_used, xs_rows, w1, w3, w2)


def _combine_kernel(pos_ref, posn_ref, y_hbm, x1_ref, route_ref, g_ref, o_ref, buf0, buf1, sem):
    i = pl.program_id(0)
    n_steps = pl.num_programs(0)
    bufs = (buf0, buf1)

    @pl.when(i == 0)
    def _():
        _row_gather(pos_ref, 2 * TM_CMB, y_hbm, buf0, sem.at[0])

    for slot in range(2):
        @pl.when((i % 2 == slot) & (i + 1 < n_steps))
        def _(slot=slot):
            _row_gather(posn_ref, 2 * TM_CMB, y_hbm, bufs[1 - slot], sem.at[1 - slot])

    for slot in range(2):
        @pl.when(i % 2 == slot)
        def _(slot=slot):
            _row_gather_wait(2 * TM_CMB, y_hbm, bufs[slot], sem.at[slot])
            y_1 = _rows_from_tiles(bufs[slot], 0, TM_CMB)
            y_2 = _rows_from_tiles(bufs[slot], TM_CMB, TM_CMB)
            x = x1_ref[...] + route_ref[:, 2:3] * y_1 + route_ref[:, 3:4] * y_2
            o_ref[...] = _rms(x, g_ref[...])


def _combine_call(pos, y_flat, x1, route, g):
    n = x1.shape[0]
    n_steps = n // TM_CMB
    row = lambda i: (i, 0)
    return pl.pallas_call(
        _combine_kernel,
        grid=(n_steps,),
        in_specs=[
            pl.BlockSpec((1, 1, 2 * TM_CMB), lambda i: (i, 0, 0), memory_space=pltpu.SMEM),
            pl.BlockSpec((1, 1, 2 * TM_CMB), lambda i: (jnp.minimum(i + 1, n_steps - 1), 0, 0),
                         memory_space=pltpu.SMEM),
            pl.BlockSpec(memory_space=pl.ANY),
            pl.BlockSpec((TM_CMB, D_MODEL), row),
            pl.BlockSpec((TM_CMB, LANES), row),
            pl.BlockSpec((1, D_MODEL), lambda i: (0, 0)),
        ],
        out_specs=pl.BlockSpec((TM_CMB, D_MODEL), row),
        out_shape=jax.ShapeDtypeStruct((n, D_MODEL), F32),
        scratch_shapes=[pltpu.VMEM((2 * TM_CMB * ROW_CHUNKS, LANES), F32),
                        pltpu.VMEM((2 * TM_CMB * ROW_CHUNKS, LANES), F32),
                        pltpu.SemaphoreType.DMA((2,))],
        compiler_params=pltpu.CompilerParams(dimension_semantics=("arbitrary",),
                                             vmem_limit_bytes=VMEM_LIMIT),
        name="combine",
    )(pos, pos, y_flat, x1, route, g)


def _layer(x2, batch, seq, norm_mix_g, w_in, b_gate, conv_w, w_attn_out, w_conv_out, w_out, norm_ffn_g,
           w_route_group, b_route_group, w_route_expert, b_route_expert, w1, w3, w2, final_g):
    n = x2.shape[0]
    q0, kv0, q1, kv1, q2, kv2, bgate, u, gates = _proj_call(
        x2, norm_mix_g[None, :], w_in.astype(BF16), b_gate[None, :], batch, seq)
    y_attn = _attn_call(q0, kv0, q1, kv1, q2, kv2, batch, seq)

    n_route = N_EXPERT_GROUPS + N_EXPERTS
    w_route = jnp.pad(jnp.concatenate([w_route_group, w_route_expert], axis=1), ((0, 0), (0, LANES - n_route)))
    b_route = jnp.pad(jnp.concatenate([b_route_group, b_route_expert]), (0, LANES - n_route))[None, :]
    wr_hi = w_route.astype(BF16)
    wr_lo = (w_route - wr_hi.astype(F32)).astype(BF16)
    n_tiles = (2 * n) // TM_MOE + N_EXPERTS
    x1, route, route_t, alloc, xs_flat = _mix_call(
        x2, y_attn, bgate, u, gates, w_attn_out.astype(BF16), w_conv_out.astype(BF16), w_out.astype(BF16),
        conv_w, norm_ffn_g[None, :], jnp.concatenate([wr_hi, wr_lo], axis=1), wr_hi, b_route, seq, n_tiles)

    i32 = jnp.int32
    taken = alloc[:, AL_NEW, :N_EXPERTS].astype(i32).reshape(-1)
    k = jnp.arange(taken.shape[0], dtype=i32)
    running = jnp.sum(jnp.where(k[:, None] >= k[None, :], taken[None, :], 0), axis=1)
    n_used = running[-1:]
    tile = jnp.arange(n_tiles, dtype=i32)
    owner = jnp.sum((running[None, :] <= tile[:, None]).astype(i32), axis=1) % N_EXPERTS
    owner = jnp.where(tile < n_used[0], owner, N_EXPERTS)
    key = owner * n_tiles + tile
    place = jnp.sum((key[None, :] < key[:, None]).astype(i32), axis=1)
    at = place[None, :] == tile[:, None]
    order = jnp.sum(jnp.where(at, tile[None, :], 0), axis=1)
    step_expert = jnp.minimum(jnp.sum(jnp.where(at, owner[None, :], 0), axis=1), N_EXPERTS - 1)

    y_flat = _expert_call(order, step_expert, n_used, xs_flat.reshape(-1, ROW_CHUNKS, LANES), w1, w3, w2)
    pos = route_t[RT_POS:RT_POS + 2].astype(i32)
    pos_tiles = pos.reshape(2, n // TM_CMB, TM_CMB).transpose(1, 0, 2).reshape(n // TM_CMB, 1, 2 * TM_CMB)
    return _combine_call(pos_tiles, y_flat, x1, route, final_g[None, :])


def kernel(x, norm_mix_g, w_in, b_gate, conv_w, w_attn_out, w_conv_out, w_out, norm_ffn_g,
           w_route_group, b_route_group, w_route_expert, b_route_expert, w1, w3, w2, norm_final_g):
    batch, seq, d = x.shape
    depth = w_in.shape[0]
    assert d == D_MODEL and depth == 1 and seq % T_ATT == 0
    out = _layer(x.reshape(batch * seq, d), batch, seq, norm_mix_g[0], w_in[0], b_gate[0], conv_w[0],
                 w_attn_out[0], w_conv_out[0], w_out[0], norm_ffn_g[0], w_route_group[0], b_route_group[0],
                 w_route_expert[0], b_route_expert[0], w1[0], w3[0], w2[0], norm_final_g)
    return out.reshape(batch, seq, d)
```

```python
import functools

import numpy as np
import jax
import jax.numpy as jnp
from jax import lax
from jax.experimental import pallas as pl
from jax.experimental.pallas import tpu as pltpu

F32 = jnp.float32
BF16 = jnp.bfloat16

D_MODEL = 1024
HEAD_DIM = 64
HEADS_PER_GROUP = 4
DILATED_PATTERNS = ((128, 1), (512, 4), (2048, 16))
N_GROUPS_A = 3
N_HEADS_A = N_GROUPS_A * HEADS_PER_GROUP
ATTN_WIDTH = N_HEADS_A * HEAD_DIM
GROUP_WIDTH = HEADS_PER_GROUP * HEAD_DIM
ALIBI_SPAN = 8.0
MASK_VALUE = -1e30
CONV_WIDTH = 768
N_EXPERT_GROUPS = 4
EXPERTS_PER_GROUP = 8
N_EXPERTS = 32
EXPERT_FF = 512
RMS_EPS = 1e-6

HALF = 64
LANES = 128
SUBLANES = 8
ROW_CHUNKS = D_MODEL // LANES

COL_K = ATTN_WIDTH
COL_V = 2 * ATTN_WIDTH
COL_BG = 3 * ATTN_WIDTH
COL_CG = COL_BG + CONV_WIDTH
COL_XIN = COL_CG + CONV_WIDTH
COL_GATE = COL_XIN + CONV_WIDTH
IN_COLS = COL_GATE + 2 * D_MODEL

TM_PROJ = 1024
T_ATT = 2048
QB = 128
KB = QB + 2 * HALF
ATT_UNROLL = 8
TM_MIX = 512
MIX_TILES = 2
TM_MOE = 512
TM_CMB = 256

VMEM_LIMIT = 56 * 1024 * 1024


def _alibi_slopes():
    return np.array([2.0 ** (-ALIBI_SPAN * (i + 1) / N_HEADS_A) for i in range(N_HEADS_A)],
                    dtype=np.float32).reshape(N_GROUPS_A, HEADS_PER_GROUP)


def _rms(x, g):
    return x * lax.rsqrt(jnp.mean(x * x, axis=-1, keepdims=True) + RMS_EPS) * g


def _proj_kernel(x_ref, g_ref, w_ref, b_ref,
                 q0_ref, kv0_ref, q1_ref, kv1_ref, q2_ref, kv2_ref, bg_ref, u_ref, gate_ref, scr):
    h = _rms(x_ref[...], g_ref[...]).astype(BF16)

    def proj(c0, width):
        return jnp.dot(h, w_ref[:, c0:c0 + width], preferred_element_type=F32)

    qscale = HEAD_DIM ** -0.5
    q0_ref[...] = (proj(0, GROUP_WIDTH) * qscale).astype(BF16)
    kv0_ref[:, 0:GROUP_WIDTH] = proj(COL_K, GROUP_WIDTH).astype(BF16)
    kv0_ref[:, GROUP_WIDTH:] = proj(COL_V, GROUP_WIDTH).astype(BF16)

    for g, q_ref, kv_ref in ((1, q1_ref, kv1_ref), (2, q2_ref, kv2_ref)):
        d = DILATED_PATTERNS[g][1]
        n = TM_PROJ // d
        parts = (proj(g * GROUP_WIDTH, GROUP_WIDTH) * qscale,
                 proj(COL_K + g * GROUP_WIDTH, GROUP_WIDTH),
                 proj(COL_V + g * GROUP_WIDTH, GROUP_WIDTH))
        for i, part in enumerate(parts):
            for c in range(2):
                scr[2 * i + c] = part[:, c * LANES:(c + 1) * LANES]
        for r in range(d):
            rows = pl.ds(r, n, stride=d)
            q_ref[r] = jnp.concatenate([scr[c, rows, :] for c in range(2)], axis=1).astype(BF16)
            kv_ref[r] = jnp.concatenate([scr[c, rows, :] for c in range(2, 6)], axis=1).astype(BF16)

    bg_ref[...] = proj(COL_BG, CONV_WIDTH).astype(BF16)
    u_ref[...] = (proj(COL_CG, CONV_WIDTH) * proj(COL_XIN, CONV_WIDTH)).astype(BF16)
    for c in range(4):
        w = 2 * D_MODEL // 4
        z = proj(COL_GATE + c * w, w) + b_ref[:, c * w:(c + 1) * w]
        gate_ref[:, c * w:(c + 1) * w] = jax.nn.sigmoid(z).astype(BF16)


def _proj_call(x2, g, w_in, b_gate, batch, seq):
    n = x2.shape[0]
    steps_per_batch = seq // TM_PROJ
    d1, d2 = DILATED_PATTERNS[1][1], DILATED_PATTERNS[2][1]
    row = lambda i: (i, 0)
    res = lambda i: (i // steps_per_batch, 0, i % steps_per_batch, 0)
    const = lambda i: (0, 0)
    out_shape = [
        jax.ShapeDtypeStruct((n, GROUP_WIDTH), BF16),
        jax.ShapeDtypeStruct((n, 2 * GROUP_WIDTH), BF16),
        jax.ShapeDtypeStruct((batch, d1, seq // d1, GROUP_WIDTH), BF16),
        jax.ShapeDtypeStruct((batch, d1, seq // d1, 2 * GROUP_WIDTH), BF16),
        jax.ShapeDtypeStruct((batch, d2, seq // d2, GROUP_WIDTH), BF16),
        jax.ShapeDtypeStruct((batch, d2, seq // d2, 2 * GROUP_WIDTH), BF16),
        jax.ShapeDtypeStruct((n, CONV_WIDTH), BF16),
        jax.ShapeDtypeStruct((n, CONV_WIDTH), BF16),
        jax.ShapeDtypeStruct((n, 2 * D_MODEL), BF16),
    ]
    out_specs = [
        pl.BlockSpec((TM_PROJ, GROUP_WIDTH), row),
        pl.BlockSpec((TM_PROJ, 2 * GROUP_WIDTH), row),
        pl.BlockSpec((None, d1, TM_PROJ // d1, GROUP_WIDTH), res),
        pl.BlockSpec((None, d1, TM_PROJ // d1, 2 * GROUP_WIDTH), res),
        pl.BlockSpec((None, d2, TM_PROJ // d2, GROUP_WIDTH), res),
        pl.BlockSpec((None, d2, TM_PROJ // d2, 2 * GROUP_WIDTH), res),
        pl.BlockSpec((TM_PROJ, CONV_WIDTH), row),
        pl.BlockSpec((TM_PROJ, CONV_WIDTH), row),
        pl.BlockSpec((TM_PROJ, 2 * D_MODEL), row),
    ]
    return pl.pallas_call(
        _proj_kernel,
        grid=(n // TM_PROJ,),
        in_specs=[
            pl.BlockSpec((TM_PROJ, D_MODEL), row),
            pl.BlockSpec((1, D_MODEL), const),
            pl.BlockSpec((D_MODEL, IN_COLS), const, pipeline_mode=pl.Buffered(1)),
            pl.BlockSpec((1, 2 * D_MODEL), const),
        ],
        out_specs=out_specs,
        out_shape=out_shape,
        scratch_shapes=[pltpu.VMEM((6, TM_PROJ, LANES), F32)],
        compiler_params=pltpu.CompilerParams(dimension_semantics=("arbitrary",),
                                             vmem_limit_bytes=VMEM_LIMIT),
        name="proj",
    )(x2, g, w_in, b_gate)


def _attn_sub_block(q_sub, kw, vw, bias_ref, g, lo, hi):
    assert KB == GROUP_WIDTH
    lane = lax.broadcasted_iota(jnp.int32, (QB, KB), 1)
    edge_ok = (lane >= lo) & (lane < hi)
    heads = [(lane >= h * HEAD_DIM) & (lane < (h + 1) * HEAD_DIM) for h in range(HEADS_PER_GROUP)]
    zero = jnp.zeros((), BF16)
    q_stack = jnp.concatenate([jnp.where(hm, q_sub, zero) for hm in heads], axis=0)
    s_all = lax.dot_general(q_stack, kw, (((1,), (1,)), ((), ())), preferred_element_type=F32)
    probs = []
    m_b = l_b = None
    for h, hm in enumerate(heads):
        s = s_all[h * QB:(h + 1) * QB] + bias_ref[g * HEADS_PER_GROUP + h]
        s = jnp.where(edge_ok, s, MASK_VALUE)
        m = jnp.max(s, axis=1, keepdims=True)
        p = jnp.exp(s - m)
        l = jnp.sum(p, axis=1, keepdims=True)
        probs.append(p.astype(BF16))
        m_b = jnp.broadcast_to(m, (QB, GROUP_WIDTH)) if m_b is None else jnp.where(hm, m, m_b)
        l_b = jnp.broadcast_to(l, (QB, GROUP_WIDTH)) if l_b is None else jnp.where(hm, l, l_b)
    o_all = jnp.dot(jnp.concatenate(probs, axis=0), vw, preferred_element_type=F32)
    acc = o_all[0:QB]
    for h in range(1, HEADS_PER_GROUP):
        acc = jnp.where(heads[h], o_all[h * QB:(h + 1) * QB], acc)
    return acc, m_b, l_b


def _attn_kernel(q0_ref, kv0_ref, kv0p_ref, kv0n_ref,
                 q1_ref, kv1_ref, kv1p_ref, kv1n_ref,
                 q2_ref, kv2_ref, kv2p_ref, kv2n_ref,
                 y_ref,
                 cat0, cat1, cat2, bias_ref, m_st, l_st, a_st, m_tmp, l_tmp, a_tmp, *, seq):
    j = pl.program_id(1)

    qi = lax.broadcasted_iota(jnp.int32, (QB, KB), 0)
    kc = lax.broadcasted_iota(jnp.int32, (QB, KB), 1)
    adelta = jnp.abs(kc - HALF - qi)
    band = adelta <= HALF
    slopes = _alibi_slopes()
    for g in range(N_GROUPS_A):
        dist = (adelta * DILATED_PATTERNS[g][1]).astype(F32)
        for h in range(HEADS_PER_GROUP):
            bias_ref[g * HEADS_PER_GROUP + h] = jnp.where(band, -(float(slopes[g, h]) * dist), MASK_VALUE)

    for cat, own, prv, nxt in ((cat0, kv0_ref, kv0p_ref, kv0n_ref),
                               (cat1, kv1_ref, kv1p_ref, kv1n_ref),
                               (cat2, kv2_ref, kv2p_ref, kv2n_ref)):
        n_own = own.shape[-2]
        cat[:, 0:HALF, :] = prv[...].reshape(cat.shape[0], HALF, 2 * GROUP_WIDTH)
        cat[:, HALF:HALF + n_own, :] = own[...].reshape(cat.shape[0], n_own, 2 * GROUP_WIDTH)
        cat[:, HALF + n_own:, :] = nxt[...].reshape(cat.shape[0], HALF, 2 * GROUP_WIDTH)

    def window(cat, r, sb):
        rows = pl.ds(pl.multiple_of(sb * QB, QB), KB)
        return cat[r, rows, 0:GROUP_WIDTH], cat[r, rows, GROUP_WIDTH:]

    def edges(g, n_res, sb):
        length = seq // DILATED_PATTERNS[g][1]
        i0 = j * n_res + sb * QB
        return jnp.maximum(0, HALF - i0), jnp.minimum(KB, length + HALF - i0)

    def body0(sb, carry):
        rows = pl.ds(pl.multiple_of(sb * QB, QB), QB)
        kw, vw = window(cat0, 0, sb)
        lo, hi = edges(0, T_ATT, sb)
        acc, m_b, l_b = _attn_sub_block(q0_ref[rows, :], kw, vw, bias_ref, 0, lo, hi)
        for c in range(2):
            cols = slice(c * LANES, (c + 1) * LANES)
            m_st[c, rows, :] = m_b[:, cols]
            l_st[c, rows, :] = l_b[:, cols]
            a_st[c, rows, :] = acc[:, cols]
        return carry

    lax.fori_loop(0, T_ATT // QB, body0, 0, unroll=ATT_UNROLL)

    for g, q_ref, cat in ((1, q1_ref, cat1), (2, q2_ref, cat2)):
        d = DILATED_PATTERNS[g][1]
        n_res = T_ATT // d
        sb_per_res = n_res // QB

        def body(idx, carry, g=g, q_ref=q_ref, cat=cat, n_res=n_res, sb_per_res=sb_per_res):
            r = idx // sb_per_res
            sb = idx % sb_per_res
            kw, vw = window(cat, r, sb)
            lo, hi = edges(g, n_res, sb)
            q_sub = q_ref[r, pl.ds(pl.multiple_of(sb * QB, QB), QB), :]
            acc, m_b, l_b = _attn_sub_block(q_sub, kw, vw, bias_ref, g, lo, hi)
            rows = pl.ds(pl.multiple_of(idx * QB, QB), QB)
            m_tmp[rows, :] = m_b
            l_tmp[rows, :] = l_b
            a_tmp[rows, :] = acc
            return carry

        lax.fori_loop(0, T_ATT // QB, body, 0, unroll=ATT_UNROLL)

        for r in range(d):
            for ch in range(sb_per_res):
                src = slice(r * n_res + ch * QB, r * n_res + (ch + 1) * QB)
                tok = pl.ds(ch * QB * d + r, QB, stride=d)
                for c in range(2):
                    cols = slice(c * LANES, (c + 1) * LANES)
                    m_new_part = m_tmp[src, cols]
                    m_old = m_st[c, tok, :]
                    m_new = jnp.maximum(m_old, m_new_part)
                    e_old = jnp.exp(m_old - m_new)
                    e_new = jnp.exp(m_new_part - m_new)
                    m_st[c, tok, :] = m_new
                    l_st[c, tok, :] = e_old * l_st[c, tok, :] + e_new * l_tmp[src, cols]
                    a_st[c, tok, :] = e_old * a_st[c, tok, :] + e_new * a_tmp[src, cols]

    for c in range(2):
        y_ref[:, c * LANES:(c + 1) * LANES] = (a_st[c] / l_st[c]).astype(BF16)


def _attn_call(q0, kv0, q1, kv1, q2, kv2, batch, seq):
    n = q0.shape[0]
    tiles = seq // T_ATT
    specs = []
    scratch = []
    blocks_per_tile = T_ATT // HALF
    n_half_blocks = n // HALF
    specs += [
        pl.BlockSpec((T_ATT, GROUP_WIDTH), lambda b, j: (b * tiles + j, 0)),
        pl.BlockSpec((T_ATT, 2 * GROUP_WIDTH), lambda b, j: (b * tiles + j, 0)),
        pl.BlockSpec((HALF, 2 * GROUP_WIDTH),
                     lambda b, j: (jnp.maximum((b * tiles + j) * blocks_per_tile - 1, 0), 0)),
        pl.BlockSpec((HALF, 2 * GROUP_WIDTH),
                     lambda b, j: (jnp.minimum((b * tiles + j + 1) * blocks_per_tile, n_half_blocks - 1), 0)),
    ]
    scratch.append(pltpu.VMEM((1, T_ATT + 2 * HALF, 2 * GROUP_WIDTH), BF16))
    for g in (1, 2):
        d = DILATED_PATTERNS[g][1]
        n_res = T_ATT // d
        per_tile = n_res // HALF
        last = seq // d // HALF - 1
        specs += [
            pl.BlockSpec((None, d, n_res, GROUP_WIDTH), lambda b, j: (b, 0, j, 0)),
            pl.BlockSpec((None, d, n_res, 2 * GROUP_WIDTH), lambda b, j: (b, 0, j, 0)),
            pl.BlockSpec((None, d, HALF, 2 * GROUP_WIDTH),
                         lambda b, j, per_tile=per_tile: (b, 0, jnp.maximum(j * per_tile - 1, 0), 0)),
            pl.BlockSpec((None, d, HALF, 2 * GROUP_WIDTH),
                         lambda b, j, per_tile=per_tile, last=last: (b, 0, jnp.minimum((j + 1) * per_tile, last), 0)),
        ]
        scratch.append(pltpu.VMEM((d, n_res + 2 * HALF, 2 * GROUP_WIDTH), BF16))
    scratch.append(pltpu.VMEM((N_HEADS_A, QB, KB), F32))
    scratch += [pltpu.VMEM((2, T_ATT, LANES), F32) for _ in range(3)]
    scratch += [pltpu.VMEM((T_ATT, GROUP_WIDTH), F32) for _ in range(3)]
    return pl.pallas_call(
        functools.partial(_attn_kernel, seq=seq),
        grid=(batch, tiles),
        in_specs=specs,
        out_specs=pl.BlockSpec((T_ATT, GROUP_WIDTH), lambda b, j: (b * tiles + j, 0)),
        out_shape=jax.ShapeDtypeStruct((n, GROUP_WIDTH), BF16),
        scratch_shapes=scratch,
        compiler_params=pltpu.CompilerParams(dimension_semantics=("arbitrary", "arbitrary"),
                                             vmem_limit_bytes=VMEM_LIMIT),
        name="attn",
    )(q0, kv0, kv0, kv0, q1, kv1, kv1, kv1, q2, kv2, kv2, kv2)


def _split_dot(a, w_cat, w_hi):
    a_hi = a.astype(BF16)
    a_lo = (a - a_hi.astype(F32)).astype(BF16)
    both = jnp.dot(a_hi, w_cat, preferred_element_type=F32)
    return both[:, 0:LANES] + both[:, LANES:] + jnp.dot(a_lo, w_hi, preferred_element_type=F32)


ST_BASE, ST_TILE, ST_FREE = 0, 1, 2
AL_NEW = 0
RT_E, RT_W, RT_POS = 0, 2, 4


def _mix_kernel(x_ref, ya_ref, bg_ref, u_ref, up_ref, un_ref, gate_ref,
                wa_ref, wc_ref, wo_ref, cw_ref, g2_ref, wr_cat_ref, wr_hi_ref, br_ref,
                x1_ref, route_ref, routet_ref, alloc_ref, xs_hbm,
                st_ref, hb0, hb1, posv, poss, psem, dsem, *, seq, trash_row):
    i = pl.program_id(0)
    last_step = pl.num_programs(0) - 1
    hbs = (hb0, hb1)

    def rows_done(h):
        for k in range(2):
            pltpu.make_async_copy(hbs[h], xs_hbm.at[pl.ds(0, TM_MIX * SUBLANES)], dsem.at[h, k]).wait()

    def send_row(h, j, first_sublane):
        src = hbs[h].at[pl.ds(first_sublane, SUBLANES)]
        for k in range(2):
            dst = xs_hbm.at[pl.ds(pl.multiple_of(poss[h, RT_POS + k, j], SUBLANES), SUBLANES)]
            pltpu.make_async_copy(src, dst, dsem.at[h, k]).start(priority=k)

    def send_rows(h):
        def one(j, carry):
            send_row(h, j, pl.multiple_of(j * SUBLANES, SUBLANES))
            return carry
        lax.fori_loop(0, TM_MIX, one, 0, unroll=8)

    def positions_to_smem(h):
        return pltpu.make_async_copy(posv, poss.at[h], psem.at[h])

    @pl.when(i == 0)
    def _():
        st_ref[...] = jnp.zeros_like(st_ref)
        hb1[...] = jnp.zeros_like(hb1)
        spare = (trash_row + lax.broadcasted_iota(jnp.int32, posv.shape, 1)
                 + jnp.where(lax.broadcasted_iota(jnp.int32, posv.shape, 0) == RT_POS + 1, TM_MIX, 0))
        posv[...] = spare * SUBLANES
        positions_to_smem(1).start()

    def tile(h):
        rows = pl.ds(h * TM_MIX, TM_MIX)
        t0 = (i * MIX_TILES + h) * TM_MIX
        positions_to_smem(1 - h).wait()

        def send_other(part):
            for j in range(part * TM_MIX // 4, (part + 1) * TM_MIX // 4):
                send_row(1 - h, j, j * SUBLANES)

        u = u_ref[rows, :].astype(F32)
        row = lax.broadcasted_iota(jnp.int32, (TM_MIX, CONV_WIDTH), 0)
        if h == 0:
            prev_row = jnp.where(t0 % seq == 0, 0.0, up_ref[15:16, :].astype(F32))
        else:
            prev_row = u_ref[h * TM_MIX - 1:h * TM_MIX, :].astype(F32)
        if h == MIX_TILES - 1:
            next_row = jnp.where((t0 + TM_MIX) % seq == 0, 0.0, un_ref[0:1, :].astype(F32))
        else:
            next_row = u_ref[(h + 1) * TM_MIX:(h + 1) * TM_MIX + 1, :].astype(F32)
        u_prev = jnp.where(row == 0, prev_row, pltpu.roll(u, 1, axis=0))
        u_next = jnp.where(row == TM_MIX - 1, next_row, pltpu.roll(u, TM_MIX - 1, axis=0))
        conv = cw_ref[0:1, :] * u_prev + cw_ref[1:2, :] * u + cw_ref[2:3, :] * u_next
        yb_in = (bg_ref[rows, :].astype(F32) * conv).astype(BF16)
        send_other(0)

        y_a = jnp.dot(ya_ref[rows, :], wa_ref[...], preferred_element_type=F32)
        send_other(1)
        y_b = jnp.dot(yb_in, wc_ref[...], preferred_element_type=F32)
        merged = gate_ref[rows, 0:D_MODEL] * y_a.astype(BF16) + gate_ref[rows, D_MODEL:] * y_b.astype(BF16)
        x1 = x_ref[rows, :] + jnp.dot(merged, wo_ref[...], preferred_element_type=F32)
        x1_ref[rows, :] = x1

        h2 = _rms(x1, g2_ref[...])
        if h == 0:
            pl.when(i > 0)(lambda: rows_done(0))
        else:
            rows_done(h)
        for c in range(ROW_CHUNKS):
            hbs[h][pl.ds(c, TM_MIX, stride=ROW_CHUNKS), :] = h2[:, c * LANES:(c + 1) * LANES]

        logits = _split_dot(h2, wr_cat_ref[...], wr_hi_ref[...]) + br_ref[...]
        send_other(2)
        send_other(3)
        lane = lax.broadcasted_iota(jnp.int32, (TM_MIX, LANES), 1)
        lane_f = lane.astype(F32)
        neg = -jnp.inf
        big = float(LANES)
        is_group = lane < N_EXPERT_GROUPS
        cm = jnp.where(is_group, logits, neg)
        cmax = jnp.max(cm, axis=1, keepdims=True)
        g_idx = jnp.min(jnp.where(cm == cmax, lane_f, big), axis=1, keepdims=True)
        p_group = 1.0 / jnp.sum(jnp.where(is_group, jnp.exp(logits - cmax), 0.0), axis=1, keepdims=True)
        f_lo = N_EXPERT_GROUPS + EXPERTS_PER_GROUP * g_idx
        in_group = (lane_f >= f_lo) & (lane_f < f_lo + EXPERTS_PER_GROUP)
        fm = jnp.where(in_group, logits, neg)
        f1 = jnp.max(fm, axis=1, keepdims=True)
        i1 = jnp.min(jnp.where(fm == f1, lane_f, big), axis=1, keepdims=True)
        fm2 = jnp.where(lane_f == i1, neg, fm)
        f2 = jnp.max(fm2, axis=1, keepdims=True)
        i2 = jnp.min(jnp.where(fm2 == f2, lane_f, big), axis=1, keepdims=True)
        e21 = jnp.exp(f2 - f1)
        w_1 = p_group / (1.0 + e21)
        w_2 = p_group * e21 / (1.0 + e21)
        e_1 = i1 - N_EXPERT_GROUPS
        e_2 = i2 - N_EXPERT_GROUPS

        onehot = jnp.where((lane_f == e_1) | (lane_f == e_2), 1.0, 0.0)
        r_i = lax.broadcasted_iota(jnp.int32, (TM_MIX, TM_MIX), 0)
        c_i = lax.broadcasted_iota(jnp.int32, (TM_MIX, TM_MIX), 1)
        tri = jnp.where(c_i < r_i, 1.0, 0.0).astype(BF16)
        base = st_ref[ST_BASE:ST_BASE + 1, :]
        before = jnp.dot(tri, onehot.astype(BF16), preferred_element_type=F32) + base

        tile_rows = float(TM_MOE)
        cur_tile = st_ref[ST_TILE:ST_TILE + 1, :]
        next_free = st_ref[ST_FREE:ST_FREE + 1, :]
        count = jnp.sum(onehot, axis=0, keepdims=True)
        slot0 = jnp.floor(base * (1.0 / tile_rows))
        partial = (base - slot0 * tile_rows) > 0.0
        slot_last = jnp.floor((base + count - 1.0) * (1.0 / tile_rows))
        n_new = jnp.where(count > 0.0, slot_last - slot0 + 1.0 - jnp.where(partial, 1.0, 0.0), 0.0)
        e_r = lax.broadcasted_iota(jnp.int32, (LANES, LANES), 0)
        e_c = lax.broadcasted_iota(jnp.int32, (LANES, LANES), 1)
        earlier = jnp.where(e_r < e_c, 1.0, 0.0).astype(BF16)
        new_before = jnp.dot(jnp.broadcast_to(n_new, (SUBLANES, LANES)).astype(BF16), earlier,
                             preferred_element_type=F32)[0:1, :]
        fresh = next_free + new_before - jnp.where(partial, 1.0, 0.0) - slot0
        partial_slot = jnp.where(partial, slot0, -1.0)

        def tile_of(slot_idx, fresh_v, cur_v, partial_v):
            return jnp.where(slot_idx == partial_v, cur_v, fresh_v + slot_idx)

        def pick(row_vec, e):
            return jnp.sum(jnp.where(lane_f == e, row_vec, 0.0), axis=1, keepdims=True)

        positions = []
        for e in (e_1, e_2):
            rank = pick(before, e)
            s = jnp.floor(rank * (1.0 / tile_rows))
            tid = tile_of(s, pick(fresh, e), pick(cur_tile, e), pick(partial_slot, e))
            positions.append(tid * tile_rows + (rank - s * tile_rows))

        st_ref[ST_BASE:ST_BASE + 1, :] = base + count
        st_ref[ST_TILE:ST_TILE + 1, :] = jnp.where(count > 0.0, tile_of(slot_last, fresh, cur_tile, partial_slot),
                                                    cur_tile)
        st_ref[ST_FREE:ST_FREE + 1, :] = next_free + jnp.sum(n_new, axis=1, keepdims=True)
        alloc_ref[h] = jnp.broadcast_to(n_new, (SUBLANES, LANES))

        route = jnp.zeros((TM_MIX, LANES), F32)
        for k, val in enumerate((e_1, e_2, w_1, w_2, positions[0], positions[1])):
            route = jnp.where(lane == k, val, route)
        route_ref[rows, :] = route
        route_t = route.T[0:SUBLANES, :]
        routet_ref[:, h * TM_MIX:(h + 1) * TM_MIX] = route_t
        posv[...] = route_t.astype(jnp.int32) * SUBLANES
        positions_to_smem(h).start()

    tile(0)
    tile(1)

    @pl.when(i == last_step)
    def _():
        positions_to_smem(1).wait()
        rows_done(0)
        send_rows(1)
        rows_done(1)

        posv[:, 0:LANES] = st_ref[...].astype(jnp.int32)
        state = positions_to_smem(0)
        state.start()
        state.wait()
        hb0[...] = jnp.zeros_like(hb0)

        def zero_jobs():
            for e in range(N_EXPERTS):
                fill = poss[0, ST_BASE, e] & (TM_MOE - 1)
                first = poss[0, ST_TILE, e] * TM_MOE
                at = fill
                size = 1
                while size < TM_MOE:
                    take = (fill > 0) & ((at & size) != 0)
                    yield take, first + at, size
                    at = at + jnp.where(take, size, 0)
                    size *= 2
            for t in range(N_EXPERTS):
                tile = poss[0, ST_FREE, 0] + t
                yield tile < trash_row // TM_MOE, tile * TM_MOE, TM_MOE

        def zero_copy(first, rows):
            return pltpu.make_async_copy(
                hb0.at[pl.ds(0, rows * SUBLANES)],
                xs_hbm.at[pl.ds(pl.multiple_of(first * SUBLANES, SUBLANES), rows * SUBLANES)], psem.at[0])

        for take, first, rows in zero_jobs():
            pl.when(take)(lambda first=first, rows=rows: zero_copy(first, rows).start())
        for take, first, rows in zero_jobs():
            pl.when(take)(lambda first=first, rows=rows: zero_copy(first, rows).wait())


def _mix_call(x2, y_attn, bgate, u, gates, wa, wc, wo, conv_w, g2, wr_cat, wr_hi, br, seq, n_tiles):
    n = x2.shape[0]
    step_rows = MIX_TILES * TM_MIX
    assert seq % step_rows == 0
    n_steps = n // step_rows
    row = lambda i: (i, 0)
    const = lambda i: (0, 0)
    halo = 16
    per_step = step_rows // halo
    last = n // halo - 1
    xs_rows = n_tiles * TM_MOE + 2 * TM_MIX
    return pl.pallas_call(
        functools.partial(_mix_kernel, seq=seq, trash_row=n_tiles * TM_MOE),
        grid=(n_steps,),
        in_specs=[
            pl.BlockSpec((step_rows, D_MODEL), row),
            pl.BlockSpec((step_rows, GROUP_WIDTH), row),
            pl.BlockSpec((step_rows, CONV_WIDTH), row),
            pl.BlockSpec((step_rows, CONV_WIDTH), row),
            pl.BlockSpec((halo, CONV_WIDTH), lambda i: (jnp.maximum(i * per_step - 1, 0), 0)),
            pl.BlockSpec((halo, CONV_WIDTH), lambda i: (jnp.minimum((i + 1) * per_step, last), 0)),
            pl.BlockSpec((step_rows, 2 * D_MODEL), row),
            pl.BlockSpec((GROUP_WIDTH, D_MODEL), const),
            pl.BlockSpec((CONV_WIDTH, D_MODEL), const),
            pl.BlockSpec((D_MODEL, D_MODEL), const),
            pl.BlockSpec((3, CONV_WIDTH), const),
            pl.BlockSpec((1, D_MODEL), const),
            pl.BlockSpec((D_MODEL, 2 * LANES), const),
            pl.BlockSpec((D_MODEL, LANES), const),
            pl.BlockSpec((1, LANES), const),
        ],
        out_specs=[
            pl.BlockSpec((step_rows, D_MODEL), row),
            pl.BlockSpec((step_rows, LANES), row),
            pl.BlockSpec((SUBLANES, step_rows), lambda i: (0, i)),
            pl.BlockSpec((MIX_TILES, SUBLANES, LANES), lambda i: (i, 0, 0)),
            pl.BlockSpec(memory_space=pl.ANY),
        ],
        out_shape=[
            jax.ShapeDtypeStruct((n, D_MODEL), F32),
            jax.ShapeDtypeStruct((n, LANES), F32),
            jax.ShapeDtypeStruct((SUBLANES, n), F32),
            jax.ShapeDtypeStruct((n // TM_MIX, SUBLANES, LANES), F32),
            jax.ShapeDtypeStruct((xs_rows * ROW_CHUNKS, LANES), F32),
        ],
        scratch_shapes=[pltpu.VMEM((SUBLANES, LANES), F32),
                        pltpu.VMEM((TM_MIX * ROW_CHUNKS, LANES), F32),
                        pltpu.VMEM((TM_MIX * ROW_CHUNKS, LANES), F32),
                        pltpu.VMEM((SUBLANES, TM_MIX), jnp.int32),
                        pltpu.SMEM((MIX_TILES, SUBLANES, TM_MIX), jnp.int32),
                        pltpu.SemaphoreType.DMA((MIX_TILES,)),
                        pltpu.SemaphoreType.DMA((MIX_TILES, 2))],
        compiler_params=pltpu.CompilerParams(dimension_semantics=("arbitrary",),
                                             vmem_limit_bytes=VMEM_LIMIT),
        name="mix",
    )(x2, y_attn, bgate, u, u, u, gates, wa, wc, wo, conv_w, g2, wr_cat, wr_hi, br)


def _row_gather(idx_ref, n_rows, src_hbm, dst, sem):
    def issue(pair, carry):
        for k in range(2):
            j = 2 * pair + k
            t = idx_ref[0, 0, j]
            pltpu.make_async_copy(src_hbm.at[pl.ds(pl.multiple_of(t * SUBLANES, SUBLANES), SUBLANES)],
                                  dst.at[pl.ds(pl.multiple_of(j * SUBLANES, SUBLANES), SUBLANES)],
                                  sem).start(priority=k)
        return carry
    lax.fori_loop(0, n_rows // 2, issue, 0, unroll=4)


def _row_gather_wait(n_rows, src_hbm, dst, sem):
    pltpu.make_async_copy(src_hbm.at[pl.ds(0, n_rows * SUBLANES)], dst, sem).wait()


def _rows_from_tiles(buf, first_row, n_rows):
    return jnp.concatenate(
        [buf[pl.ds(first_row * ROW_CHUNKS + c, n_rows, stride=ROW_CHUNKS), :] for c in range(ROW_CHUNKS)],
        axis=1)


def _expert_kernel(order_ref, te_ref, nused_ref, xs_hbm, w1_ref, w3_ref, w2_ref, y_ref,
                   xin, w13b, w2b, isem):
    i = pl.program_id(0)
    slot = i % 2
    used = i < nused_ref[0]

    def fetch(step, s):
        rows = pl.ds(order_ref[step] * TM_MOE, TM_MOE)
        return [pltpu.make_async_copy(xs_hbm.at[rows, c, :], xin.at[s, c], isem.at[s]) for c in range(ROW_CHUNKS)]

    @pl.when(i == 0)
    def _():
        for cp in fetch(0, 0):
            cp.start()

    @pl.when(i + 1 < nused_ref[0])
    def _():
        for cp in fetch(i + 1, 1 - slot):
            cp.start()

    @pl.when(jnp.logical_not(used))
    def _():
        y_ref[...] = jnp.zeros_like(y_ref)

    @pl.when(used & ((i == 0) | (te_ref[i] != te_ref[jnp.maximum(i - 1, 0)])))
    def _():
        w13b[:, 0:EXPERT_FF] = w1_ref[...].astype(BF16)
        w13b[:, EXPERT_FF:] = w3_ref[...].astype(BF16)
        w2b[...] = w2_ref[...].astype(BF16)

    @pl.when(used)
    def _():
        for cp in fetch(i, slot):
            cp.wait()
        x = jnp.concatenate([xin[slot, c] for c in range(ROW_CHUNKS)], axis=1)
        ab = jnp.dot(x.astype(BF16), w13b[...], preferred_element_type=F32)
        a = ab[:, 0:EXPERT_FF]
        hid = (a * jax.nn.sigmoid(a) * ab[:, EXPERT_FF:]).astype(BF16)
        y = jnp.dot(hid, w2b[...], preferred_element_type=F32)
        for c in range(ROW_CHUNKS):
            y_ref[pl.ds(c, TM_MOE, stride=ROW_CHUNKS), :] = y[:, c * LANES:(c + 1) * LANES]


def _expert_call(order, tile_expert, n_used, xs_rows, w1, w3, w2):
    n_tiles = order.shape[0]
    last = lambda i, nu: jnp.minimum(i, nu[0] - 1)
    wspec = lambda shape: pl.BlockSpec(
        (None,) + shape, lambda i, od, te, nu: (te[last(i, nu)], 0, 0))
    grid_spec = pltpu.PrefetchScalarGridSpec(
        num_scalar_prefetch=3,
        grid=(n_tiles,),
        in_specs=[
            pl.BlockSpec(memory_space=pl.ANY),
            wspec((D_MODEL, EXPERT_FF)),
            wspec((D_MODEL, EXPERT_FF)),
            wspec((EXPERT_FF, D_MODEL)),
        ],
        out_specs=pl.BlockSpec((TM_MOE * ROW_CHUNKS, LANES), lambda i, od, te, nu: (od[i], 0)),
        scratch_shapes=[pltpu.VMEM((2, ROW_CHUNKS, TM_MOE, LANES), F32),
                        pltpu.VMEM((D_MODEL, 2 * EXPERT_FF), BF16),
                        pltpu.VMEM((EXPERT_FF, D_MODEL), BF16),
                        pltpu.SemaphoreType.DMA((2,))],
    )
    return pl.pallas_call(
        _expert_kernel,
        grid_spec=grid_spec,
        out_shape=jax.ShapeDtypeStruct((n_tiles * TM_MOE * ROW_CHUNKS, LANES), F32),
        compiler_params=pltpu.CompilerParams(dimension_semantics=("arbitrary",),
                                             vmem_limit_bytes=VMEM_LIMIT),
        name="experts",
    )(order, tile_expert, n_used, xs_rows, w1, w3, w2)


def _combine_kernel(pos_ref, posn_ref, y_hbm, x1_ref, route_ref, g_ref, o_ref, buf0, buf1, sem):
    i = pl.program_id(0)
    n_steps = pl.num_programs(0)
    bufs = (buf0, buf1)

    @pl.when(i == 0)
    def _():
        _row_gather(pos_ref, 2 * TM_CMB, y_hbm, buf0, sem.at[0])

    for slot in range(2):
        @pl.when((i % 2 == slot) & (i + 1 < n_steps))
        def _(slot=slot):
            _row_gather(posn_ref, 2 * TM_CMB, y_hbm, bufs[1 - slot], sem.at[1 - slot])

    for slot in range(2):
        @pl.when(i % 2 == slot)
        def _(slot=slot):
            _row_gather_wait(2 * TM_CMB, y_hbm, bufs[slot], sem.at[slot])
            y_1 = _rows_from_tiles(bufs[slot], 0, TM_CMB)
            y_2 = _rows_from_tiles(bufs[slot], TM_CMB, TM_CMB)
            x = x1_ref[...] + route_ref[:, 2:3] * y_1 + route_ref[:, 3:4] * y_2
            o_ref[...] = _rms(x, g_ref[...])


def _combine_call(pos, y_flat, x1, route, g):
    n = x1.shape[0]
    n_steps = n // TM_CMB
    row = lambda i: (i, 0)
    return pl.pallas_call(
        _combine_kernel,
        grid=(n_steps,),
        in_specs=[
            pl.BlockSpec((1, 1, 2 * TM_CMB), lambda i: (i, 0, 0), memory_space=pltpu.SMEM),
            pl.BlockSpec((1, 1, 2 * TM_CMB), lambda i: (jnp.minimum(i + 1, n_steps - 1), 0, 0),
                         memory_space=pltpu.SMEM),
            pl.BlockSpec(memory_space=pl.ANY),
            pl.BlockSpec((TM_CMB, D_MODEL), row),
            pl.BlockSpec((TM_CMB, LANES), row),
            pl.BlockSpec((1, D_MODEL), lambda i: (0, 0)),
        ],
        out_specs=pl.BlockSpec((TM_CMB, D_MODEL), row),
        out_shape=jax.ShapeDtypeStruct((n, D_MODEL), F32),
        scratch_shapes=[pltpu.VMEM((2 * TM_CMB * ROW_CHUNKS, LANES), F32),
                        pltpu.VMEM((2 * TM_CMB * ROW_CHUNKS, LANES), F32),
                        pltpu.SemaphoreType.DMA((2,))],
        compiler_params=pltpu.CompilerParams(dimension_semantics=("arbitrary",),
                                             vmem_limit_bytes=VMEM_LIMIT),
        name="combine",
    )(pos, pos, y_flat, x1, route, g)


def _layer(x2, batch, seq, norm_mix_g, w_in, b_gate, conv_w, w_attn_out, w_conv_out, w_out, norm_ffn_g,
           w_route_group, b_route_group, w_route_expert, b_route_expert, w1, w3, w2, final_g):
    n = x2.shape[0]
    q0, kv0, q1, kv1, q2, kv2, bgate, u, gates = _proj_call(
        x2, norm_mix_g[None, :], w_in.astype(BF16), b_gate[None, :], batch, seq)
    y_attn = _attn_call(q0, kv0, q1, kv1, q2, kv2, batch, seq)

    n_route = N_EXPERT_GROUPS + N_EXPERTS
    w_route = jnp.pad(jnp.concatenate([w_route_group, w_route_expert], axis=1), ((0, 0), (0, LANES - n_route)))
    b_route = jnp.pad(jnp.concatenate([b_route_group, b_route_expert]), (0, LANES - n_route))[None, :]
    wr_hi = w_route.astype(BF16)
    wr_lo = (w_route - wr_hi.astype(F32)).astype(BF16)
    n_tiles = (2 * n) // TM_MOE + N_EXPERTS
    x1, route, route_t, alloc, xs_flat = _mix_call(
        x2, y_attn, bgate, u, gates, w_attn_out.astype(BF16), w_conv_out.astype(BF16), w_out.astype(BF16),
        conv_w, norm_ffn_g[None, :], jnp.concatenate([wr_hi, wr_lo], axis=1), wr_hi, b_route, seq, n_tiles)

    i32 = jnp.int32
    taken = alloc[:, AL_NEW, :N_EXPERTS].astype(i32).reshape(-1)
    k = jnp.arange(taken.shape[0], dtype=i32)
    running = jnp.sum(jnp.where(k[:, None] >= k[None, :], taken[None, :], 0), axis=1)
    n_used = running[-1:]
    tile = jnp.arange(n_tiles, dtype=i32)
    owner = jnp.sum((running[None, :] <= tile[:, None]).astype(i32), axis=1) % N_EXPERTS
    owner = jnp.where(tile < n_used[0], owner, N_EXPERTS)
    key = owner * n_tiles + tile
    place = jnp.sum((key[None, :] < key[:, None]).astype(i32), axis=1)
    at = place[None, :] == tile[:, None]
    order = jnp.sum(jnp.where(at, tile[None, :], 0), axis=1)
    step_expert = jnp.minimum(jnp.sum(jnp.where(at, owner[None, :], 0), axis=1), N_EXPERTS - 1)

    y_flat = _expert_call(order, step_expert, n_used, xs_flat.reshape(-1, ROW_CHUNKS, LANES), w1, w3, w2)
    pos = route_t[RT_POS:RT_POS + 2].astype(i32)
    pos_tiles = pos.reshape(2, n // TM_CMB, TM_CMB).transpose(1, 0, 2).reshape(n // TM_CMB, 1, 2 * TM_CMB)
    return _combine_call(pos_tiles, y_flat, x1, route, final_g[None, :])


def kernel(x, norm_mix_g, w_in, b_gate, conv_w, w_attn_out, w_conv_out, w_out, norm_ffn_g,
           w_route_group, b_route_group, w_route_expert, b_route_expert, w1, w3, w2, norm_final_g):
    batch, seq, d = x.shape
    depth = w_in.shape[0]
    assert d == D_MODEL and depth == 1 and seq % T_ATT == 0
    out = _layer(x.reshape(batch * seq, d), batch, seq, norm_mix_g[0], w_in[0], b_gate[0], conv_w[0],
                 w_attn_out[0], w_conv_out[0], w_out[0], norm_ffn_g[0], w_route_group[0], b_route_group[0],
                 w_route_expert[0], b_route_expert[0], w1[0], w3[0], w2[0], norm_final_g)
    return out.reshape(batch, seq, d)
```

```python
import functools

import numpy as np
import jax
import jax.numpy as jnp
from jax import lax
from jax.experimental import pallas as pl
from jax.experimental.pallas import tpu as pltpu

F32 = jnp.float32
BF16 = jnp.bfloat16

D_MODEL = 1024
HEAD_DIM = 64
HEADS_PER_GROUP = 4
DILATED_PATTERNS = ((128, 1), (512, 4), (2048, 16))
N_GROUPS_A = 3
N_HEADS_A = N_GROUPS_A * HEADS_PER_GROUP
ATTN_WIDTH = N_HEADS_A * HEAD_DIM
GROUP_WIDTH = HEADS_PER_GROUP * HEAD_DIM
ALIBI_SPAN = 8.0
MASK_VALUE = -1e30
CONV_WIDTH = 768
N_EXPERT_GROUPS = 4
EXPERTS_PER_GROUP = 8
N_EXPERTS = 32
EXPERT_FF = 512
RMS_EPS = 1e-6

HALF = 64
LANES = 128
SUBLANES = 8
ROW_CHUNKS = D_MODEL // LANES

COL_K = ATTN_WIDTH
COL_V = 2 * ATTN_WIDTH
COL_BG = 3 * ATTN_WIDTH
COL_CG = COL_BG + CONV_WIDTH
COL_XIN = COL_CG + CONV_WIDTH
COL_GATE = COL_XIN + CONV_WIDTH
IN_COLS = COL_GATE + 2 * D_MODEL

TM_PROJ = 1024
T_ATT = 2048
QB = 128
KB = QB + 2 * HALF
ATT_UNROLL = 8
TM_MIX = 512
MIX_TILES = 2
TM_MOE = 512
XIN_SLOTS = 3
TM_CMB = 256

VMEM_LIMIT = 56 * 1024 * 1024


def _alibi_slopes():
    return np.array([2.0 ** (-ALIBI_SPAN * (i + 1) / N_HEADS_A) for i in range(N_HEADS_A)],
                    dtype=np.float32).reshape(N_GROUPS_A, HEADS_PER_GROUP)


def _rms(x, g):
    return x * lax.rsqrt(jnp.mean(x * x, axis=-1, keepdims=True) + RMS_EPS) * g


def _proj_kernel(x_ref, g_ref, w_ref, b_ref,
                 q0_ref, kv0_ref, q1_ref, kv1_ref, q2_ref, kv2_ref, bg_ref, u_ref, gate_ref, scr):
    h = _rms(x_ref[...], g_ref[...]).astype(BF16)

    def proj(c0, width):
        return jnp.dot(h, w_ref[:, c0:c0 + width], preferred_element_type=F32)

    qscale = HEAD_DIM ** -0.5
    q0_ref[...] = (proj(0, GROUP_WIDTH) * qscale).astype(BF16)
    kv0_ref[:, 0:GROUP_WIDTH] = proj(COL_K, GROUP_WIDTH).astype(BF16)
    kv0_ref[:, GROUP_WIDTH:] = proj(COL_V, GROUP_WIDTH).astype(BF16)

    for g, q_ref, kv_ref in ((1, q1_ref, kv1_ref), (2, q2_ref, kv2_ref)):
        d = DILATED_PATTERNS[g][1]
        n = TM_PROJ // d
        parts = (proj(g * GROUP_WIDTH, GROUP_WIDTH) * qscale,
                 proj(COL_K + g * GROUP_WIDTH, GROUP_WIDTH),
                 proj(COL_V + g * GROUP_WIDTH, GROUP_WIDTH))
        for i, part in enumerate(parts):
            for c in range(2):
                scr[2 * i + c] = part[:, c * LANES:(c + 1) * LANES]
        for r in range(d):
            rows = pl.ds(r, n, stride=d)
            q_ref[r] = jnp.concatenate([scr[c, rows, :] for c in range(2)], axis=1).astype(BF16)
            kv_ref[r] = jnp.concatenate([scr[c, rows, :] for c in range(2, 6)], axis=1).astype(BF16)

    bg_ref[...] = proj(COL_BG, CONV_WIDTH).astype(BF16)
    u_ref[...] = (proj(COL_CG, CONV_WIDTH) * proj(COL_XIN, CONV_WIDTH)).astype(BF16)
    for c in range(4):
        w = 2 * D_MODEL // 4
        z = proj(COL_GATE + c * w, w) + b_ref[:, c * w:(c + 1) * w]
        gate_ref[:, c * w:(c + 1) * w] = jax.nn.sigmoid(z).astype(BF16)


def _proj_call(x2, g, w_in, b_gate, batch, seq):
    n = x2.shape[0]
    steps_per_batch = seq // TM_PROJ
    d1, d2 = DILATED_PATTERNS[1][1], DILATED_PATTERNS[2][1]
    row = lambda i: (i, 0)
    res = lambda i: (i // steps_per_batch, 0, i % steps_per_batch, 0)
    const = lambda i: (0, 0)
    out_shape = [
        jax.ShapeDtypeStruct((n, GROUP_WIDTH), BF16),
        jax.ShapeDtypeStruct((n, 2 * GROUP_WIDTH), BF16),
        jax.ShapeDtypeStruct((batch, d1, seq // d1, GROUP_WIDTH), BF16),
        jax.ShapeDtypeStruct((batch, d1, seq // d1, 2 * GROUP_WIDTH), BF16),
        jax.ShapeDtypeStruct((batch, d2, seq // d2, GROUP_WIDTH), BF16),
        jax.ShapeDtypeStruct((batch, d2, seq // d2, 2 * GROUP_WIDTH), BF16),
        jax.ShapeDtypeStruct((n, CONV_WIDTH), BF16),
        jax.ShapeDtypeStruct((n, CONV_WIDTH), BF16),
        jax.ShapeDtypeStruct((n, 2 * D_MODEL), BF16),
    ]
    out_specs = [
        pl.BlockSpec((TM_PROJ, GROUP_WIDTH), row),
        pl.BlockSpec((TM_PROJ, 2 * GROUP_WIDTH), row),
        pl.BlockSpec((None, d1, TM_PROJ // d1, GROUP_WIDTH), res),
        pl.BlockSpec((None, d1, TM_PROJ // d1, 2 * GROUP_WIDTH), res),
        pl.BlockSpec((None, d2, TM_PROJ // d2, GROUP_WIDTH), res),
        pl.BlockSpec((None, d2, TM_PROJ // d2, 2 * GROUP_WIDTH), res),
        pl.BlockSpec((TM_PROJ, CONV_WIDTH), row),
        pl.BlockSpec((TM_PROJ, CONV_WIDTH), row),
        pl.BlockSpec((TM_PROJ, 2 * D_MODEL), row),
    ]
    return pl.pallas_call(
        _proj_kernel,
        grid=(n // TM_PROJ,),
        in_specs=[
            pl.BlockSpec((TM_PROJ, D_MODEL), row),
            pl.BlockSpec((1, D_MODEL), const),
            pl.BlockSpec((D_MODEL, IN_COLS), const, pipeline_mode=pl.Buffered(1)),
            pl.BlockSpec((1, 2 * D_MODEL), const),
        ],
        out_specs=out_specs,
        out_shape=out_shape,
        scratch_shapes=[pltpu.VMEM((6, TM_PROJ, LANES), F32)],
        compiler_params=pltpu.CompilerParams(dimension_semantics=("arbitrary",),
                                             vmem_limit_bytes=VMEM_LIMIT),
        name="proj",
    )(x2, g, w_in, b_gate)


def _attn_sub_block(q_sub, kw, vw, bias_ref, g, lo, hi):
    assert KB == GROUP_WIDTH
    lane = lax.broadcasted_iota(jnp.int32, (QB, KB), 1)
    edge_ok = (lane >= lo) & (lane < hi)
    heads = [(lane >= h * HEAD_DIM) & (lane < (h + 1) * HEAD_DIM) for h in range(HEADS_PER_GROUP)]
    zero = jnp.zeros((), BF16)
    q_stack = jnp.concatenate([jnp.where(hm, q_sub, zero) for hm in heads], axis=0)
    s_all = lax.dot_general(q_stack, kw, (((1,), (1,)), ((), ())), preferred_element_type=F32)
    probs = []
    m_b = l_b = None
    for h, hm in enumerate(heads):
        s = s_all[h * QB:(h + 1) * QB] + bias_ref[g * HEADS_PER_GROUP + h]
        s = jnp.where(edge_ok, s, MASK_VALUE)
        m = jnp.max(s, axis=1, keepdims=True)
        p = jnp.exp(s - m)
        l = jnp.sum(p, axis=1, keepdims=True)
        probs.append(p.astype(BF16))
        m_b = jnp.broadcast_to(m, (QB, GROUP_WIDTH)) if m_b is None else jnp.where(hm, m, m_b)
        l_b = jnp.broadcast_to(l, (QB, GROUP_WIDTH)) if l_b is None else jnp.where(hm, l, l_b)
    o_all = jnp.dot(jnp.concatenate(probs, axis=0), vw, preferred_element_type=F32)
    acc = o_all[0:QB]
    for h in range(1, HEADS_PER_GROUP):
        acc = jnp.where(heads[h], o_all[h * QB:(h + 1) * QB], acc)
    return acc, m_b, l_b


def _attn_kernel(q0_ref, kv0_ref, kv0p_ref, kv0n_ref,
                 q1_ref, kv1_ref, kv1p_ref, kv1n_ref,
                 q2_ref, kv2_ref, kv2p_ref, kv2n_ref,
                 y_ref,
                 cat0, cat1, cat2, bias_ref, m_st, l_st, a_st, m_tmp, l_tmp, a_tmp, *, seq):
    j = pl.program_id(1)

    qi = lax.broadcasted_iota(jnp.int32, (QB, KB), 0)
    kc = lax.broadcasted_iota(jnp.int32, (QB, KB), 1)
    adelta = jnp.abs(kc - HALF - qi)
    band = adelta <= HALF
    slopes = _alibi_slopes()
    for g in range(N_GROUPS_A):
        dist = (adelta * DILATED_PATTERNS[g][1]).astype(F32)
        for h in range(HEADS_PER_GROUP):
            bias_ref[g * HEADS_PER_GROUP + h] = jnp.where(band, -(float(slopes[g, h]) * dist), MASK_VALUE)

    for cat, own, prv, nxt in ((cat0, kv0_ref, kv0p_ref, kv0n_ref),
                               (cat1, kv1_ref, kv1p_ref, kv1n_ref),
                               (cat2, kv2_ref, kv2p_ref, kv2n_ref)):
        n_own = own.shape[-2]
        cat[:, 0:HALF, :] = prv[...].reshape(cat.shape[0], HALF, 2 * GROUP_WIDTH)
        cat[:, HALF:HALF + n_own, :] = own[...].reshape(cat.shape[0], n_own, 2 * GROUP_WIDTH)
        cat[:, HALF + n_own:, :] = nxt[...].reshape(cat.shape[0], HALF, 2 * GROUP_WIDTH)

    def window(cat, r, sb):
        rows = pl.ds(pl.multiple_of(sb * QB, QB), KB)
        return cat[r, rows, 0:GROUP_WIDTH], cat[r, rows, GROUP_WIDTH:]

    def edges(g, n_res, sb):
        length = seq // DILATED_PATTERNS[g][1]
        i0 = j * n_res + sb * QB
        return jnp.maximum(0, HALF - i0), jnp.minimum(KB, length + HALF - i0)

    def body0(sb, carry):
        rows = pl.ds(pl.multiple_of(sb * QB, QB), QB)
        kw, vw = window(cat0, 0, sb)
        lo, hi = edges(0, T_ATT, sb)
        acc, m_b, l_b = _attn_sub_block(q0_ref[rows, :], kw, vw, bias_ref, 0, lo, hi)
        for c in range(2):
            cols = slice(c * LANES, (c + 1) * LANES)
            m_st[c, rows, :] = m_b[:, cols]
            l_st[c, rows, :] = l_b[:, cols]
            a_st[c, rows, :] = acc[:, cols]
        return carry

    lax.fori_loop(0, T_ATT // QB, body0, 0, unroll=ATT_UNROLL)

    for g, q_ref, cat in ((1, q1_ref, cat1), (2, q2_ref, cat2)):
        d = DILATED_PATTERNS[g][1]
        n_res = T_ATT // d
        sb_per_res = n_res // QB

        def body(idx, carry, g=g, q_ref=q_ref, cat=cat, n_res=n_res, sb_per_res=sb_per_res):
            r = idx // sb_per_res
            sb = idx % sb_per_res
            kw, vw = window(cat, r, sb)
            lo, hi = edges(g, n_res, sb)
            q_sub = q_ref[r, pl.ds(pl.multiple_of(sb * QB, QB), QB), :]
            acc, m_b, l_b = _attn_sub_block(q_sub, kw, vw, bias_ref, g, lo, hi)
            rows = pl.ds(pl.multiple_of(idx * QB, QB), QB)
            m_tmp[rows, :] = m_b
            l_tmp[rows, :] = l_b
            a_tmp[rows, :] = acc
            return carry

        lax.fori_loop(0, T_ATT // QB, body, 0, unroll=ATT_UNROLL)

        for r in range(d):
            for ch in range(sb_per_res):
                src = slice(r * n_res + ch * QB, r * n_res + (ch + 1) * QB)
                tok = pl.ds(ch * QB * d + r, QB, stride=d)
                for c in range(2):
                    cols = slice(c * LANES, (c + 1) * LANES)
                    m_new_part = m_tmp[src, cols]
                    m_old = m_st[c, tok, :]
                    m_new = jnp.maximum(m_old, m_new_part)
                    e_old = jnp.exp(m_old - m_new)
                    e_new = jnp.exp(m_new_part - m_new)
                    m_st[c, tok, :] = m_new
                    l_st[c, tok, :] = e_old * l_st[c, tok, :] + e_new * l_tmp[src, cols]
                    a_st[c, tok, :] = e_old * a_st[c, tok, :] + e_new * a_tmp[src, cols]

    for c in range(2):
        y_ref[:, c * LANES:(c + 1) * LANES] = (a_st[c] / l_st[c]).astype(BF16)


def _attn_call(q0, kv0, q1, kv1, q2, kv2, batch, seq):
    n = q0.shape[0]
    tiles = seq // T_ATT
    specs = []
    scratch = []
    blocks_per_tile = T_ATT // HALF
    n_half_blocks = n // HALF
    specs += [
        pl.BlockSpec((T_ATT, GROUP_WIDTH), lambda b, j: (b * tiles + j, 0)),
        pl.BlockSpec((T_ATT, 2 * GROUP_WIDTH), lambda b, j: (b * tiles + j, 0)),
        pl.BlockSpec((HALF, 2 * GROUP_WIDTH),
                     lambda b, j: (jnp.maximum((b * tiles + j) * blocks_per_tile - 1, 0), 0)),
        pl.BlockSpec((HALF, 2 * GROUP_WIDTH),
                     lambda b, j: (jnp.minimum((b * tiles + j + 1) * blocks_per_tile, n_half_blocks - 1), 0)),
    ]
    scratch.append(pltpu.VMEM((1, T_ATT + 2 * HALF, 2 * GROUP_WIDTH), BF16))
    for g in (1, 2):
        d = DILATED_PATTERNS[g][1]
        n_res = T_ATT // d
        per_tile = n_res // HALF
        last = seq // d // HALF - 1
        specs += [
            pl.BlockSpec((None, d, n_res, GROUP_WIDTH), lambda b, j: (b, 0, j, 0)),
            pl.BlockSpec((None, d, n_res, 2 * GROUP_WIDTH), lambda b, j: (b, 0, j, 0)),
            pl.BlockSpec((None, d, HALF, 2 * GROUP_WIDTH),
                         lambda b, j, per_tile=per_tile: (b, 0, jnp.maximum(j * per_tile - 1, 0), 0)),
            pl.BlockSpec((None, d, HALF, 2 * GROUP_WIDTH),
                         lambda b, j, per_tile=per_tile, last=last: (b, 0, jnp.minimum((j + 1) * per_tile, last), 0)),
        ]
        scratch.append(pltpu.VMEM((d, n_res + 2 * HALF, 2 * GROUP_WIDTH), BF16))
    scratch.append(pltpu.VMEM((N_HEADS_A, QB, KB), F32))
    scratch += [pltpu.VMEM((2, T_ATT, LANES), F32) for _ in range(3)]
    scratch += [pltpu.VMEM((T_ATT, GROUP_WIDTH), F32) for _ in range(3)]
    return pl.pallas_call(
        functools.partial(_attn_kernel, seq=seq),
        grid=(batch, tiles),
        in_specs=specs,
        out_specs=pl.BlockSpec((T_ATT, GROUP_WIDTH), lambda b, j: (b * tiles + j, 0)),
        out_shape=jax.ShapeDtypeStruct((n, GROUP_WIDTH), BF16),
        scratch_shapes=scratch,
        compiler_params=pltpu.CompilerParams(dimension_semantics=("arbitrary", "arbitrary"),
                                             vmem_limit_bytes=VMEM_LIMIT),
        name="attn",
    )(q0, kv0, kv0, kv0, q1, kv1, kv1, kv1, q2, kv2, kv2, kv2)


def _split_dot(a, w_cat, w_hi):
    a_hi = a.astype(BF16)
    a_lo = (a - a_hi.astype(F32)).astype(BF16)
    both = jnp.dot(a_hi, w_cat, preferred_element_type=F32)
    return both[:, 0:LANES] + both[:, LANES:] + jnp.dot(a_lo, w_hi, preferred_element_type=F32)


ST_BASE, ST_TILE, ST_FREE = 0, 1, 2
AL_NEW = 0
RT_E, RT_W, RT_POS = 0, 2, 4


def _mix_kernel(x_ref, ya_ref, bg_ref, u_ref, up_ref, un_ref, gate_ref,
                wa_ref, wc_ref, wo_ref, cw_ref, g2_ref, wr_cat_ref, wr_hi_ref, br_ref,
                x1_ref, route_ref, routet_ref, alloc_ref, xs_hbm,
                st_ref, hb0, hb1, posv, poss, psem, dsem, *, seq, trash_row):
    i = pl.program_id(0)
    last_step = pl.num_programs(0) - 1
    hbs = (hb0, hb1)

    def rows_done(h):
        for k in range(2):
            pltpu.make_async_copy(hbs[h], xs_hbm.at[pl.ds(0, TM_MIX * SUBLANES)], dsem.at[h, k]).wait()

    def send_row(h, j, first_sublane):
        src = hbs[h].at[pl.ds(first_sublane, SUBLANES)]
        for k in range(2):
            dst = xs_hbm.at[pl.ds(pl.multiple_of(poss[h, RT_POS + k, j], SUBLANES), SUBLANES)]
            pltpu.make_async_copy(src, dst, dsem.at[h, k]).start(priority=k)

    def send_rows(h):
        def one(j, carry):
            send_row(h, j, pl.multiple_of(j * SUBLANES, SUBLANES))
            return carry
        lax.fori_loop(0, TM_MIX, one, 0, unroll=8)

    def positions_to_smem(h):
        return pltpu.make_async_copy(posv, poss.at[h], psem.at[h])

    @pl.when(i == 0)
    def _():
        st_ref[...] = jnp.zeros_like(st_ref)
        hb1[...] = jnp.zeros_like(hb1)
        spare = (trash_row + lax.broadcasted_iota(jnp.int32, posv.shape, 1)
                 + jnp.where(lax.broadcasted_iota(jnp.int32, posv.shape, 0) == RT_POS + 1, TM_MIX, 0))
        posv[...] = spare * SUBLANES
        positions_to_smem(1).start()

    def tile(h):
        rows = pl.ds(h * TM_MIX, TM_MIX)
        t0 = (i * MIX_TILES + h) * TM_MIX
        positions_to_smem(1 - h).wait()

        def send_other(part):
            for j in range(part * TM_MIX // 4, (part + 1) * TM_MIX // 4):
                send_row(1 - h, j, j * SUBLANES)

        u = u_ref[rows, :].astype(F32)
        row = lax.broadcasted_iota(jnp.int32, (TM_MIX, CONV_WIDTH), 0)
        if h == 0:
            prev_row = jnp.where(t0 % seq == 0, 0.0, up_ref[15:16, :].astype(F32))
        else:
            prev_row = u_ref[h * TM_MIX - 1:h * TM_MIX, :].astype(F32)
        if h == MIX_TILES - 1:
            next_row = jnp.where((t0 + TM_MIX) % seq == 0, 0.0, un_ref[0:1, :].astype(F32))
        else:
            next_row = u_ref[(h + 1) * TM_MIX:(h + 1) * TM_MIX + 1, :].astype(F32)
        u_prev = jnp.where(row == 0, prev_row, pltpu.roll(u, 1, axis=0))
        u_next = jnp.where(row == TM_MIX - 1, next_row, pltpu.roll(u, TM_MIX - 1, axis=0))
        conv = cw_ref[0:1, :] * u_prev + cw_ref[1:2, :] * u + cw_ref[2:3, :] * u_next
        yb_in = (bg_ref[rows, :].astype(F32) * conv).astype(BF16)
        send_other(0)

        y_a = jnp.dot(ya_ref[rows, :], wa_ref[...], preferred_element_type=F32)
        send_other(1)
        y_b = jnp.dot(yb_in, wc_ref[...], preferred_element_type=F32)
        merged = gate_ref[rows, 0:D_MODEL] * y_a.astype(BF16) + gate_ref[rows, D_MODEL:] * y_b.astype(BF16)
        x1 = x_ref[rows, :] + jnp.dot(merged, wo_ref[...], preferred_element_type=F32)
        x1_ref[rows, :] = x1

        h2 = _rms(x1, g2_ref[...])
        if h == 0:
            pl.when(i > 0)(lambda: rows_done(0))
        else:
            rows_done(h)
        for c in range(ROW_CHUNKS):
            hbs[h][pl.ds(c, TM_MIX, stride=ROW_CHUNKS), :] = h2[:, c * LANES:(c + 1) * LANES]

        logits = _split_dot(h2, wr_cat_ref[...], wr_hi_ref[...]) + br_ref[...]
        send_other(2)
        send_other(3)
        lane = lax.broadcasted_iota(jnp.int32, (TM_MIX, LANES), 1)
        lane_f = lane.astype(F32)
        neg = -jnp.inf
        big = float(LANES)
        is_group = lane < N_EXPERT_GROUPS
        cm = jnp.where(is_group, logits, neg)
        cmax = jnp.max(cm, axis=1, keepdims=True)
        g_idx = jnp.min(jnp.where(cm == cmax, lane_f, big), axis=1, keepdims=True)
        p_group = 1.0 / jnp.sum(jnp.where(is_group, jnp.exp(logits - cmax), 0.0), axis=1, keepdims=True)
        f_lo = N_EXPERT_GROUPS + EXPERTS_PER_GROUP * g_idx
        in_group = (lane_f >= f_lo) & (lane_f < f_lo + EXPERTS_PER_GROUP)
        fm = jnp.where(in_group, logits, neg)
        f1 = jnp.max(fm, axis=1, keepdims=True)
        i1 = jnp.min(jnp.where(fm == f1, lane_f, big), axis=1, keepdims=True)
        fm2 = jnp.where(lane_f == i1, neg, fm)
        f2 = jnp.max(fm2, axis=1, keepdims=True)
        i2 = jnp.min(jnp.where(fm2 == f2, lane_f, big), axis=1, keepdims=True)
        e21 = jnp.exp(f2 - f1)
        w_1 = p_group / (1.0 + e21)
        w_2 = p_group * e21 / (1.0 + e21)
        e_1 = i1 - N_EXPERT_GROUPS
        e_2 = i2 - N_EXPERT_GROUPS

        onehot = jnp.where((lane_f == e_1) | (lane_f == e_2), 1.0, 0.0)
        r_i = lax.broadcasted_iota(jnp.int32, (TM_MIX, TM_MIX), 0)
        c_i = lax.broadcasted_iota(jnp.int32, (TM_MIX, TM_MIX), 1)
        tri = jnp.where(c_i < r_i, 1.0, 0.0).astype(BF16)
        base = st_ref[ST_BASE:ST_BASE + 1, :]
        before = jnp.dot(tri, onehot.astype(BF16), preferred_element_type=F32) + base

        tile_rows = float(TM_MOE)
        cur_tile = st_ref[ST_TILE:ST_TILE + 1, :]
        next_free = st_ref[ST_FREE:ST_FREE + 1, :]
        count = jnp.sum(onehot, axis=0, keepdims=True)
        slot0 = jnp.floor(base * (1.0 / tile_rows))
        partial = (base - slot0 * tile_rows) > 0.0
        slot_last = jnp.floor((base + count - 1.0) * (1.0 / tile_rows))
        n_new = jnp.where(count > 0.0, slot_last - slot0 + 1.0 - jnp.where(partial, 1.0, 0.0), 0.0)
        e_r = lax.broadcasted_iota(jnp.int32, (LANES, LANES), 0)
        e_c = lax.broadcasted_iota(jnp.int32, (LANES, LANES), 1)
        earlier = jnp.where(e_r < e_c, 1.0, 0.0).astype(BF16)
        new_before = jnp.dot(jnp.broadcast_to(n_new, (SUBLANES, LANES)).astype(BF16), earlier,
                             preferred_element_type=F32)[0:1, :]
        fresh = next_free + new_before - jnp.where(partial, 1.0, 0.0) - slot0
        partial_slot = jnp.where(partial, slot0, -1.0)

        def tile_of(slot_idx, fresh_v, cur_v, partial_v):
            return jnp.where(slot_idx == partial_v, cur_v, fresh_v + slot_idx)

        def pick(row_vec, e):
            return jnp.sum(jnp.where(lane_f == e, row_vec, 0.0), axis=1, keepdims=True)

        positions = []
        for e in (e_1, e_2):
            rank = pick(before, e)
            s = jnp.floor(rank * (1.0 / tile_rows))
            tid = tile_of(s, pick(fresh, e), pick(cur_tile, e), pick(partial_slot, e))
            positions.append(tid * tile_rows + (rank - s * tile_rows))

        st_ref[ST_BASE:ST_BASE + 1, :] = base + count
        st_ref[ST_TILE:ST_TILE + 1, :] = jnp.where(count > 0.0, tile_of(slot_last, fresh, cur_tile, partial_slot),
                                                    cur_tile)
        st_ref[ST_FREE:ST_FREE + 1, :] = next_free + jnp.sum(n_new, axis=1, keepdims=True)
        alloc_ref[h] = jnp.broadcast_to(n_new, (SUBLANES, LANES))

        route = jnp.zeros((TM_MIX, LANES), F32)
        for k, val in enumerate((e_1, e_2, w_1, w_2, positions[0], positions[1])):
            route = jnp.where(lane == k, val, route)
        route_ref[rows, :] = route
        route_t = route.T[0:SUBLANES, :]
        routet_ref[:, h * TM_MIX:(h + 1) * TM_MIX] = route_t
        posv[...] = route_t.astype(jnp.int32) * SUBLANES
        positions_to_smem(h).start()

    tile(0)
    tile(1)

    @pl.when(i == last_step)
    def _():
        positions_to_smem(1).wait()
        rows_done(0)
        send_rows(1)
        rows_done(1)

        posv[:, 0:LANES] = st_ref[...].astype(jnp.int32)
        state = positions_to_smem(0)
        state.start()
        state.wait()
        hb0[...] = jnp.zeros_like(hb0)

        def zero_jobs():
            for e in range(N_EXPERTS):
                fill = poss[0, ST_BASE, e] & (TM_MOE - 1)
                first = poss[0, ST_TILE, e] * TM_MOE
                at = fill
                size = 1
                while size < TM_MOE:
                    take = (fill > 0) & ((at & size) != 0)
                    yield take, first + at, size
                    at = at + jnp.where(take, size, 0)
                    size *= 2
            for t in range(N_EXPERTS):
                tile = poss[0, ST_FREE, 0] + t
                yield tile < trash_row // TM_MOE, tile * TM_MOE, TM_MOE

        def zero_copy(first, rows):
            return pltpu.make_async_copy(
                hb0.at[pl.ds(0, rows * SUBLANES)],
                xs_hbm.at[pl.ds(pl.multiple_of(first * SUBLANES, SUBLANES), rows * SUBLANES)], psem.at[0])

        for take, first, rows in zero_jobs():
            pl.when(take)(lambda first=first, rows=rows: zero_copy(first, rows).start())
        for take, first, rows in zero_jobs():
            pl.when(take)(lambda first=first, rows=rows: zero_copy(first, rows).wait())


def _mix_call(x2, y_attn, bgate, u, gates, wa, wc, wo, conv_w, g2, wr_cat, wr_hi, br, seq, n_tiles):
    n = x2.shape[0]
    step_rows = MIX_TILES * TM_MIX
    assert seq % step_rows == 0
    n_steps = n // step_rows
    row = lambda i: (i, 0)
    const = lambda i: (0, 0)
    halo = 16
    per_step = step_rows // halo
    last = n // halo - 1
    xs_rows = n_tiles * TM_MOE + 2 * TM_MIX
    return pl.pallas_call(
        functools.partial(_mix_kernel, seq=seq, trash_row=n_tiles * TM_MOE),
        grid=(n_steps,),
        in_specs=[
            pl.BlockSpec((step_rows, D_MODEL), row),
            pl.BlockSpec((step_rows, GROUP_WIDTH), row),
            pl.BlockSpec((step_rows, CONV_WIDTH), row),
            pl.BlockSpec((step_rows, CONV_WIDTH), row),
            pl.BlockSpec((halo, CONV_WIDTH), lambda i: (jnp.maximum(i * per_step - 1, 0), 0)),
            pl.BlockSpec((halo, CONV_WIDTH), lambda i: (jnp.minimum((i + 1) * per_step, last), 0)),
            pl.BlockSpec((step_rows, 2 * D_MODEL), row),
            pl.BlockSpec((GROUP_WIDTH, D_MODEL), const),
            pl.BlockSpec((CONV_WIDTH, D_MODEL), const),
            pl.BlockSpec((D_MODEL, D_MODEL), const),
            pl.BlockSpec((3, CONV_WIDTH), const),
            pl.BlockSpec((1, D_MODEL), const),
            pl.BlockSpec((D_MODEL, 2 * LANES), const),
            pl.BlockSpec((D_MODEL, LANES), const),
            pl.BlockSpec((1, LANES), const),
        ],
        out_specs=[
            pl.BlockSpec((step_rows, D_MODEL), row),
            pl.BlockSpec((step_rows, LANES), row),
            pl.BlockSpec((SUBLANES, step_rows), lambda i: (0, i)),
            pl.BlockSpec((MIX_TILES, SUBLANES, LANES), lambda i: (i, 0, 0)),
            pl.BlockSpec(memory_space=pl.ANY),
        ],
        out_shape=[
            jax.ShapeDtypeStruct((n, D_MODEL), F32),
            jax.ShapeDtypeStruct((n, LANES), F32),
            jax.ShapeDtypeStruct((SUBLANES, n), F32),
            jax.ShapeDtypeStruct((n // TM_MIX, SUBLANES, LANES), F32),
            jax.ShapeDtypeStruct((xs_rows * ROW_CHUNKS, LANES), F32),
        ],
        scratch_shapes=[pltpu.VMEM((SUBLANES, LANES), F32),
                        pltpu.VMEM((TM_MIX * ROW_CHUNKS, LANES), F32),
                        pltpu.VMEM((TM_MIX * ROW_CHUNKS, LANES), F32),
                        pltpu.VMEM((SUBLANES, TM_MIX), jnp.int32),
                        pltpu.SMEM((MIX_TILES, SUBLANES, TM_MIX), jnp.int32),
                        pltpu.SemaphoreType.DMA((MIX_TILES,)),
                        pltpu.SemaphoreType.DMA((MIX_TILES, 2))],
        compiler_params=pltpu.CompilerParams(dimension_semantics=("arbitrary",),
                                             vmem_limit_bytes=VMEM_LIMIT),
        name="mix",
    )(x2, y_attn, bgate, u, u, u, gates, wa, wc, wo, conv_w, g2, wr_cat, wr_hi, br)


def _row_gather(idx_ref, n_rows, src_hbm, dst, sem):
    def issue(pair, carry):
        for k in range(2):
            j = 2 * pair + k
            t = idx_ref[0, 0, j]
            pltpu.make_async_copy(src_hbm.at[pl.ds(pl.multiple_of(t * SUBLANES, SUBLANES), SUBLANES)],
                                  dst.at[pl.ds(pl.multiple_of(j * SUBLANES, SUBLANES), SUBLANES)],
                                  sem).start(priority=k)
        return carry
    lax.fori_loop(0, n_rows // 2, issue, 0, unroll=4)


def _row_gather_wait(n_rows, src_hbm, dst, sem):
    pltpu.make_async_copy(src_hbm.at[pl.ds(0, n_rows * SUBLANES)], dst, sem).wait()


def _rows_from_tiles(buf, first_row, n_rows):
    return jnp.concatenate(
        [buf[pl.ds(first_row * ROW_CHUNKS + c, n_rows, stride=ROW_CHUNKS), :] for c in range(ROW_CHUNKS)],
        axis=1)


def _expert_kernel(order_ref, te_ref, next_ref, nused_ref, xs_hbm, w1_hbm, w3_hbm, w2_hbm, y_ref,
                   xin, w1s, w3s, w2s, w13b, w2b, isem, wsem):
    i = pl.program_id(0)
    n_used = nused_ref[0]
    slot = i % XIN_SLOTS
    used = i < n_used

    def fetch(step):
        rows = pl.ds(order_ref[step] * TM_MOE, TM_MOE)
        s = step % XIN_SLOTS
        return [pltpu.make_async_copy(xs_hbm.at[rows, c, :], xin.at[s, c], isem.at[s]) for c in range(ROW_CHUNKS)]

    def weights(e):
        return [pltpu.make_async_copy(w_hbm.at[e], stage, wsem.at[k])
                for k, (w_hbm, stage) in enumerate(((w1_hbm, w1s), (w3_hbm, w3s), (w2_hbm, w2s)))]

    def start(copies):
        for cp in copies:
            cp.start()

    def wait(copies):
        for cp in copies:
            cp.wait()

    @pl.when(i == 0)
    def _():
        start(weights(te_ref[0]))
        start(fetch(0))
        pl.when(n_used > 1)(lambda: start(fetch(1)))

    pl.when(i + 2 < n_used)(lambda: start(fetch(i + 2)))

    @pl.when(jnp.logical_not(used))
    def _():
        y_ref[...] = jnp.zeros_like(y_ref)

    @pl.when(used & ((i == 0) | (te_ref[i] != te_ref[jnp.maximum(i - 1, 0)])))
    def _():
        wait(weights(te_ref[i]))
        w13b[:, 0:EXPERT_FF] = w1s[...].astype(BF16)
        w13b[:, EXPERT_FF:] = w3s[...].astype(BF16)
        w2b[...] = w2s[...].astype(BF16)
        pl.when(next_ref[i] != te_ref[i])(lambda: start(weights(next_ref[i])))

    @pl.when(used)
    def _():
        wait(fetch(i))
        x = jnp.concatenate([xin[slot, c] for c in range(ROW_CHUNKS)], axis=1)
        ab = jnp.dot(x.astype(BF16), w13b[...], preferred_element_type=F32)
        a = ab[:, 0:EXPERT_FF]
        hid = (a * jax.nn.sigmoid(a) * ab[:, EXPERT_FF:]).astype(BF16)
        y = jnp.dot(hid, w2b[...], preferred_element_type=F32)
        for c in range(ROW_CHUNKS):
            y_ref[pl.ds(c, TM_MOE, stride=ROW_CHUNKS), :] = y[:, c * LANES:(c + 1) * LANES]


def _expert_call(order, tile_expert, next_expert, n_used, xs_rows, w1, w3, w2):
    n_tiles = order.shape[0]
    any_space = pl.BlockSpec(memory_space=pl.ANY)
    grid_spec = pltpu.PrefetchScalarGridSpec(
        num_scalar_prefetch=4,
        grid=(n_tiles,),
        in_specs=[any_space, any_space, any_space, any_space],
        out_specs=pl.BlockSpec((TM_MOE * ROW_CHUNKS, LANES), lambda i, od, te, nx, nu: (od[i], 0)),
        scratch_shapes=[pltpu.VMEM((XIN_SLOTS, ROW_CHUNKS, TM_MOE, LANES), F32),
                        pltpu.VMEM((D_MODEL, EXPERT_FF), F32),
                        pltpu.VMEM((D_MODEL, EXPERT_FF), F32),
                        pltpu.VMEM((EXPERT_FF, D_MODEL), F32),
                        pltpu.VMEM((D_MODEL, 2 * EXPERT_FF), BF16),
                        pltpu.VMEM((EXPERT_FF, D_MODEL), BF16),
                        pltpu.SemaphoreType.DMA((XIN_SLOTS,)),
                        pltpu.SemaphoreType.DMA((3,))],
    )
    return pl.pallas_call(
        _expert_kernel,
        grid_spec=grid_spec,
        out_shape=jax.ShapeDtypeStruct((n_tiles * TM_MOE * ROW_CHUNKS, LANES), F32),
        compiler_params=pltpu.CompilerParams(dimension_semantics=("arbitrary",),
                                             vmem_limit_bytes=VMEM_LIMIT),
        name="experts",
    )(order, tile_expert, next_expert, n_used, xs_rows, w1, w3, w2)


def _combine_kernel(pos_ref, posn_ref, y_hbm, x1_ref, route_ref, g_ref, o_ref, buf0, buf1, sem):
    i = pl.program_id(0)
    n_steps = pl.num_programs(0)
    bufs = (buf0, buf1)

    @pl.when(i == 0)
    def _():
        _row_gather(pos_ref, 2 * TM_CMB, y_hbm, buf0, sem.at[0])

    for slot in range(2):
        @pl.when((i % 2 == slot) & (i + 1 < n_steps))
        def _(slot=slot):
            _row_gather(posn_ref, 2 * TM_CMB, y_hbm, bufs[1 - slot], sem.at[1 - slot])

    for slot in range(2):
        @pl.when(i % 2 == slot)
        def _(slot=slot):
            _row_gather_wait(2 * TM_CMB, y_hbm, bufs[slot], sem.at[slot])
            y_1 = _rows_from_tiles(bufs[slot], 0, TM_CMB)
            y_2 = _rows_from_tiles(bufs[slot], TM_CMB, TM_CMB)
            x = x1_ref[...] + route_ref[:, 2:3] * y_1 + route_ref[:, 3:4] * y_2
            o_ref[...] = _rms(x, g_ref[...])


def _combine_call(pos, y_flat, x1, route, g):
    n = x1.shape[0]
    n_steps = n // TM_CMB
    row = lambda i: (i, 0)
    return pl.pallas_call(
        _combine_kernel,
        grid=(n_steps,),
        in_specs=[
            pl.BlockSpec((1, 1, 2 * TM_CMB), lambda i: (i, 0, 0), memory_space=pltpu.SMEM),
            pl.BlockSpec((1, 1, 2 * TM_CMB), lambda i: (jnp.minimum(i + 1, n_steps - 1), 0, 0),
                         memory_space=pltpu.SMEM),
            pl.BlockSpec(memory_space=pl.ANY),
            pl.BlockSpec((TM_CMB, D_MODEL), row),
            pl.BlockSpec((TM_CMB, LANES), row),
            pl.BlockSpec((1, D_MODEL), lambda i: (0, 0)),
        ],
        out_specs=pl.BlockSpec((TM_CMB, D_MODEL), row),
        out_shape=jax.ShapeDtypeStruct((n, D_MODEL), F32),
        scratch_shapes=[pltpu.VMEM((2 * TM_CMB * ROW_CHUNKS, LANES), F32),
                        pltpu.VMEM((2 * TM_CMB * ROW_CHUNKS, LANES), F32),
                        pltpu.SemaphoreType.DMA((2,))],
        compiler_params=pltpu.CompilerParams(dimension_semantics=("arbitrary",),
                                             vmem_limit_bytes=VMEM_LIMIT),
        name="combine",
    )(pos, pos, y_flat, x1, route, g)


def _layer(x2, batch, seq, norm_mix_g, w_in, b_gate, conv_w, w_attn_out, w_conv_out, w_out, norm_ffn_g,
           w_route_group, b_route_group, w_route_expert, b_route_expert, w1, w3, w2, final_g):
    n = x2.shape[0]
    q0, kv0, q1, kv1, q2, kv2, bgate, u, gates = _proj_call(
        x2, norm_mix_g[None, :], w_in.astype(BF16), b_gate[None, :], batch, seq)
    y_attn = _attn_call(q0, kv0, q1, kv1, q2, kv2, batch, seq)

    n_route = N_EXPERT_GROUPS + N_EXPERTS
    w_route = jnp.pad(jnp.concatenate([w_route_group, w_route_expert], axis=1), ((0, 0), (0, LANES - n_route)))
    b_route = jnp.pad(jnp.concatenate([b_route_group, b_route_expert]), (0, LANES - n_route))[None, :]
    wr_hi = w_route.astype(BF16)
    wr_lo = (w_route - wr_hi.astype(F32)).astype(BF16)
    n_tiles = (2 * n) // TM_MOE + N_EXPERTS
    x1, route, route_t, alloc, xs_flat = _mix_call(
        x2, y_attn, bgate, u, gates, w_attn_out.astype(BF16), w_conv_out.astype(BF16), w_out.astype(BF16),
        conv_w, norm_ffn_g[None, :], jnp.concatenate([wr_hi, wr_lo], axis=1), wr_hi, b_route, seq, n_tiles)

    i32 = jnp.int32
    taken = alloc[:, AL_NEW, :N_EXPERTS].astype(i32).reshape(-1)
    k = jnp.arange(taken.shape[0], dtype=i32)
    running = jnp.sum(jnp.where(k[:, None] >= k[None, :], taken[None, :], 0), axis=1)
    n_used = running[-1:]
    tile = jnp.arange(n_tiles, dtype=i32)
    owner = jnp.sum((running[None, :] <= tile[:, None]).astype(i32), axis=1) % N_EXPERTS
    owner = jnp.where(tile < n_used[0], owner, N_EXPERTS)
    key = owner * n_tiles + tile
    place = jnp.sum((key[None, :] < key[:, None]).astype(i32), axis=1)
    at = place[None, :] == tile[:, None]
    order = jnp.sum(jnp.where(at, tile[None, :], 0), axis=1)
    step_owner = jnp.sum(jnp.where(at, owner[None, :], 0), axis=1)
    step_expert = jnp.minimum(step_owner, N_EXPERTS - 1)
    later = (step_owner[None, :] > step_owner[:, None]) & (step_owner[None, :] < N_EXPERTS)
    next_expert = jnp.min(jnp.where(later, step_owner[None, :], N_EXPERTS), axis=1)
    next_expert = jnp.where(next_expert < N_EXPERTS, next_expert, step_expert)

    y_flat = _expert_call(order, step_expert, next_expert, n_used, xs_flat.reshape(-1, ROW_CHUNKS, LANES),
                          w1, w3, w2)
    pos = route_t[RT_POS:RT_POS + 2].astype(i32)
    pos_tiles = pos.reshape(2, n // TM_CMB, TM_CMB).transpose(1, 0, 2).reshape(n // TM_CMB, 1, 2 * TM_CMB)
    return _combine_call(pos_tiles, y_flat, x1, route, final_g[None, :])


def kernel(x, norm_mix_g, w_in, b_gate, conv_w, w_attn_out, w_conv_out, w_out, norm_ffn_g,
           w_route_group, b_route_group, w_route_expert, b_route_expert, w1, w3, w2, norm_final_g):
    batch, seq, d = x.shape
    depth = w_in.shape[0]
    assert d == D_MODEL and depth == 1 and seq % T_ATT == 0
    out = _layer(x.reshape(batch * seq, d), batch, seq, norm_mix_g[0], w_in[0], b_gate[0], conv_w[0],
                 w_attn_out[0], w_conv_out[0], w_out[0], norm_ffn_g[0], w_route_group[0], b_route_group[0],
                 w_route_expert[0], b_route_expert[0], w1[0], w3[0], w2[0], norm_final_g)
    return out.reshape(batch, seq, d)
```

```python
import functools

import numpy as np
import jax
import jax.numpy as jnp
from jax import lax
from jax.experimental import pallas as pl
from jax.experimental.pallas import tpu as pltpu

F32 = jnp.float32
BF16 = jnp.bfloat16

D_MODEL = 1024
HEAD_DIM = 64
HEADS_PER_GROUP = 4
DILATED_PATTERNS = ((128, 1), (512, 4), (2048, 16))
N_GROUPS_A = 3
N_HEADS_A = N_GROUPS_A * HEADS_PER_GROUP
ATTN_WIDTH = N_HEADS_A * HEAD_DIM
GROUP_WIDTH = HEADS_PER_GROUP * HEAD_DIM
ALIBI_SPAN = 8.0
MASK_VALUE = -1e30
CONV_WIDTH = 768
N_EXPERT_GROUPS = 4
EXPERTS_PER_GROUP = 8
N_EXPERTS = 32
EXPERT_FF = 512
RMS_EPS = 1e-6

HALF = 64
LANES = 128
SUBLANES = 8
ROW_CHUNKS = D_MODEL // LANES

COL_K = ATTN_WIDTH
COL_V = 2 * ATTN_WIDTH
COL_BG = 3 * ATTN_WIDTH
COL_CG = COL_BG + CONV_WIDTH
COL_XIN = COL_CG + CONV_WIDTH
COL_GATE = COL_XIN + CONV_WIDTH
IN_COLS = COL_GATE + 2 * D_MODEL

TM_PROJ = 1024
T_ATT = 2048
QB = 128
KB = QB + 2 * HALF
ATT_UNROLL = 8
TM_MIX = 512
MIX_TILES = 2
TM_MOE = 512
XIN_SLOTS = 3
TM_CMB = 256

VMEM_LIMIT = 56 * 1024 * 1024


def _alibi_slopes():
    return np.array([2.0 ** (-ALIBI_SPAN * (i + 1) / N_HEADS_A) for i in range(N_HEADS_A)],
                    dtype=np.float32).reshape(N_GROUPS_A, HEADS_PER_GROUP)


def _rms(x, g):
    return x * lax.rsqrt(jnp.mean(x * x, axis=-1, keepdims=True) + RMS_EPS) * g


def _proj_kernel(x_ref, g_ref, w_ref, b_ref,
                 q0_ref, kv0_ref, q1_ref, kv1_ref, q2_ref, kv2_ref, bg_ref, u_ref, gate_ref, scr):
    h = _rms(x_ref[...], g_ref[...]).astype(BF16)

    def proj(c0, width):
        return jnp.dot(h, w_ref[:, c0:c0 + width], preferred_element_type=F32)

    qscale = HEAD_DIM ** -0.5
    q0_ref[...] = (proj(0, GROUP_WIDTH) * qscale).astype(BF16)
    kv0_ref[:, 0:GROUP_WIDTH] = proj(COL_K, GROUP_WIDTH).astype(BF16)
    kv0_ref[:, GROUP_WIDTH:] = proj(COL_V, GROUP_WIDTH).astype(BF16)

    for g, q_ref, kv_ref in ((1, q1_ref, kv1_ref), (2, q2_ref, kv2_ref)):
        d = DILATED_PATTERNS[g][1]
        n = TM_PROJ // d
        parts = (proj(g * GROUP_WIDTH, GROUP_WIDTH) * qscale,
                 proj(COL_K + g * GROUP_WIDTH, GROUP_WIDTH),
                 proj(COL_V + g * GROUP_WIDTH, GROUP_WIDTH))
        for i, part in enumerate(parts):
            for c in range(2):
                scr[2 * i + c] = part[:, c * LANES:(c + 1) * LANES]
        for r in range(d):
            rows = pl.ds(r, n, stride=d)
            q_ref[r] = jnp.concatenate([scr[c, rows, :] for c in range(2)], axis=1).astype(BF16)
            kv_ref[r] = jnp.concatenate([scr[c, rows, :] for c in range(2, 6)], axis=1).astype(BF16)

    bg_ref[...] = proj(COL_BG, CONV_WIDTH).astype(BF16)
    u_ref[...] = (proj(COL_CG, CONV_WIDTH) * proj(COL_XIN, CONV_WIDTH)).astype(BF16)
    for c in range(4):
        w = 2 * D_MODEL // 4
        z = proj(COL_GATE + c * w, w) + b_ref[:, c * w:(c + 1) * w]
        gate_ref[:, c * w:(c + 1) * w] = jax.nn.sigmoid(z).astype(BF16)


def _proj_call(x2, g, w_in, b_gate, batch, seq):
    n = x2.shape[0]
    steps_per_batch = seq // TM_PROJ
    d1, d2 = DILATED_PATTERNS[1][1], DILATED_PATTERNS[2][1]
    row = lambda i: (i, 0)
    res = lambda i: (i // steps_per_batch, 0, i % steps_per_batch, 0)
    const = lambda i: (0, 0)
    out_shape = [
        jax.ShapeDtypeStruct((n, GROUP_WIDTH), BF16),
        jax.ShapeDtypeStruct((n, 2 * GROUP_WIDTH), BF16),
        jax.ShapeDtypeStruct((batch, d1, seq // d1, GROUP_WIDTH), BF16),
        jax.ShapeDtypeStruct((batch, d1, seq // d1, 2 * GROUP_WIDTH), BF16),
        jax.ShapeDtypeStruct((batch, d2, seq // d2, GROUP_WIDTH), BF16),
        jax.ShapeDtypeStruct((batch, d2, seq // d2, 2 * GROUP_WIDTH), BF16),
        jax.ShapeDtypeStruct((n, CONV_WIDTH), BF16),
        jax.ShapeDtypeStruct((n, CONV_WIDTH), BF16),
        jax.ShapeDtypeStruct((n, 2 * D_MODEL), BF16),
    ]
    out_specs = [
        pl.BlockSpec((TM_PROJ, GROUP_WIDTH), row),
        pl.BlockSpec((TM_PROJ, 2 * GROUP_WIDTH), row),
        pl.BlockSpec((None, d1, TM_PROJ // d1, GROUP_WIDTH), res),
        pl.BlockSpec((None, d1, TM_PROJ // d1, 2 * GROUP_WIDTH), res),
        pl.BlockSpec((None, d2, TM_PROJ // d2, GROUP_WIDTH), res),
        pl.BlockSpec((None, d2, TM_PROJ // d2, 2 * GROUP_WIDTH), res),
        pl.BlockSpec((TM_PROJ, CONV_WIDTH), row),
        pl.BlockSpec((TM_PROJ, CONV_WIDTH), row),
        pl.BlockSpec((TM_PROJ, 2 * D_MODEL), row),
    ]
    return pl.pallas_call(
        _proj_kernel,
        grid=(n // TM_PROJ,),
        in_specs=[
            pl.BlockSpec((TM_PROJ, D_MODEL), row),
            pl.BlockSpec((1, D_MODEL), const),
            pl.BlockSpec((D_MODEL, IN_COLS), const, pipeline_mode=pl.Buffered(1)),
            pl.BlockSpec((1, 2 * D_MODEL), const),
        ],
        out_specs=out_specs,
        out_shape=out_shape,
        scratch_shapes=[pltpu.VMEM((6, TM_PROJ, LANES), F32)],
        compiler_params=pltpu.CompilerParams(dimension_semantics=("arbitrary",),
                                             vmem_limit_bytes=VMEM_LIMIT),
        name="proj",
    )(x2, g, w_in, b_gate)


def _attn_sub_block(q_sub, kw, vw, bias_ref, g, lo, hi):
    assert KB == GROUP_WIDTH
    lane = lax.broadcasted_iota(jnp.int32, (QB, KB), 1)
    edge_ok = (lane >= lo) & (lane < hi)
    heads = [(lane >= h * HEAD_DIM) & (lane < (h + 1) * HEAD_DIM) for h in range(HEADS_PER_GROUP)]
    zero = jnp.zeros((), BF16)
    q_stack = jnp.concatenate([jnp.where(hm, q_sub, zero) for hm in heads], axis=0)
    s_all = lax.dot_general(q_stack, kw, (((1,), (1,)), ((), ())), preferred_element_type=F32)
    probs = []
    m_b = l_b = None
    for h, hm in enumerate(heads):
        s = s_all[h * QB:(h + 1) * QB] + bias_ref[g * HEADS_PER_GROUP + h]
        s = jnp.where(edge_ok, s, MASK_VALUE)
        m = jnp.max(s, axis=1, keepdims=True)
        p = jnp.exp(s - m)
        l = jnp.sum(p, axis=1, keepdims=True)
        probs.append(p.astype(BF16))
        m_b = jnp.broadcast_to(m, (QB, GROUP_WIDTH)) if m_b is None else jnp.where(hm, m, m_b)
        l_b = jnp.broadcast_to(l, (QB, GROUP_WIDTH)) if l_b is None else jnp.where(hm, l, l_b)
    o_all = jnp.dot(jnp.concatenate(probs, axis=0), vw, preferred_element_type=F32)
    acc = o_all[0:QB]
    for h in range(1, HEADS_PER_GROUP):
        acc = jnp.where(heads[h], o_all[h * QB:(h + 1) * QB], acc)
    return acc, m_b, l_b


def _attn_kernel(q0_ref, kv0_ref, kv0p_ref, kv0n_ref,
                 q1_ref, kv1_ref, kv1p_ref, kv1n_ref,
                 q2_ref, kv2_ref, kv2p_ref, kv2n_ref,
                 y_ref,
                 cat0, cat1, cat2, bias_ref, m_st, l_st, a_st, m_tmp, l_tmp, a_tmp, *, seq):
    j = pl.program_id(1)

    qi = lax.broadcasted_iota(jnp.int32, (QB, KB), 0)
    kc = lax.broadcasted_iota(jnp.int32, (QB, KB), 1)
    adelta = jnp.abs(kc - HALF - qi)
    band = adelta <= HALF
    slopes = _alibi_slopes()
    for g in range(N_GROUPS_A):
        dist = (adelta * DILATED_PATTERNS[g][1]).astype(F32)
        for h in range(HEADS_PER_GROUP):
            bias_ref[g * HEADS_PER_GROUP + h] = jnp.where(band, -(float(slopes[g, h]) * dist), MASK_VALUE)

    for cat, own, prv, nxt in ((cat0, kv0_ref, kv0p_ref, kv0n_ref),
                               (cat1, kv1_ref, kv1p_ref, kv1n_ref),
                               (cat2, kv2_ref, kv2p_ref, kv2n_ref)):
        n_own = own.shape[-2]
        cat[:, 0:HALF, :] = prv[...].reshape(cat.shape[0], HALF, 2 * GROUP_WIDTH)
        cat[:, HALF:HALF + n_own, :] = own[...].reshape(cat.shape[0], n_own, 2 * GROUP_WIDTH)
        cat[:, HALF + n_own:, :] = nxt[...].reshape(cat.shape[0], HALF, 2 * GROUP_WIDTH)

    def window(cat, r, sb):
        rows = pl.ds(pl.multiple_of(sb * QB, QB), KB)
        return cat[r, rows, 0:GROUP_WIDTH], cat[r, rows, GROUP_WIDTH:]

    def edges(g, n_res, sb):
        length = seq // DILATED_PATTERNS[g][1]
        i0 = j * n_res + sb * QB
        return jnp.maximum(0, HALF - i0), jnp.minimum(KB, length + HALF - i0)

    def body0(sb, carry):
        rows = pl.ds(pl.multiple_of(sb * QB, QB), QB)
        kw, vw = window(cat0, 0, sb)
        lo, hi = edges(0, T_ATT, sb)
        acc, m_b, l_b = _attn_sub_block(q0_ref[rows, :], kw, vw, bias_ref, 0, lo, hi)
        for c in range(2):
            cols = slice(c * LANES, (c + 1) * LANES)
            m_st[c, rows, :] = m_b[:, cols]
            l_st[c, rows, :] = l_b[:, cols]
            a_st[c, rows, :] = acc[:, cols]
        return carry

    lax.fori_loop(0, T_ATT // QB, body0, 0, unroll=ATT_UNROLL)

    for g, q_ref, cat in ((1, q1_ref, cat1), (2, q2_ref, cat2)):
        d = DILATED_PATTERNS[g][1]
        n_res = T_ATT // d
        sb_per_res = n_res // QB

        def body(idx, carry, g=g, q_ref=q_ref, cat=cat, n_res=n_res, sb_per_res=sb_per_res):
            r = idx // sb_per_res
            sb = idx % sb_per_res
            kw, vw = window(cat, r, sb)
            lo, hi = edges(g, n_res, sb)
            q_sub = q_ref[r, pl.ds(pl.multiple_of(sb * QB, QB), QB), :]
            acc, m_b, l_b = _attn_sub_block(q_sub, kw, vw, bias_ref, g, lo, hi)
            rows = pl.ds(pl.multiple_of(idx * QB, QB), QB)
            m_tmp[rows, :] = m_b
            l_tmp[rows, :] = l_b
            a_tmp[rows, :] = acc
            return carry

        lax.fori_loop(0, T_ATT // QB, body, 0, unroll=ATT_UNROLL)

        for r in range(d):
            for ch in range(sb_per_res):
                src = slice(r * n_res + ch * QB, r * n_res + (ch + 1) * QB)
                tok = pl.ds(ch * QB * d + r, QB, stride=d)
                for c in range(2):
                    cols = slice(c * LANES, (c + 1) * LANES)
                    m_new_part = m_tmp[src, cols]
                    m_old = m_st[c, tok, :]
                    m_new = jnp.maximum(m_old, m_new_part)
                    e_old = jnp.exp(m_old - m_new)
                    e_new = jnp.exp(m_new_part - m_new)
                    l_new = e_old * l_st[c, tok, :] + e_new * l_tmp[src, cols]
                    a_new = e_old * a_st[c, tok, :] + e_new * a_tmp[src, cols]
                    if g == N_GROUPS_A - 1:
                        a_st[c, tok, :] = a_new / l_new
                    else:
                        m_st[c, tok, :] = m_new
                        l_st[c, tok, :] = l_new
                        a_st[c, tok, :] = a_new

    for c in range(2):
        y_ref[:, c * LANES:(c + 1) * LANES] = a_st[c].astype(BF16)


def _attn_call(q0, kv0, q1, kv1, q2, kv2, batch, seq):
    n = q0.shape[0]
    tiles = seq // T_ATT
    specs = []
    scratch = []
    blocks_per_tile = T_ATT // HALF
    n_half_blocks = n // HALF
    specs += [
        pl.BlockSpec((T_ATT, GROUP_WIDTH), lambda b, j: (b * tiles + j, 0)),
        pl.BlockSpec((T_ATT, 2 * GROUP_WIDTH), lambda b, j: (b * tiles + j, 0)),
        pl.BlockSpec((HALF, 2 * GROUP_WIDTH),
                     lambda b, j: (jnp.maximum((b * tiles + j) * blocks_per_tile - 1, 0), 0)),
        pl.BlockSpec((HALF, 2 * GROUP_WIDTH),
                     lambda b, j: (jnp.minimum((b * tiles + j + 1) * blocks_per_tile, n_half_blocks - 1), 0)),
    ]
    scratch.append(pltpu.VMEM((1, T_ATT + 2 * HALF, 2 * GROUP_WIDTH), BF16))
    for g in (1, 2):
        d = DILATED_PATTERNS[g][1]
        n_res = T_ATT // d
        per_tile = n_res // HALF
        last = seq // d // HALF - 1
        specs += [
            pl.BlockSpec((None, d, n_res, GROUP_WIDTH), lambda b, j: (b, 0, j, 0)),
            pl.BlockSpec((None, d, n_res, 2 * GROUP_WIDTH), lambda b, j: (b, 0, j, 0)),
            pl.BlockSpec((None, d, HALF, 2 * GROUP_WIDTH),
                         lambda b, j, per_tile=per_tile: (b, 0, jnp.maximum(j * per_tile - 1, 0), 0)),
            pl.BlockSpec((None, d, HALF, 2 * GROUP_WIDTH),
                         lambda b, j, per_tile=per_tile, last=last: (b, 0, jnp.minimum((j + 1) * per_tile, last), 0)),
        ]
        scratch.append(pltpu.VMEM((d, n_res + 2 * HALF, 2 * GROUP_WIDTH), BF16))
    scratch.append(pltpu.VMEM((N_HEADS_A, QB, KB), F32))
    scratch += [pltpu.VMEM((2, T_ATT, LANES), F32) for _ in range(3)]
    scratch += [pltpu.VMEM((T_ATT, GROUP_WIDTH), F32) for _ in range(3)]
    return pl.pallas_call(
        functools.partial(_attn_kernel, seq=seq),
        grid=(batch, tiles),
        in_specs=specs,
        out_specs=pl.BlockSpec((T_ATT, GROUP_WIDTH), lambda b, j: (b * tiles + j, 0)),
        out_shape=jax.ShapeDtypeStruct((n, GROUP_WIDTH), BF16),
        scratch_shapes=scratch,
        compiler_params=pltpu.CompilerParams(dimension_semantics=("arbitrary", "arbitrary"),
                                             vmem_limit_bytes=VMEM_LIMIT),
        name="attn",
    )(q0, kv0, kv0, kv0, q1, kv1, kv1, kv1, q2, kv2, kv2, kv2)


def _split_dot(a, w_cat, w_hi):
    a_hi = a.astype(BF16)
    a_lo = (a - a_hi.astype(F32)).astype(BF16)
    both = jnp.dot(a_hi, w_cat, preferred_element_type=F32)
    return both[:, 0:LANES] + both[:, LANES:] + jnp.dot(a_lo, w_hi, preferred_element_type=F32)


ST_BASE, ST_TILE, ST_FREE = 0, 1, 2
AL_NEW = 0
RT_E, RT_W, RT_POS = 0, 2, 4


def _mix_kernel(x_ref, ya_ref, bg_ref, u_ref, up_ref, un_ref, gate_ref,
                wa_ref, wc_ref, wo_ref, cw_ref, g2_ref, wr_cat_ref, wr_hi_ref, br_ref,
                x1_ref, route_ref, routet_ref, alloc_ref, xs_hbm,
                st_ref, hb0, hb1, posv, poss, psem, dsem, *, seq, trash_row):
    i = pl.program_id(0)
    last_step = pl.num_programs(0) - 1
    hbs = (hb0, hb1)

    def rows_done(h):
        for k in range(2):
            pltpu.make_async_copy(hbs[h], xs_hbm.at[pl.ds(0, TM_MIX * SUBLANES)], dsem.at[h, k]).wait()

    def send_row(h, j, first_sublane):
        src = hbs[h].at[pl.ds(first_sublane, SUBLANES)]
        for k in range(2):
            dst = xs_hbm.at[pl.ds(pl.multiple_of(poss[h, RT_POS + k, j], SUBLANES), SUBLANES)]
            pltpu.make_async_copy(src, dst, dsem.at[h, k]).start(priority=1)

    def send_rows(h):
        def one(j, carry):
            send_row(h, j, pl.multiple_of(j * SUBLANES, SUBLANES))
            return carry
        lax.fori_loop(0, TM_MIX, one, 0, unroll=8)

    def positions_to_smem(h):
        return pltpu.make_async_copy(posv, poss.at[h], psem.at[h])

    @pl.when(i == 0)
    def _():
        st_ref[...] = jnp.zeros_like(st_ref)
        hb1[...] = jnp.zeros_like(hb1)
        spare = (trash_row + lax.broadcasted_iota(jnp.int32, posv.shape, 1)
                 + jnp.where(lax.broadcasted_iota(jnp.int32, posv.shape, 0) == RT_POS + 1, TM_MIX, 0))
        posv[...] = spare * SUBLANES
        positions_to_smem(1).start()

    def tile(h):
        rows = pl.ds(h * TM_MIX, TM_MIX)
        t0 = (i * MIX_TILES + h) * TM_MIX
        positions_to_smem(1 - h).wait()

        def send_other(part):
            for j in range(part * TM_MIX // 4, (part + 1) * TM_MIX // 4):
                send_row(1 - h, j, j * SUBLANES)

        u = u_ref[rows, :].astype(F32)
        row = lax.broadcasted_iota(jnp.int32, (TM_MIX, CONV_WIDTH), 0)
        if h == 0:
            prev_row = jnp.where(t0 % seq == 0, 0.0, up_ref[15:16, :].astype(F32))
        else:
            prev_row = u_ref[h * TM_MIX - 1:h * TM_MIX, :].astype(F32)
        if h == MIX_TILES - 1:
            next_row = jnp.where((t0 + TM_MIX) % seq == 0, 0.0, un_ref[0:1, :].astype(F32))
        else:
            next_row = u_ref[(h + 1) * TM_MIX:(h + 1) * TM_MIX + 1, :].astype(F32)
        u_prev = jnp.where(row == 0, prev_row, pltpu.roll(u, 1, axis=0))
        u_next = jnp.where(row == TM_MIX - 1, next_row, pltpu.roll(u, TM_MIX - 1, axis=0))
        conv = cw_ref[0:1, :] * u_prev + cw_ref[1:2, :] * u + cw_ref[2:3, :] * u_next
        yb_in = (bg_ref[rows, :].astype(F32) * conv).astype(BF16)
        send_other(0)

        y_a = jnp.dot(ya_ref[rows, :], wa_ref[...], preferred_element_type=F32)
        send_other(1)
        y_b = jnp.dot(yb_in, wc_ref[...], preferred_element_type=F32)
        merged = gate_ref[rows, 0:D_MODEL] * y_a.astype(BF16) + gate_ref[rows, D_MODEL:] * y_b.astype(BF16)
        x1 = x_ref[rows, :] + jnp.dot(merged, wo_ref[...], preferred_element_type=F32)
        x1_ref[rows, :] = x1

        h2 = _rms(x1, g2_ref[...])
        if h == 0:
            pl.when(i > 0)(lambda: rows_done(0))
        else:
            rows_done(h)
        for c in range(ROW_CHUNKS):
            hbs[h][pl.ds(c, TM_MIX, stride=ROW_CHUNKS), :] = h2[:, c * LANES:(c + 1) * LANES]

        logits = _split_dot(h2, wr_cat_ref[...], wr_hi_ref[...]) + br_ref[...]
        send_other(2)
        send_other(3)
        lane = lax.broadcasted_iota(jnp.int32, (TM_MIX, LANES), 1)
        lane_f = lane.astype(F32)
        neg = -jnp.inf
        big = float(LANES)
        is_group = lane < N_EXPERT_GROUPS
        cm = jnp.where(is_group, logits, neg)
        cmax = jnp.max(cm, axis=1, keepdims=True)
        g_idx = jnp.min(jnp.where(cm == cmax, lane_f, big), axis=1, keepdims=True)
        p_group = 1.0 / jnp.sum(jnp.where(is_group, jnp.exp(logits - cmax), 0.0), axis=1, keepdims=True)
        f_lo = N_EXPERT_GROUPS + EXPERTS_PER_GROUP * g_idx
        in_group = (lane_f >= f_lo) & (lane_f < f_lo + EXPERTS_PER_GROUP)
        fm = jnp.where(in_group, logits, neg)
        f1 = jnp.max(fm, axis=1, keepdims=True)
        i1 = jnp.min(jnp.where(fm == f1, lane_f, big), axis=1, keepdims=True)
        fm2 = jnp.where(lane_f == i1, neg, fm)
        f2 = jnp.max(fm2, axis=1, keepdims=True)
        i2 = jnp.min(jnp.where(fm2 == f2, lane_f, big), axis=1, keepdims=True)
        e21 = jnp.exp(f2 - f1)
        w_1 = p_group / (1.0 + e21)
        w_2 = p_group * e21 / (1.0 + e21)
        e_1 = i1 - N_EXPERT_GROUPS
        e_2 = i2 - N_EXPERT_GROUPS

        onehot = jnp.where((lane_f == e_1) | (lane_f == e_2), 1.0, 0.0)
        r_i = lax.broadcasted_iota(jnp.int32, (TM_MIX, TM_MIX), 0)
        c_i = lax.broadcasted_iota(jnp.int32, (TM_MIX, TM_MIX), 1)
        tri = jnp.where(c_i < r_i, 1.0, 0.0).astype(BF16)
        base = st_ref[ST_BASE:ST_BASE + 1, :]
        before = jnp.dot(tri, onehot.astype(BF16), preferred_element_type=F32) + base

        tile_rows = float(TM_MOE)
        cur_tile = st_ref[ST_TILE:ST_TILE + 1, :]
        next_free = st_ref[ST_FREE:ST_FREE + 1, :]
        count = jnp.sum(onehot, axis=0, keepdims=True)
        slot0 = jnp.floor(base * (1.0 / tile_rows))
        partial = (base - slot0 * tile_rows) > 0.0
        slot_last = jnp.floor((base + count - 1.0) * (1.0 / tile_rows))
        n_new = jnp.where(count > 0.0, slot_last - slot0 + 1.0 - jnp.where(partial, 1.0, 0.0), 0.0)
        e_r = lax.broadcasted_iota(jnp.int32, (LANES, LANES), 0)
        e_c = lax.broadcasted_iota(jnp.int32, (LANES, LANES), 1)
        earlier = jnp.where(e_r < e_c, 1.0, 0.0).astype(BF16)
        new_before = jnp.dot(jnp.broadcast_to(n_new, (SUBLANES, LANES)).astype(BF16), earlier,
                             preferred_element_type=F32)[0:1, :]
        fresh = next_free + new_before - jnp.where(partial, 1.0, 0.0) - slot0
        partial_slot = jnp.where(partial, slot0, -1.0)

        def tile_of(slot_idx, fresh_v, cur_v, partial_v):
            return jnp.where(slot_idx == partial_v, cur_v, fresh_v + slot_idx)

        def pick(row_vec, e):
            return jnp.sum(jnp.where(lane_f == e, row_vec, 0.0), axis=1, keepdims=True)

        positions = []
        for e in (e_1, e_2):
            rank = pick(before, e)
            s = jnp.floor(rank * (1.0 / tile_rows))
            tid = tile_of(s, pick(fresh, e), pick(cur_tile, e), pick(partial_slot, e))
            positions.append(tid * tile_rows + (rank - s * tile_rows))

        st_ref[ST_BASE:ST_BASE + 1, :] = base + count
        st_ref[ST_TILE:ST_TILE + 1, :] = jnp.where(count > 0.0, tile_of(slot_last, fresh, cur_tile, partial_slot),
                                                    cur_tile)
        st_ref[ST_FREE:ST_FREE + 1, :] = next_free + jnp.sum(n_new, axis=1, keepdims=True)
        alloc_ref[h] = jnp.broadcast_to(n_new, (SUBLANES, LANES))

        route = jnp.zeros((TM_MIX, LANES), F32)
        for k, val in enumerate((e_1, e_2, w_1, w_2, positions[0], positions[1])):
            route = jnp.where(lane == k, val, route)
        route_ref[rows, :] = route
        route_t = route.T[0:SUBLANES, :]
        routet_ref[:, h * TM_MIX:(h + 1) * TM_MIX] = route_t
        posv[...] = route_t.astype(jnp.int32) * SUBLANES
        positions_to_smem(h).start()

    tile(0)
    tile(1)

    @pl.when(i == last_step)
    def _():
        positions_to_smem(1).wait()
        rows_done(0)
        send_rows(1)
        rows_done(1)

        posv[:, 0:LANES] = st_ref[...].astype(jnp.int32)
        state = positions_to_smem(0)
        state.start()
        state.wait()
        hb0[...] = jnp.zeros_like(hb0)

        def zero_jobs():
            for e in range(N_EXPERTS):
                fill = poss[0, ST_BASE, e] & (TM_MOE - 1)
                first = poss[0, ST_TILE, e] * TM_MOE
                at = fill
                size = 1
                while size < TM_MOE:
                    take = (fill > 0) & ((at & size) != 0)
                    yield take, first + at, size
                    at = at + jnp.where(take, size, 0)
                    size *= 2
            for t in range(N_EXPERTS):
                tile = poss[0, ST_FREE, 0] + t
                yield tile < trash_row // TM_MOE, tile * TM_MOE, TM_MOE

        def zero_copy(first, rows):
            return pltpu.make_async_copy(
                hb0.at[pl.ds(0, rows * SUBLANES)],
                xs_hbm.at[pl.ds(pl.multiple_of(first * SUBLANES, SUBLANES), rows * SUBLANES)], psem.at[0])

        for take, first, rows in zero_jobs():
            pl.when(take)(lambda first=first, rows=rows: zero_copy(first, rows).start())
        for take, first, rows in zero_jobs():
            pl.when(take)(lambda first=first, rows=rows: zero_copy(first, rows).wait())


def _mix_call(x2, y_attn, bgate, u, gates, wa, wc, wo, conv_w, g2, wr_cat, wr_hi, br, seq, n_tiles):
    n = x2.shape[0]
    step_rows = MIX_TILES * TM_MIX
    assert seq % step_rows == 0
    n_steps = n // step_rows
    row = lambda i: (i, 0)
    const = lambda i: (0, 0)
    halo = 16
    per_step = step_rows // halo
    last = n // halo - 1
    xs_rows = n_tiles * TM_MOE + 2 * TM_MIX
    return pl.pallas_call(
        functools.partial(_mix_kernel, seq=seq, trash_row=n_tiles * TM_MOE),
        grid=(n_steps,),
        in_specs=[
            pl.BlockSpec((step_rows, D_MODEL), row),
            pl.BlockSpec((step_rows, GROUP_WIDTH), row),
            pl.BlockSpec((step_rows, CONV_WIDTH), row),
            pl.BlockSpec((step_rows, CONV_WIDTH), row),
            pl.BlockSpec((halo, CONV_WIDTH), lambda i: (jnp.maximum(i * per_step - 1, 0), 0)),
            pl.BlockSpec((halo, CONV_WIDTH), lambda i: (jnp.minimum((i + 1) * per_step, last), 0)),
            pl.BlockSpec((step_rows, 2 * D_MODEL), row),
            pl.BlockSpec((GROUP_WIDTH, D_MODEL), const),
            pl.BlockSpec((CONV_WIDTH, D_MODEL), const),
            pl.BlockSpec((D_MODEL, D_MODEL), const),
            pl.BlockSpec((3, CONV_WIDTH), const),
            pl.BlockSpec((1, D_MODEL), const),
            pl.BlockSpec((D_MODEL, 2 * LANES), const),
            pl.BlockSpec((D_MODEL, LANES), const),
            pl.BlockSpec((1, LANES), const),
        ],
        out_specs=[
            pl.BlockSpec((step_rows, D_MODEL), row),
            pl.BlockSpec((step_rows, LANES), row),
            pl.BlockSpec((SUBLANES, step_rows), lambda i: (0, i)),
            pl.BlockSpec((MIX_TILES, SUBLANES, LANES), lambda i: (i, 0, 0)),
            pl.BlockSpec(memory_space=pl.ANY),
        ],
        out_shape=[
            jax.ShapeDtypeStruct((n, D_MODEL), F32),
            jax.ShapeDtypeStruct((n, LANES), F32),
            jax.ShapeDtypeStruct((SUBLANES, n), F32),
            jax.ShapeDtypeStruct((n // TM_MIX, SUBLANES, LANES), F32),
            jax.ShapeDtypeStruct((xs_rows * ROW_CHUNKS, LANES), F32),
        ],
        scratch_shapes=[pltpu.VMEM((SUBLANES, LANES), F32),
                        pltpu.VMEM((TM_MIX * ROW_CHUNKS, LANES), F32),
                        pltpu.VMEM((TM_MIX * ROW_CHUNKS, LANES), F32),
                        pltpu.VMEM((SUBLANES, TM_MIX), jnp.int32),
                        pltpu.SMEM((MIX_TILES, SUBLANES, TM_MIX), jnp.int32),
                        pltpu.SemaphoreType.DMA((MIX_TILES,)),
                        pltpu.SemaphoreType.DMA((MIX_TILES, 2))],
        compiler_params=pltpu.CompilerParams(dimension_semantics=("arbitrary",),
                                             vmem_limit_bytes=VMEM_LIMIT),
        name="mix",
    )(x2, y_attn, bgate, u, u, u, gates, wa, wc, wo, conv_w, g2, wr_cat, wr_hi, br)


def _row_gather(idx_ref, n_rows, src_hbm, dst, sem):
    def issue(pair, carry):
        for k in range(2):
            j = 2 * pair + k
            t = idx_ref[0, 0, j]
            pltpu.make_async_copy(src_hbm.at[pl.ds(pl.multiple_of(t * SUBLANES, SUBLANES), SUBLANES)],
                                  dst.at[pl.ds(pl.multiple_of(j * SUBLANES, SUBLANES), SUBLANES)],
                                  sem).start(priority=k)
        return carry
    lax.fori_loop(0, n_rows // 2, issue, 0, unroll=4)


def _row_gather_wait(n_rows, src_hbm, dst, sem):
    pltpu.make_async_copy(src_hbm.at[pl.ds(0, n_rows * SUBLANES)], dst, sem).wait()


def _rows_from_tiles(buf, first_row, n_rows):
    return jnp.concatenate(
        [buf[pl.ds(first_row * ROW_CHUNKS + c, n_rows, stride=ROW_CHUNKS), :] for c in range(ROW_CHUNKS)],
        axis=1)


def _expert_kernel(order_ref, te_ref, next_ref, nused_ref, xs_hbm, w1_hbm, w3_hbm, w2_hbm, y_ref,
                   xin, w1s, w3s, w2s, w13b, w2b, isem, wsem):
    i = pl.program_id(0)
    n_used = nused_ref[0]
    slot = i % XIN_SLOTS
    used = i < n_used

    def fetch(step):
        rows = pl.ds(order_ref[step] * TM_MOE, TM_MOE)
        s = step % XIN_SLOTS
        return [pltpu.make_async_copy(xs_hbm.at[rows, c, :], xin.at[s, c], isem.at[s]) for c in range(ROW_CHUNKS)]

    def weights(e):
        return [pltpu.make_async_copy(w_hbm.at[e], stage, wsem.at[k])
                for k, (w_hbm, stage) in enumerate(((w1_hbm, w1s), (w3_hbm, w3s), (w2_hbm, w2s)))]

    def start(copies):
        for cp in copies:
            cp.start()

    def wait(copies):
        for cp in copies:
            cp.wait()

    @pl.when(i == 0)
    def _():
        start(weights(te_ref[0]))
        start(fetch(0))
        pl.when(n_used > 1)(lambda: start(fetch(1)))

    pl.when(i + 2 < n_used)(lambda: start(fetch(i + 2)))

    @pl.when(jnp.logical_not(used))
    def _():
        y_ref[...] = jnp.zeros_like(y_ref)

    @pl.when(used & ((i == 0) | (te_ref[i] != te_ref[jnp.maximum(i - 1, 0)])))
    def _():
        wait(weights(te_ref[i]))
        w13b[:, 0:EXPERT_FF] = w1s[...].astype(BF16)
        w13b[:, EXPERT_FF:] = w3s[...].astype(BF16)
        w2b[...] = w2s[...].astype(BF16)
        pl.when(next_ref[i] != te_ref[i])(lambda: start(weights(next_ref[i])))

    @pl.when(used)
    def _():
        wait(fetch(i))
        x = jnp.concatenate([xin[slot, c] for c in range(ROW_CHUNKS)], axis=1)
        ab = jnp.dot(x.astype(BF16), w13b[...], preferred_element_type=F32)
        a = ab[:, 0:EXPERT_FF]
        hid = (a * jax.nn.sigmoid(a) * ab[:, EXPERT_FF:]).astype(BF16)
        y = jnp.dot(hid, w2b[...], preferred_element_type=F32)
        for c in range(ROW_CHUNKS):
            y_ref[pl.ds(c, TM_MOE, stride=ROW_CHUNKS), :] = y[:, c * LANES:(c + 1) * LANES]


def _expert_call(order, tile_expert, next_expert, n_used, xs_rows, w1, w3, w2):
    n_tiles = order.shape[0]
    any_space = pl.BlockSpec(memory_space=pl.ANY)
    grid_spec = pltpu.PrefetchScalarGridSpec(
        num_scalar_prefetch=4,
        grid=(n_tiles,),
        in_specs=[any_space, any_space, any_space, any_space],
        out_specs=pl.BlockSpec((TM_MOE * ROW_CHUNKS, LANES), lambda i, od, te, nx, nu: (od[i], 0)),
        scratch_shapes=[pltpu.VMEM((XIN_SLOTS, ROW_CHUNKS, TM_MOE, LANES), F32),
                        pltpu.VMEM((D_MODEL, EXPERT_FF), F32),
                        pltpu.VMEM((D_MODEL, EXPERT_FF), F32),
                        pltpu.VMEM((EXPERT_FF, D_MODEL), F32),
                        pltpu.VMEM((D_MODEL, 2 * EXPERT_FF), BF16),
                        pltpu.VMEM((EXPERT_FF, D_MODEL), BF16),
                        pltpu.SemaphoreType.DMA((XIN_SLOTS,)),
                        pltpu.SemaphoreType.DMA((3,))],
    )
    return pl.pallas_call(
        _expert_kernel,
        grid_spec=grid_spec,
        out_shape=jax.ShapeDtypeStruct((n_tiles * TM_MOE * ROW_CHUNKS, LANES), F32),
        compiler_params=pltpu.CompilerParams(dimension_semantics=("arbitrary",),
                                             vmem_limit_bytes=VMEM_LIMIT),
        name="experts",
    )(order, tile_expert, next_expert, n_used, xs_rows, w1, w3, w2)


def _combine_kernel(pos_ref, posn_ref, y_hbm, x1_ref, route_ref, g_ref, o_ref, buf0, buf1, sem):
    i = pl.program_id(0)
    n_steps = pl.num_programs(0)
    bufs = (buf0, buf1)

    @pl.when(i == 0)
    def _():
        _row_gather(pos_ref, 2 * TM_CMB, y_hbm, buf0, sem.at[0])

    for slot in range(2):
        @pl.when((i % 2 == slot) & (i + 1 < n_steps))
        def _(slot=slot):
            _row_gather(posn_ref, 2 * TM_CMB, y_hbm, bufs[1 - slot], sem.at[1 - slot])

    for slot in range(2):
        @pl.when(i % 2 == slot)
        def _(slot=slot):
            _row_gather_wait(2 * TM_CMB, y_hbm, bufs[slot], sem.at[slot])
            y_1 = _rows_from_tiles(bufs[slot], 0, TM_CMB)
            y_2 = _rows_from_tiles(bufs[slot], TM_CMB, TM_CMB)
            x = x1_ref[...] + route_ref[:, 2:3] * y_1 + route_ref[:, 3:4] * y_2
            o_ref[...] = _rms(x, g_ref[...])


def _combine_call(pos, y_flat, x1, route, g):
    n = x1.shape[0]
    n_steps = n // TM_CMB
    row = lambda i: (i, 0)
    return pl.pallas_call(
        _combine_kernel,
        grid=(n_steps,),
        in_specs=[
            pl.BlockSpec((1, 1, 2 * TM_CMB), lambda i: (i, 0, 0), memory_space=pltpu.SMEM),
            pl.BlockSpec((1, 1, 2 * TM_CMB), lambda i: (jnp.minimum(i + 1, n_steps - 1), 0, 0),
                         memory_space=pltpu.SMEM),
            pl.BlockSpec(memory_space=pl.ANY),
            pl.BlockSpec((TM_CMB, D_MODEL), row),
            pl.BlockSpec((TM_CMB, LANES), row),
            pl.BlockSpec((1, D_MODEL), lambda i: (0, 0)),
        ],
        out_specs=pl.BlockSpec((TM_CMB, D_MODEL), row),
        out_shape=jax.ShapeDtypeStruct((n, D_MODEL), F32),
        scratch_shapes=[pltpu.VMEM((2 * TM_CMB * ROW_CHUNKS, LANES), F32),
                        pltpu.VMEM((2 * TM_CMB * ROW_CHUNKS, LANES), F32),
                        pltpu.SemaphoreType.DMA((2,))],
        compiler_params=pltpu.CompilerParams(dimension_semantics=("arbitrary",),
                                             vmem_limit_bytes=VMEM_LIMIT),
        name="combine",
    )(pos, pos, y_flat, x1, route, g)


def _layer(x2, batch, seq, norm_mix_g, w_in, b_gate, conv_w, w_attn_out, w_conv_out, w_out, norm_ffn_g,
           w_route_group, b_route_group, w_route_expert, b_route_expert, w1, w3, w2, final_g):
    n = x2.shape[0]
    q0, kv0, q1, kv1, q2, kv2, bgate, u, gates = _proj_call(
        x2, norm_mix_g[None, :], w_in.astype(BF16), b_gate[None, :], batch, seq)
    y_attn = _attn_call(q0, kv0, q1, kv1, q2, kv2, batch, seq)

    n_route = N_EXPERT_GROUPS + N_EXPERTS
    w_route = jnp.pad(jnp.concatenate([w_route_group, w_route_expert], axis=1), ((0, 0), (0, LANES - n_route)))
    b_route = jnp.pad(jnp.concatenate([b_route_group, b_route_expert]), (0, LANES - n_route))[None, :]
    wr_hi = w_route.astype(BF16)
    wr_lo = (w_route - wr_hi.astype(F32)).astype(BF16)
    n_tiles = (2 * n) // TM_MOE + N_EXPERTS
    x1, route, route_t, alloc, xs_flat = _mix_call(
        x2, y_attn, bgate, u, gates, w_attn_out.astype(BF16), w_conv_out.astype(BF16), w_out.astype(BF16),
        conv_w, norm_ffn_g[None, :], jnp.concatenate([wr_hi, wr_lo], axis=1), wr_hi, b_route, seq, n_tiles)

    i32 = jnp.int32
    taken = alloc[:, AL_NEW, :N_EXPERTS].astype(i32).reshape(-1)
    k = jnp.arange(taken.shape[0], dtype=i32)
    running = jnp.sum(jnp.where(k[:, None] >= k[None, :], taken[None, :], 0), axis=1)
    n_used = running[-1:]
    tile = jnp.arange(n_tiles, dtype=i32)
    owner = jnp.sum((running[None, :] <= tile[:, None]).astype(i32), axis=1) % N_EXPERTS
    owner = jnp.where(tile < n_used[0], owner, N_EXPERTS)
    key = owner * n_tiles + tile
    place = jnp.sum((key[None, :] < key[:, None]).astype(i32), axis=1)
    at = place[None, :] == tile[:, None]
    order = jnp.sum(jnp.where(at, tile[None, :], 0), axis=1)
    step_owner = jnp.sum(jnp.where(at, owner[None, :], 0), axis=1)
    step_expert = jnp.minimum(step_owner, N_EXPERTS - 1)
    later = (step_owner[None, :] > step_owner[:, None]) & (step_owner[None, :] < N_EXPERTS)
    next_expert = jnp.min(jnp.where(later, step_owner[None, :], N_EXPERTS), axis=1)
    next_expert = jnp.where(next_expert < N_EXPERTS, next_expert, step_expert)

    y_flat = _expert_call(order, step_expert, next_expert, n_used, xs_flat.reshape(-1, ROW_CHUNKS, LANES),
                          w1, w3, w2)
    pos = route_t[RT_POS:RT_POS + 2].astype(i32)
    pos_tiles = pos.reshape(2, n // TM_CMB, TM_CMB).transpose(1, 0, 2).reshape(n // TM_CMB, 1, 2 * TM_CMB)
    return _combine_call(pos_tiles, y_flat, x1, route, final_g[None, :])


def kernel(x, norm_mix_g, w_in, b_gate, conv_w, w_attn_out, w_conv_out, w_out, norm_ffn_g,
           w_route_group, b_route_group, w_route_expert, b_route_expert, w1, w3, w2, norm_final_g):
    batch, seq, d = x.shape
    depth = w_in.shape[0]
    assert d == D_MODEL and depth == 1 and seq % T_ATT == 0
    out = _layer(x.reshape(batch * seq, d), batch, seq, norm_mix_g[0], w_in[0], b_gate[0], conv_w[0],
                 w_attn_out[0], w_conv_out[0], w_out[0], norm_ffn_g[0], w_route_group[0], b_route_group[0],
                 w_route_expert[0], b_route_expert[0], w1[0], w3[0], w2[0], norm_final_g)
    return out.reshape(batch, seq, d)
```

```python
import functools

import numpy as np
import jax
import jax.numpy as jnp
from jax import lax
from jax.experimental import pallas as pl
from jax.experimental.pallas import tpu as pltpu

F32 = jnp.float32
BF16 = jnp.bfloat16

D_MODEL = 1024
HEAD_DIM = 64
HEADS_PER_GROUP = 4
DILATED_PATTERNS = ((128, 1), (512, 4), (2048, 16))
N_GROUPS_A = 3
N_HEADS_A = N_GROUPS_A * HEADS_PER_GROUP
ATTN_WIDTH = N_HEADS_A * HEAD_DIM
GROUP_WIDTH = HEADS_PER_GROUP * HEAD_DIM
ALIBI_SPAN = 8.0
MASK_VALUE = -1e30
CONV_WIDTH = 768
N_EXPERT_GROUPS = 4
EXPERTS_PER_GROUP = 8
N_EXPERTS = 32
EXPERT_FF = 512
RMS_EPS = 1e-6

HALF = 64
LANES = 128
SUBLANES = 8
ROW_CHUNKS = D_MODEL // LANES

COL_K = ATTN_WIDTH
COL_V = 2 * ATTN_WIDTH
COL_BG = 3 * ATTN_WIDTH
COL_CG = COL_BG + CONV_WIDTH
COL_XIN = COL_CG + CONV_WIDTH
COL_GATE = COL_XIN + CONV_WIDTH
IN_COLS = COL_GATE + 2 * D_MODEL

TM_PROJ = 1024
T_ATT = 2048
QB = 128
KB = QB + 2 * HALF
ATT_UNROLL = 8
TM_MIX = 512
MIX_TILES = 2
TM_MOE = 512
XIN_SLOTS = 3
TM_CMB = 256

VMEM_LIMIT = 56 * 1024 * 1024


def _alibi_slopes():
    return np.array([2.0 ** (-ALIBI_SPAN * (i + 1) / N_HEADS_A) for i in range(N_HEADS_A)],
                    dtype=np.float32).reshape(N_GROUPS_A, HEADS_PER_GROUP)


def _rms(x, g):
    return x * lax.rsqrt(jnp.mean(x * x, axis=-1, keepdims=True) + RMS_EPS) * g


def _proj_kernel(x_ref, g_ref, w_ref, b_ref, cw_ref,
                 q0_ref, kv0_ref, q1_ref, kv1_ref, q2_ref, kv2_ref, yb_ref, edge_ref, gate_ref, scr):
    h = _rms(x_ref[...], g_ref[...]).astype(BF16)

    def proj(c0, width):
        return jnp.dot(h, w_ref[:, c0:c0 + width], preferred_element_type=F32)

    qscale = HEAD_DIM ** -0.5
    q0_ref[...] = (proj(0, GROUP_WIDTH) * qscale).astype(BF16)
    kv0_ref[:, 0:GROUP_WIDTH] = proj(COL_K, GROUP_WIDTH).astype(BF16)
    kv0_ref[:, GROUP_WIDTH:] = proj(COL_V, GROUP_WIDTH).astype(BF16)

    for g, q_ref, kv_ref in ((1, q1_ref, kv1_ref), (2, q2_ref, kv2_ref)):
        d = DILATED_PATTERNS[g][1]
        n = TM_PROJ // d
        parts = (proj(g * GROUP_WIDTH, GROUP_WIDTH) * qscale,
                 proj(COL_K + g * GROUP_WIDTH, GROUP_WIDTH),
                 proj(COL_V + g * GROUP_WIDTH, GROUP_WIDTH))
        for i, part in enumerate(parts):
            for c in range(2):
                scr[2 * i + c] = part[:, c * LANES:(c + 1) * LANES]
        for r in range(d):
            rows = pl.ds(r, n, stride=d)
            q_ref[r] = jnp.concatenate([scr[c, rows, :] for c in range(2)], axis=1).astype(BF16)
            kv_ref[r] = jnp.concatenate([scr[c, rows, :] for c in range(2, 6)], axis=1).astype(BF16)

    b_gate = proj(COL_BG, CONV_WIDTH)
    u = proj(COL_CG, CONV_WIDTH) * proj(COL_XIN, CONV_WIDTH)
    row = lax.broadcasted_iota(jnp.int32, (TM_PROJ, CONV_WIDTH), 0)
    u_prev = jnp.where(row == 0, 0.0, pltpu.roll(u, 1, axis=0))
    u_next = jnp.where(row == TM_PROJ - 1, 0.0, pltpu.roll(u, TM_PROJ - 1, axis=0))
    conv = cw_ref[0:1, :] * u_prev + cw_ref[1:2, :] * u + cw_ref[2:3, :] * u_next
    yb_ref[...] = (b_gate * conv).astype(BF16)
    edge_ref[...] = jnp.concatenate(
        [b_gate[0:1, :] * cw_ref[0:1, :], b_gate[TM_PROJ - 1:TM_PROJ, :] * cw_ref[2:3, :],
         u[0:1, :], u[TM_PROJ - 1:TM_PROJ, :], jnp.zeros((SUBLANES - 4, CONV_WIDTH), F32)], axis=0)
    for c in range(4):
        w = 2 * D_MODEL // 4
        z = proj(COL_GATE + c * w, w) + b_ref[:, c * w:(c + 1) * w]
        gate_ref[:, c * w:(c + 1) * w] = jax.nn.sigmoid(z).astype(BF16)


def _proj_call(x2, g, w_in, b_gate, conv_w, batch, seq):
    n = x2.shape[0]
    assert seq % TM_PROJ == 0
    steps_per_batch = seq // TM_PROJ
    d1, d2 = DILATED_PATTERNS[1][1], DILATED_PATTERNS[2][1]
    row = lambda i: (i, 0)
    res = lambda i: (i // steps_per_batch, 0, i % steps_per_batch, 0)
    const = lambda i: (0, 0)
    out_shape = [
        jax.ShapeDtypeStruct((n, GROUP_WIDTH), BF16),
        jax.ShapeDtypeStruct((n, 2 * GROUP_WIDTH), BF16),
        jax.ShapeDtypeStruct((batch, d1, seq // d1, GROUP_WIDTH), BF16),
        jax.ShapeDtypeStruct((batch, d1, seq // d1, 2 * GROUP_WIDTH), BF16),
        jax.ShapeDtypeStruct((batch, d2, seq // d2, GROUP_WIDTH), BF16),
        jax.ShapeDtypeStruct((batch, d2, seq // d2, 2 * GROUP_WIDTH), BF16),
        jax.ShapeDtypeStruct((n, CONV_WIDTH), BF16),
        jax.ShapeDtypeStruct((n // TM_PROJ, SUBLANES, CONV_WIDTH), F32),
        jax.ShapeDtypeStruct((n, 2 * D_MODEL), BF16),
    ]
    out_specs = [
        pl.BlockSpec((TM_PROJ, GROUP_WIDTH), row),
        pl.BlockSpec((TM_PROJ, 2 * GROUP_WIDTH), row),
        pl.BlockSpec((None, d1, TM_PROJ // d1, GROUP_WIDTH), res),
        pl.BlockSpec((None, d1, TM_PROJ // d1, 2 * GROUP_WIDTH), res),
        pl.BlockSpec((None, d2, TM_PROJ // d2, GROUP_WIDTH), res),
        pl.BlockSpec((None, d2, TM_PROJ // d2, 2 * GROUP_WIDTH), res),
        pl.BlockSpec((TM_PROJ, CONV_WIDTH), row),
        pl.BlockSpec((None, SUBLANES, CONV_WIDTH), lambda i: (i, 0, 0)),
        pl.BlockSpec((TM_PROJ, 2 * D_MODEL), row),
    ]
    return pl.pallas_call(
        _proj_kernel,
        grid=(n // TM_PROJ,),
        in_specs=[
            pl.BlockSpec((TM_PROJ, D_MODEL), row),
            pl.BlockSpec((1, D_MODEL), const),
            pl.BlockSpec((D_MODEL, IN_COLS), const, pipeline_mode=pl.Buffered(1)),
            pl.BlockSpec((1, 2 * D_MODEL), const),
            pl.BlockSpec((3, CONV_WIDTH), const),
        ],
        out_specs=out_specs,
        out_shape=out_shape,
        scratch_shapes=[pltpu.VMEM((6, TM_PROJ, LANES), F32)],
        compiler_params=pltpu.CompilerParams(dimension_semantics=("arbitrary",),
                                             vmem_limit_bytes=VMEM_LIMIT),
        name="proj",
    )(x2, g, w_in, b_gate, conv_w)


def _attn_sub_block(q_sub, kw, vw, bias_ref, g, lo, hi):
    assert KB == GROUP_WIDTH
    lane = lax.broadcasted_iota(jnp.int32, (QB, KB), 1)
    edge_ok = (lane >= lo) & (lane < hi)
    heads = [(lane >= h * HEAD_DIM) & (lane < (h + 1) * HEAD_DIM) for h in range(HEADS_PER_GROUP)]
    zero = jnp.zeros((), BF16)
    q_stack = jnp.concatenate([jnp.where(hm, q_sub, zero) for hm in heads], axis=0)
    s_all = lax.dot_general(q_stack, kw, (((1,), (1,)), ((), ())), preferred_element_type=F32)
    probs = []
    m_b = l_b = None
    for h, hm in enumerate(heads):
        s = s_all[h * QB:(h + 1) * QB] + bias_ref[g * HEADS_PER_GROUP + h]
        s = jnp.where(edge_ok, s, MASK_VALUE)
        m = jnp.max(s, axis=1, keepdims=True)
        p = jnp.exp(s - m)
        l = jnp.sum(p, axis=1, keepdims=True)
        probs.append(p.astype(BF16))
        m_b = jnp.broadcast_to(m, (QB, GROUP_WIDTH)) if m_b is None else jnp.where(hm, m, m_b)
        l_b = jnp.broadcast_to(l, (QB, GROUP_WIDTH)) if l_b is None else jnp.where(hm, l, l_b)
    o_all = jnp.dot(jnp.concatenate(probs, axis=0), vw, preferred_element_type=F32)
    acc = o_all[0:QB]
    for h in range(1, HEADS_PER_GROUP):
        acc = jnp.where(heads[h], o_all[h * QB:(h + 1) * QB], acc)
    return acc, m_b, l_b


def _attn_kernel(q0_ref, kv0_ref, kv0p_ref, kv0n_ref,
                 q1_ref, kv1_ref, kv1p_ref, kv1n_ref,
                 q2_ref, kv2_ref, kv2p_ref, kv2n_ref,
                 y_ref,
                 cat0, cat1, cat2, bias_ref, m_st, l_st, a_st, m_tmp, l_tmp, a_tmp, *, seq):
    j = pl.program_id(1)

    qi = lax.broadcasted_iota(jnp.int32, (QB, KB), 0)
    kc = lax.broadcasted_iota(jnp.int32, (QB, KB), 1)
    adelta = jnp.abs(kc - HALF - qi)
    band = adelta <= HALF
    slopes = _alibi_slopes()
    for g in range(N_GROUPS_A):
        dist = (adelta * DILATED_PATTERNS[g][1]).astype(F32)
        for h in range(HEADS_PER_GROUP):
            bias_ref[g * HEADS_PER_GROUP + h] = jnp.where(band, -(float(slopes[g, h]) * dist), MASK_VALUE)

    for cat, own, prv, nxt in ((cat0, kv0_ref, kv0p_ref, kv0n_ref),
                               (cat1, kv1_ref, kv1p_ref, kv1n_ref),
                               (cat2, kv2_ref, kv2p_ref, kv2n_ref)):
        n_own = own.shape[-2]
        cat[:, 0:HALF, :] = prv[...].reshape(cat.shape[0], HALF, 2 * GROUP_WIDTH)
        cat[:, HALF:HALF + n_own, :] = own[...].reshape(cat.shape[0], n_own, 2 * GROUP_WIDTH)
        cat[:, HALF + n_own:, :] = nxt[...].reshape(cat.shape[0], HALF, 2 * GROUP_WIDTH)

    def window(cat, r, sb):
        rows = pl.ds(pl.multiple_of(sb * QB, QB), KB)
        return cat[r, rows, 0:GROUP_WIDTH], cat[r, rows, GROUP_WIDTH:]

    def edges(g, n_res, sb):
        length = seq // DILATED_PATTERNS[g][1]
        i0 = j * n_res + sb * QB
        return jnp.maximum(0, HALF - i0), jnp.minimum(KB, length + HALF - i0)

    def body0(sb, carry):
        rows = pl.ds(pl.multiple_of(sb * QB, QB), QB)
        kw, vw = window(cat0, 0, sb)
        lo, hi = edges(0, T_ATT, sb)
        acc, m_b, l_b = _attn_sub_block(q0_ref[rows, :], kw, vw, bias_ref, 0, lo, hi)
        for c in range(2):
            cols = slice(c * LANES, (c + 1) * LANES)
            m_st[c, rows, :] = m_b[:, cols]
            l_st[c, rows, :] = l_b[:, cols]
            a_st[c, rows, :] = acc[:, cols]
        return carry

    lax.fori_loop(0, T_ATT // QB, body0, 0, unroll=ATT_UNROLL)

    for g, q_ref, cat in ((1, q1_ref, cat1), (2, q2_ref, cat2)):
        d = DILATED_PATTERNS[g][1]
        n_res = T_ATT // d
        sb_per_res = n_res // QB

        def body(idx, carry, g=g, q_ref=q_ref, cat=cat, n_res=n_res, sb_per_res=sb_per_res):
            r = idx // sb_per_res
            sb = idx % sb_per_res
            kw, vw = window(cat, r, sb)
            lo, hi = edges(g, n_res, sb)
            q_sub = q_ref[r, pl.ds(pl.multiple_of(sb * QB, QB), QB), :]
            acc, m_b, l_b = _attn_sub_block(q_sub, kw, vw, bias_ref, g, lo, hi)
            rows = pl.ds(pl.multiple_of(idx * QB, QB), QB)
            m_tmp[rows, :] = m_b
            l_tmp[rows, :] = l_b
            a_tmp[rows, :] = acc
            return carry

        lax.fori_loop(0, T_ATT // QB, body, 0, unroll=ATT_UNROLL)

        for r in range(d):
            for ch in range(sb_per_res):
                src = slice(r * n_res + ch * QB, r * n_res + (ch + 1) * QB)
                tok = pl.ds(ch * QB * d + r, QB, stride=d)
                for c in range(2):
                    cols = slice(c * LANES, (c + 1) * LANES)
                    m_new_part = m_tmp[src, cols]
                    m_old = m_st[c, tok, :]
                    m_new = jnp.maximum(m_old, m_new_part)
                    e_old = jnp.exp(m_old - m_new)
                    e_new = jnp.exp(m_new_part - m_new)
                    l_new = e_old * l_st[c, tok, :] + e_new * l_tmp[src, cols]
                    a_new = e_old * a_st[c, tok, :] + e_new * a_tmp[src, cols]
                    if g == N_GROUPS_A - 1:
                        a_st[c, tok, :] = a_new / l_new
                    else:
                        m_st[c, tok, :] = m_new
                        l_st[c, tok, :] = l_new
                        a_st[c, tok, :] = a_new

    for c in range(2):
        y_ref[:, c * LANES:(c + 1) * LANES] = a_st[c].astype(BF16)


def _attn_call(q0, kv0, q1, kv1, q2, kv2, batch, seq):
    n = q0.shape[0]
    tiles = seq // T_ATT
    specs = []
    scratch = []
    blocks_per_tile = T_ATT // HALF
    n_half_blocks = n // HALF
    specs += [
        pl.BlockSpec((T_ATT, GROUP_WIDTH), lambda b, j: (b * tiles + j, 0)),
        pl.BlockSpec((T_ATT, 2 * GROUP_WIDTH), lambda b, j: (b * tiles + j, 0)),
        pl.BlockSpec((HALF, 2 * GROUP_WIDTH),
                     lambda b, j: (jnp.maximum((b * tiles + j) * blocks_per_tile - 1, 0), 0)),
        pl.BlockSpec((HALF, 2 * GROUP_WIDTH),
                     lambda b, j: (jnp.minimum((b * tiles + j + 1) * blocks_per_tile, n_half_blocks - 1), 0)),
    ]
    scratch.append(pltpu.VMEM((1, T_ATT + 2 * HALF, 2 * GROUP_WIDTH), BF16))
    for g in (1, 2):
        d = DILATED_PATTERNS[g][1]
        n_res = T_ATT // d
        per_tile = n_res // HALF
        last = seq // d // HALF - 1
        specs += [
            pl.BlockSpec((None, d, n_res, GROUP_WIDTH), lambda b, j: (b, 0, j, 0)),
            pl.BlockSpec((None, d, n_res, 2 * GROUP_WIDTH), lambda b, j: (b, 0, j, 0)),
            pl.BlockSpec((None, d, HALF, 2 * GROUP_WIDTH),
                         lambda b, j, per_tile=per_tile: (b, 0, jnp.maximum(j * per_tile - 1, 0), 0)),
            pl.BlockSpec((None, d, HALF, 2 * GROUP_WIDTH),
                         lambda b, j, per_tile=per_tile, last=last: (b, 0, jnp.minimum((j + 1) * per_tile, last), 0)),
        ]
        scratch.append(pltpu.VMEM((d, n_res + 2 * HALF, 2 * GROUP_WIDTH), BF16))
    scratch.append(pltpu.VMEM((N_HEADS_A, QB, KB), F32))
    scratch += [pltpu.VMEM((2, T_ATT, LANES), F32) for _ in range(3)]
    scratch += [pltpu.VMEM((T_ATT, GROUP_WIDTH), F32) for _ in range(3)]
    return pl.pallas_call(
        functools.partial(_attn_kernel, seq=seq),
        grid=(batch, tiles),
        in_specs=specs,
        out_specs=pl.BlockSpec((T_ATT, GROUP_WIDTH), lambda b, j: (b * tiles + j, 0)),
        out_shape=jax.ShapeDtypeStruct((n, GROUP_WIDTH), BF16),
        scratch_shapes=scratch,
        compiler_params=pltpu.CompilerParams(dimension_semantics=("arbitrary", "arbitrary"),
                                             vmem_limit_bytes=VMEM_LIMIT),
        name="attn",
    )(q0, kv0, kv0, kv0, q1, kv1, kv1, kv1, q2, kv2, kv2, kv2)


def _split_dot(a, w_cat, w_hi):
    a_hi = a.astype(BF16)
    a_lo = (a - a_hi.astype(F32)).astype(BF16)
    both = jnp.dot(a_hi, w_cat, preferred_element_type=F32)
    return both[:, 0:LANES] + both[:, LANES:] + jnp.dot(a_lo, w_hi, preferred_element_type=F32)


ST_BASE, ST_TILE, ST_FREE = 0, 1, 2
AL_NEW = 0
RT_E, RT_W, RT_POS = 0, 2, 4


def _mix_kernel(x_ref, ya_ref, yb_ref, edge_ref, edgep_ref, edgen_ref, gate_ref,
                wa_ref, wc_ref, wo_ref, g2_ref, wr_cat_ref, wr_hi_ref, br_ref,
                x1_ref, route_ref, routet_ref, alloc_ref, xs_hbm,
                st_ref, hb0, hb1, posv, poss, psem, dsem, *, seq, trash_row):
    i = pl.program_id(0)
    last_step = pl.num_programs(0) - 1
    hbs = (hb0, hb1)

    def rows_done(h):
        for k in range(2):
            pltpu.make_async_copy(hbs[h], xs_hbm.at[pl.ds(0, TM_MIX * SUBLANES)], dsem.at[h, k]).wait()

    def send_row(h, j, first_sublane):
        src = hbs[h].at[pl.ds(first_sublane, SUBLANES)]
        for k in range(2):
            dst = xs_hbm.at[pl.ds(pl.multiple_of(poss[h, RT_POS + k, j], SUBLANES), SUBLANES)]
            pltpu.make_async_copy(src, dst, dsem.at[h, k]).start(priority=1)

    def send_rows(h):
        def one(j, carry):
            send_row(h, j, pl.multiple_of(j * SUBLANES, SUBLANES))
            return carry
        lax.fori_loop(0, TM_MIX, one, 0, unroll=8)

    def positions_to_smem(h):
        return pltpu.make_async_copy(posv, poss.at[h], psem.at[h])

    @pl.when(i == 0)
    def _():
        st_ref[...] = jnp.zeros_like(st_ref)
        hb1[...] = jnp.zeros_like(hb1)
        spare = (trash_row + lax.broadcasted_iota(jnp.int32, posv.shape, 1)
                 + jnp.where(lax.broadcasted_iota(jnp.int32, posv.shape, 0) == RT_POS + 1, TM_MIX, 0))
        posv[...] = spare * SUBLANES
        positions_to_smem(1).start()

    def tile(h):
        rows = pl.ds(h * TM_MIX, TM_MIX)
        t0 = (i * MIX_TILES + h) * TM_MIX
        positions_to_smem(1 - h).wait()

        def send_other(part):
            for j in range(part * TM_MIX // 4, (part + 1) * TM_MIX // 4):
                send_row(1 - h, j, j * SUBLANES)

        yb_in = yb_ref[rows, :]
        patch = 16
        if h == 0:
            add = jnp.where(t0 % seq == 0, 0.0, edge_ref[0:1, :] * edgep_ref[3:4, :])
            top = yb_in[0:patch, :].astype(F32)
            top = jnp.where(lax.broadcasted_iota(jnp.int32, top.shape, 0) == 0, top + add, top)
            yb_in = jnp.concatenate([top.astype(BF16), yb_in[patch:, :]], axis=0)
        if h == MIX_TILES - 1:
            add = jnp.where((t0 + TM_MIX) % seq == 0, 0.0, edge_ref[1:2, :] * edgen_ref[2:3, :])
            bot = yb_in[TM_MIX - patch:, :].astype(F32)
            bot = jnp.where(lax.broadcasted_iota(jnp.int32, bot.shape, 0) == patch - 1, bot + add, bot)
            yb_in = jnp.concatenate([yb_in[:TM_MIX - patch, :], bot.astype(BF16)], axis=0)
        send_other(0)

        y_a = jnp.dot(ya_ref[rows, :], wa_ref[...], preferred_element_type=F32)
        send_other(1)
        y_b = jnp.dot(yb_in, wc_ref[...], preferred_element_type=F32)
        merged = gate_ref[rows, 0:D_MODEL] * y_a.astype(BF16) + gate_ref[rows, D_MODEL:] * y_b.astype(BF16)
        x1 = x_ref[rows, :] + jnp.dot(merged, wo_ref[...], preferred_element_type=F32)
        x1_ref[rows, :] = x1

        h2 = _rms(x1, g2_ref[...])
        if h == 0:
            pl.when(i > 0)(lambda: rows_done(0))
        else:
            rows_done(h)
        for c in range(ROW_CHUNKS):
            hbs[h][pl.ds(c, TM_MIX, stride=ROW_CHUNKS), :] = h2[:, c * LANES:(c + 1) * LANES]

        logits = _split_dot(h2, wr_cat_ref[...], wr_hi_ref[...]) + br_ref[...]
        send_other(2)
        send_other(3)
        lane = lax.broadcasted_iota(jnp.int32, (TM_MIX, LANES), 1)
        lane_f = lane.astype(F32)
        neg = -jnp.inf
        big = float(LANES)
        is_group = lane < N_EXPERT_GROUPS
        cm = jnp.where(is_group, logits, neg)
        cmax = jnp.max(cm, axis=1, keepdims=True)
        g_idx = jnp.min(jnp.where(cm == cmax, lane_f, big), axis=1, keepdims=True)
        p_group = 1.0 / jnp.sum(jnp.where(is_group, jnp.exp(logits - cmax), 0.0), axis=1, keepdims=True)
        f_lo = N_EXPERT_GROUPS + EXPERTS_PER_GROUP * g_idx
        in_group = (lane_f >= f_lo) & (lane_f < f_lo + EXPERTS_PER_GROUP)
        fm = jnp.where(in_group, logits, neg)
        f1 = jnp.max(fm, axis=1, keepdims=True)
        i1 = jnp.min(jnp.where(fm == f1, lane_f, big), axis=1, keepdims=True)
        fm2 = jnp.where(lane_f == i1, neg, fm)
        f2 = jnp.max(fm2, axis=1, keepdims=True)
        i2 = jnp.min(jnp.where(fm2 == f2, lane_f, big), axis=1, keepdims=True)
        e21 = jnp.exp(f2 - f1)
        w_1 = p_group / (1.0 + e21)
        w_2 = p_group * e21 / (1.0 + e21)
        e_1 = i1 - N_EXPERT_GROUPS
        e_2 = i2 - N_EXPERT_GROUPS

        onehot = jnp.where((lane_f == e_1) | (lane_f == e_2), 1.0, 0.0)
        r_i = lax.broadcasted_iota(jnp.int32, (TM_MIX, TM_MIX), 0)
        c_i = lax.broadcasted_iota(jnp.int32, (TM_MIX, TM_MIX), 1)
        tri = jnp.where(c_i < r_i, 1.0, 0.0).astype(BF16)
        base = st_ref[ST_BASE:ST_BASE + 1, :]
        before = jnp.dot(tri, onehot.astype(BF16), preferred_element_type=F32) + base

        tile_rows = float(TM_MOE)
        cur_tile = st_ref[ST_TILE:ST_TILE + 1, :]
        next_free = st_ref[ST_FREE:ST_FREE + 1, :]
        count = jnp.sum(onehot, axis=0, keepdims=True)
        slot0 = jnp.floor(base * (1.0 / tile_rows))
        partial = (base - slot0 * tile_rows) > 0.0
        slot_last = jnp.floor((base + count - 1.0) * (1.0 / tile_rows))
        n_new = jnp.where(count > 0.0, slot_last - slot0 + 1.0 - jnp.where(partial, 1.0, 0.0), 0.0)
        e_r = lax.broadcasted_iota(jnp.int32, (LANES, LANES), 0)
        e_c = lax.broadcasted_iota(jnp.int32, (LANES, LANES), 1)
        earlier = jnp.where(e_r < e_c, 1.0, 0.0).astype(BF16)
        new_before = jnp.dot(jnp.broadcast_to(n_new, (SUBLANES, LANES)).astype(BF16), earlier,
                             preferred_element_type=F32)[0:1, :]
        fresh = next_free + new_before - jnp.where(partial, 1.0, 0.0) - slot0
        partial_slot = jnp.where(partial, slot0, -1.0)

        def tile_of(slot_idx, fresh_v, cur_v, partial_v):
            return jnp.where(slot_idx == partial_v, cur_v, fresh_v + slot_idx)

        def pick(row_vec, e):
            return jnp.sum(jnp.where(lane_f == e, row_vec, 0.0), axis=1, keepdims=True)

        positions = []
        for e in (e_1, e_2):
            rank = pick(before, e)
            s = jnp.floor(rank * (1.0 / tile_rows))
            tid = tile_of(s, pick(fresh, e), pick(cur_tile, e), pick(partial_slot, e))
            positions.append(tid * tile_rows + (rank - s * tile_rows))

        st_ref[ST_BASE:ST_BASE + 1, :] = base + count
        st_ref[ST_TILE:ST_TILE + 1, :] = jnp.where(count > 0.0, tile_of(slot_last, fresh, cur_tile, partial_slot),
                                                    cur_tile)
        st_ref[ST_FREE:ST_FREE + 1, :] = next_free + jnp.sum(n_new, axis=1, keepdims=True)
        alloc_ref[h] = jnp.broadcast_to(n_new, (SUBLANES, LANES))

        route = jnp.zeros((TM_MIX, LANES), F32)
        for k, val in enumerate((e_1, e_2, w_1, w_2, positions[0], positions[1])):
            route = jnp.where(lane == k, val, route)
        route_ref[rows, :] = route
        route_t = route.T[0:SUBLANES, :]
        routet_ref[:, h * TM_MIX:(h + 1) * TM_MIX] = route_t
        posv[...] = route_t.astype(jnp.int32) * SUBLANES
        positions_to_smem(h).start()

    tile(0)
    tile(1)

    @pl.when(i == last_step)
    def _():
        positions_to_smem(1).wait()
        rows_done(0)
        send_rows(1)
        rows_done(1)

        posv[:, 0:LANES] = st_ref[...].astype(jnp.int32)
        state = positions_to_smem(0)
        state.start()
        state.wait()
        hb0[...] = jnp.zeros_like(hb0)

        def zero_jobs():
            for e in range(N_EXPERTS):
                fill = poss[0, ST_BASE, e] & (TM_MOE - 1)
                first = poss[0, ST_TILE, e] * TM_MOE
                at = fill
                size = 1
                while size < TM_MOE:
                    take = (fill > 0) & ((at & size) != 0)
                    yield take, first + at, size
                    at = at + jnp.where(take, size, 0)
                    size *= 2
            for t in range(N_EXPERTS):
                tile = poss[0, ST_FREE, 0] + t
                yield tile < trash_row // TM_MOE, tile * TM_MOE, TM_MOE

        def zero_copy(first, rows):
            return pltpu.make_async_copy(
                hb0.at[pl.ds(0, rows * SUBLANES)],
                xs_hbm.at[pl.ds(pl.multiple_of(first * SUBLANES, SUBLANES), rows * SUBLANES)], psem.at[0])

        for take, first, rows in zero_jobs():
            pl.when(take)(lambda first=first, rows=rows: zero_copy(first, rows).start())
        for take, first, rows in zero_jobs():
            pl.when(take)(lambda first=first, rows=rows: zero_copy(first, rows).wait())


def _mix_call(x2, y_attn, yb_in, edges, gates, wa, wc, wo, g2, wr_cat, wr_hi, br, seq, n_tiles):
    n = x2.shape[0]
    step_rows = MIX_TILES * TM_MIX
    assert step_rows == TM_PROJ
    n_steps = n // step_rows
    row = lambda i: (i, 0)
    const = lambda i: (0, 0)
    edge = lambda shift: pl.BlockSpec((None, SUBLANES, CONV_WIDTH),
                                      lambda i: (jnp.clip(i + shift, 0, n_steps - 1), 0, 0))
    xs_rows = n_tiles * TM_MOE + 2 * TM_MIX
    return pl.pallas_call(
        functools.partial(_mix_kernel, seq=seq, trash_row=n_tiles * TM_MOE),
        grid=(n_steps,),
        in_specs=[
            pl.BlockSpec((step_rows, D_MODEL), row),
            pl.BlockSpec((step_rows, GROUP_WIDTH), row),
            pl.BlockSpec((step_rows, CONV_WIDTH), row),
            edge(0), edge(-1), edge(1),
            pl.BlockSpec((step_rows, 2 * D_MODEL), row),
            pl.BlockSpec((GROUP_WIDTH, D_MODEL), const),
            pl.BlockSpec((CONV_WIDTH, D_MODEL), const),
            pl.BlockSpec((D_MODEL, D_MODEL), const),
            pl.BlockSpec((1, D_MODEL), const),
            pl.BlockSpec((D_MODEL, 2 * LANES), const),
            pl.BlockSpec((D_MODEL, LANES), const),
            pl.BlockSpec((1, LANES), const),
        ],
        out_specs=[
            pl.BlockSpec((step_rows, D_MODEL), row),
            pl.BlockSpec((step_rows, LANES), row),
            pl.BlockSpec((SUBLANES, step_rows), lambda i: (0, i)),
            pl.BlockSpec((MIX_TILES, SUBLANES, LANES), lambda i: (i, 0, 0)),
            pl.BlockSpec(memory_space=pl.ANY),
        ],
        out_shape=[
            jax.ShapeDtypeStruct((n, D_MODEL), F32),
            jax.ShapeDtypeStruct((n, LANES), F32),
            jax.ShapeDtypeStruct((SUBLANES, n), F32),
            jax.ShapeDtypeStruct((n // TM_MIX, SUBLANES, LANES), F32),
            jax.ShapeDtypeStruct((xs_rows * ROW_CHUNKS, LANES), F32),
        ],
        scratch_shapes=[pltpu.VMEM((SUBLANES, LANES), F32),
                        pltpu.VMEM((TM_MIX * ROW_CHUNKS, LANES), F32),
                        pltpu.VMEM((TM_MIX * ROW_CHUNKS, LANES), F32),
                        pltpu.VMEM((SUBLANES, TM_MIX), jnp.int32),
                        pltpu.SMEM((MIX_TILES, SUBLANES, TM_MIX), jnp.int32),
                        pltpu.SemaphoreType.DMA((MIX_TILES,)),
                        pltpu.SemaphoreType.DMA((MIX_TILES, 2))],
        compiler_params=pltpu.CompilerParams(dimension_semantics=("arbitrary",),
                                             vmem_limit_bytes=VMEM_LIMIT),
        name="mix",
    )(x2, y_attn, yb_in, edges, edges, edges, gates, wa, wc, wo, g2, wr_cat, wr_hi, br)


def _row_gather(idx_ref, n_rows, src_hbm, dst, sem):
    def issue(pair, carry):
        for k in range(2):
            j = 2 * pair + k
            t = idx_ref[0, 0, j]
            pltpu.make_async_copy(src_hbm.at[pl.ds(pl.multiple_of(t * SUBLANES, SUBLANES), SUBLANES)],
                                  dst.at[pl.ds(pl.multiple_of(j * SUBLANES, SUBLANES), SUBLANES)],
                                  sem).start(priority=k)
        return carry
    lax.fori_loop(0, n_rows // 2, issue, 0, unroll=4)


def _row_gather_wait(n_rows, src_hbm, dst, sem):
    pltpu.make_async_copy(src_hbm.at[pl.ds(0, n_rows * SUBLANES)], dst, sem).wait()


def _rows_from_tiles(buf, first_row, n_rows):
    return jnp.concatenate(
        [buf[pl.ds(first_row * ROW_CHUNKS + c, n_rows, stride=ROW_CHUNKS), :] for c in range(ROW_CHUNKS)],
        axis=1)


def _expert_kernel(order_ref, te_ref, next_ref, nused_ref, xs_hbm, w1_hbm, w3_hbm, w2_hbm, y_ref,
                   xin, w1s, w3s, w2s, w13b, w2b, isem, wsem):
    i = pl.program_id(0)
    n_used = nused_ref[0]
    slot = i % XIN_SLOTS
    used = i < n_used

    def fetch(step):
        rows = pl.ds(order_ref[step] * TM_MOE, TM_MOE)
        s = step % XIN_SLOTS
        return [pltpu.make_async_copy(xs_hbm.at[rows, c, :], xin.at[s, c], isem.at[s]) for c in range(ROW_CHUNKS)]

    def weights(e):
        return [pltpu.make_async_copy(w_hbm.at[e], stage, wsem.at[k])
                for k, (w_hbm, stage) in enumerate(((w1_hbm, w1s), (w3_hbm, w3s), (w2_hbm, w2s)))]

    def start(copies):
        for cp in copies:
            cp.start()

    def wait(copies):
        for cp in copies:
            cp.wait()

    @pl.when(i == 0)
    def _():
        start(weights(te_ref[0]))
        start(fetch(0))
        pl.when(n_used > 1)(lambda: start(fetch(1)))

    pl.when(i + 2 < n_used)(lambda: start(fetch(i + 2)))

    @pl.when(jnp.logical_not(used))
    def _():
        y_ref[...] = jnp.zeros_like(y_ref)

    @pl.when(used & ((i == 0) | (te_ref[i] != te_ref[jnp.maximum(i - 1, 0)])))
    def _():
        wait(weights(te_ref[i]))
        w13b[:, 0:EXPERT_FF] = w1s[...].astype(BF16)
        w13b[:, EXPERT_FF:] = w3s[...].astype(BF16)
        w2b[...] = w2s[...].astype(BF16)
        pl.when(next_ref[i] != te_ref[i])(lambda: start(weights(next_ref[i])))

    @pl.when(used)
    def _():
        wait(fetch(i))
        x = jnp.concatenate([xin[slot, c] for c in range(ROW_CHUNKS)], axis=1)
        ab = jnp.dot(x.astype(BF16), w13b[...], preferred_element_type=F32)
        a = ab[:, 0:EXPERT_FF]
        hid = (a * jax.nn.sigmoid(a) * ab[:, EXPERT_FF:]).astype(BF16)
        y = jnp.dot(hid, w2b[...], preferred_element_type=F32)
        for c in range(ROW_CHUNKS):
            y_ref[pl.ds(c, TM_MOE, stride=ROW_CHUNKS), :] = y[:, c * LANES:(c + 1) * LANES]


def _expert_call(order, tile_expert, next_expert, n_used, xs_rows, w1, w3, w2):
    n_tiles = order.shape[0]
    any_space = pl.BlockSpec(memory_space=pl.ANY)
    grid_spec = pltpu.PrefetchScalarGridSpec(
        num_scalar_prefetch=4,
        grid=(n_tiles,),
        in_specs=[any_space, any_space, any_space, any_space],
        out_specs=pl.BlockSpec((TM_MOE * ROW_CHUNKS, LANES), lambda i, od, te, nx, nu: (od[i], 0)),
        scratch_shapes=[pltpu.VMEM((XIN_SLOTS, ROW_CHUNKS, TM_MOE, LANES), F32),
                        pltpu.VMEM((D_MODEL, EXPERT_FF), F32),
                        pltpu.VMEM((D_MODEL, EXPERT_FF), F32),
                        pltpu.VMEM((EXPERT_FF, D_MODEL), F32),
                        pltpu.VMEM((D_MODEL, 2 * EXPERT_FF), BF16),
                        pltpu.VMEM((EXPERT_FF, D_MODEL), BF16),
                        pltpu.SemaphoreType.DMA((XIN_SLOTS,)),
                        pltpu.SemaphoreType.DMA((3,))],
    )
    return pl.pallas_call(
        _expert_kernel,
        grid_spec=grid_spec,
        out_shape=jax.ShapeDtypeStruct((n_tiles * TM_MOE * ROW_CHUNKS, LANES), F32),
        compiler_params=pltpu.CompilerParams(dimension_semantics=("arbitrary",),
                                             vmem_limit_bytes=VMEM_LIMIT),
        name="experts",
    )(order, tile_expert, next_expert, n_used, xs_rows, w1, w3, w2)


def _combine_kernel(pos_ref, posn_ref, y_hbm, x1_ref, route_ref, g_ref, o_ref, buf0, buf1, sem):
    i = pl.program_id(0)
    n_steps = pl.num_programs(0)
    bufs = (buf0, buf1)

    @pl.when(i == 0)
    def _():
        _row_gather(pos_ref, 2 * TM_CMB, y_hbm, buf0, sem.at[0])

    for slot in range(2):
        @pl.when((i % 2 == slot) & (i + 1 < n_steps))
        def _(slot=slot):
            _row_gather(posn_ref, 2 * TM_CMB, y_hbm, bufs[1 - slot], sem.at[1 - slot])

    for slot in range(2):
        @pl.when(i % 2 == slot)
        def _(slot=slot):
            _row_gather_wait(2 * TM_CMB, y_hbm, bufs[slot], sem.at[slot])
            y_1 = _rows_from_tiles(bufs[slot], 0, TM_CMB)
            y_2 = _rows_from_tiles(bufs[slot], TM_CMB, TM_CMB)
            x = x1_ref[...] + route_ref[:, 2:3] * y_1 + route_ref[:, 3:4] * y_2
            o_ref[...] = _rms(x, g_ref[...])


def _combine_call(pos, y_flat, x1, route, g):
    n = x1.shape[0]
    n_steps = n // TM_CMB
    row = lambda i: (i, 0)
    return pl.pallas_call(
        _combine_kernel,
        grid=(n_steps,),
        in_specs=[
            pl.BlockSpec((1, 1, 2 * TM_CMB), lambda i: (i, 0, 0), memory_space=pltpu.SMEM),
            pl.BlockSpec((1, 1, 2 * TM_CMB), lambda i: (jnp.minimum(i + 1, n_steps - 1), 0, 0),
                         memory_space=pltpu.SMEM),
            pl.BlockSpec(memory_space=pl.ANY),
            pl.BlockSpec((TM_CMB, D_MODEL), row),
            pl.BlockSpec((TM_CMB, LANES), row),
            pl.BlockSpec((1, D_MODEL), lambda i: (0, 0)),
        ],
        out_specs=pl.BlockSpec((TM_CMB, D_MODEL), row),
        out_shape=jax.ShapeDtypeStruct((n, D_MODEL), F32),
        scratch_shapes=[pltpu.VMEM((2 * TM_CMB * ROW_CHUNKS, LANES), F32),
                        pltpu.VMEM((2 * TM_CMB * ROW_CHUNKS, LANES), F32),
                        pltpu.SemaphoreType.DMA((2,))],
        compiler_params=pltpu.CompilerParams(dimension_semantics=("arbitrary",),
                                             vmem_limit_bytes=VMEM_LIMIT),
        name="combine",
    )(pos, pos, y_flat, x1, route, g)


def _layer(x2, batch, seq, norm_mix_g, w_in, b_gate, conv_w, w_attn_out, w_conv_out, w_out, norm_ffn_g,
           w_route_group, b_route_group, w_route_expert, b_route_expert, w1, w3, w2, final_g):
    n = x2.shape[0]
    q0, kv0, q1, kv1, q2, kv2, yb_in, edges, gates = _proj_call(
        x2, norm_mix_g[None, :], w_in.astype(BF16), b_gate[None, :], conv_w, batch, seq)
    y_attn = _attn_call(q0, kv0, q1, kv1, q2, kv2, batch, seq)

    n_route = N_EXPERT_GROUPS + N_EXPERTS
    w_route = jnp.pad(jnp.concatenate([w_route_group, w_route_expert], axis=1), ((0, 0), (0, LANES - n_route)))
    b_route = jnp.pad(jnp.concatenate([b_route_group, b_route_expert]), (0, LANES - n_route))[None, :]
    wr_hi = w_route.astype(BF16)
    wr_lo = (w_route - wr_hi.astype(F32)).astype(BF16)
    n_tiles = (2 * n) // TM_MOE + N_EXPERTS
    x1, route, route_t, alloc, xs_flat = _mix_call(
        x2, y_attn, yb_in, edges, gates, w_attn_out.astype(BF16), w_conv_out.astype(BF16), w_out.astype(BF16),
        norm_ffn_g[None, :], jnp.concatenate([wr_hi, wr_lo], axis=1), wr_hi, b_route, seq, n_tiles)

    i32 = jnp.int32
    taken = alloc[:, AL_NEW, :N_EXPERTS].astype(i32).reshape(-1)
    k = jnp.arange(taken.shape[0], dtype=i32)
    running = jnp.sum(jnp.where(k[:, None] >= k[None, :], taken[None, :], 0), axis=1)
    n_used = running[-1:]
    tile = jnp.arange(n_tiles, dtype=i32)
    owner = jnp.sum((running[None, :] <= tile[:, None]).astype(i32), axis=1) % N_EXPERTS
    owner = jnp.where(tile < n_used[0], owner, N_EXPERTS)
    key = owner * n_tiles + tile
    place = jnp.sum((key[None, :] < key[:, None]).astype(i32), axis=1)
    at = place[None, :] == tile[:, None]
    order = jnp.sum(jnp.where(at, tile[None, :], 0), axis=1)
    step_owner = jnp.sum(jnp.where(at, owner[None, :], 0), axis=1)
    step_expert = jnp.minimum(step_owner, N_EXPERTS - 1)
    later = (step_owner[None, :] > step_owner[:, None]) & (step_owner[None, :] < N_EXPERTS)
    next_expert = jnp.min(jnp.where(later, step_owner[None, :], N_EXPERTS), axis=1)
    next_expert = jnp.where(next_expert < N_EXPERTS, next_expert, step_expert)

    y_flat = _expert_call(order, step_expert, next_expert, n_used, xs_flat.reshape(-1, ROW_CHUNKS, LANES),
                          w1, w3, w2)
    pos = route_t[RT_POS:RT_POS + 2].astype(i32)
    pos_tiles = pos.reshape(2, n // TM_CMB, TM_CMB).transpose(1, 0, 2).reshape(n // TM_CMB, 1, 2 * TM_CMB)
    return _combine_call(pos_tiles, y_flat, x1, route, final_g[None, :])


def kernel(x, norm_mix_g, w_in, b_gate, conv_w, w_attn_out, w_conv_out, w_out, norm_ffn_g,
           w_route_group, b_route_group, w_route_expert, b_route_expert, w1, w3, w2, norm_final_g):
    batch, seq, d = x.shape
    depth = w_in.shape[0]
    assert d == D_MODEL and depth == 1 and seq % T_ATT == 0
    out = _layer(x.reshape(batch * seq, d), batch, seq, norm_mix_g[0], w_in[0], b_gate[0], conv_w[0],
                 w_attn_out[0], w_conv_out[0], w_out[0], norm_ffn_g[0], w_route_group[0], b_route_group[0],
                 w_route_expert[0], b_route_expert[0], w1[0], w3[0], w2[0], norm_final_g)
    return out.reshape(batch, seq, d)
```

```python
import functools

import numpy as np
import jax
import jax.numpy as jnp
from jax import lax
from jax.experimental import pallas as pl
from jax.experimental.pallas import tpu as pltpu

F32 = jnp.float32
BF16 = jnp.bfloat16

D_MODEL = 1024
HEAD_DIM = 64
HEADS_PER_GROUP = 4
DILATED_PATTERNS = ((128, 1), (512, 4), (2048, 16))
N_GROUPS_A = 3
N_HEADS_A = N_GROUPS_A * HEADS_PER_GROUP
ATTN_WIDTH = N_HEADS_A * HEAD_DIM
GROUP_WIDTH = HEADS_PER_GROUP * HEAD_DIM
ALIBI_SPAN = 8.0
MASK_VALUE = -1e30
CONV_WIDTH = 768
N_EXPERT_GROUPS = 4
EXPERTS_PER_GROUP = 8
N_EXPERTS = 32
EXPERT_FF = 512
RMS_EPS = 1e-6

HALF = 64
LANES = 128
SUBLANES = 8
ROW_CHUNKS = D_MODEL // LANES

COL_K = ATTN_WIDTH
COL_V = 2 * ATTN_WIDTH
COL_BG = 3 * ATTN_WIDTH
COL_CG = COL_BG + CONV_WIDTH
COL_XIN = COL_CG + CONV_WIDTH
COL_GATE = COL_XIN + CONV_WIDTH
IN_COLS = COL_GATE + 2 * D_MODEL

TM_PROJ = 1024
T_ATT = 2048
QB = 128
KB = QB + 2 * HALF
ATT_UNROLL = 8
TM_MIX = 512
MIX_TILES = 2
TM_MOE = 512
XIN_SLOTS = 3
TM_CMB = 256

VMEM_LIMIT = 56 * 1024 * 1024


def _alibi_slopes():
    return np.array([2.0 ** (-ALIBI_SPAN * (i + 1) / N_HEADS_A) for i in range(N_HEADS_A)],
                    dtype=np.float32).reshape(N_GROUPS_A, HEADS_PER_GROUP)


def _rms(x, g):
    return x * lax.rsqrt(jnp.mean(x * x, axis=-1, keepdims=True) + RMS_EPS) * g


def _proj_kernel(x_ref, g_ref, w_ref, b_ref, cw_ref,
                 q0_ref, kv0_ref, q1_ref, kv1_ref, q2_ref, kv2_ref, yb_ref, edge_ref, gate_ref, scr):
    h = _rms(x_ref[...], g_ref[...]).astype(BF16)

    def proj(c0, width):
        return jnp.dot(h, w_ref[:, c0:c0 + width], preferred_element_type=F32)

    qscale = HEAD_DIM ** -0.5
    q0_ref[...] = (proj(0, GROUP_WIDTH) * qscale).astype(BF16)
    kv0_ref[:, 0:GROUP_WIDTH] = proj(COL_K, GROUP_WIDTH).astype(BF16)
    kv0_ref[:, GROUP_WIDTH:] = proj(COL_V, GROUP_WIDTH).astype(BF16)

    for g, q_ref, kv_ref in ((1, q1_ref, kv1_ref), (2, q2_ref, kv2_ref)):
        d = DILATED_PATTERNS[g][1]
        n = TM_PROJ // d
        parts = (proj(g * GROUP_WIDTH, GROUP_WIDTH) * qscale,
                 proj(COL_K + g * GROUP_WIDTH, GROUP_WIDTH),
                 proj(COL_V + g * GROUP_WIDTH, GROUP_WIDTH))
        for i, part in enumerate(parts):
            for c in range(2):
                scr[2 * i + c] = part[:, c * LANES:(c + 1) * LANES]
        for r in range(d):
            rows = pl.ds(r, n, stride=d)
            q_ref[r] = jnp.concatenate([scr[c, rows, :] for c in range(2)], axis=1).astype(BF16)
            kv_ref[r] = jnp.concatenate([scr[c, rows, :] for c in range(2, 6)], axis=1).astype(BF16)

    width = GROUP_WIDTH
    row = lax.broadcasted_iota(jnp.int32, (TM_PROJ, width), 0)
    for c0 in range(0, CONV_WIDTH, width):
        cols = slice(c0, c0 + width)
        b_gate = proj(COL_BG + c0, width)
        u = proj(COL_CG + c0, width) * proj(COL_XIN + c0, width)
        u_prev = jnp.where(row == 0, 0.0, pltpu.roll(u, 1, axis=0))
        u_next = jnp.where(row == TM_PROJ - 1, 0.0, pltpu.roll(u, TM_PROJ - 1, axis=0))
        conv = cw_ref[0:1, cols] * u_prev + cw_ref[1:2, cols] * u + cw_ref[2:3, cols] * u_next
        yb_ref[:, cols] = (b_gate * conv).astype(BF16)
        edge_ref[:, cols] = jnp.concatenate(
            [b_gate[0:1, :] * cw_ref[0:1, cols], b_gate[TM_PROJ - 1:TM_PROJ, :] * cw_ref[2:3, cols],
             u[0:1, :], u[TM_PROJ - 1:TM_PROJ, :], jnp.zeros((SUBLANES - 4, width), F32)], axis=0)
    for c in range(4):
        w = 2 * D_MODEL // 4
        z = proj(COL_GATE + c * w, w) + b_ref[:, c * w:(c + 1) * w]
        gate_ref[:, c * w:(c + 1) * w] = jax.nn.sigmoid(z).astype(BF16)


def _proj_call(x2, g, w_in, b_gate, conv_w, batch, seq):
    n = x2.shape[0]
    assert seq % TM_PROJ == 0
    steps_per_batch = seq // TM_PROJ
    d1, d2 = DILATED_PATTERNS[1][1], DILATED_PATTERNS[2][1]
    row = lambda i: (i, 0)
    res = lambda i: (i // steps_per_batch, 0, i % steps_per_batch, 0)
    const = lambda i: (0, 0)
    out_shape = [
        jax.ShapeDtypeStruct((n, GROUP_WIDTH), BF16),
        jax.ShapeDtypeStruct((n, 2 * GROUP_WIDTH), BF16),
        jax.ShapeDtypeStruct((batch, d1, seq // d1, GROUP_WIDTH), BF16),
        jax.ShapeDtypeStruct((batch, d1, seq // d1, 2 * GROUP_WIDTH), BF16),
        jax.ShapeDtypeStruct((batch, d2, seq // d2, GROUP_WIDTH), BF16),
        jax.ShapeDtypeStruct((batch, d2, seq // d2, 2 * GROUP_WIDTH), BF16),
        jax.ShapeDtypeStruct((n, CONV_WIDTH), BF16),
        jax.ShapeDtypeStruct((n // TM_PROJ, SUBLANES, CONV_WIDTH), F32),
        jax.ShapeDtypeStruct((n, 2 * D_MODEL), BF16),
    ]
    out_specs = [
        pl.BlockSpec((TM_PROJ, GROUP_WIDTH), row),
        pl.BlockSpec((TM_PROJ, 2 * GROUP_WIDTH), row),
        pl.BlockSpec((None, d1, TM_PROJ // d1, GROUP_WIDTH), res),
        pl.BlockSpec((None, d1, TM_PROJ // d1, 2 * GROUP_WIDTH), res),
        pl.BlockSpec((None, d2, TM_PROJ // d2, GROUP_WIDTH), res),
        pl.BlockSpec((None, d2, TM_PROJ // d2, 2 * GROUP_WIDTH), res),
        pl.BlockSpec((TM_PROJ, CONV_WIDTH), row),
        pl.BlockSpec((None, SUBLANES, CONV_WIDTH), lambda i: (i, 0, 0)),
        pl.BlockSpec((TM_PROJ, 2 * D_MODEL), row),
    ]
    return pl.pallas_call(
        _proj_kernel,
        grid=(n // TM_PROJ,),
        in_specs=[
            pl.BlockSpec((TM_PROJ, D_MODEL), row),
            pl.BlockSpec((1, D_MODEL), const),
            pl.BlockSpec((D_MODEL, IN_COLS), const, pipeline_mode=pl.Buffered(1)),
            pl.BlockSpec((1, 2 * D_MODEL), const),
            pl.BlockSpec((3, CONV_WIDTH), const),
        ],
        out_specs=out_specs,
        out_shape=out_shape,
        scratch_shapes=[pltpu.VMEM((6, TM_PROJ, LANES), F32)],
        compiler_params=pltpu.CompilerParams(dimension_semantics=("arbitrary",),
                                             vmem_limit_bytes=VMEM_LIMIT),
        name="proj",
    )(x2, g, w_in, b_gate, conv_w)


def _attn_sub_block(q_sub, kw, vw, bias_ref, g, lo, hi):
    assert KB == GROUP_WIDTH
    lane = lax.broadcasted_iota(jnp.int32, (QB, KB), 1)
    edge_ok = (lane >= lo) & (lane < hi)
    heads = [(lane >= h * HEAD_DIM) & (lane < (h + 1) * HEAD_DIM) for h in range(HEADS_PER_GROUP)]
    zero = jnp.zeros((), BF16)
    q_stack = jnp.concatenate([jnp.where(hm, q_sub, zero) for hm in heads], axis=0)
    s_all = lax.dot_general(q_stack, kw, (((1,), (1,)), ((), ())), preferred_element_type=F32)
    probs = []
    m_b = l_b = None
    for h, hm in enumerate(heads):
        s = s_all[h * QB:(h + 1) * QB] + bias_ref[g * HEADS_PER_GROUP + h]
        s = jnp.where(edge_ok, s, MASK_VALUE)
        m = jnp.max(s, axis=1, keepdims=True)
        p = jnp.exp(s - m)
        l = jnp.sum(p, axis=1, keepdims=True)
        probs.append(p.astype(BF16))
        m_b = jnp.broadcast_to(m, (QB, GROUP_WIDTH)) if m_b is None else jnp.where(hm, m, m_b)
        l_b = jnp.broadcast_to(l, (QB, GROUP_WIDTH)) if l_b is None else jnp.where(hm, l, l_b)
    o_all = jnp.dot(jnp.concatenate(probs, axis=0), vw, preferred_element_type=F32)
    acc = o_all[0:QB]
    for h in range(1, HEADS_PER_GROUP):
        acc = jnp.where(heads[h], o_all[h * QB:(h + 1) * QB], acc)
    return acc, m_b, l_b


def _attn_kernel(q0_ref, kv0_ref, kv0p_ref, kv0n_ref,
                 q1_ref, kv1_ref, kv1p_ref, kv1n_ref,
                 q2_ref, kv2_ref, kv2p_ref, kv2n_ref,
                 y_ref,
                 cat0, cat1, cat2, bias_ref, m_st, l_st, a_st, m_tmp, l_tmp, a_tmp, *, seq):
    j = pl.program_id(1)

    qi = lax.broadcasted_iota(jnp.int32, (QB, KB), 0)
    kc = lax.broadcasted_iota(jnp.int32, (QB, KB), 1)
    adelta = jnp.abs(kc - HALF - qi)
    band = adelta <= HALF
    slopes = _alibi_slopes()
    for g in range(N_GROUPS_A):
        dist = (adelta * DILATED_PATTERNS[g][1]).astype(F32)
        for h in range(HEADS_PER_GROUP):
            bias_ref[g * HEADS_PER_GROUP + h] = jnp.where(band, -(float(slopes[g, h]) * dist), MASK_VALUE)

    for cat, own, prv, nxt in ((cat0, kv0_ref, kv0p_ref, kv0n_ref),
                               (cat1, kv1_ref, kv1p_ref, kv1n_ref),
                               (cat2, kv2_ref, kv2p_ref, kv2n_ref)):
        n_own = own.shape[-2]
        cat[:, 0:HALF, :] = prv[...].reshape(cat.shape[0], HALF, 2 * GROUP_WIDTH)
        cat[:, HALF:HALF + n_own, :] = own[...].reshape(cat.shape[0], n_own, 2 * GROUP_WIDTH)
        cat[:, HALF + n_own:, :] = nxt[...].reshape(cat.shape[0], HALF, 2 * GROUP_WIDTH)

    def window(cat, r, sb):
        rows = pl.ds(pl.multiple_of(sb * QB, QB), KB)
        return cat[r, rows, 0:GROUP_WIDTH], cat[r, rows, GROUP_WIDTH:]

    def edges(g, n_res, sb):
        length = seq // DILATED_PATTERNS[g][1]
        i0 = j * n_res + sb * QB
        return jnp.maximum(0, HALF - i0), jnp.minimum(KB, length + HALF - i0)

    def body0(sb, carry):
        rows = pl.ds(pl.multiple_of(sb * QB, QB), QB)
        kw, vw = window(cat0, 0, sb)
        lo, hi = edges(0, T_ATT, sb)
        acc, m_b, l_b = _attn_sub_block(q0_ref[rows, :], kw, vw, bias_ref, 0, lo, hi)
        for c in range(2):
            cols = slice(c * LANES, (c + 1) * LANES)
            m_st[c, rows, :] = m_b[:, cols]
            l_st[c, rows, :] = l_b[:, cols]
            a_st[c, rows, :] = acc[:, cols]
        return carry

    lax.fori_loop(0, T_ATT // QB, body0, 0, unroll=ATT_UNROLL)

    for g, q_ref, cat in ((1, q1_ref, cat1), (2, q2_ref, cat2)):
        d = DILATED_PATTERNS[g][1]
        n_res = T_ATT // d
        sb_per_res = n_res // QB

        def body(idx, carry, g=g, q_ref=q_ref, cat=cat, n_res=n_res, sb_per_res=sb_per_res):
            r = idx // sb_per_res
            sb = idx % sb_per_res
            kw, vw = window(cat, r, sb)
            lo, hi = edges(g, n_res, sb)
            q_sub = q_ref[r, pl.ds(pl.multiple_of(sb * QB, QB), QB), :]
            acc, m_b, l_b = _attn_sub_block(q_sub, kw, vw, bias_ref, g, lo, hi)
            rows = pl.ds(pl.multiple_of(idx * QB, QB), QB)
            m_tmp[rows, :] = m_b
            l_tmp[rows, :] = l_b
            a_tmp[rows, :] = acc
            return carry

        lax.fori_loop(0, T_ATT // QB, body, 0, unroll=ATT_UNROLL)

        for r in range(d):
            for ch in range(sb_per_res):
                src = slice(r * n_res + ch * QB, r * n_res + (ch + 1) * QB)
                tok = pl.ds(ch * QB * d + r, QB, stride=d)
                for c in range(2):
                    cols = slice(c * LANES, (c + 1) * LANES)
                    m_new_part = m_tmp[src, cols]
                    m_old = m_st[c, tok, :]
                    m_new = jnp.maximum(m_old, m_new_part)
                    e_old = jnp.exp(m_old - m_new)
                    e_new = jnp.exp(m_new_part - m_new)
                    l_new = e_old * l_st[c, tok, :] + e_new * l_tmp[src, cols]
                    a_new = e_old * a_st[c, tok, :] + e_new * a_tmp[src, cols]
                    if g == N_GROUPS_A - 1:
                        a_st[c, tok, :] = a_new / l_new
                    else:
                        m_st[c, tok, :] = m_new
                        l_st[c, tok, :] = l_new
                        a_st[c, tok, :] = a_new

    for c in range(2):
        y_ref[:, c * LANES:(c + 1) * LANES] = a_st[c].astype(BF16)


def _attn_call(q0, kv0, q1, kv1, q2, kv2, batch, seq):
    n = q0.shape[0]
    tiles = seq // T_ATT
    specs = []
    scratch = []
    blocks_per_tile = T_ATT // HALF
    n_half_blocks = n // HALF
    specs += [
        pl.BlockSpec((T_ATT, GROUP_WIDTH), lambda b, j: (b * tiles + j, 0)),
        pl.BlockSpec((T_ATT, 2 * GROUP_WIDTH), lambda b, j: (b * tiles + j, 0)),
        pl.BlockSpec((HALF, 2 * GROUP_WIDTH),
                     lambda b, j: (jnp.maximum((b * tiles + j) * blocks_per_tile - 1, 0), 0)),
        pl.BlockSpec((HALF, 2 * GROUP_WIDTH),
                     lambda b, j: (jnp.minimum((b * tiles + j + 1) * blocks_per_tile, n_half_blocks - 1), 0)),
    ]
    scratch.append(pltpu.VMEM((1, T_ATT + 2 * HALF, 2 * GROUP_WIDTH), BF16))
    for g in (1, 2):
        d = DILATED_PATTERNS[g][1]
        n_res = T_ATT // d
        per_tile = n_res // HALF
        last = seq // d // HALF - 1
        specs += [
            pl.BlockSpec((None, d, n_res, GROUP_WIDTH), lambda b, j: (b, 0, j, 0)),
            pl.BlockSpec((None, d, n_res, 2 * GROUP_WIDTH), lambda b, j: (b, 0, j, 0)),
            pl.BlockSpec((None, d, HALF, 2 * GROUP_WIDTH),
                         lambda b, j, per_tile=per_tile: (b, 0, jnp.maximum(j * per_tile - 1, 0), 0)),
            pl.BlockSpec((None, d, HALF, 2 * GROUP_WIDTH),
                         lambda b, j, per_tile=per_tile, last=last: (b, 0, jnp.minimum((j + 1) * per_tile, last), 0)),
        ]
        scratch.append(pltpu.VMEM((d, n_res + 2 * HALF, 2 * GROUP_WIDTH), BF16))
    scratch.append(pltpu.VMEM((N_HEADS_A, QB, KB), F32))
    scratch += [pltpu.VMEM((2, T_ATT, LANES), F32) for _ in range(3)]
    scratch += [pltpu.VMEM((T_ATT, GROUP_WIDTH), F32) for _ in range(3)]
    return pl.pallas_call(
        functools.partial(_attn_kernel, seq=seq),
        grid=(batch, tiles),
        in_specs=specs,
        out_specs=pl.BlockSpec((T_ATT, GROUP_WIDTH), lambda b, j: (b * tiles + j, 0)),
        out_shape=jax.ShapeDtypeStruct((n, GROUP_WIDTH), BF16),
        scratch_shapes=scratch,
        compiler_params=pltpu.CompilerParams(dimension_semantics=("arbitrary", "arbitrary"),
                                             vmem_limit_bytes=VMEM_LIMIT),
        name="attn",
    )(q0, kv0, kv0, kv0, q1, kv1, kv1, kv1, q2, kv2, kv2, kv2)


def _split_dot(a, w_cat, w_hi):
    a_hi = a.astype(BF16)
    a_lo = (a - a_hi.astype(F32)).astype(BF16)
    both = jnp.dot(a_hi, w_cat, preferred_element_type=F32)
    return both[:, 0:LANES] + both[:, LANES:] + jnp.dot(a_lo, w_hi, preferred_element_type=F32)


ST_BASE, ST_TILE, ST_FREE = 0, 1, 2
AL_NEW = 0
RT_E, RT_W, RT_POS = 0, 2, 4


def _mix_kernel(x_ref, ya_ref, yb_ref, edge_ref, edgep_ref, edgen_ref, gate_ref,
                wa_ref, wc_ref, wo_ref, g2_ref, wr_cat_ref, wr_hi_ref, br_ref,
                x1_ref, route_ref, routet_ref, alloc_ref, xs_hbm,
                st_ref, hb0, hb1, posv, poss, psem, dsem, *, seq, trash_row):
    i = pl.program_id(0)
    last_step = pl.num_programs(0) - 1
    hbs = (hb0, hb1)

    def rows_done(h):
        for k in range(2):
            pltpu.make_async_copy(hbs[h], xs_hbm.at[pl.ds(0, TM_MIX * SUBLANES)], dsem.at[h, k]).wait()

    def send_row(h, j, first_sublane):
        src = hbs[h].at[pl.ds(first_sublane, SUBLANES)]
        for k in range(2):
            dst = xs_hbm.at[pl.ds(pl.multiple_of(poss[h, RT_POS + k, j], SUBLANES), SUBLANES)]
            pltpu.make_async_copy(src, dst, dsem.at[h, k]).start(priority=1)

    def send_rows(h):
        def one(j, carry):
            send_row(h, j, pl.multiple_of(j * SUBLANES, SUBLANES))
            return carry
        lax.fori_loop(0, TM_MIX, one, 0, unroll=8)

    def positions_to_smem(h):
        return pltpu.make_async_copy(posv, poss.at[h], psem.at[h])

    @pl.when(i == 0)
    def _():
        st_ref[...] = jnp.zeros_like(st_ref)
        hb1[...] = jnp.zeros_like(hb1)
        spare = (trash_row + lax.broadcasted_iota(jnp.int32, posv.shape, 1)
                 + jnp.where(lax.broadcasted_iota(jnp.int32, posv.shape, 0) == RT_POS + 1, TM_MIX, 0))
        posv[...] = spare * SUBLANES
        positions_to_smem(1).start()

    def tile(h):
        rows = pl.ds(h * TM_MIX, TM_MIX)
        t0 = (i * MIX_TILES + h) * TM_MIX
        positions_to_smem(1 - h).wait()

        def send_other(part):
            for j in range(part * TM_MIX // 4, (part + 1) * TM_MIX // 4):
                send_row(1 - h, j, j * SUBLANES)

        yb_in = yb_ref[rows, :]
        patch = 16
        if h == 0:
            add = jnp.where(t0 % seq == 0, 0.0, edge_ref[0:1, :] * edgep_ref[3:4, :])
            top = yb_in[0:patch, :].astype(F32)
            top = jnp.where(lax.broadcasted_iota(jnp.int32, top.shape, 0) == 0, top + add, top)
            yb_in = jnp.concatenate([top.astype(BF16), yb_in[patch:, :]], axis=0)
        if h == MIX_TILES - 1:
            add = jnp.where((t0 + TM_MIX) % seq == 0, 0.0, edge_ref[1:2, :] * edgen_ref[2:3, :])
            bot = yb_in[TM_MIX - patch:, :].astype(F32)
            bot = jnp.where(lax.broadcasted_iota(jnp.int32, bot.shape, 0) == patch - 1, bot + add, bot)
            yb_in = jnp.concatenate([yb_in[:TM_MIX - patch, :], bot.astype(BF16)], axis=0)
        send_other(0)

        y_a = jnp.dot(ya_ref[rows, :], wa_ref[...], preferred_element_type=F32)
        send_other(1)
        y_b = jnp.dot(yb_in, wc_ref[...], preferred_element_type=F32)
        merged = gate_ref[rows, 0:D_MODEL] * y_a.astype(BF16) + gate_ref[rows, D_MODEL:] * y_b.astype(BF16)
        x1 = x_ref[rows, :] + jnp.dot(merged, wo_ref[...], preferred_element_type=F32)
        x1_ref[rows, :] = x1

        h2 = _rms(x1, g2_ref[...])
        if h == 0:
            pl.when(i > 0)(lambda: rows_done(0))
        else:
            rows_done(h)
        for c in range(ROW_CHUNKS):
            hbs[h][pl.ds(c, TM_MIX, stride=ROW_CHUNKS), :] = h2[:, c * LANES:(c + 1) * LANES]

        logits = _split_dot(h2, wr_cat_ref[...], wr_hi_ref[...]) + br_ref[...]
        send_other(2)
        send_other(3)
        lane = lax.broadcasted_iota(jnp.int32, (TM_MIX, LANES), 1)
        lane_f = lane.astype(F32)
        neg = -jnp.inf
        big = float(LANES)
        is_group = lane < N_EXPERT_GROUPS
        cm = jnp.where(is_group, logits, neg)
        cmax = jnp.max(cm, axis=1, keepdims=True)
        g_idx = jnp.min(jnp.where(cm == cmax, lane_f, big), axis=1, keepdims=True)
        p_group = 1.0 / jnp.sum(jnp.where(is_group, jnp.exp(logits - cmax), 0.0), axis=1, keepdims=True)
        f_lo = N_EXPERT_GROUPS + EXPERTS_PER_GROUP * g_idx
        in_group = (lane_f >= f_lo) & (lane_f < f_lo + EXPERTS_PER_GROUP)
        fm = jnp.where(in_group, logits, neg)
        f1 = jnp.max(fm, axis=1, keepdims=True)
        i1 = jnp.min(jnp.where(fm == f1, lane_f, big), axis=1, keepdims=True)
        fm2 = jnp.where(lane_f == i1, neg, fm)
        f2 = jnp.max(fm2, axis=1, keepdims=True)
        i2 = jnp.min(jnp.where(fm2 == f2, lane_f, big), axis=1, keepdims=True)
        e21 = jnp.exp(f2 - f1)
        w_1 = p_group / (1.0 + e21)
        w_2 = p_group * e21 / (1.0 + e21)
        e_1 = i1 - N_EXPERT_GROUPS
        e_2 = i2 - N_EXPERT_GROUPS

        onehot = jnp.where((lane_f == e_1) | (lane_f == e_2), 1.0, 0.0)
        r_i = lax.broadcasted_iota(jnp.int32, (TM_MIX, TM_MIX), 0)
        c_i = lax.broadcasted_iota(jnp.int32, (TM_MIX, TM_MIX), 1)
        tri = jnp.where(c_i < r_i, 1.0, 0.0).astype(BF16)
        base = st_ref[ST_BASE:ST_BASE + 1, :]
        before = jnp.dot(tri, onehot.astype(BF16), preferred_element_type=F32) + base

        tile_rows = float(TM_MOE)
        cur_tile = st_ref[ST_TILE:ST_TILE + 1, :]
        next_free = st_ref[ST_FREE:ST_FREE + 1, :]
        count = jnp.sum(onehot, axis=0, keepdims=True)
        slot0 = jnp.floor(base * (1.0 / tile_rows))
        partial = (base - slot0 * tile_rows) > 0.0
        slot_last = jnp.floor((base + count - 1.0) * (1.0 / tile_rows))
        n_new = jnp.where(count > 0.0, slot_last - slot0 + 1.0 - jnp.where(partial, 1.0, 0.0), 0.0)
        e_r = lax.broadcasted_iota(jnp.int32, (LANES, LANES), 0)
        e_c = lax.broadcasted_iota(jnp.int32, (LANES, LANES), 1)
        earlier = jnp.where(e_r < e_c, 1.0, 0.0).astype(BF16)
        new_before = jnp.dot(jnp.broadcast_to(n_new, (SUBLANES, LANES)).astype(BF16), earlier,
                             preferred_element_type=F32)[0:1, :]
        fresh = next_free + new_before - jnp.where(partial, 1.0, 0.0) - slot0
        partial_slot = jnp.where(partial, slot0, -1.0)

        def tile_of(slot_idx, fresh_v, cur_v, partial_v):
            return jnp.where(slot_idx == partial_v, cur_v, fresh_v + slot_idx)

        def pick(row_vec, e):
            return jnp.sum(jnp.where(lane_f == e, row_vec, 0.0), axis=1, keepdims=True)

        positions = []
        for e in (e_1, e_2):
            rank = pick(before, e)
            s = jnp.floor(rank * (1.0 / tile_rows))
            tid = tile_of(s, pick(fresh, e), pick(cur_tile, e), pick(partial_slot, e))
            positions.append(tid * tile_rows + (rank - s * tile_rows))

        st_ref[ST_BASE:ST_BASE + 1, :] = base + count
        st_ref[ST_TILE:ST_TILE + 1, :] = jnp.where(count > 0.0, tile_of(slot_last, fresh, cur_tile, partial_slot),
                                                    cur_tile)
        st_ref[ST_FREE:ST_FREE + 1, :] = next_free + jnp.sum(n_new, axis=1, keepdims=True)
        alloc_ref[h] = jnp.broadcast_to(n_new, (SUBLANES, LANES))

        route = jnp.zeros((TM_MIX, LANES), F32)
        for k, val in enumerate((e_1, e_2, w_1, w_2, positions[0], positions[1])):
            route = jnp.where(lane == k, val, route)
        route_ref[rows, :] = route
        route_t = route.T[0:SUBLANES, :]
        routet_ref[:, h * TM_MIX:(h + 1) * TM_MIX] = route_t
        posv[...] = route_t.astype(jnp.int32) * SUBLANES
        positions_to_smem(h).start()

    tile(0)
    tile(1)

    @pl.when(i == last_step)
    def _():
        positions_to_smem(1).wait()
        rows_done(0)
        send_rows(1)
        rows_done(1)

        posv[:, 0:LANES] = st_ref[...].astype(jnp.int32)
        state = positions_to_smem(0)
        state.start()
        state.wait()
        hb0[...] = jnp.zeros_like(hb0)

        def zero_jobs():
            for e in range(N_EXPERTS):
                fill = poss[0, ST_BASE, e] & (TM_MOE - 1)
                first = poss[0, ST_TILE, e] * TM_MOE
                at = fill
                size = 1
                while size < TM_MOE:
                    take = (fill > 0) & ((at & size) != 0)
                    yield take, first + at, size
                    at = at + jnp.where(take, size, 0)
                    size *= 2
            for t in range(N_EXPERTS):
                tile = poss[0, ST_FREE, 0] + t
                yield tile < trash_row // TM_MOE, tile * TM_MOE, TM_MOE

        def zero_copy(first, rows):
            return pltpu.make_async_copy(
                hb0.at[pl.ds(0, rows * SUBLANES)],
                xs_hbm.at[pl.ds(pl.multiple_of(first * SUBLANES, SUBLANES), rows * SUBLANES)], psem.at[0])

        for take, first, rows in zero_jobs():
            pl.when(take)(lambda first=first, rows=rows: zero_copy(first, rows).start())
        for take, first, rows in zero_jobs():
            pl.when(take)(lambda first=first, rows=rows: zero_copy(first, rows).wait())


def _mix_call(x2, y_attn, yb_in, edges, gates, wa, wc, wo, g2, wr_cat, wr_hi, br, seq, n_tiles):
    n = x2.shape[0]
    step_rows = MIX_TILES * TM_MIX
    assert step_rows == TM_PROJ
    n_steps = n // step_rows
    row = lambda i: (i, 0)
    const = lambda i: (0, 0)
    edge = lambda shift: pl.BlockSpec((None, SUBLANES, CONV_WIDTH),
                                      lambda i: (jnp.clip(i + shift, 0, n_steps - 1), 0, 0))
    xs_rows = n_tiles * TM_MOE + 2 * TM_MIX
    return pl.pallas_call(
        functools.partial(_mix_kernel, seq=seq, trash_row=n_tiles * TM_MOE),
        grid=(n_steps,),
        in_specs=[
            pl.BlockSpec((step_rows, D_MODEL), row),
            pl.BlockSpec((step_rows, GROUP_WIDTH), row),
            pl.BlockSpec((step_rows, CONV_WIDTH), row),
            edge(0), edge(-1), edge(1),
            pl.BlockSpec((step_rows, 2 * D_MODEL), row),
            pl.BlockSpec((GROUP_WIDTH, D_MODEL), const),
            pl.BlockSpec((CONV_WIDTH, D_MODEL), const),
            pl.BlockSpec((D_MODEL, D_MODEL), const),
            pl.BlockSpec((1, D_MODEL), const),
            pl.BlockSpec((D_MODEL, 2 * LANES), const),
            pl.BlockSpec((D_MODEL, LANES), const),
            pl.BlockSpec((1, LANES), const),
        ],
        out_specs=[
            pl.BlockSpec((step_rows, D_MODEL), row),
            pl.BlockSpec((step_rows, LANES), row),
            pl.BlockSpec((SUBLANES, step_rows), lambda i: (0, i)),
            pl.BlockSpec((MIX_TILES, SUBLANES, LANES), lambda i: (i, 0, 0)),
            pl.BlockSpec(memory_space=pl.ANY),
        ],
        out_shape=[
            jax.ShapeDtypeStruct((n, D_MODEL), F32),
            jax.ShapeDtypeStruct((n, LANES), F32),
            jax.ShapeDtypeStruct((SUBLANES, n), F32),
            jax.ShapeDtypeStruct((n // TM_MIX, SUBLANES, LANES), F32),
            jax.ShapeDtypeStruct((xs_rows * ROW_CHUNKS, LANES), F32),
        ],
        scratch_shapes=[pltpu.VMEM((SUBLANES, LANES), F32),
                        pltpu.VMEM((TM_MIX * ROW_CHUNKS, LANES), F32),
                        pltpu.VMEM((TM_MIX * ROW_CHUNKS, LANES), F32),
                        pltpu.VMEM((SUBLANES, TM_MIX), jnp.int32),
                        pltpu.SMEM((MIX_TILES, SUBLANES, TM_MIX), jnp.int32),
                        pltpu.SemaphoreType.DMA((MIX_TILES,)),
                        pltpu.SemaphoreType.DMA((MIX_TILES, 2))],
        compiler_params=pltpu.CompilerParams(dimension_semantics=("arbitrary",),
                                             vmem_limit_bytes=VMEM_LIMIT),
        name="mix",
    )(x2, y_attn, yb_in, edges, edges, edges, gates, wa, wc, wo, g2, wr_cat, wr_hi, br)


def _row_gather(idx_ref, n_rows, src_hbm, dst, sem):
    def issue(pair, carry):
        for k in range(2):
            j = 2 * pair + k
            t = idx_ref[0, 0, j]
            pltpu.make_async_copy(src_hbm.at[pl.ds(pl.multiple_of(t * SUBLANES, SUBLANES), SUBLANES)],
                                  dst.at[pl.ds(pl.multiple_of(j * SUBLANES, SUBLANES), SUBLANES)],
                                  sem).start(priority=k)
        return carry
    lax.fori_loop(0, n_rows // 2, issue, 0, unroll=4)


def _row_gather_wait(n_rows, src_hbm, dst, sem):
    pltpu.make_async_copy(src_hbm.at[pl.ds(0, n_rows * SUBLANES)], dst, sem).wait()


def _rows_from_tiles(buf, first_row, n_rows):
    return jnp.concatenate(
        [buf[pl.ds(first_row * ROW_CHUNKS + c, n_rows, stride=ROW_CHUNKS), :] for c in range(ROW_CHUNKS)],
        axis=1)


def _expert_kernel(order_ref, te_ref, next_ref, nused_ref, xs_hbm, w1_hbm, w3_hbm, w2_hbm, y_ref,
                   xin, w1s, w3s, w2s, w13b, w2b, isem, wsem):
    i = pl.program_id(0)
    n_used = nused_ref[0]
    slot = i % XIN_SLOTS
    used = i < n_used

    def fetch(step):
        rows = pl.ds(order_ref[step] * TM_MOE, TM_MOE)
        s = step % XIN_SLOTS
        return [pltpu.make_async_copy(xs_hbm.at[rows, c, :], xin.at[s, c], isem.at[s]) for c in range(ROW_CHUNKS)]

    def weights(e):
        return [pltpu.make_async_copy(w_hbm.at[e], stage, wsem.at[k])
                for k, (w_hbm, stage) in enumerate(((w1_hbm, w1s), (w3_hbm, w3s), (w2_hbm, w2s)))]

    def start(copies):
        for cp in copies:
            cp.start()

    def wait(copies):
        for cp in copies:
            cp.wait()

    @pl.when(i == 0)
    def _():
        start(weights(te_ref[0]))
        start(fetch(0))
        pl.when(n_used > 1)(lambda: start(fetch(1)))

    pl.when(i + 2 < n_used)(lambda: start(fetch(i + 2)))

    @pl.when(jnp.logical_not(used))
    def _():
        y_ref[...] = jnp.zeros_like(y_ref)

    @pl.when(used & ((i == 0) | (te_ref[i] != te_ref[jnp.maximum(i - 1, 0)])))
    def _():
        wait(weights(te_ref[i]))
        w13b[:, 0:EXPERT_FF] = w1s[...].astype(BF16)
        w13b[:, EXPERT_FF:] = w3s[...].astype(BF16)
        w2b[...] = w2s[...].astype(BF16)
        pl.when(next_ref[i] != te_ref[i])(lambda: start(weights(next_ref[i])))

    @pl.when(used)
    def _():
        wait(fetch(i))
        x = jnp.concatenate([xin[slot, c] for c in range(ROW_CHUNKS)], axis=1)
        ab = jnp.dot(x.astype(BF16), w13b[...], preferred_element_type=F32)
        a = ab[:, 0:EXPERT_FF]
        hid = (a * jax.nn.sigmoid(a) * ab[:, EXPERT_FF:]).astype(BF16)
        y = jnp.dot(hid, w2b[...], preferred_element_type=F32)
        for c in range(ROW_CHUNKS):
            y_ref[pl.ds(c, TM_MOE, stride=ROW_CHUNKS), :] = y[:, c * LANES:(c + 1) * LANES]


def _expert_call(order, tile_expert, next_expert, n_used, xs_rows, w1, w3, w2):
    n_tiles = order.shape[0]
    any_space = pl.BlockSpec(memory_space=pl.ANY)
    grid_spec = pltpu.PrefetchScalarGridSpec(
        num_scalar_prefetch=4,
        grid=(n_tiles,),
        in_specs=[any_space, any_space, any_space, any_space],
        out_specs=pl.BlockSpec((TM_MOE * ROW_CHUNKS, LANES), lambda i, od, te, nx, nu: (od[i], 0)),
        scratch_shapes=[pltpu.VMEM((XIN_SLOTS, ROW_CHUNKS, TM_MOE, LANES), F32),
                        pltpu.VMEM((D_MODEL, EXPERT_FF), F32),
                        pltpu.VMEM((D_MODEL, EXPERT_FF), F32),
                        pltpu.VMEM((EXPERT_FF, D_MODEL), F32),
                        pltpu.VMEM((D_MODEL, 2 * EXPERT_FF), BF16),
                        pltpu.VMEM((EXPERT_FF, D_MODEL), BF16),
                        pltpu.SemaphoreType.DMA((XIN_SLOTS,)),
                        pltpu.SemaphoreType.DMA((3,))],
    )
    return pl.pallas_call(
        _expert_kernel,
        grid_spec=grid_spec,
        out_shape=jax.ShapeDtypeStruct((n_tiles * TM_MOE * ROW_CHUNKS, LANES), F32),
        compiler_params=pltpu.CompilerParams(dimension_semantics=("arbitrary",),
                                             vmem_limit_bytes=VMEM_LIMIT),
        name="experts",
    )(order, tile_expert, next_expert, n_used, xs_rows, w1, w3, w2)


def _combine_kernel(pos_ref, posn_ref, y_hbm, x1_ref, route_ref, g_ref, o_ref, buf0, buf1, sem):
    i = pl.program_id(0)
    n_steps = pl.num_programs(0)
    bufs = (buf0, buf1)

    @pl.when(i == 0)
    def _():
        _row_gather(pos_ref, 2 * TM_CMB, y_hbm, buf0, sem.at[0])

    for slot in range(2):
        @pl.when((i % 2 == slot) & (i + 1 < n_steps))
        def _(slot=slot):
            _row_gather(posn_ref, 2 * TM_CMB, y_hbm, bufs[1 - slot], sem.at[1 - slot])

    for slot in range(2):
        @pl.when(i % 2 == slot)
        def _(slot=slot):
            _row_gather_wait(2 * TM_CMB, y_hbm, bufs[slot], sem.at[slot])
            y_1 = _rows_from_tiles(bufs[slot], 0, TM_CMB)
            y_2 = _rows_from_tiles(bufs[slot], TM_CMB, TM_CMB)
            x = x1_ref[...] + route_ref[:, 2:3] * y_1 + route_ref[:, 3:4] * y_2
            o_ref[...] = _rms(x, g_ref[...])


def _combine_call(pos, y_flat, x1, route, g):
    n = x1.shape[0]
    n_steps = n // TM_CMB
    row = lambda i: (i, 0)
    return pl.pallas_call(
        _combine_kernel,
        grid=(n_steps,),
        in_specs=[
            pl.BlockSpec((1, 1, 2 * TM_CMB), lambda i: (i, 0, 0), memory_space=pltpu.SMEM),
            pl.BlockSpec((1, 1, 2 * TM_CMB), lambda i: (jnp.minimum(i + 1, n_steps - 1), 0, 0),
                         memory_space=pltpu.SMEM),
            pl.BlockSpec(memory_space=pl.ANY),
            pl.BlockSpec((TM_CMB, D_MODEL), row),
            pl.BlockSpec((TM_CMB, LANES), row),
            pl.BlockSpec((1, D_MODEL), lambda i: (0, 0)),
        ],
        out_specs=pl.BlockSpec((TM_CMB, D_MODEL), row),
        out_shape=jax.ShapeDtypeStruct((n, D_MODEL), F32),
        scratch_shapes=[pltpu.VMEM((2 * TM_CMB * ROW_CHUNKS, LANES), F32),
                        pltpu.VMEM((2 * TM_CMB * ROW_CHUNKS, LANES), F32),
                        pltpu.SemaphoreType.DMA((2,))],
        compiler_params=pltpu.CompilerParams(dimension_semantics=("arbitrary",),
                                             vmem_limit_bytes=VMEM_LIMIT),
        name="combine",
    )(pos, pos, y_flat, x1, route, g)


def _layer(x2, batch, seq, norm_mix_g, w_in, b_gate, conv_w, w_attn_out, w_conv_out, w_out, norm_ffn_g,
           w_route_group, b_route_group, w_route_expert, b_route_expert, w1, w3, w2, final_g):
    n = x2.shape[0]
    q0, kv0, q1, kv1, q2, kv2, yb_in, edges, gates = _proj_call(
        x2, norm_mix_g[None, :], w_in.astype(BF16), b_gate[None, :], conv_w, batch, seq)
    y_attn = _attn_call(q0, kv0, q1, kv1, q2, kv2, batch, seq)

    n_route = N_EXPERT_GROUPS + N_EXPERTS
    w_route = jnp.pad(jnp.concatenate([w_route_group, w_route_expert], axis=1), ((0, 0), (0, LANES - n_route)))
    b_route = jnp.pad(jnp.concatenate([b_route_group, b_route_expert]), (0, LANES - n_route))[None, :]
    wr_hi = w_route.astype(BF16)
    wr_lo = (w_route - wr_hi.astype(F32)).astype(BF16)
    n_tiles = (2 * n) // TM_MOE + N_EXPERTS
    x1, route, route_t, alloc, xs_flat = _mix_call(
        x2, y_attn, yb_in, edges, gates, w_attn_out.astype(BF16), w_conv_out.astype(BF16), w_out.astype(BF16),
        norm_ffn_g[None, :], jnp.concatenate([wr_hi, wr_lo], axis=1), wr_hi, b_route, seq, n_tiles)

    i32 = jnp.int32
    taken = alloc[:, AL_NEW, :N_EXPERTS].astype(i32).reshape(-1)
    k = jnp.arange(taken.shape[0], dtype=i32)
    running = jnp.sum(jnp.where(k[:, None] >= k[None, :], taken[None, :], 0), axis=1)
    n_used = running[-1:]
    tile = jnp.arange(n_tiles, dtype=i32)
    owner = jnp.sum((running[None, :] <= tile[:, None]).astype(i32), axis=1) % N_EXPERTS
    owner = jnp.where(tile < n_used[0], owner, N_EXPERTS)
    key = owner * n_tiles + tile
    place = jnp.sum((key[None, :] < key[:, None]).astype(i32), axis=1)
    at = place[None, :] == tile[:, None]
    order = jnp.sum(jnp.where(at, tile[None, :], 0), axis=1)
    step_owner = jnp.sum(jnp.where(at, owner[None, :], 0), axis=1)
    step_expert = jnp.minimum(step_owner, N_EXPERTS - 1)
    later = (step_owner[None, :] > step_owner[:, None]) & (step_owner[None, :] < N_EXPERTS)
    next_expert = jnp.min(jnp.where(later, step_owner[None, :], N_EXPERTS), axis=1)
    next_expert = jnp.where(next_expert < N_EXPERTS, next_expert, step_expert)

    y_flat = _expert_call(order, step_expert, next_expert, n_used, xs_flat.reshape(-1, ROW_CHUNKS, LANES),
                          w1, w3, w2)
    pos = route_t[RT_POS:RT_POS + 2].astype(i32)
    pos_tiles = pos.reshape(2, n // TM_CMB, TM_CMB).transpose(1, 0, 2).reshape(n // TM_CMB, 1, 2 * TM_CMB)
    return _combine_call(pos_tiles, y_flat, x1, route, final_g[None, :])


def kernel(x, norm_mix_g, w_in, b_gate, conv_w, w_attn_out, w_conv_out, w_out, norm_ffn_g,
           w_route_group, b_route_group, w_route_expert, b_route_expert, w1, w3, w2, norm_final_g):
    batch, seq, d = x.shape
    depth = w_in.shape[0]
    assert d == D_MODEL and depth == 1 and seq % T_ATT == 0
    out = _layer(x.reshape(batch * seq, d), batch, seq, norm_mix_g[0], w_in[0], b_gate[0], conv_w[0],
                 w_attn_out[0], w_conv_out[0], w_out[0], norm_ffn_g[0], w_route_group[0], b_route_group[0],
                 w_route_expert[0], b_route_expert[0], w1[0], w3[0], w2[0], norm_final_g)
    return out.reshape(batch, seq, d)
```

```python
import functools

import numpy as np
import jax
import jax.numpy as jnp
from jax import lax
from jax.experimental import pallas as pl
from jax.experimental.pallas import tpu as pltpu

F32 = jnp.float32
BF16 = jnp.bfloat16

D_MODEL = 1024
HEAD_DIM = 64
HEADS_PER_GROUP = 4
DILATED_PATTERNS = ((128, 1), (512, 4), (2048, 16))
N_GROUPS_A = 3
N_HEADS_A = N_GROUPS_A * HEADS_PER_GROUP
ATTN_WIDTH = N_HEADS_A * HEAD_DIM
GROUP_WIDTH = HEADS_PER_GROUP * HEAD_DIM
ALIBI_SPAN = 8.0
MASK_VALUE = -1e30
CONV_WIDTH = 768
N_EXPERT_GROUPS = 4
EXPERTS_PER_GROUP = 8
N_EXPERTS = 32
EXPERT_FF = 512
RMS_EPS = 1e-6

HALF = 64
LANES = 128
SUBLANES = 8
ROW_CHUNKS = D_MODEL // LANES

COL_K = ATTN_WIDTH
COL_V = 2 * ATTN_WIDTH
COL_BG = 3 * ATTN_WIDTH
COL_CG = COL_BG + CONV_WIDTH
COL_XIN = COL_CG + CONV_WIDTH
COL_GATE = COL_XIN + CONV_WIDTH
IN_COLS = COL_GATE + 2 * D_MODEL

TM_PROJ = 1024
T_ATT = 2048
QB = 128
KB = QB + 2 * HALF
ATT_UNROLL = 16
TM_MIX = 512
MIX_TILES = 2
TM_MOE = 512
XIN_SLOTS = 3
TM_CMB = 256

VMEM_LIMIT = 56 * 1024 * 1024


def _alibi_slopes():
    return np.array([2.0 ** (-ALIBI_SPAN * (i + 1) / N_HEADS_A) for i in range(N_HEADS_A)],
                    dtype=np.float32).reshape(N_GROUPS_A, HEADS_PER_GROUP)


def _rms(x, g):
    return x * lax.rsqrt(jnp.mean(x * x, axis=-1, keepdims=True) + RMS_EPS) * g


def _proj_kernel(x_ref, g_ref, w_ref, b_ref, cw_ref,
                 q0_ref, kv0_ref, q1_ref, kv1_ref, q2_ref, kv2_ref, yb_ref, edge_ref, gate_ref, scr):
    h = _rms(x_ref[...], g_ref[...]).astype(BF16)

    def proj(c0, width):
        return jnp.dot(h, w_ref[:, c0:c0 + width], preferred_element_type=F32)

    qscale = HEAD_DIM ** -0.5
    q0_ref[...] = (proj(0, GROUP_WIDTH) * qscale).astype(BF16)
    kv0_ref[:, 0:GROUP_WIDTH] = proj(COL_K, GROUP_WIDTH).astype(BF16)
    kv0_ref[:, GROUP_WIDTH:] = proj(COL_V, GROUP_WIDTH).astype(BF16)

    for g, q_ref, kv_ref in ((1, q1_ref, kv1_ref), (2, q2_ref, kv2_ref)):
        d = DILATED_PATTERNS[g][1]
        n = TM_PROJ // d
        parts = (proj(g * GROUP_WIDTH, GROUP_WIDTH) * qscale,
                 proj(COL_K + g * GROUP_WIDTH, GROUP_WIDTH),
                 proj(COL_V + g * GROUP_WIDTH, GROUP_WIDTH))
        for i, part in enumerate(parts):
            for c in range(2):
                scr[2 * i + c] = part[:, c * LANES:(c + 1) * LANES]
        for r in range(d):
            rows = pl.ds(r, n, stride=d)
            q_ref[r] = jnp.concatenate([scr[c, rows, :] for c in range(2)], axis=1).astype(BF16)
            kv_ref[r] = jnp.concatenate([scr[c, rows, :] for c in range(2, 6)], axis=1).astype(BF16)

    width = GROUP_WIDTH
    row = lax.broadcasted_iota(jnp.int32, (TM_PROJ, width), 0)
    for c0 in range(0, CONV_WIDTH, width):
        cols = slice(c0, c0 + width)
        b_gate = proj(COL_BG + c0, width)
        u = proj(COL_CG + c0, width) * proj(COL_XIN + c0, width)
        u_prev = jnp.where(row == 0, 0.0, pltpu.roll(u, 1, axis=0))
        u_next = jnp.where(row == TM_PROJ - 1, 0.0, pltpu.roll(u, TM_PROJ - 1, axis=0))
        conv = cw_ref[0:1, cols] * u_prev + cw_ref[1:2, cols] * u + cw_ref[2:3, cols] * u_next
        yb_ref[:, cols] = (b_gate * conv).astype(BF16)
        edge_ref[:, cols] = jnp.concatenate(
            [b_gate[0:1, :] * cw_ref[0:1, cols], b_gate[TM_PROJ - 1:TM_PROJ, :] * cw_ref[2:3, cols],
             u[0:1, :], u[TM_PROJ - 1:TM_PROJ, :], jnp.zeros((SUBLANES - 4, width), F32)], axis=0)
    for c in range(4):
        w = 2 * D_MODEL // 4
        z = proj(COL_GATE + c * w, w) + b_ref[:, c * w:(c + 1) * w]
        gate_ref[:, c * w:(c + 1) * w] = jax.nn.sigmoid(z).astype(BF16)


def _proj_call(x2, g, w_in, b_gate, conv_w, batch, seq):
    n = x2.shape[0]
    assert seq % TM_PROJ == 0
    steps_per_batch = seq // TM_PROJ
    d1, d2 = DILATED_PATTERNS[1][1], DILATED_PATTERNS[2][1]
    row = lambda i: (i, 0)
    res = lambda i: (i // steps_per_batch, 0, i % steps_per_batch, 0)
    const = lambda i: (0, 0)
    out_shape = [
        jax.ShapeDtypeStruct((n, GROUP_WIDTH), BF16),
        jax.ShapeDtypeStruct((n, 2 * GROUP_WIDTH), BF16),
        jax.ShapeDtypeStruct((batch, d1, seq // d1, GROUP_WIDTH), BF16),
        jax.ShapeDtypeStruct((batch, d1, seq // d1, 2 * GROUP_WIDTH), BF16),
        jax.ShapeDtypeStruct((batch, d2, seq // d2, GROUP_WIDTH), BF16),
        jax.ShapeDtypeStruct((batch, d2, seq // d2, 2 * GROUP_WIDTH), BF16),
        jax.ShapeDtypeStruct((n, CONV_WIDTH), BF16),
        jax.ShapeDtypeStruct((n // TM_PROJ, SUBLANES, CONV_WIDTH), F32),
        jax.ShapeDtypeStruct((n, 2 * D_MODEL), BF16),
    ]
    out_specs = [
        pl.BlockSpec((TM_PROJ, GROUP_WIDTH), row),
        pl.BlockSpec((TM_PROJ, 2 * GROUP_WIDTH), row),
        pl.BlockSpec((None, d1, TM_PROJ // d1, GROUP_WIDTH), res),
        pl.BlockSpec((None, d1, TM_PROJ // d1, 2 * GROUP_WIDTH), res),
        pl.BlockSpec((None, d2, TM_PROJ // d2, GROUP_WIDTH), res),
        pl.BlockSpec((None, d2, TM_PROJ // d2, 2 * GROUP_WIDTH), res),
        pl.BlockSpec((TM_PROJ, CONV_WIDTH), row),
        pl.BlockSpec((None, SUBLANES, CONV_WIDTH), lambda i: (i, 0, 0)),
        pl.BlockSpec((TM_PROJ, 2 * D_MODEL), row),
    ]
    return pl.pallas_call(
        _proj_kernel,
        grid=(n // TM_PROJ,),
        in_specs=[
            pl.BlockSpec((TM_PROJ, D_MODEL), row),
            pl.BlockSpec((1, D_MODEL), const),
            pl.BlockSpec((D_MODEL, IN_COLS), const, pipeline_mode=pl.Buffered(1)),
            pl.BlockSpec((1, 2 * D_MODEL), const),
            pl.BlockSpec((3, CONV_WIDTH), const),
        ],
        out_specs=out_specs,
        out_shape=out_shape,
        scratch_shapes=[pltpu.VMEM((6, TM_PROJ, LANES), F32)],
        compiler_params=pltpu.CompilerParams(dimension_semantics=("arbitrary",),
                                             vmem_limit_bytes=VMEM_LIMIT),
        name="proj",
    )(x2, g, w_in, b_gate, conv_w)


def _attn_sub_block(q_sub, kw, vw, bias_ref, g, lo, hi):
    assert KB == GROUP_WIDTH
    lane = lax.broadcasted_iota(jnp.int32, (QB, KB), 1)
    edge_ok = (lane >= lo) & (lane < hi)
    heads = [(lane >= h * HEAD_DIM) & (lane < (h + 1) * HEAD_DIM) for h in range(HEADS_PER_GROUP)]
    zero = jnp.zeros((), BF16)
    q_stack = jnp.concatenate([jnp.where(hm, q_sub, zero) for hm in heads], axis=0)
    s_all = lax.dot_general(q_stack, kw, (((1,), (1,)), ((), ())), preferred_element_type=F32)
    probs = []
    m_b = l_b = None
    for h, hm in enumerate(heads):
        s = s_all[h * QB:(h + 1) * QB] + bias_ref[g * HEADS_PER_GROUP + h]
        s = jnp.where(edge_ok, s, MASK_VALUE)
        m = jnp.max(s, axis=1, keepdims=True)
        p = jnp.exp(s - m)
        l = jnp.sum(p, axis=1, keepdims=True)
        probs.append(p.astype(BF16))
        m_b = jnp.broadcast_to(m, (QB, GROUP_WIDTH)) if m_b is None else jnp.where(hm, m, m_b)
        l_b = jnp.broadcast_to(l, (QB, GROUP_WIDTH)) if l_b is None else jnp.where(hm, l, l_b)
    o_all = jnp.dot(jnp.concatenate(probs, axis=0), vw, preferred_element_type=F32)
    acc = o_all[0:QB]
    for h in range(1, HEADS_PER_GROUP):
        acc = jnp.where(heads[h], o_all[h * QB:(h + 1) * QB], acc)
    return acc, m_b, l_b


def _attn_kernel(q0_ref, kv0_ref, kv0p_ref, kv0n_ref,
                 q1_ref, kv1_ref, kv1p_ref, kv1n_ref,
                 q2_ref, kv2_ref, kv2p_ref, kv2n_ref,
                 y_ref,
                 cat0, cat1, bias_ref, m_st, l_st, a_st, m_tmp, l_tmp, a_tmp, *, seq):
    j = pl.program_id(1)

    qi = lax.broadcasted_iota(jnp.int32, (QB, KB), 0)
    kc = lax.broadcasted_iota(jnp.int32, (QB, KB), 1)
    adelta = jnp.abs(kc - HALF - qi)
    band = adelta <= HALF
    slopes = _alibi_slopes()
    for g in range(N_GROUPS_A):
        dist = (adelta * DILATED_PATTERNS[g][1]).astype(F32)
        for h in range(HEADS_PER_GROUP):
            bias_ref[g * HEADS_PER_GROUP + h] = jnp.where(band, -(float(slopes[g, h]) * dist), MASK_VALUE)

    for cat, own, prv, nxt in ((cat0, kv0_ref, kv0p_ref, kv0n_ref),
                               (cat1, kv1_ref, kv1p_ref, kv1n_ref)):
        n_own = own.shape[-2]
        cat[:, 0:HALF, :] = prv[...].reshape(cat.shape[0], HALF, 2 * GROUP_WIDTH)
        cat[:, HALF:HALF + n_own, :] = own[...].reshape(cat.shape[0], n_own, 2 * GROUP_WIDTH)
        cat[:, HALF + n_own:, :] = nxt[...].reshape(cat.shape[0], HALF, 2 * GROUP_WIDTH)

    def window(cat, r, sb):
        rows = pl.ds(pl.multiple_of(sb * QB, QB), KB)
        return cat[r, rows, 0:GROUP_WIDTH], cat[r, rows, GROUP_WIDTH:]

    def edges(g, n_res, sb):
        length = seq // DILATED_PATTERNS[g][1]
        i0 = j * n_res + sb * QB
        return jnp.maximum(0, HALF - i0), jnp.minimum(KB, length + HALF - i0)

    def body0(sb, carry):
        rows = pl.ds(pl.multiple_of(sb * QB, QB), QB)
        kw, vw = window(cat0, 0, sb)
        lo, hi = edges(0, T_ATT, sb)
        acc, m_b, l_b = _attn_sub_block(q0_ref[rows, :], kw, vw, bias_ref, 0, lo, hi)
        for c in range(2):
            cols = slice(c * LANES, (c + 1) * LANES)
            m_st[c, rows, :] = m_b[:, cols]
            l_st[c, rows, :] = l_b[:, cols]
            a_st[c, rows, :] = acc[:, cols]
        return carry

    lax.fori_loop(0, T_ATT // QB, body0, 0, unroll=ATT_UNROLL)

    assert T_ATT // DILATED_PATTERNS[2][1] == QB
    for g, q_ref, cat in ((1, q1_ref, cat1), (2, q2_ref, None)):
        d = DILATED_PATTERNS[g][1]
        n_res = T_ATT // d
        sb_per_res = n_res // QB

        def body(idx, carry, g=g, q_ref=q_ref, cat=cat, n_res=n_res, sb_per_res=sb_per_res):
            r = idx // sb_per_res
            sb = idx % sb_per_res
            if cat is None:
                kv = jnp.concatenate([kv2p_ref[r], kv2_ref[r], kv2n_ref[r]], axis=0)
                kw, vw = kv[:, 0:GROUP_WIDTH], kv[:, GROUP_WIDTH:]
            else:
                kw, vw = window(cat, r, sb)
            lo, hi = edges(g, n_res, sb)
            q_sub = q_ref[r, pl.ds(pl.multiple_of(sb * QB, QB), QB), :]
            acc, m_b, l_b = _attn_sub_block(q_sub, kw, vw, bias_ref, g, lo, hi)
            rows = pl.ds(pl.multiple_of(idx * QB, QB), QB)
            m_tmp[rows, :] = m_b
            l_tmp[rows, :] = l_b
            a_tmp[rows, :] = acc
            return carry

        lax.fori_loop(0, T_ATT // QB, body, 0, unroll=ATT_UNROLL)

        for r in range(d):
            for ch in range(sb_per_res):
                src = slice(r * n_res + ch * QB, r * n_res + (ch + 1) * QB)
                tok = pl.ds(ch * QB * d + r, QB, stride=d)
                for c in range(2):
                    cols = slice(c * LANES, (c + 1) * LANES)
                    m_new_part = m_tmp[src, cols]
                    m_old = m_st[c, tok, :]
                    m_new = jnp.maximum(m_old, m_new_part)
                    e_old = jnp.exp(m_old - m_new)
                    e_new = jnp.exp(m_new_part - m_new)
                    l_new = e_old * l_st[c, tok, :] + e_new * l_tmp[src, cols]
                    a_new = e_old * a_st[c, tok, :] + e_new * a_tmp[src, cols]
                    if g == N_GROUPS_A - 1:
                        a_st[c, tok, :] = a_new / l_new
                    else:
                        m_st[c, tok, :] = m_new
                        l_st[c, tok, :] = l_new
                        a_st[c, tok, :] = a_new

    for c in range(2):
        y_ref[:, c * LANES:(c + 1) * LANES] = a_st[c].astype(BF16)


def _attn_call(q0, kv0, q1, kv1, q2, kv2, batch, seq):
    n = q0.shape[0]
    tiles = seq // T_ATT
    specs = []
    scratch = []
    blocks_per_tile = T_ATT // HALF
    n_half_blocks = n // HALF
    specs += [
        pl.BlockSpec((T_ATT, GROUP_WIDTH), lambda b, j: (b * tiles + j, 0)),
        pl.BlockSpec((T_ATT, 2 * GROUP_WIDTH), lambda b, j: (b * tiles + j, 0)),
        pl.BlockSpec((HALF, 2 * GROUP_WIDTH),
                     lambda b, j: (jnp.maximum((b * tiles + j) * blocks_per_tile - 1, 0), 0)),
        pl.BlockSpec((HALF, 2 * GROUP_WIDTH),
                     lambda b, j: (jnp.minimum((b * tiles + j + 1) * blocks_per_tile, n_half_blocks - 1), 0)),
    ]
    scratch.append(pltpu.VMEM((1, T_ATT + 2 * HALF, 2 * GROUP_WIDTH), BF16))
    for g in (1, 2):
        d = DILATED_PATTERNS[g][1]
        n_res = T_ATT // d
        per_tile = n_res // HALF
        last = seq // d // HALF - 1
        specs += [
            pl.BlockSpec((None, d, n_res, GROUP_WIDTH), lambda b, j: (b, 0, j, 0)),
            pl.BlockSpec((None, d, n_res, 2 * GROUP_WIDTH), lambda b, j: (b, 0, j, 0)),
            pl.BlockSpec((None, d, HALF, 2 * GROUP_WIDTH),
                         lambda b, j, per_tile=per_tile: (b, 0, jnp.maximum(j * per_tile - 1, 0), 0)),
            pl.BlockSpec((None, d, HALF, 2 * GROUP_WIDTH),
                         lambda b, j, per_tile=per_tile, last=last: (b, 0, jnp.minimum((j + 1) * per_tile, last), 0)),
        ]
        if n_res > QB:
            scratch.append(pltpu.VMEM((d, n_res + 2 * HALF, 2 * GROUP_WIDTH), BF16))
    scratch.append(pltpu.VMEM((N_HEADS_A, QB, KB), F32))
    scratch += [pltpu.VMEM((2, T_ATT, LANES), F32) for _ in range(3)]
    scratch += [pltpu.VMEM((T_ATT, GROUP_WIDTH), F32) for _ in range(3)]
    return pl.pallas_call(
        functools.partial(_attn_kernel, seq=seq),
        grid=(batch, tiles),
        in_specs=specs,
        out_specs=pl.BlockSpec((T_ATT, GROUP_WIDTH), lambda b, j: (b * tiles + j, 0)),
        out_shape=jax.ShapeDtypeStruct((n, GROUP_WIDTH), BF16),
        scratch_shapes=scratch,
        compiler_params=pltpu.CompilerParams(dimension_semantics=("arbitrary", "arbitrary"),
                                             vmem_limit_bytes=VMEM_LIMIT),
        name="attn",
    )(q0, kv0, kv0, kv0, q1, kv1, kv1, kv1, q2, kv2, kv2, kv2)


def _split_dot(a, w_cat, w_hi):
    a_hi = a.astype(BF16)
    a_lo = (a - a_hi.astype(F32)).astype(BF16)
    both = jnp.dot(a_hi, w_cat, preferred_element_type=F32)
    return both[:, 0:LANES] + both[:, LANES:] + jnp.dot(a_lo, w_hi, preferred_element_type=F32)


ST_BASE, ST_TILE, ST_FREE = 0, 1, 2
AL_NEW = 0
RT_E, RT_W, RT_POS = 0, 2, 4


def _mix_kernel(x_ref, ya_ref, yb_ref, edge_ref, edgep_ref, edgen_ref, gate_ref,
                wa_ref, wc_ref, wo_ref, g2_ref, wr_cat_ref, wr_hi_ref, br_ref,
                x1_ref, route_ref, routet_ref, alloc_ref, xs_hbm,
                st_ref, hb0, hb1, posv, poss, psem, dsem, *, seq, trash_row):
    i = pl.program_id(0)
    last_step = pl.num_programs(0) - 1
    hbs = (hb0, hb1)

    def rows_done(h):
        for k in range(2):
            pltpu.make_async_copy(hbs[h], xs_hbm.at[pl.ds(0, TM_MIX * SUBLANES)], dsem.at[h, k]).wait()

    def send_row(h, j, first_sublane):
        src = hbs[h].at[pl.ds(first_sublane, SUBLANES)]
        for k in range(2):
            dst = xs_hbm.at[pl.ds(pl.multiple_of(poss[h, RT_POS + k, j], SUBLANES), SUBLANES)]
            pltpu.make_async_copy(src, dst, dsem.at[h, k]).start(priority=1)

    def send_rows(h):
        def one(j, carry):
            send_row(h, j, pl.multiple_of(j * SUBLANES, SUBLANES))
            return carry
        lax.fori_loop(0, TM_MIX, one, 0, unroll=8)

    def positions_to_smem(h):
        return pltpu.make_async_copy(posv, poss.at[h], psem.at[h])

    @pl.when(i == 0)
    def _():
        st_ref[...] = jnp.zeros_like(st_ref)
        hb1[...] = jnp.zeros_like(hb1)
        spare = (trash_row + lax.broadcasted_iota(jnp.int32, posv.shape, 1)
                 + jnp.where(lax.broadcasted_iota(jnp.int32, posv.shape, 0) == RT_POS + 1, TM_MIX, 0))
        posv[...] = spare * SUBLANES
        positions_to_smem(1).start()

    def tile(h):
        rows = pl.ds(h * TM_MIX, TM_MIX)
        t0 = (i * MIX_TILES + h) * TM_MIX
        positions_to_smem(1 - h).wait()

        def send_other(part):
            for j in range(part * TM_MIX // 4, (part + 1) * TM_MIX // 4):
                send_row(1 - h, j, j * SUBLANES)

        yb_in = yb_ref[rows, :]
        patch = 16
        if h == 0:
            add = jnp.where(t0 % seq == 0, 0.0, edge_ref[0:1, :] * edgep_ref[3:4, :])
            top = yb_in[0:patch, :].astype(F32)
            top = jnp.where(lax.broadcasted_iota(jnp.int32, top.shape, 0) == 0, top + add, top)
            yb_in = jnp.concatenate([top.astype(BF16), yb_in[patch:, :]], axis=0)
        if h == MIX_TILES - 1:
            add = jnp.where((t0 + TM_MIX) % seq == 0, 0.0, edge_ref[1:2, :] * edgen_ref[2:3, :])
            bot = yb_in[TM_MIX - patch:, :].astype(F32)
            bot = jnp.where(lax.broadcasted_iota(jnp.int32, bot.shape, 0) == patch - 1, bot + add, bot)
            yb_in = jnp.concatenate([yb_in[:TM_MIX - patch, :], bot.astype(BF16)], axis=0)
        send_other(0)

        y_a = jnp.dot(ya_ref[rows, :], wa_ref[...], preferred_element_type=F32)
        send_other(1)
        y_b = jnp.dot(yb_in, wc_ref[...], preferred_element_type=F32)
        merged = gate_ref[rows, 0:D_MODEL] * y_a.astype(BF16) + gate_ref[rows, D_MODEL:] * y_b.astype(BF16)
        x1 = x_ref[rows, :] + jnp.dot(merged, wo_ref[...], preferred_element_type=F32)
        x1_ref[rows, :] = x1

        h2 = _rms(x1, g2_ref[...])
        if h == 0:
            pl.when(i > 0)(lambda: rows_done(0))
        else:
            rows_done(h)
        for c in range(ROW_CHUNKS):
            hbs[h][pl.ds(c, TM_MIX, stride=ROW_CHUNKS), :] = h2[:, c * LANES:(c + 1) * LANES]

        logits = _split_dot(h2, wr_cat_ref[...], wr_hi_ref[...]) + br_ref[...]
        send_other(2)
        send_other(3)
        lane = lax.broadcasted_iota(jnp.int32, (TM_MIX, LANES), 1)
        lane_f = lane.astype(F32)
        neg = -jnp.inf
        big = float(LANES)
        is_group = lane < N_EXPERT_GROUPS
        cm = jnp.where(is_group, logits, neg)
        cmax = jnp.max(cm, axis=1, keepdims=True)
        g_idx = jnp.min(jnp.where(cm == cmax, lane_f, big), axis=1, keepdims=True)
        p_group = 1.0 / jnp.sum(jnp.where(is_group, jnp.exp(logits - cmax), 0.0), axis=1, keepdims=True)
        f_lo = N_EXPERT_GROUPS + EXPERTS_PER_GROUP * g_idx
        in_group = (lane_f >= f_lo) & (lane_f < f_lo + EXPERTS_PER_GROUP)
        fm = jnp.where(in_group, logits, neg)
        f1 = jnp.max(fm, axis=1, keepdims=True)
        i1 = jnp.min(jnp.where(fm == f1, lane_f, big), axis=1, keepdims=True)
        fm2 = jnp.where(lane_f == i1, neg, fm)
        f2 = jnp.max(fm2, axis=1, keepdims=True)
        i2 = jnp.min(jnp.where(fm2 == f2, lane_f, big), axis=1, keepdims=True)
        e21 = jnp.exp(f2 - f1)
        w_1 = p_group / (1.0 + e21)
        w_2 = p_group * e21 / (1.0 + e21)
        e_1 = i1 - N_EXPERT_GROUPS
        e_2 = i2 - N_EXPERT_GROUPS

        onehot = jnp.where((lane_f == e_1) | (lane_f == e_2), 1.0, 0.0)
        r_i = lax.broadcasted_iota(jnp.int32, (TM_MIX, TM_MIX), 0)
        c_i = lax.broadcasted_iota(jnp.int32, (TM_MIX, TM_MIX), 1)
        tri = jnp.where(c_i < r_i, 1.0, 0.0).astype(BF16)
        base = st_ref[ST_BASE:ST_BASE + 1, :]
        before = jnp.dot(tri, onehot.astype(BF16), preferred_element_type=F32) + base

        tile_rows = float(TM_MOE)
        cur_tile = st_ref[ST_TILE:ST_TILE + 1, :]
        next_free = st_ref[ST_FREE:ST_FREE + 1, :]
        count = jnp.sum(onehot, axis=0, keepdims=True)
        slot0 = jnp.floor(base * (1.0 / tile_rows))
        partial = (base - slot0 * tile_rows) > 0.0
        slot_last = jnp.floor((base + count - 1.0) * (1.0 / tile_rows))
        n_new = jnp.where(count > 0.0, slot_last - slot0 + 1.0 - jnp.where(partial, 1.0, 0.0), 0.0)
        e_r = lax.broadcasted_iota(jnp.int32, (LANES, LANES), 0)
        e_c = lax.broadcasted_iota(jnp.int32, (LANES, LANES), 1)
        earlier = jnp.where(e_r < e_c, 1.0, 0.0).astype(BF16)
        new_before = jnp.dot(jnp.broadcast_to(n_new, (SUBLANES, LANES)).astype(BF16), earlier,
                             preferred_element_type=F32)[0:1, :]
        fresh = next_free + new_before - jnp.where(partial, 1.0, 0.0) - slot0
        partial_slot = jnp.where(partial, slot0, -1.0)

        def tile_of(slot_idx, fresh_v, cur_v, partial_v):
            return jnp.where(slot_idx == partial_v, cur_v, fresh_v + slot_idx)

        def pick(row_vec, e):
            return jnp.sum(jnp.where(lane_f == e, row_vec, 0.0), axis=1, keepdims=True)

        positions = []
        for e in (e_1, e_2):
            rank = pick(before, e)
            s = jnp.floor(rank * (1.0 / tile_rows))
            tid = tile_of(s, pick(fresh, e), pick(cur_tile, e), pick(partial_slot, e))
            positions.append(tid * tile_rows + (rank - s * tile_rows))

        st_ref[ST_BASE:ST_BASE + 1, :] = base + count
        st_ref[ST_TILE:ST_TILE + 1, :] = jnp.where(count > 0.0, tile_of(slot_last, fresh, cur_tile, partial_slot),
                                                    cur_tile)
        st_ref[ST_FREE:ST_FREE + 1, :] = next_free + jnp.sum(n_new, axis=1, keepdims=True)
        alloc_ref[h] = jnp.broadcast_to(n_new, (SUBLANES, LANES))

        route = jnp.zeros((TM_MIX, LANES), F32)
        for k, val in enumerate((e_1, e_2, w_1, w_2, positions[0], positions[1])):
            route = jnp.where(lane == k, val, route)
        route_ref[rows, :] = route
        route_t = route.T[0:SUBLANES, :]
        routet_ref[:, h * TM_MIX:(h + 1) * TM_MIX] = route_t
        posv[...] = route_t.astype(jnp.int32) * SUBLANES
        positions_to_smem(h).start()

    tile(0)
    tile(1)

    @pl.when(i == last_step)
    def _():
        positions_to_smem(1).wait()
        rows_done(0)
        send_rows(1)
        rows_done(1)

        posv[:, 0:LANES] = st_ref[...].astype(jnp.int32)
        state = positions_to_smem(0)
        state.start()
        state.wait()
        hb0[...] = jnp.zeros_like(hb0)

        def zero_jobs():
            for e in range(N_EXPERTS):
                fill = poss[0, ST_BASE, e] & (TM_MOE - 1)
                first = poss[0, ST_TILE, e] * TM_MOE
                at = fill
                size = 1
                while size < TM_MOE:
                    take = (fill > 0) & ((at & size) != 0)
                    yield take, first + at, size
                    at = at + jnp.where(take, size, 0)
                    size *= 2
            for t in range(N_EXPERTS):
                tile = poss[0, ST_FREE, 0] + t
                yield tile < trash_row // TM_MOE, tile * TM_MOE, TM_MOE

        def zero_copy(first, rows):
            return pltpu.make_async_copy(
                hb0.at[pl.ds(0, rows * SUBLANES)],
                xs_hbm.at[pl.ds(pl.multiple_of(first * SUBLANES, SUBLANES), rows * SUBLANES)], psem.at[0])

        for take, first, rows in zero_jobs():
            pl.when(take)(lambda first=first, rows=rows: zero_copy(first, rows).start())
        for take, first, rows in zero_jobs():
            pl.when(take)(lambda first=first, rows=rows: zero_copy(first, rows).wait())


def _mix_call(x2, y_attn, yb_in, edges, gates, wa, wc, wo, g2, wr_cat, wr_hi, br, seq, n_tiles):
    n = x2.shape[0]
    step_rows = MIX_TILES * TM_MIX
    assert step_rows == TM_PROJ
    n_steps = n // step_rows
    row = lambda i: (i, 0)
    const = lambda i: (0, 0)
    edge = lambda shift: pl.BlockSpec((None, SUBLANES, CONV_WIDTH),
                                      lambda i: (jnp.clip(i + shift, 0, n_steps - 1), 0, 0))
    xs_rows = n_tiles * TM_MOE + 2 * TM_MIX
    return pl.pallas_call(
        functools.partial(_mix_kernel, seq=seq, trash_row=n_tiles * TM_MOE),
        grid=(n_steps,),
        in_specs=[
            pl.BlockSpec((step_rows, D_MODEL), row),
            pl.BlockSpec((step_rows, GROUP_WIDTH), row),
            pl.BlockSpec((step_rows, CONV_WIDTH), row),
            edge(0), edge(-1), edge(1),
            pl.BlockSpec((step_rows, 2 * D_MODEL), row),
            pl.BlockSpec((GROUP_WIDTH, D_MODEL), const),
            pl.BlockSpec((CONV_WIDTH, D_MODEL), const),
            pl.BlockSpec((D_MODEL, D_MODEL), const),
            pl.BlockSpec((1, D_MODEL), const),
            pl.BlockSpec((D_MODEL, 2 * LANES), const),
            pl.BlockSpec((D_MODEL, LANES), const),
            pl.BlockSpec((1, LANES), const),
        ],
        out_specs=[
            pl.BlockSpec((step_rows, D_MODEL), row),
            pl.BlockSpec((step_rows, LANES), row),
            pl.BlockSpec((SUBLANES, step_rows), lambda i: (0, i)),
            pl.BlockSpec((MIX_TILES, SUBLANES, LANES), lambda i: (i, 0, 0)),
            pl.BlockSpec(memory_space=pl.ANY),
        ],
        out_shape=[
            jax.ShapeDtypeStruct((n, D_MODEL), F32),
            jax.ShapeDtypeStruct((n, LANES), F32),
            jax.ShapeDtypeStruct((SUBLANES, n), F32),
            jax.ShapeDtypeStruct((n // TM_MIX, SUBLANES, LANES), F32),
            jax.ShapeDtypeStruct((xs_rows * ROW_CHUNKS, LANES), F32),
        ],
        scratch_shapes=[pltpu.VMEM((SUBLANES, LANES), F32),
                        pltpu.VMEM((TM_MIX * ROW_CHUNKS, LANES), F32),
                        pltpu.VMEM((TM_MIX * ROW_CHUNKS, LANES), F32),
                        pltpu.VMEM((SUBLANES, TM_MIX), jnp.int32),
                        pltpu.SMEM((MIX_TILES, SUBLANES, TM_MIX), jnp.int32),
                        pltpu.SemaphoreType.DMA((MIX_TILES,)),
                        pltpu.SemaphoreType.DMA((MIX_TILES, 2))],
        compiler_params=pltpu.CompilerParams(dimension_semantics=("arbitrary",),
                                             vmem_limit_bytes=VMEM_LIMIT),
        name="mix",
    )(x2, y_attn, yb_in, edges, edges, edges, gates, wa, wc, wo, g2, wr_cat, wr_hi, br)


def _row_gather(idx_ref, n_rows, src_hbm, dst, sem):
    def issue(pair, carry):
        for k in range(2):
            j = 2 * pair + k
            t = idx_ref[0, 0, j]
            pltpu.make_async_copy(src_hbm.at[pl.ds(pl.multiple_of(t * SUBLANES, SUBLANES), SUBLANES)],
                                  dst.at[pl.ds(pl.multiple_of(j * SUBLANES, SUBLANES), SUBLANES)],
                                  sem).start(priority=k)
        return carry
    lax.fori_loop(0, n_rows // 2, issue, 0, unroll=4)


def _row_gather_wait(n_rows, src_hbm, dst, sem):
    pltpu.make_async_copy(src_hbm.at[pl.ds(0, n_rows * SUBLANES)], dst, sem).wait()


def _rows_from_tiles(buf, first_row, n_rows):
    return jnp.concatenate(
        [buf[pl.ds(first_row * ROW_CHUNKS + c, n_rows, stride=ROW_CHUNKS), :] for c in range(ROW_CHUNKS)],
        axis=1)


def _expert_kernel(order_ref, te_ref, next_ref, nused_ref, xs_hbm, w1_hbm, w3_hbm, w2_hbm, y_ref,
                   xin, w1s, w3s, w2s, w13b, w2b, isem, wsem):
    i = pl.program_id(0)
    n_used = nused_ref[0]
    slot = i % XIN_SLOTS
    used = i < n_used

    def fetch(step):
        rows = pl.ds(order_ref[step] * TM_MOE, TM_MOE)
        s = step % XIN_SLOTS
        return [pltpu.make_async_copy(xs_hbm.at[rows, c, :], xin.at[s, c], isem.at[s]) for c in range(ROW_CHUNKS)]

    def weights(e):
        return [pltpu.make_async_copy(w_hbm.at[e], stage, wsem.at[k])
                for k, (w_hbm, stage) in enumerate(((w1_hbm, w1s), (w3_hbm, w3s), (w2_hbm, w2s)))]

    def start(copies):
        for cp in copies:
            cp.start()

    def wait(copies):
        for cp in copies:
            cp.wait()

    @pl.when(i == 0)
    def _():
        start(weights(te_ref[0]))
        start(fetch(0))
        pl.when(n_used > 1)(lambda: start(fetch(1)))

    pl.when(i + 2 < n_used)(lambda: start(fetch(i + 2)))

    @pl.when(jnp.logical_not(used))
    def _():
        y_ref[...] = jnp.zeros_like(y_ref)

    @pl.when(used & ((i == 0) | (te_ref[i] != te_ref[jnp.maximum(i - 1, 0)])))
    def _():
        wait(weights(te_ref[i]))
        w13b[:, 0:EXPERT_FF] = w1s[...].astype(BF16)
        w13b[:, EXPERT_FF:] = w3s[...].astype(BF16)
        w2b[...] = w2s[...].astype(BF16)
        pl.when(next_ref[i] != te_ref[i])(lambda: start(weights(next_ref[i])))

    @pl.when(used)
    def _():
        wait(fetch(i))
        x = jnp.concatenate([xin[slot, c] for c in range(ROW_CHUNKS)], axis=1)
        ab = jnp.dot(x.astype(BF16), w13b[...], preferred_element_type=F32)
        a = ab[:, 0:EXPERT_FF]
        hid = (a * jax.nn.sigmoid(a) * ab[:, EXPERT_FF:]).astype(BF16)
        y = jnp.dot(hid, w2b[...], preferred_element_type=F32)
        for c in range(ROW_CHUNKS):
            y_ref[pl.ds(c, TM_MOE, stride=ROW_CHUNKS), :] = y[:, c * LANES:(c + 1) * LANES]


def _expert_call(order, tile_expert, next_expert, n_used, xs_rows, w1, w3, w2):
    n_tiles = order.shape[0]
    any_space = pl.BlockSpec(memory_space=pl.ANY)
    grid_spec = pltpu.PrefetchScalarGridSpec(
        num_scalar_prefetch=4,
        grid=(n_tiles,),
        in_specs=[any_space, any_space, any_space, any_space],
        out_specs=pl.BlockSpec((TM_MOE * ROW_CHUNKS, LANES), lambda i, od, te, nx, nu: (od[i], 0)),
        scratch_shapes=[pltpu.VMEM((XIN_SLOTS, ROW_CHUNKS, TM_MOE, LANES), F32),
                        pltpu.VMEM((D_MODEL, EXPERT_FF), F32),
                        pltpu.VMEM((D_MODEL, EXPERT_FF), F32),
                        pltpu.VMEM((EXPERT_FF, D_MODEL), F32),
                        pltpu.VMEM((D_MODEL, 2 * EXPERT_FF), BF16),
                        pltpu.VMEM((EXPERT_FF, D_MODEL), BF16),
                        pltpu.SemaphoreType.DMA((XIN_SLOTS,)),
                        pltpu.SemaphoreType.DMA((3,))],
    )
    return pl.pallas_call(
        _expert_kernel,
        grid_spec=grid_spec,
        out_shape=jax.ShapeDtypeStruct((n_tiles * TM_MOE * ROW_CHUNKS, LANES), F32),
        compiler_params=pltpu.CompilerParams(dimension_semantics=("arbitrary",),
                                             vmem_limit_bytes=VMEM_LIMIT),
        name="experts",
    )(order, tile_expert, next_expert, n_used, xs_rows, w1, w3, w2)


def _combine_kernel(pos_ref, posn_ref, y_hbm, x1_ref, route_ref, g_ref, o_ref, buf0, buf1, sem):
    i = pl.program_id(0)
    n_steps = pl.num_programs(0)
    bufs = (buf0, buf1)

    @pl.when(i == 0)
    def _():
        _row_gather(pos_ref, 2 * TM_CMB, y_hbm, buf0, sem.at[0])

    for slot in range(2):
        @pl.when((i % 2 == slot) & (i + 1 < n_steps))
        def _(slot=slot):
            _row_gather(posn_ref, 2 * TM_CMB, y_hbm, bufs[1 - slot], sem.at[1 - slot])

    for slot in range(2):
        @pl.when(i % 2 == slot)
        def _(slot=slot):
            _row_gather_wait(2 * TM_CMB, y_hbm, bufs[slot], sem.at[slot])
            y_1 = _rows_from_tiles(bufs[slot], 0, TM_CMB)
            y_2 = _rows_from_tiles(bufs[slot], TM_CMB, TM_CMB)
            x = x1_ref[...] + route_ref[:, 2:3] * y_1 + route_ref[:, 3:4] * y_2
            o_ref[...] = _rms(x, g_ref[...])


def _combine_call(pos, y_flat, x1, route, g):
    n = x1.shape[0]
    n_steps = n // TM_CMB
    row = lambda i: (i, 0)
    return pl.pallas_call(
        _combine_kernel,
        grid=(n_steps,),
        in_specs=[
            pl.BlockSpec((1, 1, 2 * TM_CMB), lambda i: (i, 0, 0), memory_space=pltpu.SMEM),
            pl.BlockSpec((1, 1, 2 * TM_CMB), lambda i: (jnp.minimum(i + 1, n_steps - 1), 0, 0),
                         memory_space=pltpu.SMEM),
            pl.BlockSpec(memory_space=pl.ANY),
            pl.BlockSpec((TM_CMB, D_MODEL), row),
            pl.BlockSpec((TM_CMB, LANES), row),
            pl.BlockSpec((1, D_MODEL), lambda i: (0, 0)),
        ],
        out_specs=pl.BlockSpec((TM_CMB, D_MODEL), row),
        out_shape=jax.ShapeDtypeStruct((n, D_MODEL), F32),
        scratch_shapes=[pltpu.VMEM((2 * TM_CMB * ROW_CHUNKS, LANES), F32),
                        pltpu.VMEM((2 * TM_CMB * ROW_CHUNKS, LANES), F32),
                        pltpu.SemaphoreType.DMA((2,))],
        compiler_params=pltpu.CompilerParams(dimension_semantics=("arbitrary",),
                                             vmem_limit_bytes=VMEM_LIMIT),
        name="combine",
    )(pos, pos, y_flat, x1, route, g)


def _layer(x2, batch, seq, norm_mix_g, w_in, b_gate, conv_w, w_attn_out, w_conv_out, w_out, norm_ffn_g,
           w_route_group, b_route_group, w_route_expert, b_route_expert, w1, w3, w2, final_g):
    n = x2.shape[0]
    q0, kv0, q1, kv1, q2, kv2, yb_in, edges, gates = _proj_call(
        x2, norm_mix_g[None, :], w_in.astype(BF16), b_gate[None, :], conv_w, batch, seq)
    y_attn = _attn_call(q0, kv0, q1, kv1, q2, kv2, batch, seq)

    n_route = N_EXPERT_GROUPS + N_EXPERTS
    w_route = jnp.pad(jnp.concatenate([w_route_group, w_route_expert], axis=1), ((0, 0), (0, LANES - n_route)))
    b_route = jnp.pad(jnp.concatenate([b_route_group, b_route_expert]), (0, LANES - n_route))[None, :]
    wr_hi = w_route.astype(BF16)
    wr_lo = (w_route - wr_hi.astype(F32)).astype(BF16)
    n_tiles = (2 * n) // TM_MOE + N_EXPERTS
    x1, route, route_t, alloc, xs_flat = _mix_call(
        x2, y_attn, yb_in, edges, gates, w_attn_out.astype(BF16), w_conv_out.astype(BF16), w_out.astype(BF16),
        norm_ffn_g[None, :], jnp.concatenate([wr_hi, wr_lo], axis=1), wr_hi, b_route, seq, n_tiles)

    i32 = jnp.int32
    taken = alloc[:, AL_NEW, :N_EXPERTS].astype(i32).reshape(-1)
    k = jnp.arange(taken.shape[0], dtype=i32)
    running = jnp.sum(jnp.where(k[:, None] >= k[None, :], taken[None, :], 0), axis=1)
    n_used = running[-1:]
    tile = jnp.arange(n_tiles, dtype=i32)
    owner = jnp.sum((running[None, :] <= tile[:, None]).astype(i32), axis=1) % N_EXPERTS
    owner = jnp.where(tile < n_used[0], owner, N_EXPERTS)
    key = owner * n_tiles + tile
    place = jnp.sum((key[None, :] < key[:, None]).astype(i32), axis=1)
    at = place[None, :] == tile[:, None]
    order = jnp.sum(jnp.where(at, tile[None, :], 0), axis=1)
    step_owner = jnp.sum(jnp.where(at, owner[None, :], 0), axis=1)
    step_expert = jnp.minimum(step_owner, N_EXPERTS - 1)
    later = (step_owner[None, :] > step_owner[:, None]) & (step_owner[None, :] < N_EXPERTS)
    next_expert = jnp.min(jnp.where(later, step_owner[None, :], N_EXPERTS), axis=1)
    next_expert = jnp.where(next_expert < N_EXPERTS, next_expert, step_expert)

    y_flat = _expert_call(order, step_expert, next_expert, n_used, xs_flat.reshape(-1, ROW_CHUNKS, LANES),
                          w1, w3, w2)
    pos = route_t[RT_POS:RT_POS + 2].astype(i32)
    pos_tiles = pos.reshape(2, n // TM_CMB, TM_CMB).transpose(1, 0, 2).reshape(n // TM_CMB, 1, 2 * TM_CMB)
    return _combine_call(pos_tiles, y_flat, x1, route, final_g[None, :])


def kernel(x, norm_mix_g, w_in, b_gate, conv_w, w_attn_out, w_conv_out, w_out, norm_ffn_g,
           w_route_group, b_route_group, w_route_expert, b_route_expert, w1, w3, w2, norm_final_g):
    batch, seq, d = x.shape
    depth = w_in.shape[0]
    assert d == D_MODEL and depth == 1 and seq % T_ATT == 0
    out = _layer(x.reshape(batch * seq, d), batch, seq, norm_mix_g[0], w_in[0], b_gate[0], conv_w[0],
                 w_attn_out[0], w_conv_out[0], w_out[0], norm_ffn_g[0], w_route_group[0], b_route_group[0],
                 w_route_expert[0], b_route_expert[0], w1[0], w3[0], w2[0], norm_final_g)
    return out.reshape(batch, seq, d)
```

```python
import functools

import numpy as np
import jax
import jax.numpy as jnp
from jax import lax
from jax.experimental import pallas as pl
from jax.experimental.pallas import tpu as pltpu

F32 = jnp.float32
BF16 = jnp.bfloat16

D_MODEL = 1024
HEAD_DIM = 64
HEADS_PER_GROUP = 4
DILATED_PATTERNS = ((128, 1), (512, 4), (2048, 16))
N_GROUPS_A = 3
N_HEADS_A = N_GROUPS_A * HEADS_PER_GROUP
ATTN_WIDTH = N_HEADS_A * HEAD_DIM
GROUP_WIDTH = HEADS_PER_GROUP * HEAD_DIM
ALIBI_SPAN = 8.0
MASK_VALUE = -1e30
CONV_WIDTH = 768
N_EXPERT_GROUPS = 4
EXPERTS_PER_GROUP = 8
N_EXPERTS = 32
EXPERT_FF = 512
RMS_EPS = 1e-6

HALF = 64
LANES = 128
SUBLANES = 8
ROW_CHUNKS = D_MODEL // LANES

COL_K = ATTN_WIDTH
COL_V = 2 * ATTN_WIDTH
COL_BG = 3 * ATTN_WIDTH
COL_CG = COL_BG + CONV_WIDTH
COL_XIN = COL_CG + CONV_WIDTH
COL_GATE = COL_XIN + CONV_WIDTH
IN_COLS = COL_GATE + 2 * D_MODEL

TM_PROJ = 1024
T_ATT = 2048
QB = 128
KB = QB + 2 * HALF
ATT_UNROLL = 16
TM_MIX = 512
MIX_TILES = 2
TM_MOE = 512
XIN_SLOTS = 3
TM_CMB = 256

VMEM_LIMIT = 56 * 1024 * 1024


def _alibi_slopes():
    return np.array([2.0 ** (-ALIBI_SPAN * (i + 1) / N_HEADS_A) for i in range(N_HEADS_A)],
                    dtype=np.float32).reshape(N_GROUPS_A, HEADS_PER_GROUP)


def _rms(x, g):
    return x * lax.rsqrt(jnp.mean(x * x, axis=-1, keepdims=True) + RMS_EPS) * g


def _proj_kernel(x_ref, g_ref, w_ref, b_ref, cw_ref,
                 q0_ref, kv0_ref, q1_ref, kv1_ref, q2_ref, kv2_ref, yb_ref, edge_ref, gate_ref, scr):
    h = _rms(x_ref[...], g_ref[...]).astype(BF16)

    def proj(c0, width):
        return jnp.dot(h, w_ref[:, c0:c0 + width], preferred_element_type=F32)

    qscale = HEAD_DIM ** -0.5
    q0_ref[...] = (proj(0, GROUP_WIDTH) * qscale).astype(BF16)
    kv0_ref[:, 0:GROUP_WIDTH] = proj(COL_K, GROUP_WIDTH).astype(BF16)
    kv0_ref[:, GROUP_WIDTH:] = proj(COL_V, GROUP_WIDTH).astype(BF16)

    for g, q_ref, kv_ref in ((1, q1_ref, kv1_ref), (2, q2_ref, kv2_ref)):
        d = DILATED_PATTERNS[g][1]
        n = TM_PROJ // d
        parts = (proj(g * GROUP_WIDTH, GROUP_WIDTH) * qscale,
                 proj(COL_K + g * GROUP_WIDTH, GROUP_WIDTH),
                 proj(COL_V + g * GROUP_WIDTH, GROUP_WIDTH))
        for i, part in enumerate(parts):
            for c in range(2):
                scr[2 * i + c] = part[:, c * LANES:(c + 1) * LANES]
        for r in range(d):
            rows = pl.ds(r, n, stride=d)
            q_ref[r] = jnp.concatenate([scr[c, rows, :] for c in range(2)], axis=1).astype(BF16)
            kv_ref[r] = jnp.concatenate([scr[c, rows, :] for c in range(2, 6)], axis=1).astype(BF16)

    width = GROUP_WIDTH
    row = lax.broadcasted_iota(jnp.int32, (TM_PROJ, width), 0)
    for c0 in range(0, CONV_WIDTH, width):
        cols = slice(c0, c0 + width)
        b_gate = proj(COL_BG + c0, width)
        u = proj(COL_CG + c0, width) * proj(COL_XIN + c0, width)
        u_prev = jnp.where(row == 0, 0.0, pltpu.roll(u, 1, axis=0))
        u_next = jnp.where(row == TM_PROJ - 1, 0.0, pltpu.roll(u, TM_PROJ - 1, axis=0))
        conv = cw_ref[0:1, cols] * u_prev + cw_ref[1:2, cols] * u + cw_ref[2:3, cols] * u_next
        yb_ref[:, cols] = (b_gate * conv).astype(BF16)
        edge_ref[:, cols] = jnp.concatenate(
            [b_gate[0:1, :] * cw_ref[0:1, cols], b_gate[TM_PROJ - 1:TM_PROJ, :] * cw_ref[2:3, cols],
             u[0:1, :], u[TM_PROJ - 1:TM_PROJ, :], jnp.zeros((SUBLANES - 4, width), F32)], axis=0)
    for c in range(4):
        w = 2 * D_MODEL // 4
        z = proj(COL_GATE + c * w, w) + b_ref[:, c * w:(c + 1) * w]
        gate_ref[:, c * w:(c + 1) * w] = jax.nn.sigmoid(z).astype(BF16)


def _proj_call(x2, g, w_in, b_gate, conv_w, batch, seq):
    n = x2.shape[0]
    assert seq % TM_PROJ == 0
    steps_per_batch = seq // TM_PROJ
    d1, d2 = DILATED_PATTERNS[1][1], DILATED_PATTERNS[2][1]
    row = lambda i: (i, 0)
    res = lambda i: (i // steps_per_batch, 0, i % steps_per_batch, 0)
    const = lambda i: (0, 0)
    out_shape = [
        jax.ShapeDtypeStruct((n, GROUP_WIDTH), BF16),
        jax.ShapeDtypeStruct((n, 2 * GROUP_WIDTH), BF16),
        jax.ShapeDtypeStruct((batch, d1, seq // d1, GROUP_WIDTH), BF16),
        jax.ShapeDtypeStruct((batch, d1, seq // d1, 2 * GROUP_WIDTH), BF16),
        jax.ShapeDtypeStruct((batch, d2, seq // d2, GROUP_WIDTH), BF16),
        jax.ShapeDtypeStruct((batch, d2, seq // d2, 2 * GROUP_WIDTH), BF16),
        jax.ShapeDtypeStruct((n, CONV_WIDTH), BF16),
        jax.ShapeDtypeStruct((n // TM_PROJ, SUBLANES, CONV_WIDTH), F32),
        jax.ShapeDtypeStruct((n, 2 * D_MODEL), BF16),
    ]
    out_specs = [
        pl.BlockSpec((TM_PROJ, GROUP_WIDTH), row),
        pl.BlockSpec((TM_PROJ, 2 * GROUP_WIDTH), row),
        pl.BlockSpec((None, d1, TM_PROJ // d1, GROUP_WIDTH), res),
        pl.BlockSpec((None, d1, TM_PROJ // d1, 2 * GROUP_WIDTH), res),
        pl.BlockSpec((None, d2, TM_PROJ // d2, GROUP_WIDTH), res),
        pl.BlockSpec((None, d2, TM_PROJ // d2, 2 * GROUP_WIDTH), res),
        pl.BlockSpec((TM_PROJ, CONV_WIDTH), row),
        pl.BlockSpec((None, SUBLANES, CONV_WIDTH), lambda i: (i, 0, 0)),
        pl.BlockSpec((TM_PROJ, 2 * D_MODEL), row),
    ]
    return pl.pallas_call(
        _proj_kernel,
        grid=(n // TM_PROJ,),
        in_specs=[
            pl.BlockSpec((TM_PROJ, D_MODEL), row),
            pl.BlockSpec((1, D_MODEL), const),
            pl.BlockSpec((D_MODEL, IN_COLS), const, pipeline_mode=pl.Buffered(1)),
            pl.BlockSpec((1, 2 * D_MODEL), const),
            pl.BlockSpec((3, CONV_WIDTH), const),
        ],
        out_specs=out_specs,
        out_shape=out_shape,
        scratch_shapes=[pltpu.VMEM((6, TM_PROJ, LANES), F32)],
        compiler_params=pltpu.CompilerParams(dimension_semantics=("arbitrary",),
                                             vmem_limit_bytes=VMEM_LIMIT),
        name="proj",
    )(x2, g, w_in, b_gate, conv_w)


def _attn_sub_block(q_sub, kw, vw, bias_ref, g, lo, hi):
    assert KB == GROUP_WIDTH
    lane = lax.broadcasted_iota(jnp.int32, (QB, KB), 1)
    edge_ok = (lane >= lo) & (lane < hi)
    heads = [(lane >= h * HEAD_DIM) & (lane < (h + 1) * HEAD_DIM) for h in range(HEADS_PER_GROUP)]
    zero = jnp.zeros((), BF16)
    q_stack = jnp.concatenate([jnp.where(hm, q_sub, zero) for hm in heads], axis=0)
    s_all = lax.dot_general(q_stack, kw, (((1,), (1,)), ((), ())), preferred_element_type=F32)
    probs = []
    m_b = l_b = None
    for h, hm in enumerate(heads):
        s = s_all[h * QB:(h + 1) * QB] + bias_ref[g * HEADS_PER_GROUP + h]
        s = jnp.where(edge_ok, s, MASK_VALUE)
        m = jnp.max(s, axis=1, keepdims=True)
        p = jnp.exp(s - m)
        l = jnp.sum(p, axis=1, keepdims=True)
        probs.append(p.astype(BF16))
        m_b = jnp.broadcast_to(m, (QB, GROUP_WIDTH)) if m_b is None else jnp.where(hm, m, m_b)
        l_b = jnp.broadcast_to(l, (QB, GROUP_WIDTH)) if l_b is None else jnp.where(hm, l, l_b)
    o_all = jnp.dot(jnp.concatenate(probs, axis=0), vw, preferred_element_type=F32)
    acc = o_all[0:QB]
    for h in range(1, HEADS_PER_GROUP):
        acc = jnp.where(heads[h], o_all[h * QB:(h + 1) * QB], acc)
    return acc, m_b, l_b


def _attn_kernel(q0_ref, kv0_ref, kv0p_ref, kv0n_ref,
                 q1_ref, kv1_ref, kv1p_ref, kv1n_ref,
                 q2_ref, kv2_ref, kv2p_ref, kv2n_ref,
                 y_ref,
                 cat0, cat1, bias_ref, m_st, l_st, a_st, m_tmp, l_tmp, a_tmp, *, seq):
    j = pl.program_id(1)

    qi = lax.broadcasted_iota(jnp.int32, (QB, KB), 0)
    kc = lax.broadcasted_iota(jnp.int32, (QB, KB), 1)
    adelta = jnp.abs(kc - HALF - qi)
    band = adelta <= HALF
    slopes = _alibi_slopes()
    for g in range(N_GROUPS_A):
        dist = (adelta * DILATED_PATTERNS[g][1]).astype(F32)
        for h in range(HEADS_PER_GROUP):
            bias_ref[g * HEADS_PER_GROUP + h] = jnp.where(band, -(float(slopes[g, h]) * dist), MASK_VALUE)

    for cat, own, prv, nxt in ((cat0, kv0_ref, kv0p_ref, kv0n_ref),
                               (cat1, kv1_ref, kv1p_ref, kv1n_ref)):
        n_own = own.shape[-2]
        cat[:, 0:HALF, :] = prv[...].reshape(cat.shape[0], HALF, 2 * GROUP_WIDTH)
        cat[:, HALF:HALF + n_own, :] = own[...].reshape(cat.shape[0], n_own, 2 * GROUP_WIDTH)
        cat[:, HALF + n_own:, :] = nxt[...].reshape(cat.shape[0], HALF, 2 * GROUP_WIDTH)

    def window(cat, r, sb):
        rows = pl.ds(pl.multiple_of(sb * QB, QB), KB)
        return cat[r, rows, 0:GROUP_WIDTH], cat[r, rows, GROUP_WIDTH:]

    def edges(g, n_res, sb):
        length = seq // DILATED_PATTERNS[g][1]
        i0 = j * n_res + sb * QB
        return jnp.maximum(0, HALF - i0), jnp.minimum(KB, length + HALF - i0)

    def body0(sb, carry):
        rows = pl.ds(pl.multiple_of(sb * QB, QB), QB)
        kw, vw = window(cat0, 0, sb)
        lo, hi = edges(0, T_ATT, sb)
        acc, m_b, l_b = _attn_sub_block(q0_ref[rows, :], kw, vw, bias_ref, 0, lo, hi)
        for c in range(2):
            cols = slice(c * LANES, (c + 1) * LANES)
            m_st[c, rows, :] = m_b[:, cols]
            l_st[c, rows, :] = l_b[:, cols]
            a_st[c, rows, :] = acc[:, cols]
        return carry

    lax.fori_loop(0, T_ATT // QB, body0, 0, unroll=ATT_UNROLL)

    assert T_ATT // DILATED_PATTERNS[2][1] == QB
    for g, q_ref, cat in ((1, q1_ref, cat1), (2, q2_ref, None)):
        d = DILATED_PATTERNS[g][1]
        n_res = T_ATT // d
        sb_per_res = n_res // QB

        def body(idx, carry, g=g, q_ref=q_ref, cat=cat, n_res=n_res, sb_per_res=sb_per_res):
            r = idx // sb_per_res
            sb = idx % sb_per_res
            if cat is None:
                kv = jnp.concatenate([kv2p_ref[r], kv2_ref[r], kv2n_ref[r]], axis=0)
                kw, vw = kv[:, 0:GROUP_WIDTH], kv[:, GROUP_WIDTH:]
            else:
                kw, vw = window(cat, r, sb)
            lo, hi = edges(g, n_res, sb)
            q_sub = q_ref[r, pl.ds(pl.multiple_of(sb * QB, QB), QB), :]
            acc, m_b, l_b = _attn_sub_block(q_sub, kw, vw, bias_ref, g, lo, hi)
            rows = pl.ds(pl.multiple_of(idx * QB, QB), QB)
            m_tmp[rows, :] = m_b
            l_tmp[rows, :] = l_b
            a_tmp[rows, :] = acc
            return carry

        lax.fori_loop(0, T_ATT // QB, body, 0, unroll=ATT_UNROLL)

        for r in range(d):
            for ch in range(sb_per_res):
                src = slice(r * n_res + ch * QB, r * n_res + (ch + 1) * QB)
                tok = pl.ds(ch * QB * d + r, QB, stride=d)
                for c in range(2):
                    cols = slice(c * LANES, (c + 1) * LANES)
                    m_new_part = m_tmp[src, cols]
                    m_old = m_st[c, tok, :]
                    m_new = jnp.maximum(m_old, m_new_part)
                    e_old = jnp.exp(m_old - m_new)
                    e_new = jnp.exp(m_new_part - m_new)
                    l_new = e_old * l_st[c, tok, :] + e_new * l_tmp[src, cols]
                    a_new = e_old * a_st[c, tok, :] + e_new * a_tmp[src, cols]
                    if g == N_GROUPS_A - 1:
                        a_st[c, tok, :] = a_new / l_new
                    else:
                        m_st[c, tok, :] = m_new
                        l_st[c, tok, :] = l_new
                        a_st[c, tok, :] = a_new

    for c in range(2):
        y_ref[:, c * LANES:(c + 1) * LANES] = a_st[c].astype(BF16)


def _attn_call(q0, kv0, q1, kv1, q2, kv2, batch, seq):
    n = q0.shape[0]
    tiles = seq // T_ATT
    specs = []
    scratch = []
    blocks_per_tile = T_ATT // HALF
    n_half_blocks = n // HALF
    specs += [
        pl.BlockSpec((T_ATT, GROUP_WIDTH), lambda b, j: (b * tiles + j, 0)),
        pl.BlockSpec((T_ATT, 2 * GROUP_WIDTH), lambda b, j: (b * tiles + j, 0)),
        pl.BlockSpec((HALF, 2 * GROUP_WIDTH),
                     lambda b, j: (jnp.maximum((b * tiles + j) * blocks_per_tile - 1, 0), 0)),
        pl.BlockSpec((HALF, 2 * GROUP_WIDTH),
                     lambda b, j: (jnp.minimum((b * tiles + j + 1) * blocks_per_tile, n_half_blocks - 1), 0)),
    ]
    scratch.append(pltpu.VMEM((1, T_ATT + 2 * HALF, 2 * GROUP_WIDTH), BF16))
    for g in (1, 2):
        d = DILATED_PATTERNS[g][1]
        n_res = T_ATT // d
        per_tile = n_res // HALF
        last = seq // d // HALF - 1
        specs += [
            pl.BlockSpec((None, d, n_res, GROUP_WIDTH), lambda b, j: (b, 0, j, 0)),
            pl.BlockSpec((None, d, n_res, 2 * GROUP_WIDTH), lambda b, j: (b, 0, j, 0)),
            pl.BlockSpec((None, d, HALF, 2 * GROUP_WIDTH),
                         lambda b, j, per_tile=per_tile: (b, 0, jnp.maximum(j * per_tile - 1, 0), 0)),
            pl.BlockSpec((None, d, HALF, 2 * GROUP_WIDTH),
                         lambda b, j, per_tile=per_tile, last=last: (b, 0, jnp.minimum((j + 1) * per_tile, last), 0)),
        ]
        if n_res > QB:
            scratch.append(pltpu.VMEM((d, n_res + 2 * HALF, 2 * GROUP_WIDTH), BF16))
    scratch.append(pltpu.VMEM((N_HEADS_A, QB, KB), F32))
    scratch += [pltpu.VMEM((2, T_ATT, LANES), F32) for _ in range(3)]
    scratch += [pltpu.VMEM((T_ATT, GROUP_WIDTH), F32) for _ in range(3)]
    return pl.pallas_call(
        functools.partial(_attn_kernel, seq=seq),
        grid=(batch, tiles),
        in_specs=specs,
        out_specs=pl.BlockSpec((T_ATT, GROUP_WIDTH), lambda b, j: (b * tiles + j, 0)),
        out_shape=jax.ShapeDtypeStruct((n, GROUP_WIDTH), BF16),
        scratch_shapes=scratch,
        compiler_params=pltpu.CompilerParams(dimension_semantics=("arbitrary", "arbitrary"),
                                             vmem_limit_bytes=VMEM_LIMIT),
        name="attn",
    )(q0, kv0, kv0, kv0, q1, kv1, kv1, kv1, q2, kv2, kv2, kv2)


def _split_dot(a, w_cat, w_hi):
    a_hi = a.astype(BF16)
    a_lo = (a - a_hi.astype(F32)).astype(BF16)
    both = jnp.dot(a_hi, w_cat, preferred_element_type=F32)
    return both[:, 0:LANES] + both[:, LANES:] + jnp.dot(a_lo, w_hi, preferred_element_type=F32)


ST_BASE, ST_TILE, ST_FREE = 0, 1, 2
AL_NEW = 0
RT_E, RT_W, RT_POS = 0, 2, 4


def _mix_kernel(x_ref, ya_ref, yb_ref, edge_ref, edgep_ref, edgen_ref, gate_ref,
                wa_ref, wc_ref, wo_ref, g2_ref, wr_cat_ref, wr_hi_ref, br_ref,
                x1_ref, route_ref, routet_ref, alloc_ref, xs_hbm,
                st_ref, hb0, hb1, posv, poss, psem, dsem, *, seq, trash_row):
    i = pl.program_id(0)
    last_step = pl.num_programs(0) - 1
    hbs = (hb0, hb1)

    def rows_done(h):
        for k in range(2):
            pltpu.make_async_copy(hbs[h], xs_hbm.at[pl.ds(0, TM_MIX * SUBLANES)], dsem.at[h, k]).wait()

    def send_row(h, j, first_sublane):
        src = hbs[h].at[pl.ds(first_sublane, SUBLANES)]
        for k in range(2):
            dst = xs_hbm.at[pl.ds(pl.multiple_of(poss[h, RT_POS + k, j], SUBLANES), SUBLANES)]
            pltpu.make_async_copy(src, dst, dsem.at[h, k]).start(priority=1)

    def send_rows(h):
        def one(j, carry):
            send_row(h, j, pl.multiple_of(j * SUBLANES, SUBLANES))
            return carry
        lax.fori_loop(0, TM_MIX, one, 0, unroll=8)

    def positions_to_smem(h):
        return pltpu.make_async_copy(posv, poss.at[h], psem.at[h])

    @pl.when(i == 0)
    def _():
        st_ref[...] = jnp.zeros_like(st_ref)
        hb1[...] = jnp.zeros_like(hb1)
        spare = (trash_row + lax.broadcasted_iota(jnp.int32, posv.shape, 1)
                 + jnp.where(lax.broadcasted_iota(jnp.int32, posv.shape, 0) == RT_POS + 1, TM_MIX, 0))
        posv[...] = spare * SUBLANES
        positions_to_smem(1).start()

    def tile(h):
        rows = pl.ds(h * TM_MIX, TM_MIX)
        t0 = (i * MIX_TILES + h) * TM_MIX
        positions_to_smem(1 - h).wait()

        def send_other(part):
            for j in range(part * TM_MIX // 4, (part + 1) * TM_MIX // 4):
                send_row(1 - h, j, j * SUBLANES)

        yb_in = yb_ref[rows, :]
        patch = 16
        if h == 0:
            add = jnp.where(t0 % seq == 0, 0.0, edge_ref[0:1, :] * edgep_ref[3:4, :])
            top = yb_in[0:patch, :].astype(F32)
            top = jnp.where(lax.broadcasted_iota(jnp.int32, top.shape, 0) == 0, top + add, top)
            yb_in = jnp.concatenate([top.astype(BF16), yb_in[patch:, :]], axis=0)
        if h == MIX_TILES - 1:
            add = jnp.where((t0 + TM_MIX) % seq == 0, 0.0, edge_ref[1:2, :] * edgen_ref[2:3, :])
            bot = yb_in[TM_MIX - patch:, :].astype(F32)
            bot = jnp.where(lax.broadcasted_iota(jnp.int32, bot.shape, 0) == patch - 1, bot + add, bot)
            yb_in = jnp.concatenate([yb_in[:TM_MIX - patch, :], bot.astype(BF16)], axis=0)
        send_other(0)

        y_a = jnp.dot(ya_ref[rows, :], wa_ref[...], preferred_element_type=F32)
        send_other(1)
        y_b = jnp.dot(yb_in, wc_ref[...], preferred_element_type=F32)
        merged = gate_ref[rows, 0:D_MODEL] * y_a.astype(BF16) + gate_ref[rows, D_MODEL:] * y_b.astype(BF16)
        x1 = x_ref[rows, :] + jnp.dot(merged, wo_ref[...], preferred_element_type=F32)
        x1_ref[rows, :] = x1

        h2 = _rms(x1, g2_ref[...])
        if h == 0:
            pl.when(i > 0)(lambda: rows_done(0))
        else:
            rows_done(h)
        for c in range(ROW_CHUNKS):
            hbs[h][pl.ds(c, TM_MIX, stride=ROW_CHUNKS), :] = h2[:, c * LANES:(c + 1) * LANES]

        logits = _split_dot(h2, wr_cat_ref[...], wr_hi_ref[...]) + br_ref[...]
        send_other(2)
        send_other(3)
        lane = lax.broadcasted_iota(jnp.int32, (TM_MIX, LANES), 1)
        lane_f = lane.astype(F32)
        neg = -jnp.inf
        big = float(LANES)
        is_group = lane < N_EXPERT_GROUPS
        cm = jnp.where(is_group, logits, neg)
        cmax = jnp.max(cm, axis=1, keepdims=True)
        g_idx = jnp.min(jnp.where(cm == cmax, lane_f, big), axis=1, keepdims=True)
        p_group = 1.0 / jnp.sum(jnp.where(is_group, jnp.exp(logits - cmax), 0.0), axis=1, keepdims=True)
        f_lo = N_EXPERT_GROUPS + EXPERTS_PER_GROUP * g_idx
        in_group = (lane_f >= f_lo) & (lane_f < f_lo + EXPERTS_PER_GROUP)
        fm = jnp.where(in_group, logits, neg)
        f1 = jnp.max(fm, axis=1, keepdims=True)
        i1 = jnp.min(jnp.where(fm == f1, lane_f, big), axis=1, keepdims=True)
        fm2 = jnp.where(lane_f == i1, neg, fm)
        f2 = jnp.max(fm2, axis=1, keepdims=True)
        i2 = jnp.min(jnp.where(fm2 == f2, lane_f, big), axis=1, keepdims=True)
        e21 = jnp.exp(f2 - f1)
        w_1 = p_group / (1.0 + e21)
        w_2 = p_group * e21 / (1.0 + e21)
        e_1 = i1 - N_EXPERT_GROUPS
        e_2 = i2 - N_EXPERT_GROUPS

        onehot = jnp.where((lane_f == e_1) | (lane_f == e_2), 1.0, 0.0)
        r_i = lax.broadcasted_iota(jnp.int32, (TM_MIX, TM_MIX), 0)
        c_i = lax.broadcasted_iota(jnp.int32, (TM_MIX, TM_MIX), 1)
        tri = jnp.where(c_i < r_i, 1.0, 0.0).astype(BF16)
        base = st_ref[ST_BASE:ST_BASE + 1, :]
        before = jnp.dot(tri, onehot.astype(BF16), preferred_element_type=F32) + base

        tile_rows = float(TM_MOE)
        cur_tile = st_ref[ST_TILE:ST_TILE + 1, :]
        next_free = st_ref[ST_FREE:ST_FREE + 1, :]
        count = jnp.sum(onehot, axis=0, keepdims=True)
        slot0 = jnp.floor(base * (1.0 / tile_rows))
        partial = (base - slot0 * tile_rows) > 0.0
        slot_last = jnp.floor((base + count - 1.0) * (1.0 / tile_rows))
        n_new = jnp.where(count > 0.0, slot_last - slot0 + 1.0 - jnp.where(partial, 1.0, 0.0), 0.0)
        e_r = lax.broadcasted_iota(jnp.int32, (LANES, LANES), 0)
        e_c = lax.broadcasted_iota(jnp.int32, (LANES, LANES), 1)
        earlier = jnp.where(e_r < e_c, 1.0, 0.0).astype(BF16)
        new_before = jnp.dot(jnp.broadcast_to(n_new, (SUBLANES, LANES)).astype(BF16), earlier,
                             preferred_element_type=F32)[0:1, :]
        fresh = next_free + new_before - jnp.where(partial, 1.0, 0.0) - slot0
        partial_slot = jnp.where(partial, slot0, -1.0)

        def tile_of(slot_idx, fresh_v, cur_v, partial_v):
            return jnp.where(slot_idx == partial_v, cur_v, fresh_v + slot_idx)

        s_all = jnp.floor(before * (1.0 / tile_rows))
        pos_all = tile_of(s_all, fresh, cur_tile, partial_slot) * tile_rows + (before - s_all * tile_rows)
        positions = [jnp.sum(jnp.where(lane_f == e, pos_all, 0.0), axis=1, keepdims=True) for e in (e_1, e_2)]

        st_ref[ST_BASE:ST_BASE + 1, :] = base + count
        st_ref[ST_TILE:ST_TILE + 1, :] = jnp.where(count > 0.0, tile_of(slot_last, fresh, cur_tile, partial_slot),
                                                    cur_tile)
        st_ref[ST_FREE:ST_FREE + 1, :] = next_free + jnp.sum(n_new, axis=1, keepdims=True)
        alloc_ref[h] = jnp.broadcast_to(n_new, (SUBLANES, LANES))

        route = jnp.zeros((TM_MIX, LANES), F32)
        for k, val in enumerate((e_1, e_2, w_1, w_2, positions[0], positions[1])):
            route = jnp.where(lane == k, val, route)
        route_ref[rows, :] = route
        route_t = route.T[0:SUBLANES, :]
        routet_ref[:, h * TM_MIX:(h + 1) * TM_MIX] = route_t
        posv[...] = route_t.astype(jnp.int32) * SUBLANES
        positions_to_smem(h).start()

    tile(0)
    tile(1)

    @pl.when(i == last_step)
    def _():
        positions_to_smem(1).wait()
        rows_done(0)
        send_rows(1)
        rows_done(1)

        posv[:, 0:LANES] = st_ref[...].astype(jnp.int32)
        state = positions_to_smem(0)
        state.start()
        state.wait()
        hb0[...] = jnp.zeros_like(hb0)

        def zero_jobs():
            for e in range(N_EXPERTS):
                fill = poss[0, ST_BASE, e] & (TM_MOE - 1)
                first = poss[0, ST_TILE, e] * TM_MOE
                at = fill
                size = 1
                while size < TM_MOE:
                    take = (fill > 0) & ((at & size) != 0)
                    yield take, first + at, size
                    at = at + jnp.where(take, size, 0)
                    size *= 2
            for t in range(N_EXPERTS):
                tile = poss[0, ST_FREE, 0] + t
                yield tile < trash_row // TM_MOE, tile * TM_MOE, TM_MOE

        def zero_copy(first, rows):
            return pltpu.make_async_copy(
                hb0.at[pl.ds(0, rows * SUBLANES)],
                xs_hbm.at[pl.ds(pl.multiple_of(first * SUBLANES, SUBLANES), rows * SUBLANES)], psem.at[0])

        for take, first, rows in zero_jobs():
            pl.when(take)(lambda first=first, rows=rows: zero_copy(first, rows).start())
        for take, first, rows in zero_jobs():
            pl.when(take)(lambda first=first, rows=rows: zero_copy(first, rows).wait())


def _mix_call(x2, y_attn, yb_in, edges, gates, wa, wc, wo, g2, wr_cat, wr_hi, br, seq, n_tiles):
    n = x2.shape[0]
    step_rows = MIX_TILES * TM_MIX
    assert step_rows == TM_PROJ
    n_steps = n // step_rows
    row = lambda i: (i, 0)
    const = lambda i: (0, 0)
    edge = lambda shift: pl.BlockSpec((None, SUBLANES, CONV_WIDTH),
                                      lambda i: (jnp.clip(i + shift, 0, n_steps - 1), 0, 0))
    xs_rows = n_tiles * TM_MOE + 2 * TM_MIX
    return pl.pallas_call(
        functools.partial(_mix_kernel, seq=seq, trash_row=n_tiles * TM_MOE),
        grid=(n_steps,),
        in_specs=[
            pl.BlockSpec((step_rows, D_MODEL), row),
            pl.BlockSpec((step_rows, GROUP_WIDTH), row),
            pl.BlockSpec((step_rows, CONV_WIDTH), row),
            edge(0), edge(-1), edge(1),
            pl.BlockSpec((step_rows, 2 * D_MODEL), row),
            pl.BlockSpec((GROUP_WIDTH, D_MODEL), const),
            pl.BlockSpec((CONV_WIDTH, D_MODEL), const),
            pl.BlockSpec((D_MODEL, D_MODEL), const),
            pl.BlockSpec((1, D_MODEL), const),
            pl.BlockSpec((D_MODEL, 2 * LANES), const),
            pl.BlockSpec((D_MODEL, LANES), const),
            pl.BlockSpec((1, LANES), const),
        ],
        out_specs=[
            pl.BlockSpec((step_rows, D_MODEL), row),
            pl.BlockSpec((step_rows, LANES), row),
            pl.BlockSpec((SUBLANES, step_rows), lambda i: (0, i)),
            pl.BlockSpec((MIX_TILES, SUBLANES, LANES), lambda i: (i, 0, 0)),
            pl.BlockSpec(memory_space=pl.ANY),
        ],
        out_shape=[
            jax.ShapeDtypeStruct((n, D_MODEL), F32),
            jax.ShapeDtypeStruct((n, LANES), F32),
            jax.ShapeDtypeStruct((SUBLANES, n), F32),
            jax.ShapeDtypeStruct((n // TM_MIX, SUBLANES, LANES), F32),
            jax.ShapeDtypeStruct((xs_rows * ROW_CHUNKS, LANES), F32),
        ],
        scratch_shapes=[pltpu.VMEM((SUBLANES, LANES), F32),
                        pltpu.VMEM((TM_MIX * ROW_CHUNKS, LANES), F32),
                        pltpu.VMEM((TM_MIX * ROW_CHUNKS, LANES), F32),
                        pltpu.VMEM((SUBLANES, TM_MIX), jnp.int32),
                        pltpu.SMEM((MIX_TILES, SUBLANES, TM_MIX), jnp.int32),
                        pltpu.SemaphoreType.DMA((MIX_TILES,)),
                        pltpu.SemaphoreType.DMA((MIX_TILES, 2))],
        compiler_params=pltpu.CompilerParams(dimension_semantics=("arbitrary",),
                                             vmem_limit_bytes=VMEM_LIMIT),
        name="mix",
    )(x2, y_attn, yb_in, edges, edges, edges, gates, wa, wc, wo, g2, wr_cat, wr_hi, br)


def _row_gather(idx_ref, n_rows, src_hbm, dst, sem):
    def issue(pair, carry):
        for k in range(2):
            j = 2 * pair + k
            t = idx_ref[0, 0, j]
            pltpu.make_async_copy(src_hbm.at[pl.ds(pl.multiple_of(t * SUBLANES, SUBLANES), SUBLANES)],
                                  dst.at[pl.ds(pl.multiple_of(j * SUBLANES, SUBLANES), SUBLANES)],
                                  sem).start(priority=k)
        return carry
    lax.fori_loop(0, n_rows // 2, issue, 0, unroll=4)


def _row_gather_wait(n_rows, src_hbm, dst, sem):
    pltpu.make_async_copy(src_hbm.at[pl.ds(0, n_rows * SUBLANES)], dst, sem).wait()


def _rows_from_tiles(buf, first_row, n_rows):
    return jnp.concatenate(
        [buf[pl.ds(first_row * ROW_CHUNKS + c, n_rows, stride=ROW_CHUNKS), :] for c in range(ROW_CHUNKS)],
        axis=1)


def _expert_kernel(order_ref, te_ref, next_ref, nused_ref, xs_hbm, w1_hbm, w3_hbm, w2_hbm, y_ref,
                   xin, w1s, w3s, w2s, w13b, w2b, isem, wsem):
    i = pl.program_id(0)
    n_used = nused_ref[0]
    slot = i % XIN_SLOTS
    used = i < n_used

    def fetch(step):
        rows = pl.ds(order_ref[step] * TM_MOE, TM_MOE)
        s = step % XIN_SLOTS
        return [pltpu.make_async_copy(xs_hbm.at[rows, c, :], xin.at[s, c], isem.at[s]) for c in range(ROW_CHUNKS)]

    def weights(e):
        return [pltpu.make_async_copy(w_hbm.at[e], stage, wsem.at[k])
                for k, (w_hbm, stage) in enumerate(((w1_hbm, w1s), (w3_hbm, w3s), (w2_hbm, w2s)))]

    def start(copies):
        for cp in copies:
            cp.start()

    def wait(copies):
        for cp in copies:
            cp.wait()

    @pl.when(i == 0)
    def _():
        start(weights(te_ref[0]))
        start(fetch(0))
        pl.when(n_used > 1)(lambda: start(fetch(1)))

    pl.when(i + 2 < n_used)(lambda: start(fetch(i + 2)))

    @pl.when(jnp.logical_not(used))
    def _():
        y_ref[...] = jnp.zeros_like(y_ref)

    @pl.when(used & ((i == 0) | (te_ref[i] != te_ref[jnp.maximum(i - 1, 0)])))
    def _():
        wait(weights(te_ref[i]))
        w13b[:, 0:EXPERT_FF] = w1s[...].astype(BF16)
        w13b[:, EXPERT_FF:] = w3s[...].astype(BF16)
        w2b[...] = w2s[...].astype(BF16)
        pl.when(next_ref[i] != te_ref[i])(lambda: start(weights(next_ref[i])))

    @pl.when(used)
    def _():
        wait(fetch(i))
        x = jnp.concatenate([xin[slot, c] for c in range(ROW_CHUNKS)], axis=1)
        ab = jnp.dot(x.astype(BF16), w13b[...], preferred_element_type=F32)
        a = ab[:, 0:EXPERT_FF]
        hid = (a * jax.nn.sigmoid(a) * ab[:, EXPERT_FF:]).astype(BF16)
        y = jnp.dot(hid, w2b[...], preferred_element_type=F32)
        for c in range(ROW_CHUNKS):
            y_ref[pl.ds(c, TM_MOE, stride=ROW_CHUNKS), :] = y[:, c * LANES:(c + 1) * LANES]


def _expert_call(order, tile_expert, next_expert, n_used, xs_rows, w1, w3, w2):
    n_tiles = order.shape[0]
    any_space = pl.BlockSpec(memory_space=pl.ANY)
    grid_spec = pltpu.PrefetchScalarGridSpec(
        num_scalar_prefetch=4,
        grid=(n_tiles,),
        in_specs=[any_space, any_space, any_space, any_space],
        out_specs=pl.BlockSpec((TM_MOE * ROW_CHUNKS, LANES), lambda i, od, te, nx, nu: (od[i], 0)),
        scratch_shapes=[pltpu.VMEM((XIN_SLOTS, ROW_CHUNKS, TM_MOE, LANES), F32),
                        pltpu.VMEM((D_MODEL, EXPERT_FF), F32),
                        pltpu.VMEM((D_MODEL, EXPERT_FF), F32),
                        pltpu.VMEM((EXPERT_FF, D_MODEL), F32),
                        pltpu.VMEM((D_MODEL, 2 * EXPERT_FF), BF16),
                        pltpu.VMEM((EXPERT_FF, D_MODEL), BF16),
                        pltpu.SemaphoreType.DMA((XIN_SLOTS,)),
                        pltpu.SemaphoreType.DMA((3,))],
    )
    return pl.pallas_call(
        _expert_kernel,
        grid_spec=grid_spec,
        out_shape=jax.ShapeDtypeStruct((n_tiles * TM_MOE * ROW_CHUNKS, LANES), F32),
        compiler_params=pltpu.CompilerParams(dimension_semantics=("arbitrary",),
                                             vmem_limit_bytes=VMEM_LIMIT),
        name="experts",
    )(order, tile_expert, next_expert, n_used, xs_rows, w1, w3, w2)


def _combine_kernel(pos_ref, posn_ref, y_hbm, x1_ref, route_ref, g_ref, o_ref, buf0, buf1, sem):
    i = pl.program_id(0)
    n_steps = pl.num_programs(0)
    bufs = (buf0, buf1)

    @pl.when(i == 0)
    def _():
        _row_gather(pos_ref, 2 * TM_CMB, y_hbm, buf0, sem.at[0])

    for slot in range(2):
        @pl.when((i % 2 == slot) & (i + 1 < n_steps))
        def _(slot=slot):
            _row_gather(posn_ref, 2 * TM_CMB, y_hbm, bufs[1 - slot], sem.at[1 - slot])

    for slot in range(2):
        @pl.when(i % 2 == slot)
        def _(slot=slot):
            _row_gather_wait(2 * TM_CMB, y_hbm, bufs[slot], sem.at[slot])
            y_1 = _rows_from_tiles(bufs[slot], 0, TM_CMB)
            y_2 = _rows_from_tiles(bufs[slot], TM_CMB, TM_CMB)
            x = x1_ref[...] + route_ref[:, 2:3] * y_1 + route_ref[:, 3:4] * y_2
            o_ref[...] = _rms(x, g_ref[...])


def _combine_call(pos, y_flat, x1, route, g):
    n = x1.shape[0]
    n_steps = n // TM_CMB
    row = lambda i: (i, 0)
    return pl.pallas_call(
        _combine_kernel,
        grid=(n_steps,),
        in_specs=[
            pl.BlockSpec((1, 1, 2 * TM_CMB), lambda i: (i, 0, 0), memory_space=pltpu.SMEM),
            pl.BlockSpec((1, 1, 2 * TM_CMB), lambda i: (jnp.minimum(i + 1, n_steps - 1), 0, 0),
                         memory_space=pltpu.SMEM),
            pl.BlockSpec(memory_space=pl.ANY),
            pl.BlockSpec((TM_CMB, D_MODEL), row),
            pl.BlockSpec((TM_CMB, LANES), row),
            pl.BlockSpec((1, D_MODEL), lambda i: (0, 0)),
        ],
        out_specs=pl.BlockSpec((TM_CMB, D_MODEL), row),
        out_shape=jax.ShapeDtypeStruct((n, D_MODEL), F32),
        scratch_shapes=[pltpu.VMEM((2 * TM_CMB * ROW_CHUNKS, LANES), F32),
                        pltpu.VMEM((2 * TM_CMB * ROW_CHUNKS, LANES), F32),
                        pltpu.SemaphoreType.DMA((2,))],
        compiler_params=pltpu.CompilerParams(dimension_semantics=("arbitrary",),
                                             vmem_limit_bytes=VMEM_LIMIT),
        name="combine",
    )(pos, pos, y_flat, x1, route, g)


def _layer(x2, batch, seq, norm_mix_g, w_in, b_gate, conv_w, w_attn_out, w_conv_out, w_out, norm_ffn_g,
           w_route_group, b_route_group, w_route_expert, b_route_expert, w1, w3, w2, final_g):
    n = x2.shape[0]
    q0, kv0, q1, kv1, q2, kv2, yb_in, edges, gates = _proj_call(
        x2, norm_mix_g[None, :], w_in.astype(BF16), b_gate[None, :], conv_w, batch, seq)
    y_attn = _attn_call(q0, kv0, q1, kv1, q2, kv2, batch, seq)

    n_route = N_EXPERT_GROUPS + N_EXPERTS
    w_route = jnp.pad(jnp.concatenate([w_route_group, w_route_expert], axis=1), ((0, 0), (0, LANES - n_route)))
    b_route = jnp.pad(jnp.concatenate([b_route_group, b_route_expert]), (0, LANES - n_route))[None, :]
    wr_hi = w_route.astype(BF16)
    wr_lo = (w_route - wr_hi.astype(F32)).astype(BF16)
    n_tiles = (2 * n) // TM_MOE + N_EXPERTS
    x1, route, route_t, alloc, xs_flat = _mix_call(
        x2, y_attn, yb_in, edges, gates, w_attn_out.astype(BF16), w_conv_out.astype(BF16), w_out.astype(BF16),
        norm_ffn_g[None, :], jnp.concatenate([wr_hi, wr_lo], axis=1), wr_hi, b_route, seq, n_tiles)

    i32 = jnp.int32
    taken = alloc[:, AL_NEW, :N_EXPERTS].astype(i32).reshape(-1)
    k = jnp.arange(taken.shape[0], dtype=i32)
    running = jnp.sum(jnp.where(k[:, None] >= k[None, :], taken[None, :], 0), axis=1)
    n_used = running[-1:]
    tile = jnp.arange(n_tiles, dtype=i32)
    owner = jnp.sum((running[None, :] <= tile[:, None]).astype(i32), axis=1) % N_EXPERTS
    owner = jnp.where(tile < n_used[0], owner, N_EXPERTS)
    key = owner * n_tiles + tile
    place = jnp.sum((key[None, :] < key[:, None]).astype(i32), axis=1)
    at = place[None, :] == tile[:, None]
    order = jnp.sum(jnp.where(at, tile[None, :], 0), axis=1)
    step_owner = jnp.sum(jnp.where(at, owner[None, :], 0), axis=1)
    step_expert = jnp.minimum(step_owner, N_EXPERTS - 1)
    later = (step_owner[None, :] > step_owner[:, None]) & (step_owner[None, :] < N_EXPERTS)
    next_expert = jnp.min(jnp.where(later, step_owner[None, :], N_EXPERTS), axis=1)
    next_expert = jnp.where(next_expert < N_EXPERTS, next_expert, step_expert)

    y_flat = _expert_call(order, step_expert, next_expert, n_used, xs_flat.reshape(-1, ROW_CHUNKS, LANES),
                          w1, w3, w2)
    pos = route_t[RT_POS:RT_POS + 2].astype(i32)
    pos_tiles = pos.reshape(2, n // TM_CMB, TM_CMB).transpose(1, 0, 2).reshape(n // TM_CMB, 1, 2 * TM_CMB)
    return _combine_call(pos_tiles, y_flat, x1, route, final_g[None, :])


def kernel(x, norm_mix_g, w_in, b_gate, conv_w, w_attn_out, w_conv_out, w_out, norm_ffn_g,
           w_route_group, b_route_group, w_route_expert, b_route_expert, w1, w3, w2, norm_final_g):
    batch, seq, d = x.shape
    depth = w_in.shape[0]
    assert d == D_MODEL and depth == 1 and seq % T_ATT == 0
    out = _layer(x.reshape(batch * seq, d), batch, seq, norm_mix_g[0], w_in[0], b_gate[0], conv_w[0],
                 w_attn_out[0], w_conv_out[0], w_out[0], norm_ffn_g[0], w_route_group[0], b_route_group[0],
                 w_route_expert[0], b_route_expert[0], w1[0], w3[0], w2[0], norm_final_g)
    return out.reshape(batch, seq, d)
```

```python
import functools

import numpy as np
import jax
import jax.numpy as jnp
from jax import lax
from jax.experimental import pallas as pl
from jax.experimental.pallas import tpu as pltpu

F32 = jnp.float32
BF16 = jnp.bfloat16

D_MODEL = 1024
HEAD_DIM = 64
HEADS_PER_GROUP = 4
DILATED_PATTERNS = ((128, 1), (512, 4), (2048, 16))
N_GROUPS_A = 3
N_HEADS_A = N_GROUPS_A * HEADS_PER_GROUP
ATTN_WIDTH = N_HEADS_A * HEAD_DIM
GROUP_WIDTH = HEADS_PER_GROUP * HEAD_DIM
ALIBI_SPAN = 8.0
MASK_VALUE = -1e30
CONV_WIDTH = 768
N_EXPERT_GROUPS = 4
EXPERTS_PER_GROUP = 8
N_EXPERTS = 32
EXPERT_FF = 512
RMS_EPS = 1e-6

HALF = 64
LANES = 128
SUBLANES = 8
ROW_CHUNKS = D_MODEL // LANES

COL_K = ATTN_WIDTH
COL_V = 2 * ATTN_WIDTH
COL_BG = 3 * ATTN_WIDTH
COL_CG = COL_BG + CONV_WIDTH
COL_XIN = COL_CG + CONV_WIDTH
COL_GATE = COL_XIN + CONV_WIDTH
IN_COLS = COL_GATE + 2 * D_MODEL

TM_PROJ = 1024
T_ATT = 2048
QB = 128
KB = QB + 2 * HALF
ATT_UNROLL = 16
TM_MIX = 512
MIX_TILES = 2
TM_MOE = 512
XIN_SLOTS = 3
TM_CMB = 256

VMEM_LIMIT = 56 * 1024 * 1024


def _alibi_slopes():
    return np.array([2.0 ** (-ALIBI_SPAN * (i + 1) / N_HEADS_A) for i in range(N_HEADS_A)],
                    dtype=np.float32).reshape(N_GROUPS_A, HEADS_PER_GROUP)


def _rms(x, g):
    return x * lax.rsqrt(jnp.mean(x * x, axis=-1, keepdims=True) + RMS_EPS) * g


def _proj_kernel(x_ref, g_ref, w_ref, b_ref, cw_ref,
                 q0_ref, kv0_ref, q1_ref, kv1_ref, q2_ref, kv2_ref, yb_ref, edge_ref, gate_ref, scr):
    h = _rms(x_ref[...], g_ref[...]).astype(BF16)

    def proj(c0, width):
        return jnp.dot(h, w_ref[:, c0:c0 + width], preferred_element_type=F32)

    qscale = HEAD_DIM ** -0.5
    q0_ref[...] = (proj(0, GROUP_WIDTH) * qscale).astype(BF16)
    kv0_ref[:, 0:GROUP_WIDTH] = proj(COL_K, GROUP_WIDTH).astype(BF16)
    kv0_ref[:, GROUP_WIDTH:] = proj(COL_V, GROUP_WIDTH).astype(BF16)

    for g, q_ref, kv_ref in ((1, q1_ref, kv1_ref), (2, q2_ref, kv2_ref)):
        d = DILATED_PATTERNS[g][1]
        n = TM_PROJ // d
        parts = (proj(g * GROUP_WIDTH, GROUP_WIDTH) * qscale,
                 proj(COL_K + g * GROUP_WIDTH, GROUP_WIDTH),
                 proj(COL_V + g * GROUP_WIDTH, GROUP_WIDTH))
        for i, part in enumerate(parts):
            for c in range(2):
                scr[2 * i + c] = part[:, c * LANES:(c + 1) * LANES]
        for r in range(d):
            rows = pl.ds(r, n, stride=d)
            q_ref[r] = jnp.concatenate([scr[c, rows, :] for c in range(2)], axis=1).astype(BF16)
            kv_ref[r] = jnp.concatenate([scr[c, rows, :] for c in range(2, 6)], axis=1).astype(BF16)

    width = GROUP_WIDTH
    row = lax.broadcasted_iota(jnp.int32, (TM_PROJ, width), 0)
    for c0 in range(0, CONV_WIDTH, width):
        cols = slice(c0, c0 + width)
        b_gate = proj(COL_BG + c0, width)
        u = proj(COL_CG + c0, width) * proj(COL_XIN + c0, width)
        u_prev = jnp.where(row == 0, 0.0, pltpu.roll(u, 1, axis=0))
        u_next = jnp.where(row == TM_PROJ - 1, 0.0, pltpu.roll(u, TM_PROJ - 1, axis=0))
        conv = cw_ref[0:1, cols] * u_prev + cw_ref[1:2, cols] * u + cw_ref[2:3, cols] * u_next
        yb_ref[:, cols] = (b_gate * conv).astype(BF16)
        edge_ref[:, cols] = jnp.concatenate(
            [b_gate[0:1, :] * cw_ref[0:1, cols], b_gate[TM_PROJ - 1:TM_PROJ, :] * cw_ref[2:3, cols],
             u[0:1, :], u[TM_PROJ - 1:TM_PROJ, :], jnp.zeros((SUBLANES - 4, width), F32)], axis=0)
    for c in range(4):
        w = 2 * D_MODEL // 4
        z = proj(COL_GATE + c * w, w) + b_ref[:, c * w:(c + 1) * w]
        gate_ref[:, c * w:(c + 1) * w] = jax.nn.sigmoid(z).astype(BF16)


def _proj_call(x2, g, w_in, b_gate, conv_w, batch, seq):
    n = x2.shape[0]
    assert seq % TM_PROJ == 0
    steps_per_batch = seq // TM_PROJ
    d1, d2 = DILATED_PATTERNS[1][1], DILATED_PATTERNS[2][1]
    row = lambda i: (i, 0)
    res = lambda i: (i // steps_per_batch, 0, i % steps_per_batch, 0)
    const = lambda i: (0, 0)
    out_shape = [
        jax.ShapeDtypeStruct((n, GROUP_WIDTH), BF16),
        jax.ShapeDtypeStruct((n, 2 * GROUP_WIDTH), BF16),
        jax.ShapeDtypeStruct((batch, d1, seq // d1, GROUP_WIDTH), BF16),
        jax.ShapeDtypeStruct((batch, d1, seq // d1, 2 * GROUP_WIDTH), BF16),
        jax.ShapeDtypeStruct((batch, d2, seq // d2, GROUP_WIDTH), BF16),
        jax.ShapeDtypeStruct((batch, d2, seq // d2, 2 * GROUP_WIDTH), BF16),
        jax.ShapeDtypeStruct((n, CONV_WIDTH), BF16),
        jax.ShapeDtypeStruct((n // TM_PROJ, SUBLANES, CONV_WIDTH), F32),
        jax.ShapeDtypeStruct((n, 2 * D_MODEL), BF16),
    ]
    out_specs = [
        pl.BlockSpec((TM_PROJ, GROUP_WIDTH), row),
        pl.BlockSpec((TM_PROJ, 2 * GROUP_WIDTH), row),
        pl.BlockSpec((None, d1, TM_PROJ // d1, GROUP_WIDTH), res),
        pl.BlockSpec((None, d1, TM_PROJ // d1, 2 * GROUP_WIDTH), res),
        pl.BlockSpec((None, d2, TM_PROJ // d2, GROUP_WIDTH), res),
        pl.BlockSpec((None, d2, TM_PROJ // d2, 2 * GROUP_WIDTH), res),
        pl.BlockSpec((TM_PROJ, CONV_WIDTH), row),
        pl.BlockSpec((None, SUBLANES, CONV_WIDTH), lambda i: (i, 0, 0)),
        pl.BlockSpec((TM_PROJ, 2 * D_MODEL), row),
    ]
    return pl.pallas_call(
        _proj_kernel,
        grid=(n // TM_PROJ,),
        in_specs=[
            pl.BlockSpec((TM_PROJ, D_MODEL), row),
            pl.BlockSpec((1, D_MODEL), const),
            pl.BlockSpec((D_MODEL, IN_COLS), const, pipeline_mode=pl.Buffered(1)),
            pl.BlockSpec((1, 2 * D_MODEL), const),
            pl.BlockSpec((3, CONV_WIDTH), const),
        ],
        out_specs=out_specs,
        out_shape=out_shape,
        scratch_shapes=[pltpu.VMEM((6, TM_PROJ, LANES), F32)],
        compiler_params=pltpu.CompilerParams(dimension_semantics=("arbitrary",),
                                             vmem_limit_bytes=VMEM_LIMIT),
        name="proj",
    )(x2, g, w_in, b_gate, conv_w)


def _attn_sub_block(q_sub, kw, vw, bias_ref, g, lo, hi):
    assert KB == GROUP_WIDTH
    lane = lax.broadcasted_iota(jnp.int32, (QB, KB), 1)
    edge_ok = (lane >= lo) & (lane < hi)
    heads = [(lane >= h * HEAD_DIM) & (lane < (h + 1) * HEAD_DIM) for h in range(HEADS_PER_GROUP)]
    zero = jnp.zeros((), BF16)
    q_stack = jnp.concatenate([jnp.where(hm, q_sub, zero) for hm in heads], axis=0)
    s_all = lax.dot_general(q_stack, kw, (((1,), (1,)), ((), ())), preferred_element_type=F32)
    probs = []
    m_b = l_b = None
    for h, hm in enumerate(heads):
        s = s_all[h * QB:(h + 1) * QB] + bias_ref[g * HEADS_PER_GROUP + h]
        s = jnp.where(edge_ok, s, MASK_VALUE)
        m = jnp.max(s, axis=1, keepdims=True)
        p = jnp.exp(s - m)
        l = jnp.sum(p, axis=1, keepdims=True)
        probs.append(p.astype(BF16))
        m_b = jnp.broadcast_to(m, (QB, GROUP_WIDTH)) if m_b is None else jnp.where(hm, m, m_b)
        l_b = jnp.broadcast_to(l, (QB, GROUP_WIDTH)) if l_b is None else jnp.where(hm, l, l_b)
    o_all = jnp.dot(jnp.concatenate(probs, axis=0), vw, preferred_element_type=F32)
    acc = o_all[0:QB]
    for h in range(1, HEADS_PER_GROUP):
        acc = jnp.where(heads[h], o_all[h * QB:(h + 1) * QB], acc)
    return acc, m_b, l_b


def _attn_kernel(q0_ref, kv0_ref, kv0p_ref, kv0n_ref,
                 q1_ref, kv1_ref, kv1p_ref, kv1n_ref,
                 q2_ref, kv2_ref, kv2p_ref, kv2n_ref,
                 y_ref,
                 cat0, cat1, bias_ref, m_st, l_st, a_st, m_tmp, l_tmp, a_tmp, *, seq):
    j = pl.program_id(1)

    @pl.when((pl.program_id(0) == 0) & (j == 0))
    def _():
        qi = lax.broadcasted_iota(jnp.int32, (QB, KB), 0)
        kc = lax.broadcasted_iota(jnp.int32, (QB, KB), 1)
        adelta = jnp.abs(kc - HALF - qi)
        band = adelta <= HALF
        slopes = _alibi_slopes()
        for g in range(N_GROUPS_A):
            dist = (adelta * DILATED_PATTERNS[g][1]).astype(F32)
            for h in range(HEADS_PER_GROUP):
                bias_ref[g * HEADS_PER_GROUP + h] = jnp.where(band, -(float(slopes[g, h]) * dist), MASK_VALUE)

    for cat, own, prv, nxt in ((cat0, kv0_ref, kv0p_ref, kv0n_ref),
                               (cat1, kv1_ref, kv1p_ref, kv1n_ref)):
        n_own = own.shape[-2]
        cat[:, 0:HALF, :] = prv[...].reshape(cat.shape[0], HALF, 2 * GROUP_WIDTH)
        cat[:, HALF:HALF + n_own, :] = own[...].reshape(cat.shape[0], n_own, 2 * GROUP_WIDTH)
        cat[:, HALF + n_own:, :] = nxt[...].reshape(cat.shape[0], HALF, 2 * GROUP_WIDTH)

    def window(cat, r, sb):
        rows = pl.ds(pl.multiple_of(sb * QB, QB), KB)
        return cat[r, rows, 0:GROUP_WIDTH], cat[r, rows, GROUP_WIDTH:]

    def edges(g, n_res, sb):
        length = seq // DILATED_PATTERNS[g][1]
        i0 = j * n_res + sb * QB
        return jnp.maximum(0, HALF - i0), jnp.minimum(KB, length + HALF - i0)

    def body0(sb, carry):
        rows = pl.ds(pl.multiple_of(sb * QB, QB), QB)
        kw, vw = window(cat0, 0, sb)
        lo, hi = edges(0, T_ATT, sb)
        acc, m_b, l_b = _attn_sub_block(q0_ref[rows, :], kw, vw, bias_ref, 0, lo, hi)
        for c in range(2):
            cols = slice(c * LANES, (c + 1) * LANES)
            m_st[c, rows, :] = m_b[:, cols]
            l_st[c, rows, :] = l_b[:, cols]
            a_st[c, rows, :] = acc[:, cols]
        return carry

    lax.fori_loop(0, T_ATT // QB, body0, 0, unroll=ATT_UNROLL)

    assert T_ATT // DILATED_PATTERNS[2][1] == QB
    for g, q_ref, cat in ((1, q1_ref, cat1), (2, q2_ref, None)):
        d = DILATED_PATTERNS[g][1]
        n_res = T_ATT // d
        sb_per_res = n_res // QB

        def body(idx, carry, g=g, q_ref=q_ref, cat=cat, n_res=n_res, sb_per_res=sb_per_res):
            r = idx // sb_per_res
            sb = idx % sb_per_res
            if cat is None:
                kv = jnp.concatenate([kv2p_ref[r], kv2_ref[r], kv2n_ref[r]], axis=0)
                kw, vw = kv[:, 0:GROUP_WIDTH], kv[:, GROUP_WIDTH:]
            else:
                kw, vw = window(cat, r, sb)
            lo, hi = edges(g, n_res, sb)
            q_sub = q_ref[r, pl.ds(pl.multiple_of(sb * QB, QB), QB), :]
            acc, m_b, l_b = _attn_sub_block(q_sub, kw, vw, bias_ref, g, lo, hi)
            rows = pl.ds(pl.multiple_of(idx * QB, QB), QB)
            m_tmp[rows, :] = m_b
            l_tmp[rows, :] = l_b
            a_tmp[rows, :] = acc
            return carry

        lax.fori_loop(0, T_ATT // QB, body, 0, unroll=ATT_UNROLL)

        for r in range(d):
            for ch in range(sb_per_res):
                src = slice(r * n_res + ch * QB, r * n_res + (ch + 1) * QB)
                tok = pl.ds(ch * QB * d + r, QB, stride=d)
                for c in range(2):
                    cols = slice(c * LANES, (c + 1) * LANES)
                    m_new_part = m_tmp[src, cols]
                    m_old = m_st[c, tok, :]
                    m_new = jnp.maximum(m_old, m_new_part)
                    e_old = jnp.exp(m_old - m_new)
                    e_new = jnp.exp(m_new_part - m_new)
                    l_new = e_old * l_st[c, tok, :] + e_new * l_tmp[src, cols]
                    a_new = e_old * a_st[c, tok, :] + e_new * a_tmp[src, cols]
                    if g == N_GROUPS_A - 1:
                        a_st[c, tok, :] = a_new / l_new
                    else:
                        m_st[c, tok, :] = m_new
                        l_st[c, tok, :] = l_new
                        a_st[c, tok, :] = a_new

    for c in range(2):
        y_ref[:, c * LANES:(c + 1) * LANES] = a_st[c].astype(BF16)


def _attn_call(q0, kv0, q1, kv1, q2, kv2, batch, seq):
    n = q0.shape[0]
    tiles = seq // T_ATT
    specs = []
    scratch = []
    blocks_per_tile = T_ATT // HALF
    n_half_blocks = n // HALF
    specs += [
        pl.BlockSpec((T_ATT, GROUP_WIDTH), lambda b, j: (b * tiles + j, 0)),
        pl.BlockSpec((T_ATT, 2 * GROUP_WIDTH), lambda b, j: (b * tiles + j, 0)),
        pl.BlockSpec((HALF, 2 * GROUP_WIDTH),
                     lambda b, j: (jnp.maximum((b * tiles + j) * blocks_per_tile - 1, 0), 0)),
        pl.BlockSpec((HALF, 2 * GROUP_WIDTH),
                     lambda b, j: (jnp.minimum((b * tiles + j + 1) * blocks_per_tile, n_half_blocks - 1), 0)),
    ]
    scratch.append(pltpu.VMEM((1, T_ATT + 2 * HALF, 2 * GROUP_WIDTH), BF16))
    for g in (1, 2):
        d = DILATED_PATTERNS[g][1]
        n_res = T_ATT // d
        per_tile = n_res // HALF
        last = seq // d // HALF - 1
        specs += [
            pl.BlockSpec((None, d, n_res, GROUP_WIDTH), lambda b, j: (b, 0, j, 0)),
            pl.BlockSpec((None, d, n_res, 2 * GROUP_WIDTH), lambda b, j: (b, 0, j, 0)),
            pl.BlockSpec((None, d, HALF, 2 * GROUP_WIDTH),
                         lambda b, j, per_tile=per_tile: (b, 0, jnp.maximum(j * per_tile - 1, 0), 0)),
            pl.BlockSpec((None, d, HALF, 2 * GROUP_WIDTH),
                         lambda b, j, per_tile=per_tile, last=last: (b, 0, jnp.minimum((j + 1) * per_tile, last), 0)),
        ]
        if n_res > QB:
            scratch.append(pltpu.VMEM((d, n_res + 2 * HALF, 2 * GROUP_WIDTH), BF16))
    scratch.append(pltpu.VMEM((N_HEADS_A, QB, KB), F32))
    scratch += [pltpu.VMEM((2, T_ATT, LANES), F32) for _ in range(3)]
    scratch += [pltpu.VMEM((T_ATT, GROUP_WIDTH), F32) for _ in range(3)]
    return pl.pallas_call(
        functools.partial(_attn_kernel, seq=seq),
        grid=(batch, tiles),
        in_specs=specs,
        out_specs=pl.BlockSpec((T_ATT, GROUP_WIDTH), lambda b, j: (b * tiles + j, 0)),
        out_shape=jax.ShapeDtypeStruct((n, GROUP_WIDTH), BF16),
        scratch_shapes=scratch,
        compiler_params=pltpu.CompilerParams(dimension_semantics=("arbitrary", "arbitrary"),
                                             vmem_limit_bytes=VMEM_LIMIT),
        name="attn",
    )(q0, kv0, kv0, kv0, q1, kv1, kv1, kv1, q2, kv2, kv2, kv2)


def _split_dot(a, w_cat, w_hi):
    a_hi = a.astype(BF16)
    a_lo = (a - a_hi.astype(F32)).astype(BF16)
    both = jnp.dot(a_hi, w_cat, preferred_element_type=F32)
    return both[:, 0:LANES] + both[:, LANES:] + jnp.dot(a_lo, w_hi, preferred_element_type=F32)


ST_BASE, ST_TILE, ST_FREE = 0, 1, 2
AL_NEW = 0
RT_POS = 4


def _mix_kernel(x_ref, ya_ref, yb_ref, edge_ref, edgep_ref, edgen_ref, gate_ref,
                wa_ref, wc_ref, wo_ref, g2_ref, wr_cat_ref, wr_hi_ref, br_ref,
                x1_ref, route_ref, routet_ref, alloc_ref, xs_hbm,
                st_ref, hb0, hb1, posv, poss, psem, dsem, *, seq, trash_row):
    i = pl.program_id(0)
    last_step = pl.num_programs(0) - 1
    hbs = (hb0, hb1)

    def rows_done(h):
        for k in range(2):
            pltpu.make_async_copy(hbs[h], xs_hbm.at[pl.ds(0, TM_MIX * SUBLANES)], dsem.at[h, k]).wait()

    def send_row(h, j, first_sublane):
        src = hbs[h].at[pl.ds(first_sublane, SUBLANES)]
        for k in range(2):
            dst = xs_hbm.at[pl.ds(pl.multiple_of(poss[h, RT_POS + k, j], SUBLANES), SUBLANES)]
            pltpu.make_async_copy(src, dst, dsem.at[h, k]).start(priority=1)

    def send_rows(h):
        def one(j, carry):
            send_row(h, j, pl.multiple_of(j * SUBLANES, SUBLANES))
            return carry
        lax.fori_loop(0, TM_MIX, one, 0, unroll=8)

    def positions_to_smem(h):
        return pltpu.make_async_copy(posv, poss.at[h], psem.at[h])

    @pl.when(i == 0)
    def _():
        st_ref[...] = jnp.zeros_like(st_ref)
        hb1[...] = jnp.zeros_like(hb1)
        spare = (trash_row + lax.broadcasted_iota(jnp.int32, posv.shape, 1)
                 + jnp.where(lax.broadcasted_iota(jnp.int32, posv.shape, 0) == RT_POS + 1, TM_MIX, 0))
        posv[...] = spare * SUBLANES
        positions_to_smem(1).start()

    def tile(h):
        rows = pl.ds(h * TM_MIX, TM_MIX)
        t0 = (i * MIX_TILES + h) * TM_MIX
        positions_to_smem(1 - h).wait()

        def send_other(part):
            for j in range(part * TM_MIX // 4, (part + 1) * TM_MIX // 4):
                send_row(1 - h, j, j * SUBLANES)

        yb_in = yb_ref[rows, :]
        patch = 16
        if h == 0:
            add = jnp.where(t0 % seq == 0, 0.0, edge_ref[0:1, :] * edgep_ref[3:4, :])
            top = yb_in[0:patch, :].astype(F32)
            top = jnp.where(lax.broadcasted_iota(jnp.int32, top.shape, 0) == 0, top + add, top)
            yb_in = jnp.concatenate([top.astype(BF16), yb_in[patch:, :]], axis=0)
        if h == MIX_TILES - 1:
            add = jnp.where((t0 + TM_MIX) % seq == 0, 0.0, edge_ref[1:2, :] * edgen_ref[2:3, :])
            bot = yb_in[TM_MIX - patch:, :].astype(F32)
            bot = jnp.where(lax.broadcasted_iota(jnp.int32, bot.shape, 0) == patch - 1, bot + add, bot)
            yb_in = jnp.concatenate([yb_in[:TM_MIX - patch, :], bot.astype(BF16)], axis=0)
        send_other(0)

        y_a = jnp.dot(ya_ref[rows, :], wa_ref[...], preferred_element_type=F32)
        send_other(1)
        y_b = jnp.dot(yb_in, wc_ref[...], preferred_element_type=F32)
        merged = gate_ref[rows, 0:D_MODEL] * y_a.astype(BF16) + gate_ref[rows, D_MODEL:] * y_b.astype(BF16)
        x1 = x_ref[rows, :] + jnp.dot(merged, wo_ref[...], preferred_element_type=F32)
        x1_ref[rows, :] = x1

        h2 = _rms(x1, g2_ref[...])
        if h == 0:
            pl.when(i > 0)(lambda: rows_done(0))
        else:
            rows_done(h)
        for c in range(ROW_CHUNKS):
            hbs[h][pl.ds(c, TM_MIX, stride=ROW_CHUNKS), :] = h2[:, c * LANES:(c + 1) * LANES]

        logits = _split_dot(h2, wr_cat_ref[...], wr_hi_ref[...]) + br_ref[...]
        send_other(2)
        send_other(3)
        lane = lax.broadcasted_iota(jnp.int32, (TM_MIX, LANES), 1)
        lane_f = lane.astype(F32)
        neg = -jnp.inf
        big = float(LANES)
        is_group = lane < N_EXPERT_GROUPS
        cm = jnp.where(is_group, logits, neg)
        cmax = jnp.max(cm, axis=1, keepdims=True)
        g_idx = jnp.min(jnp.where(cm == cmax, lane_f, big), axis=1, keepdims=True)
        p_group = 1.0 / jnp.sum(jnp.where(is_group, jnp.exp(logits - cmax), 0.0), axis=1, keepdims=True)
        f_lo = N_EXPERT_GROUPS + EXPERTS_PER_GROUP * g_idx
        in_group = (lane_f >= f_lo) & (lane_f < f_lo + EXPERTS_PER_GROUP)
        fm = jnp.where(in_group, logits, neg)
        f1 = jnp.max(fm, axis=1, keepdims=True)
        i1 = jnp.min(jnp.where(fm == f1, lane_f, big), axis=1, keepdims=True)
        fm2 = jnp.where(lane_f == i1, neg, fm)
        f2 = jnp.max(fm2, axis=1, keepdims=True)
        i2 = jnp.min(jnp.where(fm2 == f2, lane_f, big), axis=1, keepdims=True)
        e21 = jnp.exp(f2 - f1)
        w_1 = p_group / (1.0 + e21)
        w_2 = p_group * e21 / (1.0 + e21)
        e_1 = i1 - N_EXPERT_GROUPS
        e_2 = i2 - N_EXPERT_GROUPS

        onehot = jnp.where((lane_f == e_1) | (lane_f == e_2), 1.0, 0.0)
        r_i = lax.broadcasted_iota(jnp.int32, (TM_MIX, TM_MIX), 0)
        c_i = lax.broadcasted_iota(jnp.int32, (TM_MIX, TM_MIX), 1)
        tri = jnp.where(c_i < r_i, 1.0, 0.0).astype(BF16)
        base = st_ref[ST_BASE:ST_BASE + 1, :]
        before = jnp.dot(tri, onehot.astype(BF16), preferred_element_type=F32) + base

        tile_rows = float(TM_MOE)
        cur_tile = st_ref[ST_TILE:ST_TILE + 1, :]
        next_free = st_ref[ST_FREE:ST_FREE + 1, :]
        count = jnp.sum(onehot, axis=0, keepdims=True)
        slot0 = jnp.floor(base * (1.0 / tile_rows))
        partial = (base - slot0 * tile_rows) > 0.0
        slot_last = jnp.floor((base + count - 1.0) * (1.0 / tile_rows))
        n_new = jnp.where(count > 0.0, slot_last - slot0 + 1.0 - jnp.where(partial, 1.0, 0.0), 0.0)
        e_r = lax.broadcasted_iota(jnp.int32, (LANES, LANES), 0)
        e_c = lax.broadcasted_iota(jnp.int32, (LANES, LANES), 1)
        earlier = jnp.where(e_r < e_c, 1.0, 0.0).astype(BF16)
        new_before = jnp.dot(jnp.broadcast_to(n_new, (SUBLANES, LANES)).astype(BF16), earlier,
                             preferred_element_type=F32)[0:1, :]
        fresh = next_free + new_before - jnp.where(partial, 1.0, 0.0) - slot0
        partial_slot = jnp.where(partial, slot0, -1.0)

        def tile_of(slot_idx, fresh_v, cur_v, partial_v):
            return jnp.where(slot_idx == partial_v, cur_v, fresh_v + slot_idx)

        s_all = jnp.floor(before * (1.0 / tile_rows))
        pos_all = tile_of(s_all, fresh, cur_tile, partial_slot) * tile_rows + (before - s_all * tile_rows)
        positions = [jnp.sum(jnp.where(lane_f == e, pos_all, 0.0), axis=1, keepdims=True) for e in (e_1, e_2)]

        st_ref[ST_BASE:ST_BASE + 1, :] = base + count
        st_ref[ST_TILE:ST_TILE + 1, :] = jnp.where(count > 0.0, tile_of(slot_last, fresh, cur_tile, partial_slot),
                                                    cur_tile)
        st_ref[ST_FREE:ST_FREE + 1, :] = next_free + jnp.sum(n_new, axis=1, keepdims=True)
        alloc_ref[h] = jnp.broadcast_to(n_new, (SUBLANES, LANES))

        route = jnp.zeros((TM_MIX, LANES), F32)
        for k, val in enumerate((e_1, e_2, w_1, w_2, positions[0], positions[1])):
            route = jnp.where(lane == k, val, route)
        route_ref[rows, :] = route
        route_t = route.T[0:SUBLANES, :]
        routet_ref[:, h * TM_MIX:(h + 1) * TM_MIX] = route_t
        posv[...] = route_t.astype(jnp.int32) * SUBLANES
        positions_to_smem(h).start()

    tile(0)
    tile(1)

    @pl.when(i == last_step)
    def _():
        positions_to_smem(1).wait()
        rows_done(0)
        send_rows(1)
        rows_done(1)

        posv[:, 0:LANES] = st_ref[...].astype(jnp.int32)
        state = positions_to_smem(0)
        state.start()
        state.wait()
        hb0[...] = jnp.zeros_like(hb0)

        def zero_jobs():
            for e in range(N_EXPERTS):
                fill = poss[0, ST_BASE, e] & (TM_MOE - 1)
                first = poss[0, ST_TILE, e] * TM_MOE
                at = fill
                size = 1
                while size < TM_MOE:
                    take = (fill > 0) & ((at & size) != 0)
                    yield take, first + at, size
                    at = at + jnp.where(take, size, 0)
                    size *= 2
            for t in range(N_EXPERTS):
                tile = poss[0, ST_FREE, 0] + t
                yield tile < trash_row // TM_MOE, tile * TM_MOE, TM_MOE

        def zero_copy(first, rows):
            return pltpu.make_async_copy(
                hb0.at[pl.ds(0, rows * SUBLANES)],
                xs_hbm.at[pl.ds(pl.multiple_of(first * SUBLANES, SUBLANES), rows * SUBLANES)], psem.at[0])

        for take, first, rows in zero_jobs():
            pl.when(take)(lambda first=first, rows=rows: zero_copy(first, rows).start())
        for take, first, rows in zero_jobs():
            pl.when(take)(lambda first=first, rows=rows: zero_copy(first, rows).wait())


def _mix_call(x2, y_attn, yb_in, edges, gates, wa, wc, wo, g2, wr_cat, wr_hi, br, seq, n_tiles):
    n = x2.shape[0]
    step_rows = MIX_TILES * TM_MIX
    assert step_rows == TM_PROJ
    n_steps = n // step_rows
    row = lambda i: (i, 0)
    const = lambda i: (0, 0)
    edge = lambda shift: pl.BlockSpec((None, SUBLANES, CONV_WIDTH),
                                      lambda i: (jnp.clip(i + shift, 0, n_steps - 1), 0, 0))
    xs_rows = n_tiles * TM_MOE + 2 * TM_MIX
    return pl.pallas_call(
        functools.partial(_mix_kernel, seq=seq, trash_row=n_tiles * TM_MOE),
        grid=(n_steps,),
        in_specs=[
            pl.BlockSpec((step_rows, D_MODEL), row),
            pl.BlockSpec((step_rows, GROUP_WIDTH), row),
            pl.BlockSpec((step_rows, CONV_WIDTH), row),
            edge(0), edge(-1), edge(1),
            pl.BlockSpec((step_rows, 2 * D_MODEL), row),
            pl.BlockSpec((GROUP_WIDTH, D_MODEL), const),
            pl.BlockSpec((CONV_WIDTH, D_MODEL), const),
            pl.BlockSpec((D_MODEL, D_MODEL), const),
            pl.BlockSpec((1, D_MODEL), const),
            pl.BlockSpec((D_MODEL, 2 * LANES), const),
            pl.BlockSpec((D_MODEL, LANES), const),
            pl.BlockSpec((1, LANES), const),
        ],
        out_specs=[
            pl.BlockSpec((step_rows, D_MODEL), row),
            pl.BlockSpec((step_rows, LANES), row),
            pl.BlockSpec((SUBLANES, step_rows), lambda i: (0, i)),
            pl.BlockSpec((MIX_TILES, SUBLANES, LANES), lambda i: (i, 0, 0)),
            pl.BlockSpec(memory_space=pl.ANY),
        ],
        out_shape=[
            jax.ShapeDtypeStruct((n, D_MODEL), F32),
            jax.ShapeDtypeStruct((n, LANES), F32),
            jax.ShapeDtypeStruct((SUBLANES, n), F32),
            jax.ShapeDtypeStruct((n // TM_MIX, SUBLANES, LANES), F32),
            jax.ShapeDtypeStruct((xs_rows * ROW_CHUNKS, LANES), F32),
        ],
        scratch_shapes=[pltpu.VMEM((SUBLANES, LANES), F32),
                        pltpu.VMEM((TM_MIX * ROW_CHUNKS, LANES), F32),
                        pltpu.VMEM((TM_MIX * ROW_CHUNKS, LANES), F32),
                        pltpu.VMEM((SUBLANES, TM_MIX), jnp.int32),
                        pltpu.SMEM((MIX_TILES, SUBLANES, TM_MIX), jnp.int32),
                        pltpu.SemaphoreType.DMA((MIX_TILES,)),
                        pltpu.SemaphoreType.DMA((MIX_TILES, 2))],
        compiler_params=pltpu.CompilerParams(dimension_semantics=("arbitrary",),
                                             vmem_limit_bytes=VMEM_LIMIT),
        name="mix",
    )(x2, y_attn, yb_in, edges, edges, edges, gates, wa, wc, wo, g2, wr_cat, wr_hi, br)


def _row_gather(idx_ref, n_rows, src_hbm, dst, sem):
    def issue(pair, carry):
        for k in range(2):
            j = 2 * pair + k
            t = idx_ref[0, 0, j]
            pltpu.make_async_copy(src_hbm.at[pl.ds(pl.multiple_of(t * SUBLANES, SUBLANES), SUBLANES)],
                                  dst.at[pl.ds(pl.multiple_of(j * SUBLANES, SUBLANES), SUBLANES)],
                                  sem).start(priority=k)
        return carry
    lax.fori_loop(0, n_rows // 2, issue, 0, unroll=4)


def _row_gather_wait(n_rows, src_hbm, dst, sem):
    pltpu.make_async_copy(src_hbm.at[pl.ds(0, n_rows * SUBLANES)], dst, sem).wait()


def _rows_from_tiles(buf, first_row, n_rows):
    return jnp.concatenate(
        [buf[pl.ds(first_row * ROW_CHUNKS + c, n_rows, stride=ROW_CHUNKS), :] for c in range(ROW_CHUNKS)],
        axis=1)


def _expert_kernel(order_ref, te_ref, next_ref, nused_ref, xs_hbm, w1_hbm, w3_hbm, w2_hbm, y_ref,
                   xin, w1s, w3s, w2s, w13b, w2b, isem, wsem):
    i = pl.program_id(0)
    n_used = nused_ref[0]
    slot = i % XIN_SLOTS
    used = i < n_used

    def fetch(step):
        rows = pl.ds(order_ref[step] * TM_MOE, TM_MOE)
        s = step % XIN_SLOTS
        return [pltpu.make_async_copy(xs_hbm.at[rows, c, :], xin.at[s, c], isem.at[s]) for c in range(ROW_CHUNKS)]

    def weights(e):
        return [pltpu.make_async_copy(w_hbm.at[e], stage, wsem.at[k])
                for k, (w_hbm, stage) in enumerate(((w1_hbm, w1s), (w3_hbm, w3s), (w2_hbm, w2s)))]

    def start(copies):
        for cp in copies:
            cp.start()

    def wait(copies):
        for cp in copies:
            cp.wait()

    @pl.when(i == 0)
    def _():
        start(weights(te_ref[0]))
        start(fetch(0))
        pl.when(n_used > 1)(lambda: start(fetch(1)))

    pl.when(i + 2 < n_used)(lambda: start(fetch(i + 2)))

    @pl.when(jnp.logical_not(used))
    def _():
        y_ref[...] = jnp.zeros_like(y_ref)

    @pl.when(used & ((i == 0) | (te_ref[i] != te_ref[jnp.maximum(i - 1, 0)])))
    def _():
        wait(weights(te_ref[i]))
        w13b[:, 0:EXPERT_FF] = w1s[...].astype(BF16)
        w13b[:, EXPERT_FF:] = w3s[...].astype(BF16)
        w2b[...] = w2s[...].astype(BF16)
        pl.when(next_ref[i] != te_ref[i])(lambda: start(weights(next_ref[i])))

    @pl.when(used)
    def _():
        wait(fetch(i))
        x = jnp.concatenate([xin[slot, c] for c in range(ROW_CHUNKS)], axis=1)
        ab = jnp.dot(x.astype(BF16), w13b[...], preferred_element_type=F32)
        a = ab[:, 0:EXPERT_FF]
        hid = (a * jax.nn.sigmoid(a) * ab[:, EXPERT_FF:]).astype(BF16)
        y = jnp.dot(hid, w2b[...], preferred_element_type=F32)
        for c in range(ROW_CHUNKS):
            y_ref[pl.ds(c, TM_MOE, stride=ROW_CHUNKS), :] = y[:, c * LANES:(c + 1) * LANES]


def _expert_call(order, tile_expert, next_expert, n_used, xs_rows, w1, w3, w2):
    n_tiles = order.shape[0]
    any_space = pl.BlockSpec(memory_space=pl.ANY)
    grid_spec = pltpu.PrefetchScalarGridSpec(
        num_scalar_prefetch=4,
        grid=(n_tiles,),
        in_specs=[any_space, any_space, any_space, any_space],
        out_specs=pl.BlockSpec((TM_MOE * ROW_CHUNKS, LANES), lambda i, od, te, nx, nu: (od[i], 0)),
        scratch_shapes=[pltpu.VMEM((XIN_SLOTS, ROW_CHUNKS, TM_MOE, LANES), F32),
                        pltpu.VMEM((D_MODEL, EXPERT_FF), F32),
                        pltpu.VMEM((D_MODEL, EXPERT_FF), F32),
                        pltpu.VMEM((EXPERT_FF, D_MODEL), F32),
                        pltpu.VMEM((D_MODEL, 2 * EXPERT_FF), BF16),
                        pltpu.VMEM((EXPERT_FF, D_MODEL), BF16),
                        pltpu.SemaphoreType.DMA((XIN_SLOTS,)),
                        pltpu.SemaphoreType.DMA((3,))],
    )
    return pl.pallas_call(
        _expert_kernel,
        grid_spec=grid_spec,
        out_shape=jax.ShapeDtypeStruct((n_tiles * TM_MOE * ROW_CHUNKS, LANES), F32),
        compiler_params=pltpu.CompilerParams(dimension_semantics=("arbitrary",),
                                             vmem_limit_bytes=VMEM_LIMIT),
        name="experts",
    )(order, tile_expert, next_expert, n_used, xs_rows, w1, w3, w2)


def _combine_kernel(pos_ref, posn_ref, y_hbm, x1_ref, route_ref, g_ref, o_ref, buf0, buf1, sem):
    i = pl.program_id(0)
    n_steps = pl.num_programs(0)
    bufs = (buf0, buf1)

    @pl.when(i == 0)
    def _():
        _row_gather(pos_ref, 2 * TM_CMB, y_hbm, buf0, sem.at[0])

    for slot in range(2):
        @pl.when((i % 2 == slot) & (i + 1 < n_steps))
        def _(slot=slot):
            _row_gather(posn_ref, 2 * TM_CMB, y_hbm, bufs[1 - slot], sem.at[1 - slot])

    for slot in range(2):
        @pl.when(i % 2 == slot)
        def _(slot=slot):
            _row_gather_wait(2 * TM_CMB, y_hbm, bufs[slot], sem.at[slot])
            y_1 = _rows_from_tiles(bufs[slot], 0, TM_CMB)
            y_2 = _rows_from_tiles(bufs[slot], TM_CMB, TM_CMB)
            x = x1_ref[...] + route_ref[:, 2:3] * y_1 + route_ref[:, 3:4] * y_2
            o_ref[...] = _rms(x, g_ref[...])


def _combine_call(pos, y_flat, x1, route, g):
    n = x1.shape[0]
    n_steps = n // TM_CMB
    row = lambda i: (i, 0)
    return pl.pallas_call(
        _combine_kernel,
        grid=(n_steps,),
        in_specs=[
            pl.BlockSpec((1, 1, 2 * TM_CMB), lambda i: (i, 0, 0), memory_space=pltpu.SMEM),
            pl.BlockSpec((1, 1, 2 * TM_CMB), lambda i: (jnp.minimum(i + 1, n_steps - 1), 0, 0),
                         memory_space=pltpu.SMEM),
            pl.BlockSpec(memory_space=pl.ANY),
            pl.BlockSpec((TM_CMB, D_MODEL), row),
            pl.BlockSpec((TM_CMB, LANES), row),
            pl.BlockSpec((1, D_MODEL), lambda i: (0, 0)),
        ],
        out_specs=pl.BlockSpec((TM_CMB, D_MODEL), row),
        out_shape=jax.ShapeDtypeStruct((n, D_MODEL), F32),
        scratch_shapes=[pltpu.VMEM((2 * TM_CMB * ROW_CHUNKS, LANES), F32),
                        pltpu.VMEM((2 * TM_CMB * ROW_CHUNKS, LANES), F32),
                        pltpu.SemaphoreType.DMA((2,))],
        compiler_params=pltpu.CompilerParams(dimension_semantics=("arbitrary",),
                                             vmem_limit_bytes=VMEM_LIMIT),
        name="combine",
    )(pos, pos, y_flat, x1, route, g)


def _layer(x2, batch, seq, norm_mix_g, w_in, b_gate, conv_w, w_attn_out, w_conv_out, w_out, norm_ffn_g,
           w_route_group, b_route_group, w_route_expert, b_route_expert, w1, w3, w2, final_g):
    n = x2.shape[0]
    q0, kv0, q1, kv1, q2, kv2, yb_in, edges, gates = _proj_call(
        x2, norm_mix_g[None, :], w_in.astype(BF16), b_gate[None, :], conv_w, batch, seq)
    y_attn = _attn_call(q0, kv0, q1, kv1, q2, kv2, batch, seq)

    n_route = N_EXPERT_GROUPS + N_EXPERTS
    w_route = jnp.pad(jnp.concatenate([w_route_group, w_route_expert], axis=1), ((0, 0), (0, LANES - n_route)))
    b_route = jnp.pad(jnp.concatenate([b_route_group, b_route_expert]), (0, LANES - n_route))[None, :]
    wr_hi = w_route.astype(BF16)
    wr_lo = (w_route - wr_hi.astype(F32)).astype(BF16)
    n_tiles = (2 * n) // TM_MOE + N_EXPERTS
    x1, route, route_t, alloc, xs_flat = _mix_call(
        x2, y_attn, yb_in, edges, gates, w_attn_out.astype(BF16), w_conv_out.astype(BF16), w_out.astype(BF16),
        norm_ffn_g[None, :], jnp.concatenate([wr_hi, wr_lo], axis=1), wr_hi, b_route, seq, n_tiles)

    i32 = jnp.int32
    taken = alloc[:, AL_NEW, :N_EXPERTS].astype(i32).reshape(-1)
    k = jnp.arange(taken.shape[0], dtype=i32)
    running = jnp.sum(jnp.where(k[:, None] >= k[None, :], taken[None, :], 0), axis=1)
    n_used = running[-1:]
    tile = jnp.arange(n_tiles, dtype=i32)
    owner = jnp.sum((running[None, :] <= tile[:, None]).astype(i32), axis=1) % N_EXPERTS
    owner = jnp.where(tile < n_used[0], owner, N_EXPERTS)
    key = owner * n_tiles + tile
    place = jnp.sum((key[None, :] < key[:, None]).astype(i32), axis=1)
    at = place[None, :] == tile[:, None]
    order = jnp.sum(jnp.where(at, tile[None, :], 0), axis=1)
    step_owner = jnp.sum(jnp.where(at, owner[None, :], 0), axis=1)
    step_expert = jnp.minimum(step_owner, N_EXPERTS - 1)
    later = (step_owner[None, :] > step_owner[:, None]) & (step_owner[None, :] < N_EXPERTS)
    next_expert = jnp.min(jnp.where(later, step_owner[None, :], N_EXPERTS), axis=1)
    next_expert = jnp.where(next_expert < N_EXPERTS, next_expert, step_expert)

    y_flat = _expert_call(order, step_expert, next_expert, n_used, xs_flat.reshape(-1, ROW_CHUNKS, LANES),
                          w1, w3, w2)
    pos = route_t[RT_POS:RT_POS + 2].astype(i32)
    pos_tiles = pos.reshape(2, n // TM_CMB, TM_CMB).transpose(1, 0, 2).reshape(n // TM_CMB, 1, 2 * TM_CMB)
    return _combine_call(pos_tiles, y_flat, x1, route, final_g[None, :])


def kernel(x, norm_mix_g, w_in, b_gate, conv_w, w_attn_out, w_conv_out, w_out, norm_ffn_g,
           w_route_group, b_route_group, w_route_expert, b_route_expert, w1, w3, w2, norm_final_g):
    batch, seq, d = x.shape
    depth = w_in.shape[0]
    assert d == D_MODEL and depth == 1 and seq % T_ATT == 0
    out = _layer(x.reshape(batch * seq, d), batch, seq, norm_mix_g[0], w_in[0], b_gate[0], conv_w[0],
                 w_attn_out[0], w_conv_out[0], w_out[0], norm_ffn_g[0], w_route_group[0], b_route_group[0],
                 w_route_expert[0], b_route_expert[0], w1[0], w3[0], w2[0], norm_final_g)
    return out.reshape(batch, seq, d)
```

```python
import functools

import numpy as np
import jax
import jax.numpy as jnp
from jax import lax
from jax.experimental import pallas as pl
from jax.experimental.pallas import tpu as pltpu

F32 = jnp.float32
BF16 = jnp.bfloat16

D_MODEL = 1024
HEAD_DIM = 64
HEADS_PER_GROUP = 4
DILATED_PATTERNS = ((128, 1), (512, 4), (2048, 16))
N_GROUPS_A = 3
N_HEADS_A = N_GROUPS_A * HEADS_PER_GROUP
ATTN_WIDTH = N_HEADS_A * HEAD_DIM
GROUP_WIDTH = HEADS_PER_GROUP * HEAD_DIM
ALIBI_SPAN = 8.0
MASK_VALUE = -1e30
CONV_WIDTH = 768
N_EXPERT_GROUPS = 4
EXPERTS_PER_GROUP = 8
N_EXPERTS = 32
EXPERT_FF = 512
RMS_EPS = 1e-6

HALF = 64
LANES = 128
SUBLANES = 8
ROW_CHUNKS = D_MODEL // LANES

COL_K = ATTN_WIDTH
COL_V = 2 * ATTN_WIDTH
COL_BG = 3 * ATTN_WIDTH
COL_CG = COL_BG + CONV_WIDTH
COL_XIN = COL_CG + CONV_WIDTH
COL_GATE = COL_XIN + CONV_WIDTH
IN_COLS = COL_GATE + 2 * D_MODEL

TM_PROJ = 1024
T_ATT = 2048
QB = 128
KB = QB + 2 * HALF
ATT_UNROLL = 16
TM_MIX = 512
MIX_TILES = 2
TM_MOE = 512
XIN_SLOTS = 3
TM_CMB = 256

VMEM_LIMIT = 56 * 1024 * 1024


def _alibi_slopes():
    return np.array([2.0 ** (-ALIBI_SPAN * (i + 1) / N_HEADS_A) for i in range(N_HEADS_A)],
                    dtype=np.float32).reshape(N_GROUPS_A, HEADS_PER_GROUP)


def _rms(x, g):
    return x * lax.rsqrt(jnp.mean(x * x, axis=-1, keepdims=True) + RMS_EPS) * g


def _proj_kernel(x_ref, g_ref, w_ref, b_ref, cw_ref,
                 q0_ref, kv0_ref, q1_ref, kv1_ref, q2_ref, kv2_ref, yb_ref, edge_ref, gate_ref, scr):
    h = _rms(x_ref[...], g_ref[...]).astype(BF16)

    def proj(c0, width):
        return jnp.dot(h, w_ref[:, c0:c0 + width], preferred_element_type=F32)

    qscale = HEAD_DIM ** -0.5
    q0_ref[...] = (proj(0, GROUP_WIDTH) * qscale).astype(BF16)
    kv0_ref[:, 0:GROUP_WIDTH] = proj(COL_K, GROUP_WIDTH).astype(BF16)
    kv0_ref[:, GROUP_WIDTH:] = proj(COL_V, GROUP_WIDTH).astype(BF16)

    for g, q_ref, kv_ref in ((1, q1_ref, kv1_ref), (2, q2_ref, kv2_ref)):
        d = DILATED_PATTERNS[g][1]
        n = TM_PROJ // d
        parts = (proj(g * GROUP_WIDTH, GROUP_WIDTH) * qscale,
                 proj(COL_K + g * GROUP_WIDTH, GROUP_WIDTH),
                 proj(COL_V + g * GROUP_WIDTH, GROUP_WIDTH))
        for i, part in enumerate(parts):
            for c in range(2):
                scr[2 * i + c] = part[:, c * LANES:(c + 1) * LANES]
        for r in range(d):
            rows = pl.ds(r, n, stride=d)
            q_ref[r] = jnp.concatenate([scr[c, rows, :] for c in range(2)], axis=1).astype(BF16)
            kv_ref[r] = jnp.concatenate([scr[c, rows, :] for c in range(2, 6)], axis=1).astype(BF16)

    width = GROUP_WIDTH
    row = lax.broadcasted_iota(jnp.int32, (TM_PROJ, width), 0)
    for c0 in range(0, CONV_WIDTH, width):
        cols = slice(c0, c0 + width)
        b_gate = proj(COL_BG + c0, width)
        u = proj(COL_CG + c0, width) * proj(COL_XIN + c0, width)
        u_prev = jnp.where(row == 0, 0.0, pltpu.roll(u, 1, axis=0))
        u_next = jnp.where(row == TM_PROJ - 1, 0.0, pltpu.roll(u, TM_PROJ - 1, axis=0))
        conv = cw_ref[0:1, cols] * u_prev + cw_ref[1:2, cols] * u + cw_ref[2:3, cols] * u_next
        yb_ref[:, cols] = (b_gate * conv).astype(BF16)
        edge_ref[:, cols] = jnp.concatenate(
            [b_gate[0:1, :] * cw_ref[0:1, cols], b_gate[TM_PROJ - 1:TM_PROJ, :] * cw_ref[2:3, cols],
             u[0:1, :], u[TM_PROJ - 1:TM_PROJ, :], jnp.zeros((SUBLANES - 4, width), F32)], axis=0)
    for c in range(4):
        w = 2 * D_MODEL // 4
        z = proj(COL_GATE + c * w, w) + b_ref[:, c * w:(c + 1) * w]
        gate_ref[:, c * w:(c + 1) * w] = jax.nn.sigmoid(z).astype(BF16)


def _proj_call(x2, g, w_in, b_gate, conv_w, batch, seq):
    n = x2.shape[0]
    assert seq % TM_PROJ == 0
    steps_per_batch = seq // TM_PROJ
    d1, d2 = DILATED_PATTERNS[1][1], DILATED_PATTERNS[2][1]
    row = lambda i: (i, 0)
    res = lambda i: (i // steps_per_batch, 0, i % steps_per_batch, 0)
    const = lambda i: (0, 0)
    out_shape = [
        jax.ShapeDtypeStruct((n, GROUP_WIDTH), BF16),
        jax.ShapeDtypeStruct((n, 2 * GROUP_WIDTH), BF16),
        jax.ShapeDtypeStruct((batch, d1, seq // d1, GROUP_WIDTH), BF16),
        jax.ShapeDtypeStruct((batch, d1, seq // d1, 2 * GROUP_WIDTH), BF16),
        jax.ShapeDtypeStruct((batch, d2, seq // d2, GROUP_WIDTH), BF16),
        jax.ShapeDtypeStruct((batch, d2, seq // d2, 2 * GROUP_WIDTH), BF16),
        jax.ShapeDtypeStruct((n, CONV_WIDTH), BF16),
        jax.ShapeDtypeStruct((n // TM_PROJ, SUBLANES, CONV_WIDTH), F32),
        jax.ShapeDtypeStruct((n, 2 * D_MODEL), BF16),
    ]
    out_specs = [
        pl.BlockSpec((TM_PROJ, GROUP_WIDTH), row),
        pl.BlockSpec((TM_PROJ, 2 * GROUP_WIDTH), row),
        pl.BlockSpec((None, d1, TM_PROJ // d1, GROUP_WIDTH), res),
        pl.BlockSpec((None, d1, TM_PROJ // d1, 2 * GROUP_WIDTH), res),
        pl.BlockSpec((None, d2, TM_PROJ // d2, GROUP_WIDTH), res),
        pl.BlockSpec((None, d2, TM_PROJ // d2, 2 * GROUP_WIDTH), res),
        pl.BlockSpec((TM_PROJ, CONV_WIDTH), row),
        pl.BlockSpec((None, SUBLANES, CONV_WIDTH), lambda i: (i, 0, 0)),
        pl.BlockSpec((TM_PROJ, 2 * D_MODEL), row),
    ]
    return pl.pallas_call(
        _proj_kernel,
        grid=(n // TM_PROJ,),
        in_specs=[
            pl.BlockSpec((TM_PROJ, D_MODEL), row),
            pl.BlockSpec((1, D_MODEL), const),
            pl.BlockSpec((D_MODEL, IN_COLS), const, pipeline_mode=pl.Buffered(1)),
            pl.BlockSpec((1, 2 * D_MODEL), const),
            pl.BlockSpec((3, CONV_WIDTH), const),
        ],
        out_specs=out_specs,
        out_shape=out_shape,
        scratch_shapes=[pltpu.VMEM((6, TM_PROJ, LANES), F32)],
        compiler_params=pltpu.CompilerParams(dimension_semantics=("arbitrary",),
                                             vmem_limit_bytes=VMEM_LIMIT),
        name="proj",
    )(x2, g, w_in, b_gate, conv_w)


def _attn_sub_block(q_sub, kw, vw, bias_ref, g, lo, hi):
    assert KB == GROUP_WIDTH
    lane = lax.broadcasted_iota(jnp.int32, (QB, KB), 1)
    edge_ok = (lane >= lo) & (lane < hi)
    heads = [(lane >= h * HEAD_DIM) & (lane < (h + 1) * HEAD_DIM) for h in range(HEADS_PER_GROUP)]
    zero = jnp.zeros((), BF16)
    q_stack = jnp.concatenate([jnp.where(hm, q_sub, zero) for hm in heads], axis=0)
    s_all = lax.dot_general(q_stack, kw, (((1,), (1,)), ((), ())), preferred_element_type=F32)
    probs = []
    m_b = l_b = None
    for h, hm in enumerate(heads):
        s = s_all[h * QB:(h + 1) * QB] + bias_ref[g * HEADS_PER_GROUP + h]
        s = jnp.where(edge_ok, s, MASK_VALUE)
        m = jnp.max(s, axis=1, keepdims=True)
        p = jnp.exp(s - m)
        l = jnp.sum(p, axis=1, keepdims=True)
        probs.append(p.astype(BF16))
        m_b = jnp.broadcast_to(m, (QB, GROUP_WIDTH)) if m_b is None else jnp.where(hm, m, m_b)
        l_b = jnp.broadcast_to(l, (QB, GROUP_WIDTH)) if l_b is None else jnp.where(hm, l, l_b)
    o_all = jnp.dot(jnp.concatenate(probs, axis=0), vw, preferred_element_type=F32)
    acc = o_all[0:QB]
    for h in range(1, HEADS_PER_GROUP):
        acc = jnp.where(heads[h], o_all[h * QB:(h + 1) * QB], acc)
    return acc, m_b, l_b


def _attn_kernel(q0_ref, kv0_ref, kv0p_ref, kv0n_ref,
                 q1_ref, kv1_ref, kv1p_ref, kv1n_ref,
                 q2_ref, kv2_ref, kv2p_ref, kv2n_ref,
                 y_ref,
                 cat0, cat1, bias_ref, m_st, l_st, a_st, m_tmp, l_tmp, a_tmp, *, seq):
    j = pl.program_id(1)

    @pl.when((pl.program_id(0) == 0) & (j == 0))
    def _():
        qi = lax.broadcasted_iota(jnp.int32, (QB, KB), 0)
        kc = lax.broadcasted_iota(jnp.int32, (QB, KB), 1)
        adelta = jnp.abs(kc - HALF - qi)
        band = adelta <= HALF
        slopes = _alibi_slopes()
        for g in range(N_GROUPS_A):
            dist = (adelta * DILATED_PATTERNS[g][1]).astype(F32)
            for h in range(HEADS_PER_GROUP):
                bias_ref[g * HEADS_PER_GROUP + h] = jnp.where(band, -(float(slopes[g, h]) * dist), MASK_VALUE)

    for cat, own, prv, nxt in ((cat0, kv0_ref, kv0p_ref, kv0n_ref),
                               (cat1, kv1_ref, kv1p_ref, kv1n_ref)):
        n_own = own.shape[-2]
        cat[:, 0:HALF, :] = prv[...].reshape(cat.shape[0], HALF, 2 * GROUP_WIDTH)
        cat[:, HALF:HALF + n_own, :] = own[...].reshape(cat.shape[0], n_own, 2 * GROUP_WIDTH)
        cat[:, HALF + n_own:, :] = nxt[...].reshape(cat.shape[0], HALF, 2 * GROUP_WIDTH)

    def window(cat, r, sb):
        rows = pl.ds(pl.multiple_of(sb * QB, QB), KB)
        return cat[r, rows, 0:GROUP_WIDTH], cat[r, rows, GROUP_WIDTH:]

    def edges(g, n_res, sb):
        length = seq // DILATED_PATTERNS[g][1]
        i0 = j * n_res + sb * QB
        return jnp.maximum(0, HALF - i0), jnp.minimum(KB, length + HALF - i0)

    def body0(sb, carry):
        rows = pl.ds(pl.multiple_of(sb * QB, QB), QB)
        kw, vw = window(cat0, 0, sb)
        lo, hi = edges(0, T_ATT, sb)
        acc, m_b, l_b = _attn_sub_block(q0_ref[rows, :], kw, vw, bias_ref, 0, lo, hi)
        for c in range(2):
            cols = slice(c * LANES, (c + 1) * LANES)
            m_st[c, rows, :] = m_b[:, cols]
            l_st[c, rows, :] = l_b[:, cols]
            a_st[c, rows, :] = acc[:, cols]
        return carry

    lax.fori_loop(0, T_ATT // QB, body0, 0, unroll=ATT_UNROLL)

    assert T_ATT // DILATED_PATTERNS[2][1] == QB
    for g, q_ref, cat in ((1, q1_ref, cat1), (2, q2_ref, None)):
        d = DILATED_PATTERNS[g][1]
        n_res = T_ATT // d
        sb_per_res = n_res // QB

        def body(idx, carry, g=g, q_ref=q_ref, cat=cat, n_res=n_res, sb_per_res=sb_per_res):
            r = idx // sb_per_res
            sb = idx % sb_per_res
            if cat is None:
                kv = jnp.concatenate([kv2p_ref[r], kv2_ref[r], kv2n_ref[r]], axis=0)
                kw, vw = kv[:, 0:GROUP_WIDTH], kv[:, GROUP_WIDTH:]
            else:
                kw, vw = window(cat, r, sb)
            lo, hi = edges(g, n_res, sb)
            q_sub = q_ref[r, pl.ds(pl.multiple_of(sb * QB, QB), QB), :]
            acc, m_b, l_b = _attn_sub_block(q_sub, kw, vw, bias_ref, g, lo, hi)
            rows = pl.ds(pl.multiple_of(idx * QB, QB), QB)
            m_tmp[rows, :] = m_b
            l_tmp[rows, :] = l_b
            a_tmp[rows, :] = acc
            return carry

        lax.fori_loop(0, T_ATT // QB, body, 0, unroll=ATT_UNROLL)

        for r in range(d):
            for ch in range(sb_per_res):
                src = slice(r * n_res + ch * QB, r * n_res + (ch + 1) * QB)
                tok = pl.ds(ch * QB * d + r, QB, stride=d)
                for c in range(2):
                    cols = slice(c * LANES, (c + 1) * LANES)
                    m_new_part = m_tmp[src, cols]
                    m_old = m_st[c, tok, :]
                    m_new = jnp.maximum(m_old, m_new_part)
                    e_old = jnp.exp(m_old - m_new)
                    e_new = jnp.exp(m_new_part - m_new)
                    l_new = e_old * l_st[c, tok, :] + e_new * l_tmp[src, cols]
                    a_new = e_old * a_st[c, tok, :] + e_new * a_tmp[src, cols]
                    if g == N_GROUPS_A - 1:
                        a_st[c, tok, :] = a_new / l_new
                    else:
                        m_st[c, tok, :] = m_new
                        l_st[c, tok, :] = l_new
                        a_st[c, tok, :] = a_new

    for c in range(2):
        y_ref[:, c * LANES:(c + 1) * LANES] = a_st[c].astype(BF16)


def _attn_call(q0, kv0, q1, kv1, q2, kv2, batch, seq):
    n = q0.shape[0]
    tiles = seq // T_ATT
    specs = []
    scratch = []
    blocks_per_tile = T_ATT // HALF
    n_half_blocks = n // HALF
    specs += [
        pl.BlockSpec((T_ATT, GROUP_WIDTH), lambda b, j: (b * tiles + j, 0)),
        pl.BlockSpec((T_ATT, 2 * GROUP_WIDTH), lambda b, j: (b * tiles + j, 0)),
        pl.BlockSpec((HALF, 2 * GROUP_WIDTH),
                     lambda b, j: (jnp.maximum((b * tiles + j) * blocks_per_tile - 1, 0), 0)),
        pl.BlockSpec((HALF, 2 * GROUP_WIDTH),
                     lambda b, j: (jnp.minimum((b * tiles + j + 1) * blocks_per_tile, n_half_blocks - 1), 0)),
    ]
    scratch.append(pltpu.VMEM((1, T_ATT + 2 * HALF, 2 * GROUP_WIDTH), BF16))
    for g in (1, 2):
        d = DILATED_PATTERNS[g][1]
        n_res = T_ATT // d
        per_tile = n_res // HALF
        last = seq // d // HALF - 1
        specs += [
            pl.BlockSpec((None, d, n_res, GROUP_WIDTH), lambda b, j: (b, 0, j, 0)),
            pl.BlockSpec((None, d, n_res, 2 * GROUP_WIDTH), lambda b, j: (b, 0, j, 0)),
            pl.BlockSpec((None, d, HALF, 2 * GROUP_WIDTH),
                         lambda b, j, per_tile=per_tile: (b, 0, jnp.maximum(j * per_tile - 1, 0), 0)),
            pl.BlockSpec((None, d, HALF, 2 * GROUP_WIDTH),
                         lambda b, j, per_tile=per_tile, last=last: (b, 0, jnp.minimum((j + 1) * per_tile, last), 0)),
        ]
        if n_res > QB:
            scratch.append(pltpu.VMEM((d, n_res + 2 * HALF, 2 * GROUP_WIDTH), BF16))
    scratch.append(pltpu.VMEM((N_HEADS_A, QB, KB), F32))
    scratch += [pltpu.VMEM((2, T_ATT, LANES), F32) for _ in range(3)]
    scratch += [pltpu.VMEM((T_ATT, GROUP_WIDTH), F32) for _ in range(3)]
    return pl.pallas_call(
        functools.partial(_attn_kernel, seq=seq),
        grid=(batch, tiles),
        in_specs=specs,
        out_specs=pl.BlockSpec((T_ATT, GROUP_WIDTH), lambda b, j: (b * tiles + j, 0)),
        out_shape=jax.ShapeDtypeStruct((n, GROUP_WIDTH), BF16),
        scratch_shapes=scratch,
        compiler_params=pltpu.CompilerParams(dimension_semantics=("arbitrary", "arbitrary"),
                                             vmem_limit_bytes=VMEM_LIMIT),
        name="attn",
    )(q0, kv0, kv0, kv0, q1, kv1, kv1, kv1, q2, kv2, kv2, kv2)


def _split_dot(a, w_cat, w_hi):
    a_hi = a.astype(BF16)
    a_lo = (a - a_hi.astype(F32)).astype(BF16)
    both = jnp.dot(a_hi, w_cat, preferred_element_type=F32)
    return both[:, 0:LANES] + both[:, LANES:] + jnp.dot(a_lo, w_hi, preferred_element_type=F32)


ST_BASE, ST_TILE, ST_FREE = 0, 1, 2
AL_NEW = 0
RT_POS = 4


def _mix_kernel(x_ref, ya_ref, yb_ref, edge_ref, edgep_ref, edgen_ref, gate_ref,
                wa_f32, wc_f32, wo_f32, g2_ref, wr_cat_ref, wr_hi_ref, br_ref,
                x1_ref, route_ref, routet_ref, alloc_ref, xs_hbm,
                st_ref, hb0, hb1, posv, poss, wa_ref, wc_ref, wo_ref, psem, dsem, *, seq, trash_row):
    i = pl.program_id(0)
    last_step = pl.num_programs(0) - 1
    hbs = (hb0, hb1)

    def rows_done(h):
        for k in range(2):
            pltpu.make_async_copy(hbs[h], xs_hbm.at[pl.ds(0, TM_MIX * SUBLANES)], dsem.at[h, k]).wait()

    def send_row(h, j, first_sublane):
        src = hbs[h].at[pl.ds(first_sublane, SUBLANES)]
        for k in range(2):
            dst = xs_hbm.at[pl.ds(pl.multiple_of(poss[h, RT_POS + k, j], SUBLANES), SUBLANES)]
            pltpu.make_async_copy(src, dst, dsem.at[h, k]).start(priority=1)

    def send_rows(h):
        def one(j, carry):
            send_row(h, j, pl.multiple_of(j * SUBLANES, SUBLANES))
            return carry
        lax.fori_loop(0, TM_MIX, one, 0, unroll=8)

    def positions_to_smem(h):
        return pltpu.make_async_copy(posv, poss.at[h], psem.at[h])

    @pl.when(i == 0)
    def _():
        st_ref[...] = jnp.zeros_like(st_ref)
        wa_ref[...] = wa_f32[...].astype(BF16)
        wc_ref[...] = wc_f32[...].astype(BF16)
        wo_ref[...] = wo_f32[...].astype(BF16)
        hb1[...] = jnp.zeros_like(hb1)
        spare = (trash_row + lax.broadcasted_iota(jnp.int32, posv.shape, 1)
                 + jnp.where(lax.broadcasted_iota(jnp.int32, posv.shape, 0) == RT_POS + 1, TM_MIX, 0))
        posv[...] = spare * SUBLANES
        positions_to_smem(1).start()

    def tile(h):
        rows = pl.ds(h * TM_MIX, TM_MIX)
        t0 = (i * MIX_TILES + h) * TM_MIX
        positions_to_smem(1 - h).wait()

        def send_other(part):
            for j in range(part * TM_MIX // 4, (part + 1) * TM_MIX // 4):
                send_row(1 - h, j, j * SUBLANES)

        yb_in = yb_ref[rows, :]
        patch = 16
        if h == 0:
            add = jnp.where(t0 % seq == 0, 0.0, edge_ref[0:1, :] * edgep_ref[3:4, :])
            top = yb_in[0:patch, :].astype(F32)
            top = jnp.where(lax.broadcasted_iota(jnp.int32, top.shape, 0) == 0, top + add, top)
            yb_in = jnp.concatenate([top.astype(BF16), yb_in[patch:, :]], axis=0)
        if h == MIX_TILES - 1:
            add = jnp.where((t0 + TM_MIX) % seq == 0, 0.0, edge_ref[1:2, :] * edgen_ref[2:3, :])
            bot = yb_in[TM_MIX - patch:, :].astype(F32)
            bot = jnp.where(lax.broadcasted_iota(jnp.int32, bot.shape, 0) == patch - 1, bot + add, bot)
            yb_in = jnp.concatenate([yb_in[:TM_MIX - patch, :], bot.astype(BF16)], axis=0)
        send_other(0)

        y_a = jnp.dot(ya_ref[rows, :], wa_ref[...], preferred_element_type=F32)
        send_other(1)
        y_b = jnp.dot(yb_in, wc_ref[...], preferred_element_type=F32)
        merged = gate_ref[rows, 0:D_MODEL] * y_a.astype(BF16) + gate_ref[rows, D_MODEL:] * y_b.astype(BF16)
        x1 = x_ref[rows, :] + jnp.dot(merged, wo_ref[...], preferred_element_type=F32)
        x1_ref[rows, :] = x1

        h2 = _rms(x1, g2_ref[...])
        if h == 0:
            pl.when(i > 0)(lambda: rows_done(0))
        else:
            rows_done(h)
        for c in range(ROW_CHUNKS):
            hbs[h][pl.ds(c, TM_MIX, stride=ROW_CHUNKS), :] = h2[:, c * LANES:(c + 1) * LANES]

        logits = _split_dot(h2, wr_cat_ref[...], wr_hi_ref[...]) + br_ref[...]
        send_other(2)
        send_other(3)
        lane = lax.broadcasted_iota(jnp.int32, (TM_MIX, LANES), 1)
        lane_f = lane.astype(F32)
        neg = -jnp.inf
        big = float(LANES)
        is_group = lane < N_EXPERT_GROUPS
        cm = jnp.where(is_group, logits, neg)
        cmax = jnp.max(cm, axis=1, keepdims=True)
        g_idx = jnp.min(jnp.where(cm == cmax, lane_f, big), axis=1, keepdims=True)
        p_group = 1.0 / jnp.sum(jnp.where(is_group, jnp.exp(logits - cmax), 0.0), axis=1, keepdims=True)
        f_lo = N_EXPERT_GROUPS + EXPERTS_PER_GROUP * g_idx
        in_group = (lane_f >= f_lo) & (lane_f < f_lo + EXPERTS_PER_GROUP)
        fm = jnp.where(in_group, logits, neg)
        f1 = jnp.max(fm, axis=1, keepdims=True)
        i1 = jnp.min(jnp.where(fm == f1, lane_f, big), axis=1, keepdims=True)
        fm2 = jnp.where(lane_f == i1, neg, fm)
        f2 = jnp.max(fm2, axis=1, keepdims=True)
        i2 = jnp.min(jnp.where(fm2 == f2, lane_f, big), axis=1, keepdims=True)
        e21 = jnp.exp(f2 - f1)
        w_1 = p_group / (1.0 + e21)
        w_2 = p_group * e21 / (1.0 + e21)
        e_1 = i1 - N_EXPERT_GROUPS
        e_2 = i2 - N_EXPERT_GROUPS

        onehot = jnp.where((lane_f == e_1) | (lane_f == e_2), 1.0, 0.0)
        r_i = lax.broadcasted_iota(jnp.int32, (TM_MIX, TM_MIX), 0)
        c_i = lax.broadcasted_iota(jnp.int32, (TM_MIX, TM_MIX), 1)
        tri = jnp.where(c_i < r_i, 1.0, 0.0).astype(BF16)
        base = st_ref[ST_BASE:ST_BASE + 1, :]
        before = jnp.dot(tri, onehot.astype(BF16), preferred_element_type=F32) + base

        tile_rows = float(TM_MOE)
        cur_tile = st_ref[ST_TILE:ST_TILE + 1, :]
        next_free = st_ref[ST_FREE:ST_FREE + 1, :]
        count = jnp.sum(onehot, axis=0, keepdims=True)
        slot0 = jnp.floor(base * (1.0 / tile_rows))
        partial = (base - slot0 * tile_rows) > 0.0
        slot_last = jnp.floor((base + count - 1.0) * (1.0 / tile_rows))
        n_new = jnp.where(count > 0.0, slot_last - slot0 + 1.0 - jnp.where(partial, 1.0, 0.0), 0.0)
        e_r = lax.broadcasted_iota(jnp.int32, (LANES, LANES), 0)
        e_c = lax.broadcasted_iota(jnp.int32, (LANES, LANES), 1)
        earlier = jnp.where(e_r < e_c, 1.0, 0.0).astype(BF16)
        new_before = jnp.dot(jnp.broadcast_to(n_new, (SUBLANES, LANES)).astype(BF16), earlier,
                             preferred_element_type=F32)[0:1, :]
        fresh = next_free + new_before - jnp.where(partial, 1.0, 0.0) - slot0
        partial_slot = jnp.where(partial, slot0, -1.0)

        def tile_of(slot_idx, fresh_v, cur_v, partial_v):
            return jnp.where(slot_idx == partial_v, cur_v, fresh_v + slot_idx)

        s_all = jnp.floor(before * (1.0 / tile_rows))
        pos_all = tile_of(s_all, fresh, cur_tile, partial_slot) * tile_rows + (before - s_all * tile_rows)
        positions = [jnp.sum(jnp.where(lane_f == e, pos_all, 0.0), axis=1, keepdims=True) for e in (e_1, e_2)]

        st_ref[ST_BASE:ST_BASE + 1, :] = base + count
        st_ref[ST_TILE:ST_TILE + 1, :] = jnp.where(count > 0.0, tile_of(slot_last, fresh, cur_tile, partial_slot),
                                                    cur_tile)
        st_ref[ST_FREE:ST_FREE + 1, :] = next_free + jnp.sum(n_new, axis=1, keepdims=True)
        alloc_ref[h] = jnp.broadcast_to(n_new, (SUBLANES, LANES))

        route = jnp.zeros((TM_MIX, LANES), F32)
        for k, val in enumerate((e_1, e_2, w_1, w_2, positions[0], positions[1])):
            route = jnp.where(lane == k, val, route)
        route_ref[rows, :] = route
        route_t = route.T[0:SUBLANES, :]
        routet_ref[:, h * TM_MIX:(h + 1) * TM_MIX] = route_t
        posv[...] = route_t.astype(jnp.int32) * SUBLANES
        positions_to_smem(h).start()

    tile(0)
    tile(1)

    @pl.when(i == last_step)
    def _():
        positions_to_smem(1).wait()
        rows_done(0)
        send_rows(1)
        rows_done(1)

        posv[:, 0:LANES] = st_ref[...].astype(jnp.int32)
        state = positions_to_smem(0)
        state.start()
        state.wait()
        hb0[...] = jnp.zeros_like(hb0)

        def zero_jobs():
            for e in range(N_EXPERTS):
                fill = poss[0, ST_BASE, e] & (TM_MOE - 1)
                first = poss[0, ST_TILE, e] * TM_MOE
                at = fill
                size = 1
                while size < TM_MOE:
                    take = (fill > 0) & ((at & size) != 0)
                    yield take, first + at, size
                    at = at + jnp.where(take, size, 0)
                    size *= 2
            for t in range(N_EXPERTS):
                tile = poss[0, ST_FREE, 0] + t
                yield tile < trash_row // TM_MOE, tile * TM_MOE, TM_MOE

        def zero_copy(first, rows):
            return pltpu.make_async_copy(
                hb0.at[pl.ds(0, rows * SUBLANES)],
                xs_hbm.at[pl.ds(pl.multiple_of(first * SUBLANES, SUBLANES), rows * SUBLANES)], psem.at[0])

        for take, first, rows in zero_jobs():
            pl.when(take)(lambda first=first, rows=rows: zero_copy(first, rows).start())
        for take, first, rows in zero_jobs():
            pl.when(take)(lambda first=first, rows=rows: zero_copy(first, rows).wait())


def _mix_call(x2, y_attn, yb_in, edges, gates, wa, wc, wo, g2, wr_cat, wr_hi, br, seq, n_tiles):
    n = x2.shape[0]
    step_rows = MIX_TILES * TM_MIX
    assert step_rows == TM_PROJ
    n_steps = n // step_rows
    row = lambda i: (i, 0)
    const = lambda i: (0, 0)
    edge = lambda shift: pl.BlockSpec((None, SUBLANES, CONV_WIDTH),
                                      lambda i: (jnp.clip(i + shift, 0, n_steps - 1), 0, 0))
    xs_rows = n_tiles * TM_MOE + 2 * TM_MIX
    return pl.pallas_call(
        functools.partial(_mix_kernel, seq=seq, trash_row=n_tiles * TM_MOE),
        grid=(n_steps,),
        in_specs=[
            pl.BlockSpec((step_rows, D_MODEL), row),
            pl.BlockSpec((step_rows, GROUP_WIDTH), row),
            pl.BlockSpec((step_rows, CONV_WIDTH), row),
            edge(0), edge(-1), edge(1),
            pl.BlockSpec((step_rows, 2 * D_MODEL), row),
            pl.BlockSpec((GROUP_WIDTH, D_MODEL), const, pipeline_mode=pl.Buffered(1)),
            pl.BlockSpec((CONV_WIDTH, D_MODEL), const, pipeline_mode=pl.Buffered(1)),
            pl.BlockSpec((D_MODEL, D_MODEL), const, pipeline_mode=pl.Buffered(1)),
            pl.BlockSpec((1, D_MODEL), const),
            pl.BlockSpec((D_MODEL, 2 * LANES), const),
            pl.BlockSpec((D_MODEL, LANES), const),
            pl.BlockSpec((1, LANES), const),
        ],
        out_specs=[
            pl.BlockSpec((step_rows, D_MODEL), row),
            pl.BlockSpec((step_rows, LANES), row),
            pl.BlockSpec((SUBLANES, step_rows), lambda i: (0, i)),
            pl.BlockSpec((MIX_TILES, SUBLANES, LANES), lambda i: (i, 0, 0)),
            pl.BlockSpec(memory_space=pl.ANY),
        ],
        out_shape=[
            jax.ShapeDtypeStruct((n, D_MODEL), F32),
            jax.ShapeDtypeStruct((n, LANES), F32),
            jax.ShapeDtypeStruct((SUBLANES, n), F32),
            jax.ShapeDtypeStruct((n // TM_MIX, SUBLANES, LANES), F32),
            jax.ShapeDtypeStruct((xs_rows * ROW_CHUNKS, LANES), F32),
        ],
        scratch_shapes=[pltpu.VMEM((SUBLANES, LANES), F32),
                        pltpu.VMEM((TM_MIX * ROW_CHUNKS, LANES), F32),
                        pltpu.VMEM((TM_MIX * ROW_CHUNKS, LANES), F32),
                        pltpu.VMEM((SUBLANES, TM_MIX), jnp.int32),
                        pltpu.SMEM((MIX_TILES, SUBLANES, TM_MIX), jnp.int32),
                        pltpu.VMEM((GROUP_WIDTH, D_MODEL), BF16),
                        pltpu.VMEM((CONV_WIDTH, D_MODEL), BF16),
                        pltpu.VMEM((D_MODEL, D_MODEL), BF16),
                        pltpu.SemaphoreType.DMA((MIX_TILES,)),
                        pltpu.SemaphoreType.DMA((MIX_TILES, 2))],
        compiler_params=pltpu.CompilerParams(dimension_semantics=("arbitrary",),
                                             vmem_limit_bytes=VMEM_LIMIT),
        name="mix",
    )(x2, y_attn, yb_in, edges, edges, edges, gates, wa, wc, wo, g2, wr_cat, wr_hi, br)


def _row_gather(idx_ref, n_rows, src_hbm, dst, sem):
    def issue(pair, carry):
        for k in range(2):
            j = 2 * pair + k
            t = idx_ref[0, 0, j]
            pltpu.make_async_copy(src_hbm.at[pl.ds(pl.multiple_of(t * SUBLANES, SUBLANES), SUBLANES)],
                                  dst.at[pl.ds(pl.multiple_of(j * SUBLANES, SUBLANES), SUBLANES)],
                                  sem).start(priority=k)
        return carry
    lax.fori_loop(0, n_rows // 2, issue, 0, unroll=4)


def _row_gather_wait(n_rows, src_hbm, dst, sem):
    pltpu.make_async_copy(src_hbm.at[pl.ds(0, n_rows * SUBLANES)], dst, sem).wait()


def _rows_from_tiles(buf, first_row, n_rows):
    return jnp.concatenate(
        [buf[pl.ds(first_row * ROW_CHUNKS + c, n_rows, stride=ROW_CHUNKS), :] for c in range(ROW_CHUNKS)],
        axis=1)


def _expert_kernel(order_ref, te_ref, next_ref, nused_ref, xs_hbm, w1_hbm, w3_hbm, w2_hbm, y_ref,
                   xin, w1s, w3s, w2s, w13b, w2b, isem, wsem):
    i = pl.program_id(0)
    n_used = nused_ref[0]
    slot = i % XIN_SLOTS
    used = i < n_used

    def fetch(step):
        rows = pl.ds(order_ref[step] * TM_MOE, TM_MOE)
        s = step % XIN_SLOTS
        return [pltpu.make_async_copy(xs_hbm.at[rows, c, :], xin.at[s, c], isem.at[s]) for c in range(ROW_CHUNKS)]

    def weights(e):
        return [pltpu.make_async_copy(w_hbm.at[e], stage, wsem.at[k])
                for k, (w_hbm, stage) in enumerate(((w1_hbm, w1s), (w3_hbm, w3s), (w2_hbm, w2s)))]

    def start(copies):
        for cp in copies:
            cp.start()

    def wait(copies):
        for cp in copies:
            cp.wait()

    @pl.when(i == 0)
    def _():
        start(weights(te_ref[0]))
        start(fetch(0))
        pl.when(n_used > 1)(lambda: start(fetch(1)))

    pl.when(i + 2 < n_used)(lambda: start(fetch(i + 2)))

    @pl.when(jnp.logical_not(used))
    def _():
        y_ref[...] = jnp.zeros_like(y_ref)

    @pl.when(used & ((i == 0) | (te_ref[i] != te_ref[jnp.maximum(i - 1, 0)])))
    def _():
        wait(weights(te_ref[i]))
        w13b[:, 0:EXPERT_FF] = w1s[...].astype(BF16)
        w13b[:, EXPERT_FF:] = w3s[...].astype(BF16)
        w2b[...] = w2s[...].astype(BF16)
        pl.when(next_ref[i] != te_ref[i])(lambda: start(weights(next_ref[i])))

    @pl.when(used)
    def _():
        wait(fetch(i))
        x = jnp.concatenate([xin[slot, c] for c in range(ROW_CHUNKS)], axis=1)
        ab = jnp.dot(x.astype(BF16), w13b[...], preferred_element_type=F32)
        a = ab[:, 0:EXPERT_FF]
        hid = (a * jax.nn.sigmoid(a) * ab[:, EXPERT_FF:]).astype(BF16)
        y = jnp.dot(hid, w2b[...], preferred_element_type=F32)
        for c in range(ROW_CHUNKS):
            y_ref[pl.ds(c, TM_MOE, stride=ROW_CHUNKS), :] = y[:, c * LANES:(c + 1) * LANES]


def _expert_call(order, tile_expert, next_expert, n_used, xs_rows, w1, w3, w2):
    n_tiles = order.shape[0]
    any_space = pl.BlockSpec(memory_space=pl.ANY)
    grid_spec = pltpu.PrefetchScalarGridSpec(
        num_scalar_prefetch=4,
        grid=(n_tiles,),
        in_specs=[any_space, any_space, any_space, any_space],
        out_specs=pl.BlockSpec((TM_MOE * ROW_CHUNKS, LANES), lambda i, od, te, nx, nu: (od[i], 0)),
        scratch_shapes=[pltpu.VMEM((XIN_SLOTS, ROW_CHUNKS, TM_MOE, LANES), F32),
                        pltpu.VMEM((D_MODEL, EXPERT_FF), F32),
                        pltpu.VMEM((D_MODEL, EXPERT_FF), F32),
                        pltpu.VMEM((EXPERT_FF, D_MODEL), F32),
                        pltpu.VMEM((D_MODEL, 2 * EXPERT_FF), BF16),
                        pltpu.VMEM((EXPERT_FF, D_MODEL), BF16),
                        pltpu.SemaphoreType.DMA((XIN_SLOTS,)),
                        pltpu.SemaphoreType.DMA((3,))],
    )
    return pl.pallas_call(
        _expert_kernel,
        grid_spec=grid_spec,
        out_shape=jax.ShapeDtypeStruct((n_tiles * TM_MOE * ROW_CHUNKS, LANES), F32),
        compiler_params=pltpu.CompilerParams(dimension_semantics=("arbitrary",),
                                             vmem_limit_bytes=VMEM_LIMIT),
        name="experts",
    )(order, tile_expert, next_expert, n_used, xs_rows, w1, w3, w2)


def _combine_kernel(pos_ref, posn_ref, y_hbm, x1_ref, route_ref, g_ref, o_ref, buf0, buf1, sem):
    i = pl.program_id(0)
    n_steps = pl.num_programs(0)
    bufs = (buf0, buf1)

    @pl.when(i == 0)
    def _():
        _row_gather(pos_ref, 2 * TM_CMB, y_hbm, buf0, sem.at[0])

    for slot in range(2):
        @pl.when((i % 2 == slot) & (i + 1 < n_steps))
        def _(slot=slot):
            _row_gather(posn_ref, 2 * TM_CMB, y_hbm, bufs[1 - slot], sem.at[1 - slot])

    for slot in range(2):
        @pl.when(i % 2 == slot)
        def _(slot=slot):
            _row_gather_wait(2 * TM_CMB, y_hbm, bufs[slot], sem.at[slot])
            y_1 = _rows_from_tiles(bufs[slot], 0, TM_CMB)
            y_2 = _rows_from_tiles(bufs[slot], TM_CMB, TM_CMB)
            x = x1_ref[...] + route_ref[:, 2:3] * y_1 + route_ref[:, 3:4] * y_2
            o_ref[...] = _rms(x, g_ref[...])


def _combine_call(pos, y_flat, x1, route, g):
    n = x1.shape[0]
    n_steps = n // TM_CMB
    row = lambda i: (i, 0)
    return pl.pallas_call(
        _combine_kernel,
        grid=(n_steps,),
        in_specs=[
            pl.BlockSpec((1, 1, 2 * TM_CMB), lambda i: (i, 0, 0), memory_space=pltpu.SMEM),
            pl.BlockSpec((1, 1, 2 * TM_CMB), lambda i: (jnp.minimum(i + 1, n_steps - 1), 0, 0),
                         memory_space=pltpu.SMEM),
            pl.BlockSpec(memory_space=pl.ANY),
            pl.BlockSpec((TM_CMB, D_MODEL), row),
            pl.BlockSpec((TM_CMB, LANES), row),
            pl.BlockSpec((1, D_MODEL), lambda i: (0, 0)),
        ],
        out_specs=pl.BlockSpec((TM_CMB, D_MODEL), row),
        out_shape=jax.ShapeDtypeStruct((n, D_MODEL), F32),
        scratch_shapes=[pltpu.VMEM((2 * TM_CMB * ROW_CHUNKS, LANES), F32),
                        pltpu.VMEM((2 * TM_CMB * ROW_CHUNKS, LANES), F32),
                        pltpu.SemaphoreType.DMA((2,))],
        compiler_params=pltpu.CompilerParams(dimension_semantics=("arbitrary",),
                                             vmem_limit_bytes=VMEM_LIMIT),
        name="combine",
    )(pos, pos, y_flat, x1, route, g)


def _layer(x2, batch, seq, norm_mix_g, w_in, b_gate, conv_w, w_attn_out, w_conv_out, w_out, norm_ffn_g,
           w_route_group, b_route_group, w_route_expert, b_route_expert, w1, w3, w2, final_g):
    n = x2.shape[0]
    q0, kv0, q1, kv1, q2, kv2, yb_in, edges, gates = _proj_call(
        x2, norm_mix_g[None, :], w_in.astype(BF16), b_gate[None, :], conv_w, batch, seq)
    y_attn = _attn_call(q0, kv0, q1, kv1, q2, kv2, batch, seq)

    n_route = N_EXPERT_GROUPS + N_EXPERTS
    w_route = jnp.pad(jnp.concatenate([w_route_group, w_route_expert], axis=1), ((0, 0), (0, LANES - n_route)))
    b_route = jnp.pad(jnp.concatenate([b_route_group, b_route_expert]), (0, LANES - n_route))[None, :]
    wr_hi = w_route.astype(BF16)
    wr_lo = (w_route - wr_hi.astype(F32)).astype(BF16)
    n_tiles = (2 * n) // TM_MOE + N_EXPERTS
    x1, route, route_t, alloc, xs_flat = _mix_call(
        x2, y_attn, yb_in, edges, gates, w_attn_out, w_conv_out, w_out,
        norm_ffn_g[None, :], jnp.concatenate([wr_hi, wr_lo], axis=1), wr_hi, b_route, seq, n_tiles)

    i32 = jnp.int32
    taken = alloc[:, AL_NEW, :N_EXPERTS].astype(i32).reshape(-1)
    k = jnp.arange(taken.shape[0], dtype=i32)
    running = jnp.sum(jnp.where(k[:, None] >= k[None, :], taken[None, :], 0), axis=1)
    n_used = running[-1:]
    tile = jnp.arange(n_tiles, dtype=i32)
    owner = jnp.sum((running[None, :] <= tile[:, None]).astype(i32), axis=1) % N_EXPERTS
    owner = jnp.where(tile < n_used[0], owner, N_EXPERTS)
    key = owner * n_tiles + tile
    place = jnp.sum((key[None, :] < key[:, None]).astype(i32), axis=1)
    at = place[None, :] == tile[:, None]
    order = jnp.sum(jnp.where(at, tile[None, :], 0), axis=1)
    step_owner = jnp.sum(jnp.where(at, owner[None, :], 0), axis=1)
    step_expert = jnp.minimum(step_owner, N_EXPERTS - 1)
    later = (step_owner[None, :] > step_owner[:, None]) & (step_owner[None, :] < N_EXPERTS)
    next_expert = jnp.min(jnp.where(later, step_owner[None, :], N_EXPERTS), axis=1)
    next_expert = jnp.where(next_expert < N_EXPERTS, next_expert, step_expert)

    y_flat = _expert_call(order, step_expert, next_expert, n_used, xs_flat.reshape(-1, ROW_CHUNKS, LANES),
                          w1, w3, w2)
    pos = route_t[RT_POS:RT_POS + 2].astype(i32)
    pos_tiles = pos.reshape(2, n // TM_CMB, TM_CMB).transpose(1, 0, 2).reshape(n // TM_CMB, 1, 2 * TM_CMB)
    return _combine_call(pos_tiles, y_flat, x1, route, final_g[None, :])


def kernel(x, norm_mix_g, w_in, b_gate, conv_w, w_attn_out, w_conv_out, w_out, norm_ffn_g,
           w_route_group, b_route_group, w_route_expert, b_route_expert, w1, w3, w2, norm_final_g):
    batch, seq, d = x.shape
    depth = w_in.shape[0]
    assert d == D_MODEL and depth == 1 and seq % T_ATT == 0
    out = _layer(x.reshape(batch * seq, d), batch, seq, norm_mix_g[0], w_in[0], b_gate[0], conv_w[0],
                 w_attn_out[0], w_conv_out[0], w_out[0], norm_ffn_g[0], w_route_group[0], b_route_group[0],
                 w_route_expert[0], b_route_expert[0], w1[0], w3[0], w2[0], norm_final_g)
    return out.reshape(batch, seq, d)
```

```python
import functools

import numpy as np
import jax
import jax.numpy as jnp
from jax import lax
from jax.experimental import pallas as pl
from jax.experimental.pallas import tpu as pltpu

F32 = jnp.float32
BF16 = jnp.bfloat16

D_MODEL = 1024
HEAD_DIM = 64
HEADS_PER_GROUP = 4
DILATED_PATTERNS = ((128, 1), (512, 4), (2048, 16))
N_GROUPS_A = 3
N_HEADS_A = N_GROUPS_A * HEADS_PER_GROUP
ATTN_WIDTH = N_HEADS_A * HEAD_DIM
GROUP_WIDTH = HEADS_PER_GROUP * HEAD_DIM
ALIBI_SPAN = 8.0
MASK_VALUE = -1e30
CONV_WIDTH = 768
N_EXPERT_GROUPS = 4
EXPERTS_PER_GROUP = 8
N_EXPERTS = 32
EXPERT_FF = 512
RMS_EPS = 1e-6

HALF = 64
LANES = 128
SUBLANES = 8
ROW_CHUNKS = D_MODEL // LANES

COL_K = ATTN_WIDTH
COL_V = 2 * ATTN_WIDTH
COL_BG = 3 * ATTN_WIDTH
COL_CG = COL_BG + CONV_WIDTH
COL_XIN = COL_CG + CONV_WIDTH
COL_GATE = COL_XIN + CONV_WIDTH
IN_COLS = COL_GATE + 2 * D_MODEL

TM_PROJ = 1024
T_ATT = 2048
QB = 128
KB = QB + 2 * HALF
ATT_UNROLL = 16
TM_MIX = 512
MIX_TILES = 2
TM_MOE = 512
XIN_SLOTS = 3
TM_CMB = 256

VMEM_LIMIT = 56 * 1024 * 1024


def _alibi_slopes():
    return np.array([2.0 ** (-ALIBI_SPAN * (i + 1) / N_HEADS_A) for i in range(N_HEADS_A)],
                    dtype=np.float32).reshape(N_GROUPS_A, HEADS_PER_GROUP)


def _rms(x, g):
    return x * lax.rsqrt(jnp.mean(x * x, axis=-1, keepdims=True) + RMS_EPS) * g


def _proj_kernel(x_ref, g_ref, w_ref, b_ref, cw_ref,
                 q0_ref, kv0_ref, q1_ref, kv1_ref, q2_ref, kv2_ref, yb_ref, edge_ref, gate_ref, scr):
    h = _rms(x_ref[...], g_ref[...]).astype(BF16)

    def proj(c0, width):
        return jnp.dot(h, w_ref[:, c0:c0 + width], preferred_element_type=F32)

    qscale = HEAD_DIM ** -0.5
    q0_ref[...] = (proj(0, GROUP_WIDTH) * qscale).astype(BF16)
    kv0_ref[:, 0:GROUP_WIDTH] = proj(COL_K, GROUP_WIDTH).astype(BF16)
    kv0_ref[:, GROUP_WIDTH:] = proj(COL_V, GROUP_WIDTH).astype(BF16)

    for g, q_ref, kv_ref in ((1, q1_ref, kv1_ref), (2, q2_ref, kv2_ref)):
        d = DILATED_PATTERNS[g][1]
        n = TM_PROJ // d
        parts = (proj(g * GROUP_WIDTH, GROUP_WIDTH) * qscale,
                 proj(COL_K + g * GROUP_WIDTH, GROUP_WIDTH),
                 proj(COL_V + g * GROUP_WIDTH, GROUP_WIDTH))
        for i, part in enumerate(parts):
            for c in range(2):
                scr[2 * i + c] = part[:, c * LANES:(c + 1) * LANES]
        for r in range(d):
            rows = pl.ds(r, n, stride=d)
            q_ref[r] = jnp.concatenate([scr[c, rows, :] for c in range(2)], axis=1).astype(BF16)
            kv_ref[r] = jnp.concatenate([scr[c, rows, :] for c in range(2, 6)], axis=1).astype(BF16)

    width = GROUP_WIDTH
    row = lax.broadcasted_iota(jnp.int32, (TM_PROJ, width), 0)
    for c0 in range(0, CONV_WIDTH, width):
        cols = slice(c0, c0 + width)
        b_gate = proj(COL_BG + c0, width)
        u = proj(COL_CG + c0, width) * proj(COL_XIN + c0, width)
        u_prev = jnp.where(row == 0, 0.0, pltpu.roll(u, 1, axis=0))
        u_next = jnp.where(row == TM_PROJ - 1, 0.0, pltpu.roll(u, TM_PROJ - 1, axis=0))
        conv = cw_ref[0:1, cols] * u_prev + cw_ref[1:2, cols] * u + cw_ref[2:3, cols] * u_next
        yb_ref[:, cols] = (b_gate * conv).astype(BF16)
        edge_ref[:, cols] = jnp.concatenate(
            [b_gate[0:1, :] * cw_ref[0:1, cols], b_gate[TM_PROJ - 1:TM_PROJ, :] * cw_ref[2:3, cols],
             u[0:1, :], u[TM_PROJ - 1:TM_PROJ, :], jnp.zeros((SUBLANES - 4, width), F32)], axis=0)
    for c in range(4):
        w = 2 * D_MODEL // 4
        z = proj(COL_GATE + c * w, w) + b_ref[:, c * w:(c + 1) * w]
        gate_ref[:, c * w:(c + 1) * w] = jax.nn.sigmoid(z).astype(BF16)


def _proj_call(x2, g, w_in, b_gate, conv_w, batch, seq):
    n = x2.shape[0]
    assert seq % TM_PROJ == 0
    steps_per_batch = seq // TM_PROJ
    d1, d2 = DILATED_PATTERNS[1][1], DILATED_PATTERNS[2][1]
    row = lambda i: (i, 0)
    res = lambda i: (i // steps_per_batch, 0, i % steps_per_batch, 0)
    const = lambda i: (0, 0)
    out_shape = [
        jax.ShapeDtypeStruct((n, GROUP_WIDTH), BF16),
        jax.ShapeDtypeStruct((n, 2 * GROUP_WIDTH), BF16),
        jax.ShapeDtypeStruct((batch, d1, seq // d1, GROUP_WIDTH), BF16),
        jax.ShapeDtypeStruct((batch, d1, seq // d1, 2 * GROUP_WIDTH), BF16),
        jax.ShapeDtypeStruct((batch, d2, seq // d2, GROUP_WIDTH), BF16),
        jax.ShapeDtypeStruct((batch, d2, seq // d2, 2 * GROUP_WIDTH), BF16),
        jax.ShapeDtypeStruct((n, CONV_WIDTH), BF16),
        jax.ShapeDtypeStruct((n // TM_PROJ, SUBLANES, CONV_WIDTH), F32),
        jax.ShapeDtypeStruct((n, 2 * D_MODEL), BF16),
    ]
    out_specs = [
        pl.BlockSpec((TM_PROJ, GROUP_WIDTH), row),
        pl.BlockSpec((TM_PROJ, 2 * GROUP_WIDTH), row),
        pl.BlockSpec((None, d1, TM_PROJ // d1, GROUP_WIDTH), res),
        pl.BlockSpec((None, d1, TM_PROJ // d1, 2 * GROUP_WIDTH), res),
        pl.BlockSpec((None, d2, TM_PROJ // d2, GROUP_WIDTH), res),
        pl.BlockSpec((None, d2, TM_PROJ // d2, 2 * GROUP_WIDTH), res),
        pl.BlockSpec((TM_PROJ, CONV_WIDTH), row),
        pl.BlockSpec((None, SUBLANES, CONV_WIDTH), lambda i: (i, 0, 0)),
        pl.BlockSpec((TM_PROJ, 2 * D_MODEL), row),
    ]
    return pl.pallas_call(
        _proj_kernel,
        grid=(n // TM_PROJ,),
        in_specs=[
            pl.BlockSpec((TM_PROJ, D_MODEL), row),
            pl.BlockSpec((1, D_MODEL), const),
            pl.BlockSpec((D_MODEL, IN_COLS), const, pipeline_mode=pl.Buffered(1)),
            pl.BlockSpec((1, 2 * D_MODEL), const),
            pl.BlockSpec((3, CONV_WIDTH), const),
        ],
        out_specs=out_specs,
        out_shape=out_shape,
        scratch_shapes=[pltpu.VMEM((6, TM_PROJ, LANES), F32)],
        compiler_params=pltpu.CompilerParams(dimension_semantics=("arbitrary",),
                                             vmem_limit_bytes=VMEM_LIMIT),
        name="proj",
    )(x2, g, w_in, b_gate, conv_w)


def _attn_sub_block(q_sub, kw, vw, bias_ref, g, lo, hi):
    assert KB == GROUP_WIDTH
    lane = lax.broadcasted_iota(jnp.int32, (QB, KB), 1)
    edge_ok = (lane >= lo) & (lane < hi)
    heads = [(lane >= h * HEAD_DIM) & (lane < (h + 1) * HEAD_DIM) for h in range(HEADS_PER_GROUP)]
    zero = jnp.zeros((), BF16)
    q_stack = jnp.concatenate([jnp.where(hm, q_sub, zero) for hm in heads], axis=0)
    s_all = lax.dot_general(q_stack, kw, (((1,), (1,)), ((), ())), preferred_element_type=F32)
    probs = []
    m_b = l_b = None
    for h, hm in enumerate(heads):
        s = s_all[h * QB:(h + 1) * QB] + bias_ref[g * HEADS_PER_GROUP + h]
        s = jnp.where(edge_ok, s, MASK_VALUE)
        m = jnp.max(s, axis=1, keepdims=True)
        p = jnp.exp(s - m)
        l = jnp.sum(p, axis=1, keepdims=True)
        probs.append(p.astype(BF16))
        m_b = jnp.broadcast_to(m, (QB, GROUP_WIDTH)) if m_b is None else jnp.where(hm, m, m_b)
        l_b = jnp.broadcast_to(l, (QB, GROUP_WIDTH)) if l_b is None else jnp.where(hm, l, l_b)
    o_all = jnp.dot(jnp.concatenate(probs, axis=0), vw, preferred_element_type=F32)
    acc = o_all[0:QB]
    for h in range(1, HEADS_PER_GROUP):
        acc = jnp.where(heads[h], o_all[h * QB:(h + 1) * QB], acc)
    return acc, m_b, l_b


def _attn_kernel(q0_ref, kv0_ref, kv0p_ref, kv0n_ref,
                 q1_ref, kv1_ref, kv1p_ref, kv1n_ref,
                 q2_ref, kv2_ref, kv2p_ref, kv2n_ref,
                 y_ref,
                 cat0, cat1, bias_ref, m_st, l_st, a_st, m_tmp, l_tmp, a_tmp, *, seq):
    j = pl.program_id(1)

    @pl.when((pl.program_id(0) == 0) & (j == 0))
    def _():
        qi = lax.broadcasted_iota(jnp.int32, (QB, KB), 0)
        kc = lax.broadcasted_iota(jnp.int32, (QB, KB), 1)
        adelta = jnp.abs(kc - HALF - qi)
        band = adelta <= HALF
        slopes = _alibi_slopes()
        for g in range(N_GROUPS_A):
            dist = (adelta * DILATED_PATTERNS[g][1]).astype(F32)
            for h in range(HEADS_PER_GROUP):
                bias_ref[g * HEADS_PER_GROUP + h] = jnp.where(band, -(float(slopes[g, h]) * dist), MASK_VALUE)

    for cat, own, prv, nxt in ((cat0, kv0_ref, kv0p_ref, kv0n_ref),
                               (cat1, kv1_ref, kv1p_ref, kv1n_ref)):
        n_own = own.shape[-2]
        cat[:, 0:HALF, :] = prv[...].reshape(cat.shape[0], HALF, 2 * GROUP_WIDTH)
        cat[:, HALF:HALF + n_own, :] = own[...].reshape(cat.shape[0], n_own, 2 * GROUP_WIDTH)
        cat[:, HALF + n_own:, :] = nxt[...].reshape(cat.shape[0], HALF, 2 * GROUP_WIDTH)

    def window(cat, r, sb):
        rows = pl.ds(pl.multiple_of(sb * QB, QB), KB)
        return cat[r, rows, 0:GROUP_WIDTH], cat[r, rows, GROUP_WIDTH:]

    def edges(g, n_res, sb):
        length = seq // DILATED_PATTERNS[g][1]
        i0 = j * n_res + sb * QB
        return jnp.maximum(0, HALF - i0), jnp.minimum(KB, length + HALF - i0)

    def body0(sb, carry):
        rows = pl.ds(pl.multiple_of(sb * QB, QB), QB)
        kw, vw = window(cat0, 0, sb)
        lo, hi = edges(0, T_ATT, sb)
        acc, m_b, l_b = _attn_sub_block(q0_ref[rows, :], kw, vw, bias_ref, 0, lo, hi)
        for c in range(2):
            cols = slice(c * LANES, (c + 1) * LANES)
            m_st[c, rows, :] = m_b[:, cols]
            l_st[c, rows, :] = l_b[:, cols]
            a_st[c, rows, :] = acc[:, cols]
        return carry

    lax.fori_loop(0, T_ATT // QB, body0, 0, unroll=ATT_UNROLL)

    assert T_ATT // DILATED_PATTERNS[2][1] == QB
    for g, q_ref, cat in ((1, q1_ref, cat1), (2, q2_ref, None)):
        d = DILATED_PATTERNS[g][1]
        n_res = T_ATT // d
        sb_per_res = n_res // QB

        def body(idx, carry, g=g, q_ref=q_ref, cat=cat, n_res=n_res, sb_per_res=sb_per_res):
            r = idx // sb_per_res
            sb = idx % sb_per_res
            if cat is None:
                kv = jnp.concatenate([kv2p_ref[r], kv2_ref[r], kv2n_ref[r]], axis=0)
                kw, vw = kv[:, 0:GROUP_WIDTH], kv[:, GROUP_WIDTH:]
            else:
                kw, vw = window(cat, r, sb)
            lo, hi = edges(g, n_res, sb)
            q_sub = q_ref[r, pl.ds(pl.multiple_of(sb * QB, QB), QB), :]
            acc, m_b, l_b = _attn_sub_block(q_sub, kw, vw, bias_ref, g, lo, hi)
            rows = pl.ds(pl.multiple_of(idx * QB, QB), QB)
            m_tmp[rows, :] = m_b
            l_tmp[rows, :] = l_b
            a_tmp[rows, :] = acc
            return carry

        lax.fori_loop(0, T_ATT // QB, body, 0, unroll=ATT_UNROLL)

        for r in range(d):
            for ch in range(sb_per_res):
                src = slice(r * n_res + ch * QB, r * n_res + (ch + 1) * QB)
                tok = pl.ds(ch * QB * d + r, QB, stride=d)
                for c in range(2):
                    cols = slice(c * LANES, (c + 1) * LANES)
                    m_new_part = m_tmp[src, cols]
                    m_old = m_st[c, tok, :]
                    m_new = jnp.maximum(m_old, m_new_part)
                    e_old = jnp.exp(m_old - m_new)
                    e_new = jnp.exp(m_new_part - m_new)
                    l_new = e_old * l_st[c, tok, :] + e_new * l_tmp[src, cols]
                    a_new = e_old * a_st[c, tok, :] + e_new * a_tmp[src, cols]
                    if g == N_GROUPS_A - 1:
                        a_st[c, tok, :] = a_new / l_new
                    else:
                        m_st[c, tok, :] = m_new
                        l_st[c, tok, :] = l_new
                        a_st[c, tok, :] = a_new

    for c in range(2):
        y_ref[:, c * LANES:(c + 1) * LANES] = a_st[c].astype(BF16)


def _attn_call(q0, kv0, q1, kv1, q2, kv2, batch, seq):
    n = q0.shape[0]
    tiles = seq // T_ATT
    specs = []
    scratch = []
    blocks_per_tile = T_ATT // HALF
    n_half_blocks = n // HALF
    specs += [
        pl.BlockSpec((T_ATT, GROUP_WIDTH), lambda b, j: (b * tiles + j, 0)),
        pl.BlockSpec((T_ATT, 2 * GROUP_WIDTH), lambda b, j: (b * tiles + j, 0)),
        pl.BlockSpec((HALF, 2 * GROUP_WIDTH),
                     lambda b, j: (jnp.maximum((b * tiles + j) * blocks_per_tile - 1, 0), 0)),
        pl.BlockSpec((HALF, 2 * GROUP_WIDTH),
                     lambda b, j: (jnp.minimum((b * tiles + j + 1) * blocks_per_tile, n_half_blocks - 1), 0)),
    ]
    scratch.append(pltpu.VMEM((1, T_ATT + 2 * HALF, 2 * GROUP_WIDTH), BF16))
    for g in (1, 2):
        d = DILATED_PATTERNS[g][1]
        n_res = T_ATT // d
        per_tile = n_res // HALF
        last = seq // d // HALF - 1
        specs += [
            pl.BlockSpec((None, d, n_res, GROUP_WIDTH), lambda b, j: (b, 0, j, 0)),
            pl.BlockSpec((None, d, n_res, 2 * GROUP_WIDTH), lambda b, j: (b, 0, j, 0)),
            pl.BlockSpec((None, d, HALF, 2 * GROUP_WIDTH),
                         lambda b, j, per_tile=per_tile: (b, 0, jnp.maximum(j * per_tile - 1, 0), 0)),
            pl.BlockSpec((None, d, HALF, 2 * GROUP_WIDTH),
                         lambda b, j, per_tile=per_tile, last=last: (b, 0, jnp.minimum((j + 1) * per_tile, last), 0)),
        ]
        if n_res > QB:
            scratch.append(pltpu.VMEM((d, n_res + 2 * HALF, 2 * GROUP_WIDTH), BF16))
    scratch.append(pltpu.VMEM((N_HEADS_A, QB, KB), F32))
    scratch += [pltpu.VMEM((2, T_ATT, LANES), F32) for _ in range(3)]
    scratch += [pltpu.VMEM((T_ATT, GROUP_WIDTH), F32) for _ in range(3)]
    return pl.pallas_call(
        functools.partial(_attn_kernel, seq=seq),
        grid=(batch, tiles),
        in_specs=specs,
        out_specs=pl.BlockSpec((T_ATT, GROUP_WIDTH), lambda b, j: (b * tiles + j, 0)),
        out_shape=jax.ShapeDtypeStruct((n, GROUP_WIDTH), BF16),
        scratch_shapes=scratch,
        compiler_params=pltpu.CompilerParams(dimension_semantics=("arbitrary", "arbitrary"),
                                             vmem_limit_bytes=VMEM_LIMIT),
        name="attn",
    )(q0, kv0, kv0, kv0, q1, kv1, kv1, kv1, q2, kv2, kv2, kv2)


ST_BASE, ST_TILE, ST_FREE = 0, 1, 2
AL_NEW = 0
RT_POS = 4


def _mix_kernel(x_ref, ya_ref, yb_ref, edge_ref, edgep_ref, edgen_ref, gate_ref,
                wa_f32, wc_f32, wo_f32, g2_ref, wr_cat_ref, wr_hi_ref, br_ref,
                x1_ref, route_ref, routet_ref, alloc_ref, xs_hbm,
                st_ref, hb0, hb1, posv, poss, wa_ref, wc_ref, wo_ref, psem, dsem, *, seq, trash_row):
    i = pl.program_id(0)
    last_step = pl.num_programs(0) - 1
    hbs = (hb0, hb1)

    def rows_done(h):
        for k in range(2):
            pltpu.make_async_copy(hbs[h], xs_hbm.at[pl.ds(0, TM_MIX * SUBLANES)], dsem.at[h, k]).wait()

    def send_row(h, j, first_sublane):
        src = hbs[h].at[pl.ds(first_sublane, SUBLANES)]
        for k in range(2):
            dst = xs_hbm.at[pl.ds(pl.multiple_of(poss[h, RT_POS + k, j], SUBLANES), SUBLANES)]
            pltpu.make_async_copy(src, dst, dsem.at[h, k]).start(priority=1)

    def send_rows(h):
        def one(j, carry):
            send_row(h, j, pl.multiple_of(j * SUBLANES, SUBLANES))
            return carry
        lax.fori_loop(0, TM_MIX, one, 0, unroll=8)

    def positions_to_smem(h):
        return pltpu.make_async_copy(posv, poss.at[h], psem.at[h])

    @pl.when(i == 0)
    def _():
        st_ref[...] = jnp.zeros_like(st_ref)
        wa_ref[...] = wa_f32[...].astype(BF16)
        wc_ref[...] = wc_f32[...].astype(BF16)
        wo_ref[...] = wo_f32[...].astype(BF16)
        hb1[...] = jnp.zeros_like(hb1)
        spare = (trash_row + lax.broadcasted_iota(jnp.int32, posv.shape, 1)
                 + jnp.where(lax.broadcasted_iota(jnp.int32, posv.shape, 0) == RT_POS + 1, TM_MIX, 0))
        posv[...] = spare * SUBLANES
        positions_to_smem(1).start()

    def tile(h):
        rows = pl.ds(h * TM_MIX, TM_MIX)
        t0 = (i * MIX_TILES + h) * TM_MIX
        positions_to_smem(1 - h).wait()

        def send_other(part):
            for j in range(part * TM_MIX // 4, (part + 1) * TM_MIX // 4):
                send_row(1 - h, j, j * SUBLANES)

        yb_in = yb_ref[rows, :]
        patch = 16
        if h == 0:
            add = jnp.where(t0 % seq == 0, 0.0, edge_ref[0:1, :] * edgep_ref[3:4, :])
            top = yb_in[0:patch, :].astype(F32)
            top = jnp.where(lax.broadcasted_iota(jnp.int32, top.shape, 0) == 0, top + add, top)
            yb_in = jnp.concatenate([top.astype(BF16), yb_in[patch:, :]], axis=0)
        if h == MIX_TILES - 1:
            add = jnp.where((t0 + TM_MIX) % seq == 0, 0.0, edge_ref[1:2, :] * edgen_ref[2:3, :])
            bot = yb_in[TM_MIX - patch:, :].astype(F32)
            bot = jnp.where(lax.broadcasted_iota(jnp.int32, bot.shape, 0) == patch - 1, bot + add, bot)
            yb_in = jnp.concatenate([yb_in[:TM_MIX - patch, :], bot.astype(BF16)], axis=0)
        send_other(0)

        y_a = jnp.dot(ya_ref[rows, :], wa_ref[...], preferred_element_type=F32)
        send_other(1)
        y_b = jnp.dot(yb_in, wc_ref[...], preferred_element_type=F32)
        merged = gate_ref[rows, 0:D_MODEL] * y_a.astype(BF16) + gate_ref[rows, D_MODEL:] * y_b.astype(BF16)
        x1 = x_ref[rows, :] + jnp.dot(merged, wo_ref[...], preferred_element_type=F32)
        x1_ref[rows, :] = x1

        if h == 0:
            pl.when(i > 0)(lambda: rows_done(0))
        else:
            rows_done(h)
        half = TM_MIX // 2
        his, los = [], []
        for r0 in (0, half):
            h2 = _rms(x1[r0:r0 + half, :], g2_ref[...])
            for c in range(ROW_CHUNKS):
                hbs[h][pl.ds(r0 * ROW_CHUNKS + c, half, stride=ROW_CHUNKS), :] = h2[:, c * LANES:(c + 1) * LANES]
            hi = h2.astype(BF16)
            his.append(hi)
            los.append((h2 - hi.astype(F32)).astype(BF16))
        a_hi = jnp.concatenate(his, axis=0)
        a_lo = jnp.concatenate(los, axis=0)

        both = jnp.dot(a_hi, wr_cat_ref[...], preferred_element_type=F32)
        logits = (both[:, 0:LANES] + both[:, LANES:]
                  + jnp.dot(a_lo, wr_hi_ref[...], preferred_element_type=F32) + br_ref[...])
        send_other(2)
        send_other(3)
        lane = lax.broadcasted_iota(jnp.int32, (TM_MIX, LANES), 1)
        lane_f = lane.astype(F32)
        neg = -jnp.inf
        big = float(LANES)
        is_group = lane < N_EXPERT_GROUPS
        cm = jnp.where(is_group, logits, neg)
        cmax = jnp.max(cm, axis=1, keepdims=True)
        g_idx = jnp.min(jnp.where(cm == cmax, lane_f, big), axis=1, keepdims=True)
        p_group = 1.0 / jnp.sum(jnp.where(is_group, jnp.exp(logits - cmax), 0.0), axis=1, keepdims=True)
        f_lo = N_EXPERT_GROUPS + EXPERTS_PER_GROUP * g_idx
        in_group = (lane_f >= f_lo) & (lane_f < f_lo + EXPERTS_PER_GROUP)
        fm = jnp.where(in_group, logits, neg)
        f1 = jnp.max(fm, axis=1, keepdims=True)
        i1 = jnp.min(jnp.where(fm == f1, lane_f, big), axis=1, keepdims=True)
        fm2 = jnp.where(lane_f == i1, neg, fm)
        f2 = jnp.max(fm2, axis=1, keepdims=True)
        i2 = jnp.min(jnp.where(fm2 == f2, lane_f, big), axis=1, keepdims=True)
        e21 = jnp.exp(f2 - f1)
        w_1 = p_group / (1.0 + e21)
        w_2 = p_group * e21 / (1.0 + e21)
        e_1 = i1 - N_EXPERT_GROUPS
        e_2 = i2 - N_EXPERT_GROUPS

        onehot = jnp.where((lane_f == e_1) | (lane_f == e_2), 1.0, 0.0)
        r_i = lax.broadcasted_iota(jnp.int32, (TM_MIX, TM_MIX), 0)
        c_i = lax.broadcasted_iota(jnp.int32, (TM_MIX, TM_MIX), 1)
        tri = jnp.where(c_i < r_i, 1.0, 0.0).astype(BF16)
        base = st_ref[ST_BASE:ST_BASE + 1, :]
        before = jnp.dot(tri, onehot.astype(BF16), preferred_element_type=F32) + base

        tile_rows = float(TM_MOE)
        cur_tile = st_ref[ST_TILE:ST_TILE + 1, :]
        next_free = st_ref[ST_FREE:ST_FREE + 1, :]
        count = jnp.sum(onehot, axis=0, keepdims=True)
        slot0 = jnp.floor(base * (1.0 / tile_rows))
        partial = (base - slot0 * tile_rows) > 0.0
        slot_last = jnp.floor((base + count - 1.0) * (1.0 / tile_rows))
        n_new = jnp.where(count > 0.0, slot_last - slot0 + 1.0 - jnp.where(partial, 1.0, 0.0), 0.0)
        e_r = lax.broadcasted_iota(jnp.int32, (LANES, LANES), 0)
        e_c = lax.broadcasted_iota(jnp.int32, (LANES, LANES), 1)
        earlier = jnp.where(e_r < e_c, 1.0, 0.0).astype(BF16)
        new_before = jnp.dot(jnp.broadcast_to(n_new, (SUBLANES, LANES)).astype(BF16), earlier,
                             preferred_element_type=F32)[0:1, :]
        fresh = next_free + new_before - jnp.where(partial, 1.0, 0.0) - slot0
        partial_slot = jnp.where(partial, slot0, -1.0)

        def tile_of(slot_idx, fresh_v, cur_v, partial_v):
            return jnp.where(slot_idx == partial_v, cur_v, fresh_v + slot_idx)

        s_all = jnp.floor(before * (1.0 / tile_rows))
        pos_all = tile_of(s_all, fresh, cur_tile, partial_slot) * tile_rows + (before - s_all * tile_rows)
        positions = [jnp.sum(jnp.where(lane_f == e, pos_all, 0.0), axis=1, keepdims=True) for e in (e_1, e_2)]

        st_ref[ST_BASE:ST_BASE + 1, :] = base + count
        st_ref[ST_TILE:ST_TILE + 1, :] = jnp.where(count > 0.0, tile_of(slot_last, fresh, cur_tile, partial_slot),
                                                    cur_tile)
        st_ref[ST_FREE:ST_FREE + 1, :] = next_free + jnp.sum(n_new, axis=1, keepdims=True)
        alloc_ref[h] = jnp.broadcast_to(n_new, (SUBLANES, LANES))

        route = jnp.zeros((TM_MIX, LANES), F32)
        for k, val in enumerate((e_1, e_2, w_1, w_2, positions[0], positions[1])):
            route = jnp.where(lane == k, val, route)
        route_ref[rows, :] = route
        route_t = route.T[0:SUBLANES, :]
        routet_ref[:, h * TM_MIX:(h + 1) * TM_MIX] = route_t
        posv[...] = route_t.astype(jnp.int32) * SUBLANES
        positions_to_smem(h).start()

    tile(0)
    tile(1)

    @pl.when(i == last_step)
    def _():
        positions_to_smem(1).wait()
        rows_done(0)
        send_rows(1)
        rows_done(1)

        posv[:, 0:LANES] = st_ref[...].astype(jnp.int32)
        state = positions_to_smem(0)
        state.start()
        state.wait()
        hb0[...] = jnp.zeros_like(hb0)

        def zero_jobs():
            for e in range(N_EXPERTS):
                fill = poss[0, ST_BASE, e] & (TM_MOE - 1)
                first = poss[0, ST_TILE, e] * TM_MOE
                at = fill
                size = 1
                while size < TM_MOE:
                    take = (fill > 0) & ((at & size) != 0)
                    yield take, first + at, size
                    at = at + jnp.where(take, size, 0)
                    size *= 2
            for t in range(N_EXPERTS):
                tile = poss[0, ST_FREE, 0] + t
                yield tile < trash_row // TM_MOE, tile * TM_MOE, TM_MOE

        def zero_copy(first, rows):
            return pltpu.make_async_copy(
                hb0.at[pl.ds(0, rows * SUBLANES)],
                xs_hbm.at[pl.ds(pl.multiple_of(first * SUBLANES, SUBLANES), rows * SUBLANES)], psem.at[0])

        for take, first, rows in zero_jobs():
            pl.when(take)(lambda first=first, rows=rows: zero_copy(first, rows).start())
        for take, first, rows in zero_jobs():
            pl.when(take)(lambda first=first, rows=rows: zero_copy(first, rows).wait())


def _mix_call(x2, y_attn, yb_in, edges, gates, wa, wc, wo, g2, wr_cat, wr_hi, br, seq, n_tiles):
    n = x2.shape[0]
    step_rows = MIX_TILES * TM_MIX
    assert step_rows == TM_PROJ
    n_steps = n // step_rows
    row = lambda i: (i, 0)
    const = lambda i: (0, 0)
    edge = lambda shift: pl.BlockSpec((None, SUBLANES, CONV_WIDTH),
                                      lambda i: (jnp.clip(i + shift, 0, n_steps - 1), 0, 0))
    xs_rows = n_tiles * TM_MOE + 2 * TM_MIX
    return pl.pallas_call(
        functools.partial(_mix_kernel, seq=seq, trash_row=n_tiles * TM_MOE),
        grid=(n_steps,),
        in_specs=[
            pl.BlockSpec((step_rows, D_MODEL), row),
            pl.BlockSpec((step_rows, GROUP_WIDTH), row),
            pl.BlockSpec((step_rows, CONV_WIDTH), row),
            edge(0), edge(-1), edge(1),
            pl.BlockSpec((step_rows, 2 * D_MODEL), row),
            pl.BlockSpec((GROUP_WIDTH, D_MODEL), const, pipeline_mode=pl.Buffered(1)),
            pl.BlockSpec((CONV_WIDTH, D_MODEL), const, pipeline_mode=pl.Buffered(1)),
            pl.BlockSpec((D_MODEL, D_MODEL), const, pipeline_mode=pl.Buffered(1)),
            pl.BlockSpec((1, D_MODEL), const),
            pl.BlockSpec((D_MODEL, 2 * LANES), const),
            pl.BlockSpec((D_MODEL, LANES), const),
            pl.BlockSpec((1, LANES), const),
        ],
        out_specs=[
            pl.BlockSpec((step_rows, D_MODEL), row),
            pl.BlockSpec((step_rows, LANES), row),
            pl.BlockSpec((SUBLANES, step_rows), lambda i: (0, i)),
            pl.BlockSpec((MIX_TILES, SUBLANES, LANES), lambda i: (i, 0, 0)),
            pl.BlockSpec(memory_space=pl.ANY),
        ],
        out_shape=[
            jax.ShapeDtypeStruct((n, D_MODEL), F32),
            jax.ShapeDtypeStruct((n, LANES), F32),
            jax.ShapeDtypeStruct((SUBLANES, n), F32),
            jax.ShapeDtypeStruct((n // TM_MIX, SUBLANES, LANES), F32),
            jax.ShapeDtypeStruct((xs_rows * ROW_CHUNKS, LANES), F32),
        ],
        scratch_shapes=[pltpu.VMEM((SUBLANES, LANES), F32),
                        pltpu.VMEM((TM_MIX * ROW_CHUNKS, LANES), F32),
                        pltpu.VMEM((TM_MIX * ROW_CHUNKS, LANES), F32),
                        pltpu.VMEM((SUBLANES, TM_MIX), jnp.int32),
                        pltpu.SMEM((MIX_TILES, SUBLANES, TM_MIX), jnp.int32),
                        pltpu.VMEM((GROUP_WIDTH, D_MODEL), BF16),
                        pltpu.VMEM((CONV_WIDTH, D_MODEL), BF16),
                        pltpu.VMEM((D_MODEL, D_MODEL), BF16),
                        pltpu.SemaphoreType.DMA((MIX_TILES,)),
                        pltpu.SemaphoreType.DMA((MIX_TILES, 2))],
        compiler_params=pltpu.CompilerParams(dimension_semantics=("arbitrary",),
                                             vmem_limit_bytes=VMEM_LIMIT),
        name="mix",
    )(x2, y_attn, yb_in, edges, edges, edges, gates, wa, wc, wo, g2, wr_cat, wr_hi, br)


def _row_gather(idx_ref, n_rows, src_hbm, dst, sem):
    def issue(pair, carry):
        for k in range(2):
            j = 2 * pair + k
            t = idx_ref[0, 0, j]
            pltpu.make_async_copy(src_hbm.at[pl.ds(pl.multiple_of(t * SUBLANES, SUBLANES), SUBLANES)],
                                  dst.at[pl.ds(pl.multiple_of(j * SUBLANES, SUBLANES), SUBLANES)],
                                  sem).start(priority=k)
        return carry
    lax.fori_loop(0, n_rows // 2, issue, 0, unroll=4)


def _row_gather_wait(n_rows, src_hbm, dst, sem):
    pltpu.make_async_copy(src_hbm.at[pl.ds(0, n_rows * SUBLANES)], dst, sem).wait()


def _rows_from_tiles(buf, first_row, n_rows):
    return jnp.concatenate(
        [buf[pl.ds(first_row * ROW_CHUNKS + c, n_rows, stride=ROW_CHUNKS), :] for c in range(ROW_CHUNKS)],
        axis=1)


def _expert_kernel(order_ref, te_ref, next_ref, nused_ref, xs_hbm, w1_hbm, w3_hbm, w2_hbm, y_ref,
                   xin, w1s, w3s, w2s, w13b, w2b, isem, wsem):
    i = pl.program_id(0)
    n_used = nused_ref[0]
    slot = i % XIN_SLOTS
    used = i < n_used

    def fetch(step):
        rows = pl.ds(order_ref[step] * TM_MOE, TM_MOE)
        s = step % XIN_SLOTS
        return [pltpu.make_async_copy(xs_hbm.at[rows, c, :], xin.at[s, c], isem.at[s]) for c in range(ROW_CHUNKS)]

    def weights(e):
        return [pltpu.make_async_copy(w_hbm.at[e], stage, wsem.at[k])
                for k, (w_hbm, stage) in enumerate(((w1_hbm, w1s), (w3_hbm, w3s), (w2_hbm, w2s)))]

    def start(copies):
        for cp in copies:
            cp.start()

    def wait(copies):
        for cp in copies:
            cp.wait()

    @pl.when(i == 0)
    def _():
        start(weights(te_ref[0]))
        start(fetch(0))
        pl.when(n_used > 1)(lambda: start(fetch(1)))

    pl.when(i + 2 < n_used)(lambda: start(fetch(i + 2)))

    @pl.when(jnp.logical_not(used))
    def _():
        y_ref[...] = jnp.zeros_like(y_ref)

    @pl.when(used & ((i == 0) | (te_ref[i] != te_ref[jnp.maximum(i - 1, 0)])))
    def _():
        wait(weights(te_ref[i]))
        w13b[:, 0:EXPERT_FF] = w1s[...].astype(BF16)
        w13b[:, EXPERT_FF:] = w3s[...].astype(BF16)
        w2b[...] = w2s[...].astype(BF16)
        pl.when(next_ref[i] != te_ref[i])(lambda: start(weights(next_ref[i])))

    @pl.when(used)
    def _():
        wait(fetch(i))
        x = jnp.concatenate([xin[slot, c] for c in range(ROW_CHUNKS)], axis=1)
        ab = jnp.dot(x.astype(BF16), w13b[...], preferred_element_type=F32)
        a = ab[:, 0:EXPERT_FF]
        hid = (a * jax.nn.sigmoid(a) * ab[:, EXPERT_FF:]).astype(BF16)
        y = jnp.dot(hid, w2b[...], preferred_element_type=F32)
        for c in range(ROW_CHUNKS):
            y_ref[pl.ds(c, TM_MOE, stride=ROW_CHUNKS), :] = y[:, c * LANES:(c + 1) * LANES]


def _expert_call(order, tile_expert, next_expert, n_used, xs_rows, w1, w3, w2):
    n_tiles = order.shape[0]
    any_space = pl.BlockSpec(memory_space=pl.ANY)
    grid_spec = pltpu.PrefetchScalarGridSpec(
        num_scalar_prefetch=4,
        grid=(n_tiles,),
        in_specs=[any_space, any_space, any_space, any_space],
        out_specs=pl.BlockSpec((TM_MOE * ROW_CHUNKS, LANES), lambda i, od, te, nx, nu: (od[i], 0)),
        scratch_shapes=[pltpu.VMEM((XIN_SLOTS, ROW_CHUNKS, TM_MOE, LANES), F32),
                        pltpu.VMEM((D_MODEL, EXPERT_FF), F32),
                        pltpu.VMEM((D_MODEL, EXPERT_FF), F32),
                        pltpu.VMEM((EXPERT_FF, D_MODEL), F32),
                        pltpu.VMEM((D_MODEL, 2 * EXPERT_FF), BF16),
                        pltpu.VMEM((EXPERT_FF, D_MODEL), BF16),
                        pltpu.SemaphoreType.DMA((XIN_SLOTS,)),
                        pltpu.SemaphoreType.DMA((3,))],
    )
    return pl.pallas_call(
        _expert_kernel,
        grid_spec=grid_spec,
        out_shape=jax.ShapeDtypeStruct((n_tiles * TM_MOE * ROW_CHUNKS, LANES), F32),
        compiler_params=pltpu.CompilerParams(dimension_semantics=("arbitrary",),
                                             vmem_limit_bytes=VMEM_LIMIT),
        name="experts",
    )(order, tile_expert, next_expert, n_used, xs_rows, w1, w3, w2)


def _combine_kernel(pos_ref, posn_ref, y_hbm, x1_ref, route_ref, g_ref, o_ref, buf0, buf1, sem):
    i = pl.program_id(0)
    n_steps = pl.num_programs(0)
    bufs = (buf0, buf1)

    @pl.when(i == 0)
    def _():
        _row_gather(pos_ref, 2 * TM_CMB, y_hbm, buf0, sem.at[0])

    for slot in range(2):
        @pl.when((i % 2 == slot) & (i + 1 < n_steps))
        def _(slot=slot):
            _row_gather(posn_ref, 2 * TM_CMB, y_hbm, bufs[1 - slot], sem.at[1 - slot])

    for slot in range(2):
        @pl.when(i % 2 == slot)
        def _(slot=slot):
            _row_gather_wait(2 * TM_CMB, y_hbm, bufs[slot], sem.at[slot])
            y_1 = _rows_from_tiles(bufs[slot], 0, TM_CMB)
            y_2 = _rows_from_tiles(bufs[slot], TM_CMB, TM_CMB)
            x = x1_ref[...] + route_ref[:, 2:3] * y_1 + route_ref[:, 3:4] * y_2
            o_ref[...] = _rms(x, g_ref[...])


def _combine_call(pos, y_flat, x1, route, g):
    n = x1.shape[0]
    n_steps = n // TM_CMB
    row = lambda i: (i, 0)
    return pl.pallas_call(
        _combine_kernel,
        grid=(n_steps,),
        in_specs=[
            pl.BlockSpec((1, 1, 2 * TM_CMB), lambda i: (i, 0, 0), memory_space=pltpu.SMEM),
            pl.BlockSpec((1, 1, 2 * TM_CMB), lambda i: (jnp.minimum(i + 1, n_steps - 1), 0, 0),
                         memory_space=pltpu.SMEM),
            pl.BlockSpec(memory_space=pl.ANY),
            pl.BlockSpec((TM_CMB, D_MODEL), row),
            pl.BlockSpec((TM_CMB, LANES), row),
            pl.BlockSpec((1, D_MODEL), lambda i: (0, 0)),
        ],
        out_specs=pl.BlockSpec((TM_CMB, D_MODEL), row),
        out_shape=jax.ShapeDtypeStruct((n, D_MODEL), F32),
        scratch_shapes=[pltpu.VMEM((2 * TM_CMB * ROW_CHUNKS, LANES), F32),
                        pltpu.VMEM((2 * TM_CMB * ROW_CHUNKS, LANES), F32),
                        pltpu.SemaphoreType.DMA((2,))],
        compiler_params=pltpu.CompilerParams(dimension_semantics=("arbitrary",),
                                             vmem_limit_bytes=VMEM_LIMIT),
        name="combine",
    )(pos, pos, y_flat, x1, route, g)


def _layer(x2, batch, seq, norm_mix_g, w_in, b_gate, conv_w, w_attn_out, w_conv_out, w_out, norm_ffn_g,
           w_route_group, b_route_group, w_route_expert, b_route_expert, w1, w3, w2, final_g):
    n = x2.shape[0]
    q0, kv0, q1, kv1, q2, kv2, yb_in, edges, gates = _proj_call(
        x2, norm_mix_g[None, :], w_in.astype(BF16), b_gate[None, :], conv_w, batch, seq)
    y_attn = _attn_call(q0, kv0, q1, kv1, q2, kv2, batch, seq)

    n_route = N_EXPERT_GROUPS + N_EXPERTS
    w_route = jnp.pad(jnp.concatenate([w_route_group, w_route_expert], axis=1), ((0, 0), (0, LANES - n_route)))
    b_route = jnp.pad(jnp.concatenate([b_route_group, b_route_expert]), (0, LANES - n_route))[None, :]
    wr_hi = w_route.astype(BF16)
    wr_lo = (w_route - wr_hi.astype(F32)).astype(BF16)
    n_tiles = (2 * n) // TM_MOE + N_EXPERTS
    x1, route, route_t, alloc, xs_flat = _mix_call(
        x2, y_attn, yb_in, edges, gates, w_attn_out, w_conv_out, w_out,
        norm_ffn_g[None, :], jnp.concatenate([wr_hi, wr_lo], axis=1), wr_hi, b_route, seq, n_tiles)

    i32 = jnp.int32
    taken = alloc[:, AL_NEW, :N_EXPERTS].astype(i32).reshape(-1)
    k = jnp.arange(taken.shape[0], dtype=i32)
    running = jnp.sum(jnp.where(k[:, None] >= k[None, :], taken[None, :], 0), axis=1)
    n_used = running[-1:]
    tile = jnp.arange(n_tiles, dtype=i32)
    owner = jnp.sum((running[None, :] <= tile[:, None]).astype(i32), axis=1) % N_EXPERTS
    owner = jnp.where(tile < n_used[0], owner, N_EXPERTS)
    key = owner * n_tiles + tile
    place = jnp.sum((key[None, :] < key[:, None]).astype(i32), axis=1)
    at = place[None, :] == tile[:, None]
    order = jnp.sum(jnp.where(at, tile[None, :], 0), axis=1)
    step_owner = jnp.sum(jnp.where(at, owner[None, :], 0), axis=1)
    step_expert = jnp.minimum(step_owner, N_EXPERTS - 1)
    later = (step_owner[None, :] > step_owner[:, None]) & (step_owner[None, :] < N_EXPERTS)
    next_expert = jnp.min(jnp.where(later, step_owner[None, :], N_EXPERTS), axis=1)
    next_expert = jnp.where(next_expert < N_EXPERTS, next_expert, step_expert)

    y_flat = _expert_call(order, step_expert, next_expert, n_used, xs_flat.reshape(-1, ROW_CHUNKS, LANES),
                          w1, w3, w2)
    pos = route_t[RT_POS:RT_POS + 2].astype(i32)
    pos_tiles = pos.reshape(2, n // TM_CMB, TM_CMB).transpose(1, 0, 2).reshape(n // TM_CMB, 1, 2 * TM_CMB)
    return _combine_call(pos_tiles, y_flat, x1, route, final_g[None, :])


def kernel(x, norm_mix_g, w_in, b_gate, conv_w, w_attn_out, w_conv_out, w_out, norm_ffn_g,
           w_route_group, b_route_group, w_route_expert, b_route_expert, w1, w3, w2, norm_final_g):
    batch, seq, d = x.shape
    depth = w_in.shape[0]
    assert d == D_MODEL and depth == 1 and seq % T_ATT == 0
    out = _layer(x.reshape(batch * seq, d), batch, seq, norm_mix_g[0], w_in[0], b_gate[0], conv_w[0],
                 w_attn_out[0], w_conv_out[0], w_out[0], norm_ffn_g[0], w_route_group[0], b_route_group[0],
                 w_route_expert[0], b_route_expert[0], w1[0], w3[0], w2[0], norm_final_g)
    return out.reshape(batch, seq, d)
```
